```python
import jax, jax.numpy as jnp
from jax import lax
import numpy as np

D_MODEL = 1024
BATCH = 32
SEQ = 256
DEPTH = 2
DEC_BATCH = 4
DEC_SEQ = 2048
PAST_LEN = 512

GRID_W = 64
HEAD_DIM = 64
N_HEADS_NA = 8
WIN_ROWS = 8
WIN_COLS = 16
N_HEADS_SWA = 4
N_KV_SWA = 2
SWA_RADIUS = 128
SWA_BLOCK = 128
N_HEADS_RET = 4
RET_CHUNK = 128
CTX_Q_BLOCK = 128
WIDTH_NA = N_HEADS_NA * HEAD_DIM
WIDTH_SWA = N_HEADS_SWA * HEAD_DIM
WIDTH_KV_SWA = N_KV_SWA * HEAD_DIM
WIDTH_RET = N_HEADS_RET * HEAD_DIM
IN_SPLITS = (WIDTH_NA, WIDTH_NA, WIDTH_NA, WIDTH_SWA, WIDTH_KV_SWA, WIDTH_KV_SWA,
             WIDTH_RET, WIDTH_RET, WIDTH_RET, WIDTH_RET, D_MODEL, D_MODEL, D_MODEL)
IN_COLS = sum(IN_SPLITS)
N_GROUPS = 4
EXPERTS_PER_GROUP = 4
N_EXPERTS = N_GROUPS * EXPERTS_PER_GROUP
TOP_K_EXPERT = 2
D_EXPERT = D_MODEL // 4
ROPE_BASE = 10000.0
EPS = 1e-6

kernel_name = "hybrid_dit_natten_swa_retention_hmoe_step"

F32 = jnp.float32


def _rms(x, g):
    xf = x.astype(F32)
    y = xf * lax.rsqrt(jnp.mean(xf * xf, axis=-1, keepdims=True) + EPS)
    return (y * g.astype(F32)).astype(x.dtype)


def _heads(t, n):
    return t.reshape(t.shape[:-1] + (n, HEAD_DIM))


def _split_in(z):
    idx = np.cumsum(IN_SPLITS)[:-1].tolist()
    return jnp.split(z, idx, axis=-1)


def _modulation(cond, ada_w, ada_b):
    m = jax.nn.silu(cond) @ ada_w + ada_b
    return jnp.split(m, 6, axis=-1)


def _axial_rope(x):
    T = x.shape[1]
    t = jnp.arange(T)
    n_freq = HEAD_DIM // 4
    inv = ROPE_BASE ** (-jnp.arange(n_freq, dtype=F32) / n_freq)
    ang = jnp.concatenate([(t // GRID_W).astype(F32)[:, None] * inv,
                           (t % GRID_W).astype(F32)[:, None] * inv], axis=-1)
    cos = jnp.cos(ang)[None, :, None, :].astype(x.dtype)
    sin = jnp.sin(ang)[None, :, None, :].astype(x.dtype)
    x1, x2 = x[..., 0::2], x[..., 1::2]
    return jnp.stack([x1 * cos - x2 * sin, x1 * sin + x2 * cos], axis=-1).reshape(x.shape)


def _ctx_attention(q, k, v, sink):
    B, L, Hq, d = q.shape
    Hkv = k.shape[1]
    G = Hq // Hkv
    nb = L // CTX_Q_BLOCK
    qb = q.reshape(B, nb, CTX_Q_BLOCK, Hkv, G, d).transpose(1, 0, 2, 3, 4, 5)

    def block(qblk):
        s = jnp.einsum('bqkgd,bksd->bkgqs', qblk, k).astype(F32) * d ** -0.5
        if sink is not None:
            s_sink = jnp.broadcast_to(sink.astype(F32).reshape(1, Hkv, G, 1, 1), s.shape[:-1] + (1,))
            s = jnp.concatenate([s_sink, s], axis=-1)
        pr = jax.nn.softmax(s, axis=-1)
        if sink is not None:
            pr = pr[..., 1:]
        return jnp.einsum('bkgqs,bksd->bqkgd', pr.astype(v.dtype), v)

    o = lax.map(block, qb)
    return o.transpose(1, 0, 2, 3, 4, 5).reshape(B, L, Hq * d)


def _neigh_attention(q, k, v, k_ctx, v_ctx, rpb):
    B, T, H, d = q.shape
    rows = T // GRID_W
    wr = min(WIN_ROWS, rows)
    r = np.arange(rows)
    cc = np.arange(GRID_W)
    row_start = np.clip(r - wr // 2, 0, rows - wr)
    col_start = np.clip(cc - WIN_COLS // 2, 0, GRID_W - WIN_COLS)
    col_idx = col_start[:, None] + np.arange(WIN_COLS)[None, :]
    row_off = row_start[:, None] + np.arange(wr)[None, :] - r[:, None]
    col_off = col_idx - cc[:, None]
    bias = rpb[:, row_off[:, :, None, None] + WIN_ROWS - 1,
               col_off[None, None, :, :] + WIN_COLS - 1]
    bias = bias.transpose(1, 0, 3, 2, 4)
    kg = k.reshape(B, rows, GRID_W, H, d)
    vg = v.reshape(B, rows, GRID_W, H, d)
    qg = q.reshape(B, rows, GRID_W, H, d).transpose(1, 0, 2, 3, 4)
    Lc = k_ctx.shape[2]
    scale = d ** -0.5

    def one_row(args):
        rs, q_row, b_row = args
        k_blk = lax.dynamic_slice_in_dim(kg, rs, wr, axis=1)
        v_blk = lax.dynamic_slice_in_dim(vg, rs, wr, axis=1)
        k_nb = k_blk[:, :, col_idx]
        v_nb = v_blk[:, :, col_idx]
        s_loc = jnp.einsum('bqhd,brqwhd->bhqrw', q_row, k_nb).astype(F32) * scale + b_row.astype(F32)
        s_ctx = jnp.einsum('bqhd,bhsd->bhqs', q_row, k_ctx).astype(F32) * scale
        s = jnp.concatenate([s_ctx, s_loc.reshape(B, H, GRID_W, wr * WIN_COLS)], axis=-1)
        pr = jax.nn.softmax(s, axis=-1).astype(v.dtype)
        p_ctx = pr[..., :Lc]
        p_loc = pr[..., Lc:].reshape(B, H, GRID_W, wr, WIN_COLS)
        return (jnp.einsum('bhqs,bhsd->bqhd', p_ctx, v_ctx)
                + jnp.einsum('bhqrw,brqwhd->bqhd', p_loc, v_nb))

    o = lax.map(one_row, (jnp.asarray(row_start, jnp.int32), qg, bias))
    return o.transpose(1, 0, 2, 3, 4).reshape(B, T, H * d)


def _window_attention(q, k, v, k_ctx, v_ctx, sink):
    B, T, Hq, d = q.shape
    Hkv = k.shape[2]
    G = Hq // Hkv
    blk = SWA_BLOCK
    nb = T // blk
    pad = ((0, 0), (blk, blk), (0, 0), (0, 0))
    kp = jnp.pad(k, pad).reshape(B, nb + 2, blk, Hkv, d)
    vp = jnp.pad(v, pad).reshape(B, nb + 2, blk, Hkv, d)
    kw = jnp.concatenate([kp[:, :-2], kp[:, 1:-1], kp[:, 2:]], axis=2)
    vw = jnp.concatenate([vp[:, :-2], vp[:, 1:-1], vp[:, 2:]], axis=2)
    qb = q.reshape(B, nb, blk, Hkv, G, d)
    qpos = np.arange(nb)[:, None, None] * blk + np.arange(blk)[None, :, None]
    kpos = (np.arange(nb)[:, None, None] - 1) * blk + np.arange(3 * blk)[None, None, :]
    valid = (np.abs(qpos - kpos) <= SWA_RADIUS) & (kpos >= 0) & (kpos < T)
    scale = d ** -0.5
    s_loc = jnp.einsum('bnqkgd,bnskd->bnkgqs', qb, kw).astype(F32) * scale
    s_loc = jnp.where(valid[None, :, None, None], s_loc, -jnp.inf)
    s_ctx = jnp.einsum('bnqkgd,bksd->bnkgqs', qb, k_ctx).astype(F32) * scale
    s_sink = jnp.broadcast_to(sink.astype(F32).reshape(1, 1, Hkv, G, 1, 1), s_ctx.shape[:-1] + (1,))
    s = jnp.concatenate([s_sink, s_ctx, s_loc], axis=-1)
    pr = jax.nn.softmax(s, axis=-1).astype(v.dtype)
    Lc = k_ctx.shape[2]
    p_ctx = pr[..., 1:1 + Lc]
    p_loc = pr[..., 1 + Lc:]
    o = (jnp.einsum('bnkgqs,bksd->bnqkgd', p_ctx, v_ctx)
         + jnp.einsum('bnkgqs,bnskd->bnqkgd', p_loc, vw))
    return o.reshape(B, T, Hq * d)


def _retention_scan(q, k, v, log_gamma, s0):
    B, T, H, d = q.shape
    C = RET_CHUNK
    n = T // C

    def chunks(t):
        return t.reshape(B, n, C, H, d).transpose(1, 0, 3, 2, 4)

    pos = jnp.arange(C, dtype=F32)
    diff = pos[:, None] - pos[None, :]
    intra = jnp.where(diff >= 0, jnp.exp(log_gamma[:, None, None] * jnp.maximum(diff, 0.0)), 0.0)
    q_dec = jnp.exp(log_gamma[:, None] * (pos + 1.0))[..., None]
    k_dec = jnp.exp(log_gamma[:, None] * (C - 1.0 - pos))[..., None]
    c_dec = jnp.exp(log_gamma * C)[:, None, None]

    def step(s, qkv):
        qc, kc, vc = qkv
        a = jnp.einsum('bhqd,bhkd->bhqk', qc, kc) * intra
        o = jnp.einsum('bhqk,bhke->bhqe', a, vc) + jnp.einsum('bhqd,bhde->bhqe', qc, s) * q_dec
        s = c_dec * s + jnp.einsum('bhkd,bhke->bhde', kc * k_dec, vc)
        return s, o

    s_fin, o = lax.scan(step, s0, (chunks(q), chunks(k), chunks(v)))
    return o.transpose(1, 0, 3, 2, 4).reshape(B, T, H, d), s_fin


def _retention(q, k, v, gate, decay_logit, norm_g, s0_f, s0_b):
    qf = q.astype(F32)
    kf = k.astype(F32) * HEAD_DIM ** -0.5
    vf = v.astype(F32)
    lg = jax.nn.log_sigmoid(decay_logit.astype(F32))
    o_f, s_f = _retention_scan(qf, kf, vf, lg[0], s0_f.astype(F32))
    o_b, s_b = _retention_scan(qf[:, ::-1], kf[:, ::-1], vf[:, ::-1], lg[1], s0_b.astype(F32))
    o = o_f + o_b[:, ::-1]
    o = o * lax.rsqrt(jnp.mean(o * o, axis=-1, keepdims=True) + EPS)
    o = o.reshape(o.shape[:2] + (-1,)) * norm_g.astype(F32)
    return o.astype(q.dtype) * jax.nn.silu(gate), s_f, s_b


def _merge_branches(o_na, o_swa, o_ret, ga, gs, gr, p):
    z = (jax.nn.sigmoid(ga) * (o_na @ p["w_br_na"])
         + jax.nn.sigmoid(gs) * (o_swa @ p["w_br_swa"])
         + jax.nn.sigmoid(gr) * (o_ret @ p["w_br_ret"]))
    return z @ p["w_out"]


def _hier_moe(h, p):
    B, T, Dm = h.shape
    x = h.reshape(B * T, Dm)
    g_logit = (x @ p["router_group_w"] + p["router_group_b"]).astype(F32)
    g_prob = jax.nn.softmax(g_logit, axis=-1)
    _, g_sel = lax.top_k(g_logit, 1)
    p_group = jnp.take_along_axis(g_prob, g_sel, axis=-1)
    e_logit = (x @ p["router_expert_w"] + p["router_expert_b"]).astype(F32)
    e_logit = e_logit.reshape(-1, N_GROUPS, EXPERTS_PER_GROUP)
    e_in = jnp.take_along_axis(e_logit, g_sel[:, :, None], axis=1)[:, 0]
    e_top, e_sel = lax.top_k(e_in, TOP_K_EXPERT)
    w = jax.nn.softmax(e_top, axis=-1) * p_group
    eid = g_sel * EXPERTS_PER_GROUP + e_sel
    combine = jnp.sum(jax.nn.one_hot(eid, N_EXPERTS, dtype=F32) * w[..., None], axis=1)
    hg = jnp.einsum('nd,edf->nef', x, p["w_gate"])
    hu = jnp.einsum('nd,edf->nef', x, p["w_up"])
    a = jax.nn.silu(hg) * hu * combine.astype(x.dtype)[..., None]
    y = jnp.einsum('nef,efd->nd', a, p["w_down"])
    return y.reshape(B, T, Dm)


def _context_layer(x, mod, p):
    B, L, _ = x.shape
    sh1, sc1, g1, sh2, sc2, g2 = mod
    h = _rms(x, p["norm1_g"]) * (1 + sc1) + sh1
    qa, ka, va, qs, ks, vs, qr, kr, vr, gr, ga, gs, gre = _split_in(h @ p["w_in"])
    ka_c = _heads(ka, N_HEADS_NA).transpose(0, 2, 1, 3)
    va_c = _heads(va, N_HEADS_NA).transpose(0, 2, 1, 3)
    ks_c = _heads(ks, N_KV_SWA).transpose(0, 2, 1, 3)
    vs_c = _heads(vs, N_KV_SWA).transpose(0, 2, 1, 3)
    o_na = _ctx_attention(_heads(qa, N_HEADS_NA), ka_c, va_c, None)
    o_swa = _ctx_attention(_heads(qs, N_HEADS_SWA), ks_c, vs_c, p["swa_sink"])
    zero = jnp.zeros((B, N_HEADS_RET, HEAD_DIM, HEAD_DIM), F32)
    o_ret, s_f, s_b = _retention(_heads(qr, N_HEADS_RET), _heads(kr, N_HEADS_RET), _heads(vr, N_HEADS_RET),
                                 gr, p["ret_decay"], p["ret_norm_g"], zero, zero)
    x = x + g1 * _merge_branches(o_na, o_swa, o_ret, ga, gs, gre, p)
    h2 = _rms(x, p["norm2_g"]) * (1 + sc2) + sh2
    x = x + g2 * _hier_moe(h2, p)
    return x, ka_c, va_c, ks_c, vs_c, jnp.stack([s_f, s_b], axis=1)


def _latent_layer(x, mod, p, k_na_c, v_na_c, k_swa_c, v_swa_c, s_ret_c):
    sh1, sc1, g1, sh2, sc2, g2 = mod
    h = _rms(x, p["norm1_g"]) * (1 + sc1) + sh1
    qa, ka, va, qs, ks, vs, qr, kr, vr, gr, ga, gs, gre = _split_in(h @ p["w_in"])
    o_na = _neigh_attention(_heads(qa, N_HEADS_NA), _heads(ka, N_HEADS_NA), _heads(va, N_HEADS_NA),
                            k_na_c, v_na_c, p["na_rpb"])
    o_swa = _window_attention(_axial_rope(_heads(qs, N_HEADS_SWA)), _axial_rope(_heads(ks, N_KV_SWA)),
                              _heads(vs, N_KV_SWA), k_swa_c, v_swa_c, p["swa_sink"])
    o_ret, _, _ = _retention(_heads(qr, N_HEADS_RET), _heads(kr, N_HEADS_RET), _heads(vr, N_HEADS_RET),
                             gr, p["ret_decay"], p["ret_norm_g"], s_ret_c[:, 0], s_ret_c[:, 1])
    x = x + g1 * _merge_branches(o_na, o_swa, o_ret, ga, gs, gre, p)
    h2 = _rms(x, p["norm2_g"]) * (1 + sc2) + sh2
    return x + g2 * _hier_moe(h2, p)


def setup_inputs(seed: int = 0) -> dict:
    key = jax.random.key(seed)
    ks = jax.random.split(key, 40)
    D = D_MODEL

    def nrm(k, shape, scale):
        return jax.random.normal(k, shape, F32) * scale

    ret_init = jnp.log(2.0 ** jnp.arange(5, 5 + N_HEADS_RET, dtype=F32) - 1.0)
    return {
        "x_prompt": nrm(ks[0], (BATCH, SEQ, D), 1.0),
        "x_sample": nrm(ks[1], (DEC_BATCH, DEC_SEQ, D), 1.0),
        "c": nrm(ks[2], (DEC_BATCH, D), 1.0),
        "cache_na_k": nrm(ks[3], (DEC_BATCH, DEPTH, N_HEADS_NA, PAST_LEN, HEAD_DIM), 1.0),
        "cache_na_v": nrm(ks[4], (DEC_BATCH, DEPTH, N_HEADS_NA, PAST_LEN, HEAD_DIM), 1.0),
        "cache_swa_k": nrm(ks[5], (DEC_BATCH, DEPTH, N_KV_SWA, PAST_LEN, HEAD_DIM), 1.0),
        "cache_swa_v": nrm(ks[6], (DEC_BATCH, DEPTH, N_KV_SWA, PAST_LEN, HEAD_DIM), 1.0),
        "state_ret": nrm(ks[7], (DEC_BATCH, DEPTH, 2, N_HEADS_RET, HEAD_DIM, HEAD_DIM), 0.5),
        "c_ctx": nrm(ks[8], (D,), 1.0),
        "ada_w": nrm(ks[9], (DEPTH, D, 6 * D), 0.5 * D ** -0.5),
        "ada_b": nrm(ks[10], (DEPTH, 6 * D), 0.02),
        "norm1_g": 1.0 + nrm(ks[11], (DEPTH, D), 0.05),
        "norm2_g": 1.0 + nrm(ks[12], (DEPTH, D), 0.05),
        "w_in": nrm(ks[13], (DEPTH, D, IN_COLS), D ** -0.5),
        "na_rpb": nrm(ks[14], (DEPTH, N_HEADS_NA, 2 * WIN_ROWS - 1, 2 * WIN_COLS - 1), 0.1),
        "swa_sink": nrm(ks[15], (DEPTH, N_HEADS_SWA), 0.5),
        "ret_decay": ret_init + nrm(ks[16], (DEPTH, 2, N_HEADS_RET), 0.1),
        "ret_norm_g": 1.0 + nrm(ks[17], (DEPTH, WIDTH_RET), 0.05),
        "w_br_na": nrm(ks[18], (DEPTH, WIDTH_NA, D), WIDTH_NA ** -0.5),
        "w_br_swa": nrm(ks[19], (DEPTH, WIDTH_SWA, D), WIDTH_SWA ** -0.5),
        "w_br_ret": nrm(ks[20], (DEPTH, WIDTH_RET, D), WIDTH_RET ** -0.5),
        "w_out": nrm(ks[21], (DEPTH, D, D), D ** -0.5),
        "router_group_w": nrm(ks[22], (DEPTH, D, N_GROUPS), D ** -0.5),
        "router_group_b": nrm(ks[23], (DEPTH, N_GROUPS), 0.01),
        "router_expert_w": nrm(ks[24], (DEPTH, D, N_EXPERTS), D ** -0.5),
        "router_expert_b": nrm(ks[25], (DEPTH, N_EXPERTS), 0.01),
        "w_gate": nrm(ks[26], (DEPTH, N_EXPERTS, D, D_EXPERT), D ** -0.5),
        "w_up": nrm(ks[27], (DEPTH, N_EXPERTS, D, D_EXPERT), D ** -0.5),
        "w_down": nrm(ks[28], (DEPTH, N_EXPERTS, D_EXPERT, D), D_EXPERT ** -0.5),
        "final_norm_g": 1.0 + nrm(ks[29], (D,), 0.05),
    }


def reference(x_prompt, x_sample, c, cache_na_k, cache_na_v, cache_swa_k, cache_swa_v, state_ret,
              c_ctx, ada_w, ada_b, norm1_g, norm2_g, w_in, na_rpb, swa_sink, ret_decay, ret_norm_g,
              w_br_na, w_br_swa, w_br_ret, w_out, router_group_w, router_group_b, router_expert_w,
              router_expert_b, w_gate, w_up, w_down, final_norm_g):
    xp = x_prompt
    xs = x_sample
    na_k, na_v, swa_k, swa_v, ret_s = [], [], [], [], []
    for l in range(DEPTH):
        p = {
            "norm1_g": norm1_g[l], "norm2_g": norm2_g[l], "w_in": w_in[l], "na_rpb": na_rpb[l],
            "swa_sink": swa_sink[l], "ret_decay": ret_decay[l], "ret_norm_g": ret_norm_g[l],
            "w_br_na": w_br_na[l], "w_br_swa": w_br_swa[l], "w_br_ret": w_br_ret[l], "w_out": w_out[l],
            "router_group_w": router_group_w[l], "router_group_b": router_group_b[l],
            "router_expert_w": router_expert_w[l], "router_expert_b": router_expert_b[l],
            "w_gate": w_gate[l], "w_up": w_up[l], "w_down": w_down[l],
        }
        mod_ctx = _modulation(c_ctx, ada_w[l], ada_b[l])
        xp, ka, va, ksw, vsw, sr = _context_layer(xp, mod_ctx, p)
        na_k.append(ka)
        na_v.append(va)
        swa_k.append(ksw)
        swa_v.append(vsw)
        ret_s.append(sr)
        mod_lat = _modulation(c[:, None, :], ada_w[l], ada_b[l])
        xs = _latent_layer(xs, mod_lat, p, cache_na_k[:, l], cache_na_v[:, l],
                           cache_swa_k[:, l], cache_swa_v[:, l], state_ret[:, l])
    y_prompt = _rms(xp, final_norm_g)
    y_sample = _rms(xs, final_norm_g)
    new_na_k = jnp.stack(na_k, axis=1)
    new_na_v = jnp.stack(na_v, axis=1)
    new_swa_k = jnp.stack(swa_k, axis=1)
    new_swa_v = jnp.stack(swa_v, axis=1)
    new_state_ret = jnp.stack(ret_s, axis=1)
    return (y_prompt, y_sample, new_na_k, new_na_v, new_swa_k, new_swa_v, new_state_ret)
```

```python
import functools

import numpy as np
import jax
import jax.numpy as jnp
from jax import lax
from jax.experimental import pallas as pl
from jax.experimental.pallas import tpu as pltpu

F32 = jnp.float32
BF16 = jnp.bfloat16

D_MODEL = 1024
HEAD_DIM = 64
GRID_W = 64
N_HEADS_NA = 8
WIN_ROWS = 8
WIN_COLS = 16
N_HEADS_SWA = 4
N_KV_SWA = 2
SWA_RADIUS = 128
SWA_BLOCK = 128
N_HEADS_RET = 4
RET_CHUNK = 128
WIDTH_NA = N_HEADS_NA * HEAD_DIM
WIDTH_SWA = N_HEADS_SWA * HEAD_DIM
WIDTH_KV_SWA = N_KV_SWA * HEAD_DIM
WIDTH_RET = N_HEADS_RET * HEAD_DIM
IN_COLS = 3 * WIDTH_NA + WIDTH_SWA + 2 * WIDTH_KV_SWA + 4 * WIDTH_RET + 3 * D_MODEL
OFF_NA = 0
OFF_SWA = 3 * WIDTH_NA
OFF_RET = OFF_SWA + WIDTH_SWA + 2 * WIDTH_KV_SWA
OFF_GATES = OFF_RET + 4 * WIDTH_RET
N_GROUPS = 4
EXPERTS_PER_GROUP = 4
N_EXPERTS = N_GROUPS * EXPERTS_PER_GROUP
D_EXPERT = D_MODEL // 4
ROPE_BASE = 10000.0
EPS = 1e-6
SCALE = HEAD_DIM ** -0.5
MASKED = -1e30
ROUTER_LANES = 128

MIB = 1024 * 1024


def _mm(a, b):
    return jnp.dot(a.astype(BF16), b.astype(BF16), preferred_element_type=F32)


def _mm_nt(a, b):
    return lax.dot_general(a.astype(BF16), b.astype(BF16), (((1,), (1,)), ((), ())),
                           preferred_element_type=F32)


def _sigmoid(x):
    return 1.0 / (1.0 + jnp.exp(-x))


def _silu(x):
    return x * _sigmoid(x)


def _params(semantics, vmem_mib):
    return pltpu.CompilerParams(dimension_semantics=semantics, vmem_limit_bytes=vmem_mib * MIB)


def _mod_body(cond_ref, w_ref, b_ref, o_ref):
    o_ref[0] = _mm(_silu(cond_ref[...]), w_ref[0]) + b_ref[0]


def _modulation(cond, ada_w, ada_b):
    depth, d, n = ada_w.shape
    tn = 1024
    return pl.pallas_call(
        _mod_body,
        grid=(depth, n // tn),
        in_specs=[pl.BlockSpec((8, d), lambda l, j: (0, 0)),
                  pl.BlockSpec((1, d, tn), lambda l, j: (l, 0, j)),
                  pl.BlockSpec((1, 1, tn), lambda l, j: (l, 0, j))],
        out_specs=pl.BlockSpec((1, 8, tn), lambda l, j: (l, 0, j)),
        out_shape=jax.ShapeDtypeStruct((depth, 8, n), F32),
        compiler_params=_params(("arbitrary", "arbitrary"), 32),
        name="modulation",
    )(cond, ada_w, ada_b.reshape(depth, 1, n))


def _inproj_body(x_ref, mod_ref, g_ref, w_ref, o_ref, h_scr, w_scr):
    i = pl.program_id(0)
    j = pl.program_id(1)

    @pl.when(i == 0)
    def _():
        w_scr[j] = w_ref[0].astype(BF16)

    @pl.when(j == 0)
    def _():
        x = x_ref[...]
        m = mod_ref[0]
        y = x * lax.rsqrt(jnp.mean(x * x, axis=-1, keepdims=True) + EPS) * g_ref[0]
        h_scr[...] = (y * (1.0 + m[:, D_MODEL:2 * D_MODEL]) + m[:, 0:D_MODEL]).astype(BF16)

    o_ref[...] = jnp.dot(h_scr[...], w_scr[j], preferred_element_type=F32)


def _in_proj(x, mod_rows, norm_g, w_in, layer):
    n, d = x.shape
    n_cond = mod_rows.shape[0]
    tm, tn = 1024, 1024
    nj = IN_COLS // tn
    per_cond = n // n_cond
    return pl.pallas_call(
        _inproj_body,
        grid=(n // tm, nj),
        in_specs=[pl.BlockSpec((tm, d), lambda i, j: (i, 0)),
                  pl.BlockSpec((1, 1, 6 * d), lambda i, j: (i * tm // per_cond, 0, 0)),
                  pl.BlockSpec((1, 1, d), lambda i, j: (layer, 0, 0)),
                  pl.BlockSpec((1, d, tn), lambda i, j: (layer, 0, jnp.where(i == 0, j, nj - 1)))],
        out_specs=pl.BlockSpec((tm, tn), lambda i, j: (i, j)),
        out_shape=jax.ShapeDtypeStruct((n, IN_COLS), F32),
        scratch_shapes=[pltpu.VMEM((tm, d), BF16), pltpu.VMEM((nj, d, tn), BF16)],
        compiler_params=_params(("arbitrary", "arbitrary"), 48),
        name="in_proj",
    )(x, mod_rows, norm_g.reshape(norm_g.shape[0], 1, d), w_in)


def _log_sigmoid(x):
    return jnp.minimum(x, 0.0) - jnp.log1p(jnp.exp(-jnp.abs(x)))


def _retention(z_ref, c0, decay_ref, ng_ref, s0_ref, o_ref, st_ref, tab_scr, d_scr, u_scr, s_scr, n_tok):
    C = RET_CHUNK
    nc = n_tok // C
    d = HEAD_DIM
    pos = lax.broadcasted_iota(jnp.int32, (C, d), 0).astype(F32)
    ri = lax.broadcasted_iota(jnp.int32, (C, C), 0).astype(F32)
    ci = lax.broadcasted_iota(jnp.int32, (C, C), 1).astype(F32)
    diff = ri - ci
    for h in range(N_HEADS_RET):
        lgf = _log_sigmoid(jnp.full((C, C), decay_ref[0, h], F32))
        lgb = _log_sigmoid(jnp.full((C, C), decay_ref[1, h], F32))
        d_scr[h] = (jnp.where(diff >= 0, jnp.exp(lgf * jnp.maximum(diff, 0.0)), 0.0)
                    + jnp.where(diff <= 0, jnp.exp(lgb * jnp.maximum(-diff, 0.0)), 0.0))
        lf = _log_sigmoid(jnp.full((C, d), decay_ref[0, h], F32))
        lb = _log_sigmoid(jnp.full((C, d), decay_ref[1, h], F32))
        hs = slice(h * d, (h + 1) * d)
        tab_scr[0, :, hs] = jnp.exp(lf * (C - 1.0 - pos))
        tab_scr[1, :, hs] = jnp.exp(lb * pos)
        tab_scr[2, :, hs] = jnp.exp(lf * (pos + 1.0))
        tab_scr[3, :, hs] = jnp.exp(lb * (C - pos))
        tab_scr[4, :, hs] = jnp.exp(lf * C)
        tab_scr[5, :, hs] = jnp.exp(lb * C)

    def rows(c):
        return pl.ds(pl.multiple_of(c * C, C), C)

    def part(p):
        return slice(c0 + p * WIDTH_RET, c0 + (p + 1) * WIDTH_RET)

    def head_cols(p, h):
        return slice(c0 + p * WIDTH_RET + h * d, c0 + p * WIDTH_RET + (h + 1) * d)

    def upd(c, carry):
        k = z_ref[rows(c), part(1)] * SCALE
        kf_t = (k * tab_scr[0]).T
        kb_t = (k * tab_scr[1]).T
        for h in range(N_HEADS_RET):
            v = z_ref[rows(c), head_cols(2, h)]
            u_scr[0, c, h] = _mm(kf_t[h * d:(h + 1) * d, :], v)
            u_scr[1, c, h] = _mm(kb_t[h * d:(h + 1) * d, :], v)
        return carry

    lax.fori_loop(0, nc, upd, 0)

    for h in range(N_HEADS_RET):
        cf = tab_scr[4, 0:d, h * d:(h + 1) * d]
        cb = tab_scr[5, 0:d, h * d:(h + 1) * d]
        sf0 = jnp.zeros((d, d), F32) if s0_ref is None else s0_ref[0, 0, 0, h]
        sb0 = jnp.zeros((d, d), F32) if s0_ref is None else s0_ref[0, 0, 1, h]

        def scan_f(c, s):
            s_scr[0, c, h] = s
            return cf * s + u_scr[0, c, h]

        def scan_b(t, s):
            c = nc - 1 - t
            s_scr[1, c, h] = s
            return cb * s + u_scr[1, c, h]

        sf = lax.fori_loop(0, nc, scan_f, sf0)
        sb = lax.fori_loop(0, nc, scan_b, sb0)
        if st_ref is not None:
            st_ref[0, 0, h] = sf
            st_ref[0, 1, h] = sb

    def out(c, carry):
        for h in range(N_HEADS_RET):
            hs = slice(h * d, (h + 1) * d)
            q = z_ref[rows(c), head_cols(0, h)]
            k = z_ref[rows(c), head_cols(1, h)] * SCALE
            v = z_ref[rows(c), head_cols(2, h)]
            gate = z_ref[rows(c), head_cols(3, h)]
            a = _mm_nt(q, k) * d_scr[h]
            o = (_mm(a, v) + _mm(q, s_scr[0, c, h]) * tab_scr[2, :, hs]
                 + _mm(q, s_scr[1, c, h]) * tab_scr[3, :, hs])
            o = o * lax.rsqrt(jnp.mean(o * o, axis=-1, keepdims=True) + EPS)
            o = o * ng_ref[:, hs]
            o_ref[rows(c), hs] = (o * _silu(gate)).astype(o_ref.dtype)
        return carry

    lax.fori_loop(0, nc, out, 0)


def _ret_scratch(n_tok):
    nc = n_tok // RET_CHUNK
    return [pltpu.VMEM((6, RET_CHUNK, WIDTH_RET), F32),
            pltpu.VMEM((N_HEADS_RET, RET_CHUNK, RET_CHUNK), F32),
            pltpu.VMEM((2, nc, N_HEADS_RET, HEAD_DIM, HEAD_DIM), F32),
            pltpu.VMEM((2, nc, N_HEADS_RET, HEAD_DIM, HEAD_DIM), F32)]


def _ctx_attn_body(sink_ref, decay_ref, z_ref, ng_ref,
                   ona_ref, oswa_ref, oret_ref, kna_ref, vna_ref, kswa_ref, vswa_ref, st_ref,
                   tab_scr, d_scr, u_scr, s_scr, *, seq):
    d = HEAD_DIM
    for h in range(N_HEADS_NA):
        q = z_ref[:, OFF_NA + h * d:OFF_NA + (h + 1) * d]
        k = z_ref[:, OFF_NA + WIDTH_NA + h * d:OFF_NA + WIDTH_NA + (h + 1) * d]
        v = z_ref[:, OFF_NA + 2 * WIDTH_NA + h * d:OFF_NA + 2 * WIDTH_NA + (h + 1) * d]
        kna_ref[0, h] = k
        vna_ref[0, h] = v
        s = _mm_nt(q, k) * SCALE
        e = jnp.exp(s - jnp.max(s, axis=-1, keepdims=True))
        o = _mm(e, v) / jnp.sum(e, axis=-1, keepdims=True)
        ona_ref[:, h * d:(h + 1) * d] = o.astype(ona_ref.dtype)
    group = N_HEADS_SWA // N_KV_SWA
    for g in range(N_KV_SWA):
        lo = OFF_SWA + WIDTH_SWA
        k = z_ref[:, lo + g * d:lo + (g + 1) * d]
        v = z_ref[:, lo + WIDTH_KV_SWA + g * d:lo + WIDTH_KV_SWA + (g + 1) * d]
        kswa_ref[0, g] = k
        vswa_ref[0, g] = v
        for jq in range(group):
            h = g * group + jq
            q = z_ref[:, OFF_SWA + h * d:OFF_SWA + (h + 1) * d]
            s = _mm_nt(q, k) * SCALE
            m = jnp.maximum(jnp.max(s, axis=-1, keepdims=True), sink_ref[h])
            e = jnp.exp(s - m)
            den = jnp.sum(e, axis=-1, keepdims=True) + jnp.exp(sink_ref[h] - m)
            oswa_ref[:, h * d:(h + 1) * d] = (_mm(e, v) / den).astype(oswa_ref.dtype)
    _retention(z_ref, OFF_RET, decay_ref, ng_ref, None, oret_ref, st_ref,
               tab_scr, d_scr, u_scr, s_scr, seq)


def _ctx_mixers(z, swa_sink, ret_decay, ret_norm_g, batch, seq):
    n = batch * seq
    width = OFF_GATES
    smem = pl.BlockSpec(memory_space=pltpu.SMEM)
    kv = lambda heads: pl.BlockSpec((1, heads, seq, HEAD_DIM), lambda b: (b, 0, 0, 0))
    return pl.pallas_call(
        functools.partial(_ctx_attn_body, seq=seq),
        grid=(batch,),
        in_specs=[smem, smem,
                  pl.BlockSpec((seq, width), lambda b: (b, 0)),
                  pl.BlockSpec((1, WIDTH_RET), lambda b: (0, 0))],
        out_specs=[pl.BlockSpec((seq, WIDTH_NA), lambda b: (b, 0)),
                   pl.BlockSpec((seq, WIDTH_SWA), lambda b: (b, 0)),
                   pl.BlockSpec((seq, WIDTH_RET), lambda b: (b, 0)),
                   kv(N_HEADS_NA), kv(N_HEADS_NA), kv(N_KV_SWA), kv(N_KV_SWA),
                   pl.BlockSpec((1, 2, N_HEADS_RET, HEAD_DIM, HEAD_DIM), lambda b: (b, 0, 0, 0, 0))],
        out_shape=[jax.ShapeDtypeStruct((n, WIDTH_NA), BF16),
                   jax.ShapeDtypeStruct((n, WIDTH_SWA), BF16),
                   jax.ShapeDtypeStruct((n, WIDTH_RET), BF16),
                   jax.ShapeDtypeStruct((batch, N_HEADS_NA, seq, HEAD_DIM), F32),
                   jax.ShapeDtypeStruct((batch, N_HEADS_NA, seq, HEAD_DIM), F32),
                   jax.ShapeDtypeStruct((batch, N_KV_SWA, seq, HEAD_DIM), F32),
                   jax.ShapeDtypeStruct((batch, N_KV_SWA, seq, HEAD_DIM), F32),
                   jax.ShapeDtypeStruct((batch, 2, N_HEADS_RET, HEAD_DIM, HEAD_DIM), F32)],
        scratch_shapes=_ret_scratch(seq),
        compiler_params=_params(("arbitrary",), 32),
        name="ctx_mixers",
    )(swa_sink, ret_decay, z, ret_norm_g.reshape(1, WIDTH_RET))


def _na_bias_table(rpb, rows):
    assert rows >= 2 * WIN_ROWS
    half = WIN_ROWS // 2
    class_rows = list(range(half)) + [half] + list(range(rows - half + 1, rows))
    r = np.asarray(class_rows)
    row_start = np.clip(r - half, 0, rows - WIN_ROWS)
    row_idx = row_start[:, None] + np.arange(WIN_ROWS)[None, :] - r[:, None] + WIN_ROWS - 1
    cc = np.arange(GRID_W)
    col_start = np.clip(cc - WIN_COLS // 2, 0, GRID_W - WIN_COLS)
    valid = (cc[None, :] >= col_start[:, None]) & (cc[None, :] < col_start[:, None] + WIN_COLS)
    col_idx = np.clip(cc[None, :] - cc[:, None] + WIN_COLS - 1, 0, 2 * WIN_COLS - 2)
    tab = rpb[:, row_idx[:, :, None, None], col_idx[None, None, :, :]]
    tab = jnp.where(valid[None, None, None], tab.astype(F32), MASKED)
    n_cls = len(class_rows)
    return tab.transpose(1, 0, 3, 2, 4).reshape(n_cls, rpb.shape[0], GRID_W, WIN_ROWS * GRID_W)


def _lat_na_body(z_ref, kc_ref, vc_ref, bias_ref, o_ref, *, rows):
    d = HEAD_DIM
    r = pl.program_id(1)
    rs = jnp.clip(r - WIN_ROWS // 2, 0, rows - WIN_ROWS)
    qrows = pl.ds(pl.multiple_of(r * GRID_W, GRID_W), GRID_W)
    wrows = pl.ds(pl.multiple_of(rs * GRID_W, GRID_W), WIN_ROWS * GRID_W)
    for h in range(N_HEADS_NA):
        q = z_ref[qrows, h * d:(h + 1) * d]
        kl = z_ref[wrows, WIDTH_NA + h * d:WIDTH_NA + (h + 1) * d]
        vl = z_ref[wrows, 2 * WIDTH_NA + h * d:2 * WIDTH_NA + (h + 1) * d]
        s_ctx = _mm_nt(q, kc_ref[0, 0, h]) * SCALE
        s_loc = _mm_nt(q, kl) * SCALE + bias_ref[0, h]
        m = jnp.maximum(jnp.max(s_ctx, axis=-1, keepdims=True), jnp.max(s_loc, axis=-1, keepdims=True))
        e_ctx = jnp.exp(s_ctx - m)
        e_loc = jnp.exp(s_loc - m)
        den = jnp.sum(e_ctx, axis=-1, keepdims=True) + jnp.sum(e_loc, axis=-1, keepdims=True)
        o = (_mm(e_ctx, vc_ref[0, 0, h]) + _mm(e_loc, vl)) / den
        o_ref[:, h * d:(h + 1) * d] = o.astype(o_ref.dtype)


def _lat_na(z, cache_k, cache_v, bias_tab, layer, batch, seq):
    rows = seq // GRID_W
    half = WIN_ROWS // 2
    past = cache_k.shape[3]

    def row_class(r):
        return jnp.where(r < half, r, jnp.where(r > rows - half, r - (rows - 2 * half), half))

    cache = pl.BlockSpec((1, 1, N_HEADS_NA, past, HEAD_DIM), lambda b, r: (b, layer, 0, 0, 0))
    return pl.pallas_call(
        functools.partial(_lat_na_body, rows=rows),
        grid=(batch, rows),
        in_specs=[pl.BlockSpec((seq, 3 * WIDTH_NA), lambda b, r: (b, 0)),
                  cache, cache,
                  pl.BlockSpec((1, N_HEADS_NA, GRID_W, WIN_ROWS * GRID_W),
                               lambda b, r: (row_class(r), 0, 0, 0))],
        out_specs=pl.BlockSpec((GRID_W, WIDTH_NA), lambda b, r: (b * rows + r, 0)),
        out_shape=jax.ShapeDtypeStruct((batch * seq, WIDTH_NA), BF16),
        compiler_params=_params(("arbitrary", "arbitrary"), 48),
        name="lat_na",
    )(z, cache_k, cache_v, bias_tab)


def _rope_tables(seq):
    t = jnp.arange(seq)
    n_freq = HEAD_DIM // 4
    inv = ROPE_BASE ** (-jnp.arange(n_freq, dtype=F32) / n_freq)
    ang = jnp.concatenate([(t // GRID_W).astype(F32)[:, None] * inv,
                           (t % GRID_W).astype(F32)[:, None] * inv], axis=-1)
    cos = jnp.repeat(jnp.cos(ang), 2, axis=-1)
    sign = jnp.tile(jnp.asarray([-1.0, 1.0], F32), HEAD_DIM // 2)
    sin = jnp.repeat(jnp.sin(ang), 2, axis=-1) * sign
    return jnp.tile(cos, (1, N_HEADS_SWA)), jnp.tile(sin, (1, N_HEADS_SWA))


def _rope(x, cos, sin_signed):
    width = x.shape[-1]
    lane = lax.broadcasted_iota(jnp.int32, x.shape, x.ndim - 1)
    partner = jnp.where((lane & 1) == 0, pltpu.roll(x, width - 1, x.ndim - 1), pltpu.roll(x, 1, x.ndim - 1))
    return x * cos + partner * sin_signed


def _lat_swa_body(sink_ref, z_ref, kc_ref, vc_ref, cos_ref, sin_ref, o_ref, kr_scr, *, seq):
    d = HEAD_DIM
    blk = SWA_BLOCK
    n = pl.program_id(1)

    @pl.when(n == 0)
    def _():
        kr_scr[...] = _rope(z_ref[:, WIDTH_SWA:WIDTH_SWA + WIDTH_KV_SWA],
                            cos_ref[:, :WIDTH_KV_SWA], sin_ref[:, :WIDTH_KV_SWA])

    qrows = pl.ds(pl.multiple_of(n * blk, blk), blk)
    kstart = jnp.clip((n - 1) * blk, 0, seq - 3 * blk)
    krows = pl.ds(pl.multiple_of(kstart, blk), 3 * blk)
    q_all = _rope(z_ref[qrows, 0:WIDTH_SWA], cos_ref[qrows, :], sin_ref[qrows, :])
    qpos = n * blk + lax.broadcasted_iota(jnp.int32, (blk, 3 * blk), 0)
    kpos = kstart + lax.broadcasted_iota(jnp.int32, (blk, 3 * blk), 1)
    valid = jnp.abs(qpos - kpos) <= SWA_RADIUS
    group = N_HEADS_SWA // N_KV_SWA
    for g in range(N_KV_SWA):
        kl = kr_scr[krows, g * d:(g + 1) * d]
        vl = z_ref[krows, WIDTH_SWA + WIDTH_KV_SWA + g * d:WIDTH_SWA + WIDTH_KV_SWA + (g + 1) * d]
        for jq in range(group):
            h = g * group + jq
            q = q_all[:, h * d:(h + 1) * d]
            s_loc = jnp.where(valid, _mm_nt(q, kl) * SCALE, MASKED)
            s_ctx = _mm_nt(q, kc_ref[0, 0, g]) * SCALE
            m = jnp.maximum(jnp.maximum(jnp.max(s_loc, axis=-1, keepdims=True),
                                        jnp.max(s_ctx, axis=-1, keepdims=True)), sink_ref[h])
            e_loc = jnp.exp(s_loc - m)
            e_ctx = jnp.exp(s_ctx - m)
            den = (jnp.sum(e_loc, axis=-1, keepdims=True) + jnp.sum(e_ctx, axis=-1, keepdims=True)
                   + jnp.exp(sink_ref[h] - m))
            o = (_mm(e_ctx, vc_ref[0, 0, g]) + _mm(e_loc, vl)) / den
            o_ref[:, h * d:(h + 1) * d] = o.astype(o_ref.dtype)


def _lat_swa(z, cache_k, cache_v, swa_sink, cos, sin, layer, batch, seq):
    nb = seq // SWA_BLOCK
    past = cache_k.shape[3]
    width = WIDTH_SWA + 2 * WIDTH_KV_SWA
    assert OFF_SWA % width == 0 and seq >= 3 * SWA_BLOCK
    cache = pl.BlockSpec((1, 1, N_KV_SWA, past, HEAD_DIM), lambda b, n: (b, layer, 0, 0, 0))
    table = pl.BlockSpec((seq, WIDTH_SWA), lambda b, n: (0, 0))
    return pl.pallas_call(
        functools.partial(_lat_swa_body, seq=seq),
        grid=(batch, nb),
        in_specs=[pl.BlockSpec(memory_space=pltpu.SMEM),
                  pl.BlockSpec((seq, width), lambda b, n: (b, OFF_SWA // width)),
                  cache, cache, table, table],
        out_specs=pl.BlockSpec((SWA_BLOCK, WIDTH_SWA), lambda b, n: (b * nb + n, 0)),
        out_shape=jax.ShapeDtypeStruct((batch * seq, WIDTH_SWA), BF16),
        scratch_shapes=[pltpu.VMEM((seq, WIDTH_KV_SWA), F32)],
        compiler_params=_params(("arbitrary", "arbitrary"), 32),
        name="lat_swa",
    )(swa_sink, z, cache_k, cache_v, cos, sin)


def _lat_ret_body(decay_ref, z_ref, ng_ref, s0_ref, o_ref, tab_scr, d_scr, u_scr, s_scr, *, seq):
    _retention(z_ref, 0, decay_ref, ng_ref, s0_ref, o_ref, None, tab_scr, d_scr, u_scr, s_scr, seq)


def _lat_ret(z, ret_decay, ret_norm_g, state, layer, batch, seq):
    width = 4 * WIDTH_RET
    assert OFF_RET % width == 0
    return pl.pallas_call(
        functools.partial(_lat_ret_body, seq=seq),
        grid=(batch,),
        in_specs=[pl.BlockSpec(memory_space=pltpu.SMEM),
                  pl.BlockSpec((seq, width), lambda b: (b, OFF_RET // width)),
                  pl.BlockSpec((1, WIDTH_RET), lambda b: (0, 0)),
                  pl.BlockSpec((1, 1, 2, N_HEADS_RET, HEAD_DIM, HEAD_DIM),
                               lambda b: (b, layer, 0, 0, 0, 0))],
        out_specs=pl.BlockSpec((seq, WIDTH_RET), lambda b: (b, 0)),
        out_shape=jax.ShapeDtypeStruct((batch * seq, WIDTH_RET), BF16),
        scratch_shapes=_ret_scratch(seq),
        compiler_params=_params(("arbitrary",), 40),
        name="lat_ret",
    )(ret_decay, z, ret_norm_g.reshape(1, WIDTH_RET), state)


def _merge_body(ona_ref, oswa_ref, oret_ref, zg_ref, x_ref, mod_ref, g2_ref,
                wna_ref, wswa_ref, wret_ref, wout_ref, rw_ref, rb_ref,
                x1_ref, h2_ref, comb_ref, wna_scr, wswa_scr, wret_scr, wout_scr):
    @pl.when(pl.program_id(0) == 0)
    def _():
        wna_scr[...] = wna_ref[0].astype(BF16)
        wswa_scr[...] = wswa_ref[0].astype(BF16)
        wret_scr[...] = wret_ref[0].astype(BF16)
        wout_scr[...] = wout_ref[0].astype(BF16)

    D = D_MODEL
    m = mod_ref[0]
    z = (_sigmoid(zg_ref[:, 0:D]) * jnp.dot(ona_ref[...], wna_scr[...], preferred_element_type=F32)
         + _sigmoid(zg_ref[:, D:2 * D]) * jnp.dot(oswa_ref[...], wswa_scr[...], preferred_element_type=F32)
         + _sigmoid(zg_ref[:, 2 * D:3 * D]) * jnp.dot(oret_ref[...], wret_scr[...], preferred_element_type=F32))
    y = jnp.dot(z.astype(BF16), wout_scr[...], preferred_element_type=F32)
    x1 = x_ref[...] + m[:, 2 * D:3 * D] * y
    x1_ref[...] = x1
    h2 = x1 * lax.rsqrt(jnp.mean(x1 * x1, axis=-1, keepdims=True) + EPS) * g2_ref[0]
    h2 = h2 * (1.0 + m[:, 4 * D:5 * D]) + m[:, 3 * D:4 * D]
    h2_ref[...] = h2.astype(BF16)

    logit = jnp.dot(h2, rw_ref[0], preferred_element_type=F32, precision=lax.Precision.HIGHEST) + rb_ref[0]
    lane = lax.broadcasted_iota(jnp.int32, logit.shape, 1).astype(F32)
    neg = jnp.float32(-jnp.inf)
    big = jnp.float32(ROUTER_LANES)
    is_group = lane < N_GROUPS
    is_expert = (lane >= N_GROUPS) & (lane < N_GROUPS + N_EXPERTS)
    gl = jnp.where(is_group, logit, neg)
    gmax = jnp.max(gl, axis=-1, keepdims=True)
    gsel = jnp.min(jnp.where(gl == gmax, lane, big), axis=-1, keepdims=True)
    p_group = 1.0 / jnp.sum(jnp.where(is_group, jnp.exp(gl - gmax), 0.0), axis=-1, keepdims=True)
    eidx = lane - N_GROUPS
    in_group = is_expert & (jnp.floor(eidx / EXPERTS_PER_GROUP) == gsel)
    el = jnp.where(in_group, logit, neg)
    top1 = jnp.max(el, axis=-1, keepdims=True)
    sel1 = jnp.min(jnp.where(el == top1, lane, big), axis=-1, keepdims=True)
    el2 = jnp.where(lane == sel1, neg, el)
    top2 = jnp.max(el2, axis=-1, keepdims=True)
    sel2 = jnp.min(jnp.where(el2 == top2, lane, big), axis=-1, keepdims=True)
    e2 = jnp.exp(top2 - top1)
    w1 = p_group / (1.0 + e2)
    w2 = p_group * e2 / (1.0 + e2)
    comb = jnp.where(lane == sel1, w1, 0.0) + jnp.where(lane == sel2, w2, 0.0)
    comb_ref[...] = comb[:, N_GROUPS:N_GROUPS + N_EXPERTS]


def _merge(o_na, o_swa, o_ret, z, x, mod_rows, norm2_g, w_br_na, w_br_swa, w_br_ret, w_out,
           router_w, router_b, layer):
    n, d = x.shape
    n_cond = mod_rows.shape[0]
    per_cond = n // n_cond
    tm = 512
    gate_width = 3 * d
    assert OFF_GATES % gate_width == 0
    row = lambda w: pl.BlockSpec((tm, w), lambda i: (i, 0))
    weight = lambda k: pl.BlockSpec((1, k, d), lambda i: (layer, 0, 0))
    return pl.pallas_call(
        _merge_body,
        grid=(n // tm,),
        in_specs=[row(WIDTH_NA), row(WIDTH_SWA), row(WIDTH_RET),
                  pl.BlockSpec((tm, gate_width), lambda i: (i, OFF_GATES // gate_width)),
                  row(d),
                  pl.BlockSpec((1, 1, 6 * d), lambda i: (i * tm // per_cond, 0, 0)),
                  pl.BlockSpec((1, 1, d), lambda i: (layer, 0, 0)),
                  weight(WIDTH_NA), weight(WIDTH_SWA), weight(WIDTH_RET), weight(d),
                  pl.BlockSpec((1, d, ROUTER_LANES), lambda i: (layer, 0, 0)),
                  pl.BlockSpec((1, 1, ROUTER_LANES), lambda i: (layer, 0, 0))],
        out_specs=[row(d), row(d), row(N_EXPERTS)],
        out_shape=[jax.ShapeDtypeStruct((n, d), F32),
                   jax.ShapeDtypeStruct((n, d), BF16),
                   jax.ShapeDtypeStruct((n, N_EXPERTS), F32)],
        scratch_shapes=[pltpu.VMEM((WIDTH_NA, d), BF16), pltpu.VMEM((WIDTH_SWA, d), BF16),
                        pltpu.VMEM((WIDTH_RET, d), BF16), pltpu.VMEM((d, d), BF16)],
        compiler_params=_params(("arbitrary",), 56),
        name="merge_router",
    )(o_na, o_swa, o_ret, z, x, mod_rows, norm2_g.reshape(norm2_g.shape[0], 1, d),
      w_br_na, w_br_swa, w_br_ret, w_out, router_w, router_b)


def _moe_body(h_ref, comb_ref, x1_ref, mod_ref, wg_ref, wu_ref, wd_ref, fg_ref, o_ref, *, final):
    e = pl.program_id(1)
    h = h_ref[...]
    hg = jnp.dot(h, wg_ref[0, 0].astype(BF16), preferred_element_type=F32)
    hu = jnp.dot(h, wu_ref[0, 0].astype(BF16), preferred_element_type=F32)
    comb = comb_ref[...]
    lane = lax.broadcasted_iota(jnp.int32, comb.shape, 1)
    c = jnp.sum(jnp.where(lane == e, comb, 0.0), axis=-1, keepdims=True)
    a = _silu(hg) * hu * c
    y = jnp.dot(a.astype(BF16), wd_ref[0, 0].astype(BF16), preferred_element_type=F32)

    @pl.when(e == 0)
    def _():
        o_ref[...] = y

    @pl.when(e > 0)
    def _():
        o_ref[...] += y

    @pl.when(e == N_EXPERTS - 1)
    def _():
        x2 = x1_ref[...] + mod_ref[0][:, 5 * D_MODEL:6 * D_MODEL] * o_ref[...]
        if final:
            x2 = x2 * lax.rsqrt(jnp.mean(x2 * x2, axis=-1, keepdims=True) + EPS) * fg_ref[...]
        o_ref[...] = x2


def _moe(h2, comb, x1, mod_rows, w_gate, w_up, w_down, final_g, layer, final):
    n, d = x1.shape
    n_cond = mod_rows.shape[0]
    per_cond = n // n_cond
    tm = 1024
    return pl.pallas_call(
        functools.partial(_moe_body, final=final),
        grid=(n // tm, N_EXPERTS),
        in_specs=[pl.BlockSpec((tm, d), lambda i, e: (i, 0)),
                  pl.BlockSpec((tm, N_EXPERTS), lambda i, e: (i, 0)),
                  pl.BlockSpec((tm, d), lambda i, e: (i, 0)),
                  pl.BlockSpec((1, 1, 6 * d), lambda i, e: (i * tm // per_cond, 0, 0)),
                  pl.BlockSpec((1, 1, d, D_EXPERT), lambda i, e: (layer, e, 0, 0)),
                  pl.BlockSpec((1, 1, d, D_EXPERT), lambda i, e: (layer, e, 0, 0)),
                  pl.BlockSpec((1, 1, D_EXPERT, d), lambda i, e: (layer, e, 0, 0)),
                  pl.BlockSpec((1, d), lambda i, e: (0, 0))],
        out_specs=pl.BlockSpec((tm, d), lambda i, e: (i, 0)),
        out_shape=jax.ShapeDtypeStruct((n, d), F32),
        compiler_params=_params(("arbitrary", "arbitrary"), 56),
        name="moe",
    )(h2, comb, x1, mod_rows, w_gate, w_up, w_down, final_g.reshape(1, d))


def kernel(x_prompt, x_sample, c, cache_na_k, cache_na_v, cache_swa_k, cache_swa_v, state_ret, c_ctx, ada_w, ada_b, norm1_g, norm2_g, w_in, na_rpb, swa_sink, ret_decay, ret_norm_g, w_br_na, w_br_swa, w_br_ret, w_out, router_group_w, router_group_b, router_expert_w, router_expert_b, w_gate, w_up, w_down, final_norm_g):
    batch, seq, d = x_prompt.shape
    dec_batch, dec_seq, _ = x_sample.shape
    depth = ada_w.shape[0]
    assert d == D_MODEL and dec_batch + 1 <= 8

    cond = jnp.concatenate([c_ctx[None], c, jnp.zeros((8 - 1 - dec_batch, d), F32)], axis=0)
    mod = _modulation(cond, ada_w, ada_b)
    router_w = jnp.concatenate([router_group_w, router_expert_w], axis=-1)
    router_w = jnp.pad(router_w, ((0, 0), (0, 0), (0, ROUTER_LANES - router_w.shape[-1])))
    router_b = jnp.concatenate([router_group_b, router_expert_b], axis=-1)
    router_b = jnp.pad(router_b, ((0, 0), (0, ROUTER_LANES - router_b.shape[-1])))[:, None, :]
    cos, sin = _rope_tables(dec_seq)

    xp = x_prompt.reshape(batch * seq, d)
    xs = x_sample.reshape(dec_batch * dec_seq, d)
    na_k, na_v, swa_k, swa_v, ret_s = [], [], [], [], []
    for l in range(depth):
        final = l == depth - 1
        mod_ctx = mod[l, 0:1][:, None, :]
        mod_lat = mod[l, 1:1 + dec_batch][:, None, :]
        merge_w = (norm2_g, w_br_na, w_br_swa, w_br_ret, w_out, router_w, router_b, l)
        moe_w = (w_gate, w_up, w_down, final_norm_g, l, final)

        z = _in_proj(xp, mod_ctx, norm1_g, w_in, l)
        o_na, o_swa, o_ret, ka, va, ksw, vsw, sr = _ctx_mixers(
            z, swa_sink[l], ret_decay[l], ret_norm_g[l], batch, seq)
        na_k.append(ka)
        na_v.append(va)
        swa_k.append(ksw)
        swa_v.append(vsw)
        ret_s.append(sr)
        x1, h2, comb = _merge(o_na, o_swa, o_ret, z, xp, mod_ctx, *merge_w)
        xp = _moe(h2, comb, x1, mod_ctx, *moe_w)

        z = _in_proj(xs, mod_lat, norm1_g, w_in, l)
        o_na = _lat_na(z, cache_na_k, cache_na_v, _na_bias_table(na_rpb[l], dec_seq // GRID_W),
                       l, dec_batch, dec_seq)
        o_swa = _lat_swa(z, cache_swa_k, cache_swa_v, swa_sink[l], cos, sin, l, dec_batch, dec_seq)
        o_ret = _lat_ret(z, ret_decay[l], ret_norm_g[l], state_ret, l, dec_batch, dec_seq)
        x1, h2, comb = _merge(o_na, o_swa, o_ret, z, xs, mod_lat, *merge_w)
        xs = _moe(h2, comb, x1, mod_lat, *moe_w)

    return (xp.reshape(batch, seq, d), xs.reshape(dec_batch, dec_seq, d),
            jnp.stack(na_k, axis=1), jnp.stack(na_v, axis=1),
            jnp.stack(swa_k, axis=1), jnp.stack(swa_v, axis=1), jnp.stack(ret_s, axis=1))
```

```python
import functools

import numpy as np
import jax
import jax.numpy as jnp
from jax import lax
from jax.experimental import pallas as pl
from jax.experimental.pallas import tpu as pltpu

F32 = jnp.float32
BF16 = jnp.bfloat16

D_MODEL = 1024
HEAD_DIM = 64
GRID_W = 64
N_HEADS_NA = 8
WIN_ROWS = 8
WIN_COLS = 16
N_HEADS_SWA = 4
N_KV_SWA = 2
SWA_RADIUS = 128
SWA_BLOCK = 128
N_HEADS_RET = 4
RET_CHUNK = 128
WIDTH_NA = N_HEADS_NA * HEAD_DIM
WIDTH_SWA = N_HEADS_SWA * HEAD_DIM
WIDTH_KV_SWA = N_KV_SWA * HEAD_DIM
WIDTH_RET = N_HEADS_RET * HEAD_DIM
IN_COLS = 3 * WIDTH_NA + WIDTH_SWA + 2 * WIDTH_KV_SWA + 4 * WIDTH_RET + 3 * D_MODEL
OFF_NA = 0
OFF_SWA = 3 * WIDTH_NA
OFF_RET = OFF_SWA + WIDTH_SWA + 2 * WIDTH_KV_SWA
OFF_GATES = OFF_RET + 4 * WIDTH_RET
N_GROUPS = 4
EXPERTS_PER_GROUP = 4
N_EXPERTS = N_GROUPS * EXPERTS_PER_GROUP
D_EXPERT = D_MODEL // 4
ROPE_BASE = 10000.0
EPS = 1e-6
SCALE = HEAD_DIM ** -0.5
MASKED = -1e30
ROUTER_LANES = 128

MIB = 1024 * 1024


def _mm(a, b):
    return jnp.dot(a.astype(BF16), b.astype(BF16), preferred_element_type=F32)


def _mm_nt(a, b):
    return lax.dot_general(a.astype(BF16), b.astype(BF16), (((1,), (1,)), ((), ())),
                           preferred_element_type=F32)


def _sigmoid(x):
    return 1.0 / (1.0 + jnp.exp(-x))


def _silu(x):
    return x * _sigmoid(x)


def _params(semantics, vmem_mib):
    return pltpu.CompilerParams(dimension_semantics=semantics, vmem_limit_bytes=vmem_mib * MIB)


def _mod_body(cond_ref, w_ref, b_ref, o_ref):
    o_ref[0] = _mm(_silu(cond_ref[...]), w_ref[0]) + b_ref[0]


def _modulation(cond, ada_w, ada_b):
    depth, d, n = ada_w.shape
    tn = 1024
    return pl.pallas_call(
        _mod_body,
        grid=(depth, n // tn),
        in_specs=[pl.BlockSpec((8, d), lambda l, j: (0, 0)),
                  pl.BlockSpec((1, d, tn), lambda l, j: (l, 0, j)),
                  pl.BlockSpec((1, 1, tn), lambda l, j: (l, 0, j))],
        out_specs=pl.BlockSpec((1, 8, tn), lambda l, j: (l, 0, j)),
        out_shape=jax.ShapeDtypeStruct((depth, 8, n), F32),
        compiler_params=_params(("arbitrary", "arbitrary"), 32),
        name="modulation",
    )(cond, ada_w, ada_b.reshape(depth, 1, n))


def _inproj_body(x_ref, mod_ref, g_ref, w_ref, o_ref, h_scr, w_scr):
    i = pl.program_id(0)
    j = pl.program_id(1)

    @pl.when(i == 0)
    def _():
        w_scr[j] = w_ref[0].astype(BF16)

    @pl.when(j == 0)
    def _():
        x = x_ref[...]
        m = mod_ref[0]
        y = x * lax.rsqrt(jnp.mean(x * x, axis=-1, keepdims=True) + EPS) * g_ref[0]
        h_scr[...] = (y * (1.0 + m[:, D_MODEL:2 * D_MODEL]) + m[:, 0:D_MODEL]).astype(BF16)

    o_ref[...] = jnp.dot(h_scr[...], w_scr[j], preferred_element_type=F32)


def _in_proj(x, mod_rows, norm_g, w_in, layer):
    n, d = x.shape
    n_cond = mod_rows.shape[0]
    tm, tn = 1024, 1024
    nj = IN_COLS // tn
    per_cond = n // n_cond
    return pl.pallas_call(
        _inproj_body,
        grid=(n // tm, nj),
        in_specs=[pl.BlockSpec((tm, d), lambda i, j: (i, 0)),
                  pl.BlockSpec((1, 1, 6 * d), lambda i, j: (i * tm // per_cond, 0, 0)),
                  pl.BlockSpec((1, 1, d), lambda i, j: (layer, 0, 0)),
                  pl.BlockSpec((1, d, tn), lambda i, j: (layer, 0, jnp.where(i == 0, j, nj - 1)))],
        out_specs=pl.BlockSpec((tm, tn), lambda i, j: (i, j)),
        out_shape=jax.ShapeDtypeStruct((n, IN_COLS), F32),
        scratch_shapes=[pltpu.VMEM((tm, d), BF16), pltpu.VMEM((nj, d, tn), BF16)],
        compiler_params=_params(("arbitrary", "arbitrary"), 48),
        name="in_proj",
    )(x, mod_rows, norm_g.reshape(norm_g.shape[0], 1, d), w_in)


def _log_sigmoid(x):
    return jnp.minimum(x, 0.0) - jnp.log1p(jnp.exp(-jnp.abs(x)))


def _retention(z_ref, c0, decay_ref, ng_ref, s0_ref, o_ref, st_ref, tab_scr, d_scr, u_scr, s_scr, n_tok):
    C = RET_CHUNK
    nc = n_tok // C
    d = HEAD_DIM
    pos = lax.broadcasted_iota(jnp.int32, (C, d), 0).astype(F32)
    ri = lax.broadcasted_iota(jnp.int32, (C, C), 0).astype(F32)
    ci = lax.broadcasted_iota(jnp.int32, (C, C), 1).astype(F32)
    diff = ri - ci
    for h in range(N_HEADS_RET):
        lgf = _log_sigmoid(jnp.full((C, C), decay_ref[0, h], F32))
        lgb = _log_sigmoid(jnp.full((C, C), decay_ref[1, h], F32))
        d_scr[h] = (jnp.where(diff >= 0, jnp.exp(lgf * jnp.maximum(diff, 0.0)), 0.0)
                    + jnp.where(diff <= 0, jnp.exp(lgb * jnp.maximum(-diff, 0.0)), 0.0))
        lf = _log_sigmoid(jnp.full((C, d), decay_ref[0, h], F32))
        lb = _log_sigmoid(jnp.full((C, d), decay_ref[1, h], F32))
        hs = slice(h * d, (h + 1) * d)
        tab_scr[0, :, hs] = jnp.exp(lf * (C - 1.0 - pos))
        tab_scr[1, :, hs] = jnp.exp(lb * pos)
        tab_scr[2, :, hs] = jnp.exp(lf * (pos + 1.0))
        tab_scr[3, :, hs] = jnp.exp(lb * (C - pos))
        tab_scr[4, :, hs] = jnp.exp(lf * C)
        tab_scr[5, :, hs] = jnp.exp(lb * C)

    def rows(c):
        return pl.ds(pl.multiple_of(c * C, C), C)

    def part(p):
        return slice(c0 + p * WIDTH_RET, c0 + (p + 1) * WIDTH_RET)

    def head_cols(p, h):
        return slice(c0 + p * WIDTH_RET + h * d, c0 + p * WIDTH_RET + (h + 1) * d)

    def upd(c, carry):
        k = z_ref[rows(c), part(1)] * SCALE
        kf_t = (k * tab_scr[0]).T
        kb_t = (k * tab_scr[1]).T
        for h in range(N_HEADS_RET):
            v = z_ref[rows(c), head_cols(2, h)]
            u_scr[0, c, h] = _mm(kf_t[h * d:(h + 1) * d, :], v)
            u_scr[1, c, h] = _mm(kb_t[h * d:(h + 1) * d, :], v)
        return carry

    lax.fori_loop(0, nc, upd, 0)

    for h in range(N_HEADS_RET):
        cf = tab_scr[4, 0:d, h * d:(h + 1) * d]
        cb = tab_scr[5, 0:d, h * d:(h + 1) * d]
        sf0 = jnp.zeros((d, d), F32) if s0_ref is None else s0_ref[0, 0, 0, h]
        sb0 = jnp.zeros((d, d), F32) if s0_ref is None else s0_ref[0, 0, 1, h]

        def scan_f(c, s):
            s_scr[0, c, h] = s
            return cf * s + u_scr[0, c, h]

        def scan_b(t, s):
            c = nc - 1 - t
            s_scr[1, c, h] = s
            return cb * s + u_scr[1, c, h]

        sf = lax.fori_loop(0, nc, scan_f, sf0)
        sb = lax.fori_loop(0, nc, scan_b, sb0)
        if st_ref is not None:
            st_ref[0, 0, h] = sf
            st_ref[0, 1, h] = sb

    def out(c, carry):
        for h in range(N_HEADS_RET):
            hs = slice(h * d, (h + 1) * d)
            q = z_ref[rows(c), head_cols(0, h)]
            k = z_ref[rows(c), head_cols(1, h)] * SCALE
            v = z_ref[rows(c), head_cols(2, h)]
            gate = z_ref[rows(c), head_cols(3, h)]
            a = _mm_nt(q, k) * d_scr[h]
            o = (_mm(a, v) + _mm(q, s_scr[0, c, h]) * tab_scr[2, :, hs]
                 + _mm(q, s_scr[1, c, h]) * tab_scr[3, :, hs])
            o = o * lax.rsqrt(jnp.mean(o * o, axis=-1, keepdims=True) + EPS)
            o = o * ng_ref[:, hs]
            o_ref[rows(c), hs] = (o * _silu(gate)).astype(o_ref.dtype)
        return carry

    lax.fori_loop(0, nc, out, 0)


def _ret_scratch(n_tok):
    nc = n_tok // RET_CHUNK
    return [pltpu.VMEM((6, RET_CHUNK, WIDTH_RET), F32),
            pltpu.VMEM((N_HEADS_RET, RET_CHUNK, RET_CHUNK), F32),
            pltpu.VMEM((2, nc, N_HEADS_RET, HEAD_DIM, HEAD_DIM), F32),
            pltpu.VMEM((2, nc, N_HEADS_RET, HEAD_DIM, HEAD_DIM), F32)]


def _ctx_attn_body(sink_ref, decay_ref, z_ref, ng_ref,
                   ona_ref, oswa_ref, oret_ref, kna_ref, vna_ref, kswa_ref, vswa_ref, st_ref,
                   tab_scr, d_scr, u_scr, s_scr, *, seq):
    d = HEAD_DIM
    for h in range(N_HEADS_NA):
        q = z_ref[:, OFF_NA + h * d:OFF_NA + (h + 1) * d]
        k = z_ref[:, OFF_NA + WIDTH_NA + h * d:OFF_NA + WIDTH_NA + (h + 1) * d]
        v = z_ref[:, OFF_NA + 2 * WIDTH_NA + h * d:OFF_NA + 2 * WIDTH_NA + (h + 1) * d]
        kna_ref[0, h] = k
        vna_ref[0, h] = v
        s = _mm_nt(q, k) * SCALE
        e = jnp.exp(s - jnp.max(s, axis=-1, keepdims=True))
        o = _mm(e, v) / jnp.sum(e, axis=-1, keepdims=True)
        ona_ref[:, h * d:(h + 1) * d] = o.astype(ona_ref.dtype)
    group = N_HEADS_SWA // N_KV_SWA
    for g in range(N_KV_SWA):
        lo = OFF_SWA + WIDTH_SWA
        k = z_ref[:, lo + g * d:lo + (g + 1) * d]
        v = z_ref[:, lo + WIDTH_KV_SWA + g * d:lo + WIDTH_KV_SWA + (g + 1) * d]
        kswa_ref[0, g] = k
        vswa_ref[0, g] = v
        for jq in range(group):
            h = g * group + jq
            q = z_ref[:, OFF_SWA + h * d:OFF_SWA + (h + 1) * d]
            s = _mm_nt(q, k) * SCALE
            m = jnp.maximum(jnp.max(s, axis=-1, keepdims=True), sink_ref[h])
            e = jnp.exp(s - m)
            den = jnp.sum(e, axis=-1, keepdims=True) + jnp.exp(sink_ref[h] - m)
            oswa_ref[:, h * d:(h + 1) * d] = (_mm(e, v) / den).astype(oswa_ref.dtype)
    _retention(z_ref, OFF_RET, decay_ref, ng_ref, None, oret_ref, st_ref,
               tab_scr, d_scr, u_scr, s_scr, seq)


def _ctx_mixers(z, swa_sink, ret_decay, ret_norm_g, batch, seq):
    n = batch * seq
    width = OFF_GATES
    smem = pl.BlockSpec(memory_space=pltpu.SMEM)
    kv = lambda heads: pl.BlockSpec((1, heads, seq, HEAD_DIM), lambda b: (b, 0, 0, 0))
    return pl.pallas_call(
        functools.partial(_ctx_attn_body, seq=seq),
        grid=(batch,),
        in_specs=[smem, smem,
                  pl.BlockSpec((seq, width), lambda b: (b, 0)),
                  pl.BlockSpec((1, WIDTH_RET), lambda b: (0, 0))],
        out_specs=[pl.BlockSpec((seq, WIDTH_NA), lambda b: (b, 0)),
                   pl.BlockSpec((seq, WIDTH_SWA), lambda b: (b, 0)),
                   pl.BlockSpec((seq, WIDTH_RET), lambda b: (b, 0)),
                   kv(N_HEADS_NA), kv(N_HEADS_NA), kv(N_KV_SWA), kv(N_KV_SWA),
                   pl.BlockSpec((1, 2, N_HEADS_RET, HEAD_DIM, HEAD_DIM), lambda b: (b, 0, 0, 0, 0))],
        out_shape=[jax.ShapeDtypeStruct((n, WIDTH_NA), BF16),
                   jax.ShapeDtypeStruct((n, WIDTH_SWA), BF16),
                   jax.ShapeDtypeStruct((n, WIDTH_RET), BF16),
                   jax.ShapeDtypeStruct((batch, N_HEADS_NA, seq, HEAD_DIM), F32),
                   jax.ShapeDtypeStruct((batch, N_HEADS_NA, seq, HEAD_DIM), F32),
                   jax.ShapeDtypeStruct((batch, N_KV_SWA, seq, HEAD_DIM), F32),
                   jax.ShapeDtypeStruct((batch, N_KV_SWA, seq, HEAD_DIM), F32),
                   jax.ShapeDtypeStruct((batch, 2, N_HEADS_RET, HEAD_DIM, HEAD_DIM), F32)],
        scratch_shapes=_ret_scratch(seq),
        compiler_params=_params(("arbitrary",), 32),
        name="ctx_mixers",
    )(swa_sink, ret_decay, z, ret_norm_g.reshape(1, WIDTH_RET))


def _na_bias_table(rpb, rows):
    assert rows >= 2 * WIN_ROWS
    n_heads, n_ro, n_co = rpb.shape
    half = WIN_ROWS // 2
    cc = np.arange(GRID_W)
    col_start = np.clip(cc - WIN_COLS // 2, 0, GRID_W - WIN_COLS)
    valid = (cc[None, :] >= col_start[:, None]) & (cc[None, :] < col_start[:, None] + WIN_COLS)
    span = 2 * GRID_W - 1
    lead = GRID_W - WIN_COLS
    ext = jnp.pad(rpb.astype(F32), ((0, 0), (0, 0), (lead, span + 1 - lead - n_co)))
    band = jnp.tile(ext, (1, 1, GRID_W))[..., :GRID_W * span].reshape(n_heads, n_ro, GRID_W, span)
    band = jnp.where(valid[None, None], band[..., GRID_W - 1:], MASKED)
    class_rows = list(range(half)) + [half] + list(range(rows - half + 1, rows))
    tabs = []
    for r in class_rows:
        row_start = min(max(r - half, 0), rows - WIN_ROWS)
        tabs.append(jnp.concatenate([band[:, row_start + j - r + WIN_ROWS - 1] for j in range(WIN_ROWS)],
                                    axis=-1))
    return jnp.stack(tabs, axis=0)


def _lat_na_body(z_ref, kc_ref, vc_ref, bias_ref, o_ref, *, rows):
    d = HEAD_DIM
    r = pl.program_id(1)
    rs = jnp.clip(r - WIN_ROWS // 2, 0, rows - WIN_ROWS)
    qrows = pl.ds(pl.multiple_of(r * GRID_W, GRID_W), GRID_W)
    wrows = pl.ds(pl.multiple_of(rs * GRID_W, GRID_W), WIN_ROWS * GRID_W)
    for h in range(N_HEADS_NA):
        q = z_ref[qrows, h * d:(h + 1) * d]
        kl = z_ref[wrows, WIDTH_NA + h * d:WIDTH_NA + (h + 1) * d]
        vl = z_ref[wrows, 2 * WIDTH_NA + h * d:2 * WIDTH_NA + (h + 1) * d]
        s_ctx = _mm_nt(q, kc_ref[0, 0, h]) * SCALE
        s_loc = _mm_nt(q, kl) * SCALE + bias_ref[0, h]
        m = jnp.maximum(jnp.max(s_ctx, axis=-1, keepdims=True), jnp.max(s_loc, axis=-1, keepdims=True))
        e_ctx = jnp.exp(s_ctx - m)
        e_loc = jnp.exp(s_loc - m)
        den = jnp.sum(e_ctx, axis=-1, keepdims=True) + jnp.sum(e_loc, axis=-1, keepdims=True)
        o = (_mm(e_ctx, vc_ref[0, 0, h]) + _mm(e_loc, vl)) / den
        o_ref[:, h * d:(h + 1) * d] = o.astype(o_ref.dtype)


def _lat_na(z, cache_k, cache_v, bias_tab, layer, batch, seq):
    rows = seq // GRID_W
    half = WIN_ROWS // 2
    past = cache_k.shape[3]

    def row_class(r):
        return jnp.where(r < half, r, jnp.where(r > rows - half, r - (rows - 2 * half), half))

    cache = pl.BlockSpec((1, 1, N_HEADS_NA, past, HEAD_DIM), lambda b, r: (b, layer, 0, 0, 0))
    return pl.pallas_call(
        functools.partial(_lat_na_body, rows=rows),
        grid=(batch, rows),
        in_specs=[pl.BlockSpec((seq, 3 * WIDTH_NA), lambda b, r: (b, 0)),
                  cache, cache,
                  pl.BlockSpec((1, N_HEADS_NA, GRID_W, WIN_ROWS * GRID_W),
                               lambda b, r: (row_class(r), 0, 0, 0))],
        out_specs=pl.BlockSpec((GRID_W, WIDTH_NA), lambda b, r: (b * rows + r, 0)),
        out_shape=jax.ShapeDtypeStruct((batch * seq, WIDTH_NA), BF16),
        compiler_params=_params(("arbitrary", "arbitrary"), 48),
        name="lat_na",
    )(z, cache_k, cache_v, bias_tab)


def _rope_tables(seq):
    t = jnp.arange(seq)
    n_freq = HEAD_DIM // 4
    inv = ROPE_BASE ** (-jnp.arange(n_freq, dtype=F32) / n_freq)
    ang = jnp.concatenate([(t // GRID_W).astype(F32)[:, None] * inv,
                           (t % GRID_W).astype(F32)[:, None] * inv], axis=-1)
    cos = jnp.repeat(jnp.cos(ang), 2, axis=-1)
    sign = jnp.tile(jnp.asarray([-1.0, 1.0], F32), HEAD_DIM // 2)
    sin = jnp.repeat(jnp.sin(ang), 2, axis=-1) * sign
    return jnp.tile(cos, (1, N_HEADS_SWA)), jnp.tile(sin, (1, N_HEADS_SWA))


def _rope(x, cos, sin_signed):
    width = x.shape[-1]
    lane = lax.broadcasted_iota(jnp.int32, x.shape, x.ndim - 1)
    partner = jnp.where((lane & 1) == 0, pltpu.roll(x, width - 1, x.ndim - 1), pltpu.roll(x, 1, x.ndim - 1))
    return x * cos + partner * sin_signed


def _lat_swa_body(sink_ref, z_ref, kc_ref, vc_ref, cos_ref, sin_ref, o_ref, kr_scr, *, seq):
    d = HEAD_DIM
    blk = SWA_BLOCK
    n = pl.program_id(1)

    @pl.when(n == 0)
    def _():
        kr_scr[...] = _rope(z_ref[:, WIDTH_SWA:WIDTH_SWA + WIDTH_KV_SWA],
                            cos_ref[:, :WIDTH_KV_SWA], sin_ref[:, :WIDTH_KV_SWA])

    qrows = pl.ds(pl.multiple_of(n * blk, blk), blk)
    kstart = jnp.clip((n - 1) * blk, 0, seq - 3 * blk)
    krows = pl.ds(pl.multiple_of(kstart, blk), 3 * blk)
    q_all = _rope(z_ref[qrows, 0:WIDTH_SWA], cos_ref[qrows, :], sin_ref[qrows, :])
    qpos = n * blk + lax.broadcasted_iota(jnp.int32, (blk, 3 * blk), 0)
    kpos = kstart + lax.broadcasted_iota(jnp.int32, (blk, 3 * blk), 1)
    valid = jnp.abs(qpos - kpos) <= SWA_RADIUS
    group = N_HEADS_SWA // N_KV_SWA
    for g in range(N_KV_SWA):
        kl = kr_scr[krows, g * d:(g + 1) * d]
        vl = z_ref[krows, WIDTH_SWA + WIDTH_KV_SWA + g * d:WIDTH_SWA + WIDTH_KV_SWA + (g + 1) * d]
        for jq in range(group):
            h = g * group + jq
            q = q_all[:, h * d:(h + 1) * d]
            s_loc = jnp.where(valid, _mm_nt(q, kl) * SCALE, MASKED)
            s_ctx = _mm_nt(q, kc_ref[0, 0, g]) * SCALE
            m = jnp.maximum(jnp.maximum(jnp.max(s_loc, axis=-1, keepdims=True),
                                        jnp.max(s_ctx, axis=-1, keepdims=True)), sink_ref[h])
            e_loc = jnp.exp(s_loc - m)
            e_ctx = jnp.exp(s_ctx - m)
            den = (jnp.sum(e_loc, axis=-1, keepdims=True) + jnp.sum(e_ctx, axis=-1, keepdims=True)
                   + jnp.exp(sink_ref[h] - m))
            o = (_mm(e_ctx, vc_ref[0, 0, g]) + _mm(e_loc, vl)) / den
            o_ref[:, h * d:(h + 1) * d] = o.astype(o_ref.dtype)


def _lat_swa(z, cache_k, cache_v, swa_sink, cos, sin, layer, batch, seq):
    nb = seq // SWA_BLOCK
    past = cache_k.shape[3]
    width = WIDTH_SWA + 2 * WIDTH_KV_SWA
    assert OFF_SWA % width == 0 and seq >= 3 * SWA_BLOCK
    cache = pl.BlockSpec((1, 1, N_KV_SWA, past, HEAD_DIM), lambda b, n: (b, layer, 0, 0, 0))
    table = pl.BlockSpec((seq, WIDTH_SWA), lambda b, n: (0, 0))
    return pl.pallas_call(
        functools.partial(_lat_swa_body, seq=seq),
        grid=(batch, nb),
        in_specs=[pl.BlockSpec(memory_space=pltpu.SMEM),
                  pl.BlockSpec((seq, width), lambda b, n: (b, OFF_SWA // width)),
                  cache, cache, table, table],
        out_specs=pl.BlockSpec((SWA_BLOCK, WIDTH_SWA), lambda b, n: (b * nb + n, 0)),
        out_shape=jax.ShapeDtypeStruct((batch * seq, WIDTH_SWA), BF16),
        scratch_shapes=[pltpu.VMEM((seq, WIDTH_KV_SWA), F32)],
        compiler_params=_params(("arbitrary", "arbitrary"), 32),
        name="lat_swa",
    )(swa_sink, z, cache_k, cache_v, cos, sin)


def _lat_ret_body(decay_ref, z_ref, ng_ref, s0_ref, o_ref, tab_scr, d_scr, u_scr, s_scr, *, seq):
    _retention(z_ref, 0, decay_ref, ng_ref, s0_ref, o_ref, None, tab_scr, d_scr, u_scr, s_scr, seq)


def _lat_ret(z, ret_decay, ret_norm_g, state, layer, batch, seq):
    width = 4 * WIDTH_RET
    assert OFF_RET % width == 0
    return pl.pallas_call(
        functools.partial(_lat_ret_body, seq=seq),
        grid=(batch,),
        in_specs=[pl.BlockSpec(memory_space=pltpu.SMEM),
                  pl.BlockSpec((seq, width), lambda b: (b, OFF_RET // width)),
                  pl.BlockSpec((1, WIDTH_RET), lambda b: (0, 0)),
                  pl.BlockSpec((1, 1, 2, N_HEADS_RET, HEAD_DIM, HEAD_DIM),
                               lambda b: (b, layer, 0, 0, 0, 0))],
        out_specs=pl.BlockSpec((seq, WIDTH_RET), lambda b: (b, 0)),
        out_shape=jax.ShapeDtypeStruct((batch * seq, WIDTH_RET), BF16),
        scratch_shapes=_ret_scratch(seq),
        compiler_params=_params(("arbitrary",), 40),
        name="lat_ret",
    )(ret_decay, z, ret_norm_g.reshape(1, WIDTH_RET), state)


def _merge_body(ona_ref, oswa_ref, oret_ref, zg_ref, x_ref, mod_ref, g2_ref,
                wna_ref, wswa_ref, wret_ref, wout_ref, rw_ref, rb_ref,
                x1_ref, h2_ref, comb_ref, wna_scr, wswa_scr, wret_scr, wout_scr, rw_scr):
    @pl.when(pl.program_id(0) == 0)
    def _():
        wna_scr[...] = wna_ref[0].astype(BF16)
        wswa_scr[...] = wswa_ref[0].astype(BF16)
        wret_scr[...] = wret_ref[0].astype(BF16)
        wout_scr[...] = wout_ref[0].astype(BF16)
        rw = rw_ref[0]
        rw_hi = rw.astype(BF16)
        rw_scr[0] = rw_hi
        rw_scr[1] = (rw - rw_hi.astype(F32)).astype(BF16)

    D = D_MODEL
    m = mod_ref[0]
    z = (_sigmoid(zg_ref[:, 0:D]) * jnp.dot(ona_ref[...], wna_scr[...], preferred_element_type=F32)
         + _sigmoid(zg_ref[:, D:2 * D]) * jnp.dot(oswa_ref[...], wswa_scr[...], preferred_element_type=F32)
         + _sigmoid(zg_ref[:, 2 * D:3 * D]) * jnp.dot(oret_ref[...], wret_scr[...], preferred_element_type=F32))
    y = jnp.dot(z.astype(BF16), wout_scr[...], preferred_element_type=F32)
    x1 = x_ref[...] + m[:, 2 * D:3 * D] * y
    x1_ref[...] = x1
    h2 = x1 * lax.rsqrt(jnp.mean(x1 * x1, axis=-1, keepdims=True) + EPS) * g2_ref[0]
    h2 = h2 * (1.0 + m[:, 4 * D:5 * D]) + m[:, 3 * D:4 * D]
    h2_ref[...] = h2.astype(BF16)

    h_hi = h2.astype(BF16)
    h_lo = (h2 - h_hi.astype(F32)).astype(BF16)
    logit = (jnp.dot(h_hi, rw_scr[0], preferred_element_type=F32)
             + jnp.dot(h_hi, rw_scr[1], preferred_element_type=F32)
             + jnp.dot(h_lo, rw_scr[0], preferred_element_type=F32)) + rb_ref[0]
    lane = lax.broadcasted_iota(jnp.int32, logit.shape, 1).astype(F32)
    neg = jnp.float32(-jnp.inf)
    big = jnp.float32(ROUTER_LANES)
    is_group = lane < N_GROUPS
    is_expert = (lane >= N_GROUPS) & (lane < N_GROUPS + N_EXPERTS)
    gl = jnp.where(is_group, logit, neg)
    gmax = jnp.max(gl, axis=-1, keepdims=True)
    gsel = jnp.min(jnp.where(gl == gmax, lane, big), axis=-1, keepdims=True)
    p_group = 1.0 / jnp.sum(jnp.where(is_group, jnp.exp(gl - gmax), 0.0), axis=-1, keepdims=True)
    eidx = lane - N_GROUPS
    in_group = is_expert & (jnp.floor(eidx / EXPERTS_PER_GROUP) == gsel)
    el = jnp.where(in_group, logit, neg)
    top1 = jnp.max(el, axis=-1, keepdims=True)
    sel1 = jnp.min(jnp.where(el == top1, lane, big), axis=-1, keepdims=True)
    el2 = jnp.where(lane == sel1, neg, el)
    top2 = jnp.max(el2, axis=-1, keepdims=True)
    sel2 = jnp.min(jnp.where(el2 == top2, lane, big), axis=-1, keepdims=True)
    e2 = jnp.exp(top2 - top1)
    w1 = p_group / (1.0 + e2)
    w2 = p_group * e2 / (1.0 + e2)
    comb = jnp.where(lane == sel1, w1, 0.0) + jnp.where(lane == sel2, w2, 0.0)
    comb_ref[...] = comb[:, N_GROUPS:N_GROUPS + N_EXPERTS]


def _merge(o_na, o_swa, o_ret, z, x, mod_rows, norm2_g, w_br_na, w_br_swa, w_br_ret, w_out,
           router_w, router_b, layer):
    n, d = x.shape
    n_cond = mod_rows.shape[0]
    per_cond = n // n_cond
    tm = 512
    gate_width = 3 * d
    assert OFF_GATES % gate_width == 0
    row = lambda w: pl.BlockSpec((tm, w), lambda i: (i, 0))
    weight = lambda k: pl.BlockSpec((1, k, d), lambda i: (layer, 0, 0))
    return pl.pallas_call(
        _merge_body,
        grid=(n // tm,),
        in_specs=[row(WIDTH_NA), row(WIDTH_SWA), row(WIDTH_RET),
                  pl.BlockSpec((tm, gate_width), lambda i: (i, OFF_GATES // gate_width)),
                  row(d),
                  pl.BlockSpec((1, 1, 6 * d), lambda i: (i * tm // per_cond, 0, 0)),
                  pl.BlockSpec((1, 1, d), lambda i: (layer, 0, 0)),
                  weight(WIDTH_NA), weight(WIDTH_SWA), weight(WIDTH_RET), weight(d),
                  pl.BlockSpec((1, d, ROUTER_LANES), lambda i: (layer, 0, 0)),
                  pl.BlockSpec((1, 1, ROUTER_LANES), lambda i: (layer, 0, 0))],
        out_specs=[row(d), row(d), row(N_EXPERTS)],
        out_shape=[jax.ShapeDtypeStruct((n, d), F32),
                   jax.ShapeDtypeStruct((n, d), BF16),
                   jax.ShapeDtypeStruct((n, N_EXPERTS), F32)],
        scratch_shapes=[pltpu.VMEM((WIDTH_NA, d), BF16), pltpu.VMEM((WIDTH_SWA, d), BF16),
                        pltpu.VMEM((WIDTH_RET, d), BF16), pltpu.VMEM((d, d), BF16),
                        pltpu.VMEM((2, d, ROUTER_LANES), BF16)],
        compiler_params=_params(("arbitrary",), 56),
        name="merge_router",
    )(o_na, o_swa, o_ret, z, x, mod_rows, norm2_g.reshape(norm2_g.shape[0], 1, d),
      w_br_na, w_br_swa, w_br_ret, w_out, router_w, router_b)


def _moe_body(h_ref, comb_ref, x1_ref, mod_ref, wg_ref, wu_ref, wd_ref, fg_ref, o_ref, *, final):
    e = pl.program_id(1)
    h = h_ref[...]
    hg = jnp.dot(h, wg_ref[0, 0].astype(BF16), preferred_element_type=F32)
    hu = jnp.dot(h, wu_ref[0, 0].astype(BF16), preferred_element_type=F32)
    comb = comb_ref[...]
    lane = lax.broadcasted_iota(jnp.int32, comb.shape, 1)
    c = jnp.sum(jnp.where(lane == e, comb, 0.0), axis=-1, keepdims=True)
    a = _silu(hg) * hu * c
    y = jnp.dot(a.astype(BF16), wd_ref[0, 0].astype(BF16), preferred_element_type=F32)

    @pl.when(e == 0)
    def _():
        o_ref[...] = y

    @pl.when(e > 0)
    def _():
        o_ref[...] += y

    @pl.when(e == N_EXPERTS - 1)
    def _():
        x2 = x1_ref[...] + mod_ref[0][:, 5 * D_MODEL:6 * D_MODEL] * o_ref[...]
        if final:
            x2 = x2 * lax.rsqrt(jnp.mean(x2 * x2, axis=-1, keepdims=True) + EPS) * fg_ref[...]
        o_ref[...] = x2


def _moe(h2, comb, x1, mod_rows, w_gate, w_up, w_down, final_g, layer, final):
    n, d = x1.shape
    n_cond = mod_rows.shape[0]
    per_cond = n // n_cond
    tm = 1024
    return pl.pallas_call(
        functools.partial(_moe_body, final=final),
        grid=(n // tm, N_EXPERTS),
        in_specs=[pl.BlockSpec((tm, d), lambda i, e: (i, 0)),
                  pl.BlockSpec((tm, N_EXPERTS), lambda i, e: (i, 0)),
                  pl.BlockSpec((tm, d), lambda i, e: (i, 0)),
                  pl.BlockSpec((1, 1, 6 * d), lambda i, e: (i * tm // per_cond, 0, 0)),
                  pl.BlockSpec((1, 1, d, D_EXPERT), lambda i, e: (layer, e, 0, 0)),
                  pl.BlockSpec((1, 1, d, D_EXPERT), lambda i, e: (layer, e, 0, 0)),
                  pl.BlockSpec((1, 1, D_EXPERT, d), lambda i, e: (layer, e, 0, 0)),
                  pl.BlockSpec((1, d), lambda i, e: (0, 0))],
        out_specs=pl.BlockSpec((tm, d), lambda i, e: (i, 0)),
        out_shape=jax.ShapeDtypeStruct((n, d), F32),
        compiler_params=_params(("arbitrary", "arbitrary"), 56),
        name="moe",
    )(h2, comb, x1, mod_rows, w_gate, w_up, w_down, final_g.reshape(1, d))


def kernel(x_prompt, x_sample, c, cache_na_k, cache_na_v, cache_swa_k, cache_swa_v, state_ret, c_ctx, ada_w, ada_b, norm1_g, norm2_g, w_in, na_rpb, swa_sink, ret_decay, ret_norm_g, w_br_na, w_br_swa, w_br_ret, w_out, router_group_w, router_group_b, router_expert_w, router_expert_b, w_gate, w_up, w_down, final_norm_g):
    batch, seq, d = x_prompt.shape
    dec_batch, dec_seq, _ = x_sample.shape
    depth = ada_w.shape[0]
    assert d == D_MODEL and dec_batch + 1 <= 8

    cond = jnp.concatenate([c_ctx[None], c, jnp.zeros((8 - 1 - dec_batch, d), F32)], axis=0)
    mod = _modulation(cond, ada_w, ada_b)
    router_w = jnp.concatenate([router_group_w, router_expert_w], axis=-1)
    router_w = jnp.pad(router_w, ((0, 0), (0, 0), (0, ROUTER_LANES - router_w.shape[-1])))
    router_b = jnp.concatenate([router_group_b, router_expert_b], axis=-1)
    router_b = jnp.pad(router_b, ((0, 0), (0, ROUTER_LANES - router_b.shape[-1])))[:, None, :]
    cos, sin = _rope_tables(dec_seq)

    xp = x_prompt.reshape(batch * seq, d)
    xs = x_sample.reshape(dec_batch * dec_seq, d)
    na_k, na_v, swa_k, swa_v, ret_s = [], [], [], [], []
    for l in range(depth):
        final = l == depth - 1
        mod_ctx = mod[l, 0:1][:, None, :]
        mod_lat = mod[l, 1:1 + dec_batch][:, None, :]
        merge_w = (norm2_g, w_br_na, w_br_swa, w_br_ret, w_out, router_w, router_b, l)
        moe_w = (w_gate, w_up, w_down, final_norm_g, l, final)

        z = _in_proj(xp, mod_ctx, norm1_g, w_in, l)
        o_na, o_swa, o_ret, ka, va, ksw, vsw, sr = _ctx_mixers(
            z, swa_sink[l], ret_decay[l], ret_norm_g[l], batch, seq)
        na_k.append(ka)
        na_v.append(va)
        swa_k.append(ksw)
        swa_v.append(vsw)
        ret_s.append(sr)
        x1, h2, comb = _merge(o_na, o_swa, o_ret, z, xp, mod_ctx, *merge_w)
        xp = _moe(h2, comb, x1, mod_ctx, *moe_w)

        z = _in_proj(xs, mod_lat, norm1_g, w_in, l)
        o_na = _lat_na(z, cache_na_k, cache_na_v, _na_bias_table(na_rpb[l], dec_seq // GRID_W),
                       l, dec_batch, dec_seq)
        o_swa = _lat_swa(z, cache_swa_k, cache_swa_v, swa_sink[l], cos, sin, l, dec_batch, dec_seq)
        o_ret = _lat_ret(z, ret_decay[l], ret_norm_g[l], state_ret, l, dec_batch, dec_seq)
        x1, h2, comb = _merge(o_na, o_swa, o_ret, z, xs, mod_lat, *merge_w)
        xs = _moe(h2, comb, x1, mod_lat, *moe_w)

    return (xp.reshape(batch, seq, d), xs.reshape(dec_batch, dec_seq, d),
            jnp.stack(na_k, axis=1), jnp.stack(na_v, axis=1),
            jnp.stack(swa_k, axis=1), jnp.stack(swa_v, axis=1), jnp.stack(ret_s, axis=1))
```

```python
import functools

import numpy as np
import jax
import jax.numpy as jnp
from jax import lax
from jax.experimental import pallas as pl
from jax.experimental.pallas import tpu as pltpu

F32 = jnp.float32
BF16 = jnp.bfloat16

D_MODEL = 1024
HEAD_DIM = 64
GRID_W = 64
N_HEADS_NA = 8
WIN_ROWS = 8
WIN_COLS = 16
N_HEADS_SWA = 4
N_KV_SWA = 2
SWA_RADIUS = 128
SWA_BLOCK = 128
N_HEADS_RET = 4
RET_CHUNK = 128
WIDTH_NA = N_HEADS_NA * HEAD_DIM
WIDTH_SWA = N_HEADS_SWA * HEAD_DIM
WIDTH_KV_SWA = N_KV_SWA * HEAD_DIM
WIDTH_RET = N_HEADS_RET * HEAD_DIM
IN_COLS = 3 * WIDTH_NA + WIDTH_SWA + 2 * WIDTH_KV_SWA + 4 * WIDTH_RET + 3 * D_MODEL
OFF_NA = 0
OFF_SWA = 3 * WIDTH_NA
OFF_RET = OFF_SWA + WIDTH_SWA + 2 * WIDTH_KV_SWA
OFF_GATES = OFF_RET + 4 * WIDTH_RET
N_GROUPS = 4
EXPERTS_PER_GROUP = 4
N_EXPERTS = N_GROUPS * EXPERTS_PER_GROUP
D_EXPERT = D_MODEL // 4
ROPE_BASE = 10000.0
EPS = 1e-6
SCALE = HEAD_DIM ** -0.5
MASKED = -1e30
ROUTER_LANES = 128

MIB = 1024 * 1024


def _mm(a, b):
    return jnp.dot(a.astype(BF16), b.astype(BF16), preferred_element_type=F32)


def _mm_nt(a, b):
    return lax.dot_general(a.astype(BF16), b.astype(BF16), (((1,), (1,)), ((), ())),
                           preferred_element_type=F32)


def _sigmoid(x):
    return 1.0 / (1.0 + jnp.exp(-x))


def _silu(x):
    return x * _sigmoid(x)


def _params(semantics, vmem_mib):
    return pltpu.CompilerParams(dimension_semantics=semantics, vmem_limit_bytes=vmem_mib * MIB)


def _mod_body(cond_ref, w_ref, b_ref, o_ref):
    o_ref[0] = _mm(_silu(cond_ref[...]), w_ref[0]) + b_ref[0]


def _modulation(cond, ada_w, ada_b):
    depth, d, n = ada_w.shape
    tn = 1024
    return pl.pallas_call(
        _mod_body,
        grid=(depth, n // tn),
        in_specs=[pl.BlockSpec((8, d), lambda l, j: (0, 0)),
                  pl.BlockSpec((1, d, tn), lambda l, j: (l, 0, j)),
                  pl.BlockSpec((1, 1, tn), lambda l, j: (l, 0, j))],
        out_specs=pl.BlockSpec((1, 8, tn), lambda l, j: (l, 0, j)),
        out_shape=jax.ShapeDtypeStruct((depth, 8, n), F32),
        compiler_params=_params(("arbitrary", "arbitrary"), 32),
        name="modulation",
    )(cond, ada_w, ada_b.reshape(depth, 1, n))


def _inproj_body(x_ref, mod_ref, g_ref, w_ref, o_ref, h_scr, w_scr):
    i = pl.program_id(0)
    j = pl.program_id(1)

    @pl.when(i == 0)
    def _():
        w_scr[j] = w_ref[0].astype(BF16)

    @pl.when(j == 0)
    def _():
        x = x_ref[...]
        m = mod_ref[0]
        y = x * lax.rsqrt(jnp.mean(x * x, axis=-1, keepdims=True) + EPS) * g_ref[0]
        h_scr[...] = (y * (1.0 + m[:, D_MODEL:2 * D_MODEL]) + m[:, 0:D_MODEL]).astype(BF16)

    o_ref[...] = jnp.dot(h_scr[...], w_scr[j], preferred_element_type=F32)


def _in_proj(x, mod_rows, norm_g, w_in, layer):
    n, d = x.shape
    n_cond = mod_rows.shape[0]
    tm, tn = 1024, 1024
    nj = IN_COLS // tn
    per_cond = n // n_cond
    return pl.pallas_call(
        _inproj_body,
        grid=(n // tm, nj),
        in_specs=[pl.BlockSpec((tm, d), lambda i, j: (i, 0)),
                  pl.BlockSpec((1, 1, 6 * d), lambda i, j: (i * tm // per_cond, 0, 0)),
                  pl.BlockSpec((1, 1, d), lambda i, j: (layer, 0, 0)),
                  pl.BlockSpec((1, d, tn), lambda i, j: (layer, 0, jnp.where(i == 0, j, nj - 1)))],
        out_specs=pl.BlockSpec((tm, tn), lambda i, j: (i, j)),
        out_shape=jax.ShapeDtypeStruct((n, IN_COLS), F32),
        scratch_shapes=[pltpu.VMEM((tm, d), BF16), pltpu.VMEM((nj, d, tn), BF16)],
        compiler_params=_params(("arbitrary", "arbitrary"), 48),
        name="in_proj",
    )(x, mod_rows, norm_g.reshape(norm_g.shape[0], 1, d), w_in)


def _log_sigmoid(x):
    return jnp.minimum(x, 0.0) - jnp.log1p(jnp.exp(-jnp.abs(x)))


def _retention(z_ref, c0, decay_ref, ng_ref, s0_ref, o_ref, st_ref, tab_scr, d_scr, u_scr, s_scr, n_tok):
    C = RET_CHUNK
    nc = n_tok // C
    d = HEAD_DIM
    pos = lax.broadcasted_iota(jnp.int32, (C, d), 0).astype(F32)
    ri = lax.broadcasted_iota(jnp.int32, (C, C), 0).astype(F32)
    ci = lax.broadcasted_iota(jnp.int32, (C, C), 1).astype(F32)
    diff = ri - ci
    for h in range(N_HEADS_RET):
        lgf = _log_sigmoid(jnp.full((C, C), decay_ref[0, h], F32))
        lgb = _log_sigmoid(jnp.full((C, C), decay_ref[1, h], F32))
        d_scr[h] = (jnp.where(diff >= 0, jnp.exp(lgf * jnp.maximum(diff, 0.0)), 0.0)
                    + jnp.where(diff <= 0, jnp.exp(lgb * jnp.maximum(-diff, 0.0)), 0.0))
        lf = _log_sigmoid(jnp.full((C, d), decay_ref[0, h], F32))
        lb = _log_sigmoid(jnp.full((C, d), decay_ref[1, h], F32))
        hs = slice(h * d, (h + 1) * d)
        tab_scr[0, :, hs] = jnp.exp(lf * (C - 1.0 - pos))
        tab_scr[1, :, hs] = jnp.exp(lb * pos)
        tab_scr[2, :, hs] = jnp.exp(lf * (pos + 1.0))
        tab_scr[3, :, hs] = jnp.exp(lb * (C - pos))
        tab_scr[4, :, hs] = jnp.exp(lf * C)
        tab_scr[5, :, hs] = jnp.exp(lb * C)

    def rows(c):
        return pl.ds(pl.multiple_of(c * C, C), C)

    def part(p):
        return slice(c0 + p * WIDTH_RET, c0 + (p + 1) * WIDTH_RET)

    def head_cols(p, h):
        return slice(c0 + p * WIDTH_RET + h * d, c0 + p * WIDTH_RET + (h + 1) * d)

    def upd(c, carry):
        k = z_ref[rows(c), part(1)] * SCALE
        kf_t = (k * tab_scr[0]).T
        kb_t = (k * tab_scr[1]).T
        for h in range(N_HEADS_RET):
            v = z_ref[rows(c), head_cols(2, h)]
            u_scr[0, c, h] = _mm(kf_t[h * d:(h + 1) * d, :], v)
            u_scr[1, c, h] = _mm(kb_t[h * d:(h + 1) * d, :], v)
        return carry

    lax.fori_loop(0, nc, upd, 0)

    for h in range(N_HEADS_RET):
        cf = tab_scr[4, 0:d, h * d:(h + 1) * d]
        cb = tab_scr[5, 0:d, h * d:(h + 1) * d]
        sf0 = jnp.zeros((d, d), F32) if s0_ref is None else s0_ref[0, 0, 0, h]
        sb0 = jnp.zeros((d, d), F32) if s0_ref is None else s0_ref[0, 0, 1, h]

        def scan_f(c, s):
            s_scr[0, c, h] = s
            return cf * s + u_scr[0, c, h]

        def scan_b(t, s):
            c = nc - 1 - t
            s_scr[1, c, h] = s
            return cb * s + u_scr[1, c, h]

        sf = lax.fori_loop(0, nc, scan_f, sf0)
        sb = lax.fori_loop(0, nc, scan_b, sb0)
        if st_ref is not None:
            st_ref[0, 0, h] = sf
            st_ref[0, 1, h] = sb

    def out(c, carry):
        for h in range(N_HEADS_RET):
            hs = slice(h * d, (h + 1) * d)
            q = z_ref[rows(c), head_cols(0, h)]
            k = z_ref[rows(c), head_cols(1, h)] * SCALE
            v = z_ref[rows(c), head_cols(2, h)]
            gate = z_ref[rows(c), head_cols(3, h)]
            a = _mm_nt(q, k) * d_scr[h]
            o = (_mm(a, v) + _mm(q, s_scr[0, c, h]) * tab_scr[2, :, hs]
                 + _mm(q, s_scr[1, c, h]) * tab_scr[3, :, hs])
            o = o * lax.rsqrt(jnp.mean(o * o, axis=-1, keepdims=True) + EPS)
            o = o * ng_ref[:, hs]
            o_ref[rows(c), hs] = (o * _silu(gate)).astype(o_ref.dtype)
        return carry

    lax.fori_loop(0, nc, out, 0)


def _ret_scratch(n_tok):
    nc = n_tok // RET_CHUNK
    return [pltpu.VMEM((6, RET_CHUNK, WIDTH_RET), F32),
            pltpu.VMEM((N_HEADS_RET, RET_CHUNK, RET_CHUNK), F32),
            pltpu.VMEM((2, nc, N_HEADS_RET, HEAD_DIM, HEAD_DIM), F32),
            pltpu.VMEM((2, nc, N_HEADS_RET, HEAD_DIM, HEAD_DIM), F32)]


def _ctx_attn_body(sink_ref, decay_ref, z_ref, ng_ref, *refs, seq, n_prev):
    (ona_ref, oswa_ref, oret_ref, kna_ref, vna_ref, kswa_ref, vswa_ref, st_ref,
     tab_scr, d_scr, u_scr, s_scr) = refs[n_prev:]
    d = HEAD_DIM
    for h in range(N_HEADS_NA):
        q = z_ref[:, OFF_NA + h * d:OFF_NA + (h + 1) * d]
        k = z_ref[:, OFF_NA + WIDTH_NA + h * d:OFF_NA + WIDTH_NA + (h + 1) * d]
        v = z_ref[:, OFF_NA + 2 * WIDTH_NA + h * d:OFF_NA + 2 * WIDTH_NA + (h + 1) * d]
        kna_ref[0, 0, h] = k
        vna_ref[0, 0, h] = v
        s = _mm_nt(q, k) * SCALE
        e = jnp.exp(s - jnp.max(s, axis=-1, keepdims=True))
        o = _mm(e, v) / jnp.sum(e, axis=-1, keepdims=True)
        ona_ref[:, h * d:(h + 1) * d] = o.astype(ona_ref.dtype)
    group = N_HEADS_SWA // N_KV_SWA
    for g in range(N_KV_SWA):
        lo = OFF_SWA + WIDTH_SWA
        k = z_ref[:, lo + g * d:lo + (g + 1) * d]
        v = z_ref[:, lo + WIDTH_KV_SWA + g * d:lo + WIDTH_KV_SWA + (g + 1) * d]
        kswa_ref[0, 0, g] = k
        vswa_ref[0, 0, g] = v
        for jq in range(group):
            h = g * group + jq
            q = z_ref[:, OFF_SWA + h * d:OFF_SWA + (h + 1) * d]
            s = _mm_nt(q, k) * SCALE
            m = jnp.maximum(jnp.max(s, axis=-1, keepdims=True), sink_ref[h])
            e = jnp.exp(s - m)
            den = jnp.sum(e, axis=-1, keepdims=True) + jnp.exp(sink_ref[h] - m)
            oswa_ref[:, h * d:(h + 1) * d] = (_mm(e, v) / den).astype(oswa_ref.dtype)
    _retention(z_ref, OFF_RET, decay_ref, ng_ref, None, oret_ref, st_ref.at[0],
               tab_scr, d_scr, u_scr, s_scr, seq)


def _ctx_mixers(z, swa_sink, ret_decay, ret_norm_g, caches, layer, depth, batch, seq):
    n = batch * seq
    width = OFF_GATES
    smem = pl.BlockSpec(memory_space=pltpu.SMEM)
    kv = lambda heads: pl.BlockSpec((1, 1, heads, seq, HEAD_DIM), lambda b: (b, layer, 0, 0, 0))
    kv_shape = lambda heads: jax.ShapeDtypeStruct((batch, depth, heads, seq, HEAD_DIM), F32)
    in_specs = [smem, smem,
                pl.BlockSpec((seq, width), lambda b: (b, 0)),
                pl.BlockSpec((1, WIDTH_RET), lambda b: (0, 0))]
    args = [swa_sink, ret_decay, z, ret_norm_g.reshape(1, WIDTH_RET)]
    n_prev = 0 if caches is None else len(caches)
    aliases = {len(args) + i: 3 + i for i in range(n_prev)}
    if caches is not None:
        in_specs += [pl.BlockSpec(memory_space=pl.ANY)] * n_prev
        args += list(caches)
    return pl.pallas_call(
        functools.partial(_ctx_attn_body, seq=seq, n_prev=n_prev),
        grid=(batch,),
        in_specs=in_specs,
        out_specs=[pl.BlockSpec((seq, WIDTH_NA), lambda b: (b, 0)),
                   pl.BlockSpec((seq, WIDTH_SWA), lambda b: (b, 0)),
                   pl.BlockSpec((seq, WIDTH_RET), lambda b: (b, 0)),
                   kv(N_HEADS_NA), kv(N_HEADS_NA), kv(N_KV_SWA), kv(N_KV_SWA),
                   pl.BlockSpec((1, 1, 2, N_HEADS_RET, HEAD_DIM, HEAD_DIM),
                                lambda b: (b, layer, 0, 0, 0, 0))],
        out_shape=[jax.ShapeDtypeStruct((n, WIDTH_NA), BF16),
                   jax.ShapeDtypeStruct((n, WIDTH_SWA), BF16),
                   jax.ShapeDtypeStruct((n, WIDTH_RET), BF16),
                   kv_shape(N_HEADS_NA), kv_shape(N_HEADS_NA), kv_shape(N_KV_SWA), kv_shape(N_KV_SWA),
                   jax.ShapeDtypeStruct((batch, depth, 2, N_HEADS_RET, HEAD_DIM, HEAD_DIM), F32)],
        scratch_shapes=_ret_scratch(seq),
        input_output_aliases=aliases,
        compiler_params=_params(("arbitrary",), 32),
        name="ctx_mixers",
    )(*args)


NA_QROWS = 4
NA_WROWS = NA_QROWS + WIN_ROWS


def _na_bias_table(rpb, rows):
    n_groups = rows // NA_QROWS
    assert rows % NA_QROWS == 0 and n_groups >= 3
    n_heads, n_ro, n_co = rpb.shape
    half = WIN_ROWS // 2
    cc = np.arange(GRID_W)
    col_start = np.clip(cc - WIN_COLS // 2, 0, GRID_W - WIN_COLS)
    valid = (cc[None, :] >= col_start[:, None]) & (cc[None, :] < col_start[:, None] + WIN_COLS)
    span = 2 * GRID_W - 1
    lead = GRID_W - WIN_COLS
    ext = jnp.pad(rpb.astype(F32), ((0, 0), (0, 0), (lead, span + 1 - lead - n_co)))
    band = jnp.tile(ext, (1, 1, GRID_W))[..., :GRID_W * span].reshape(n_heads, n_ro, GRID_W, span)
    band = jnp.where(valid[None, None], band[..., GRID_W - 1:], MASKED)
    outside = jnp.full((n_heads, GRID_W, GRID_W), MASKED, F32)
    tabs = []
    for g in (0, 1, n_groups - 1):
        ws = min(max(NA_QROWS * g - half, 0), rows - NA_WROWS)
        q_blocks = []
        for i in range(NA_QROWS):
            r = NA_QROWS * g + i
            rs = min(max(r - half, 0), rows - WIN_ROWS)
            q_blocks.append(jnp.concatenate(
                [band[:, ws + j - r + WIN_ROWS - 1] if rs <= ws + j < rs + WIN_ROWS else outside
                 for j in range(NA_WROWS)], axis=-1))
        tabs.append(jnp.concatenate(q_blocks, axis=-2))
    return jnp.stack(tabs, axis=0)


def _lat_na_body(z_ref, kc_ref, vc_ref, bias_ref, o_ref, kc_scr, vc_scr, *, rows):
    d = HEAD_DIM
    nq = NA_QROWS * GRID_W
    g = pl.program_id(1)

    @pl.when(g == 0)
    def _():
        kc_scr[...] = kc_ref[0, 0].astype(BF16)
        vc_scr[...] = vc_ref[0, 0].astype(BF16)

    ws = jnp.clip(NA_QROWS * g - WIN_ROWS // 2, 0, rows - NA_WROWS)
    qrows = pl.ds(pl.multiple_of(g * nq, nq), nq)
    wrows = pl.ds(pl.multiple_of(ws * GRID_W, GRID_W), NA_WROWS * GRID_W)
    low_half = lax.broadcasted_iota(jnp.int32, (nq, 2 * d), 1) < d
    for pair in range(N_HEADS_NA // 2):
        cols = slice(pair * 2 * d, (pair + 1) * 2 * d)
        q2 = (z_ref[qrows, cols] * SCALE).astype(BF16)
        kl2 = z_ref[wrows, WIDTH_NA + pair * 2 * d:WIDTH_NA + (pair + 1) * 2 * d].astype(BF16)
        vl2 = z_ref[wrows, 2 * WIDTH_NA + pair * 2 * d:2 * WIDTH_NA + (pair + 1) * 2 * d].astype(BF16)
        for hh in range(2):
            h = 2 * pair + hh
            qm = jnp.where(low_half == (hh == 0), q2, jnp.zeros_like(q2))
            s_loc = _mm_nt(qm, kl2) + bias_ref[0, h]
            s_ctx = _mm_nt(q2[:, hh * d:(hh + 1) * d], kc_scr[h])
            m = jnp.maximum(jnp.max(s_ctx, axis=-1, keepdims=True), jnp.max(s_loc, axis=-1, keepdims=True))
            e_ctx = jnp.exp(s_ctx - m)
            e_loc = jnp.exp(s_loc - m)
            den = jnp.sum(e_ctx, axis=-1, keepdims=True) + jnp.sum(e_loc, axis=-1, keepdims=True)
            o = (_mm(e_ctx, vc_scr[h]) + _mm(e_loc, vl2)[:, hh * d:(hh + 1) * d]) / den
            o_ref[:, h * d:(h + 1) * d] = o.astype(o_ref.dtype)


def _lat_na(z, cache_k, cache_v, bias_tab, layer, batch, seq):
    rows = seq // GRID_W
    n_groups = rows // NA_QROWS
    nq = NA_QROWS * GRID_W
    past = cache_k.shape[3]

    def group_class(g):
        return jnp.where(g == 0, 0, jnp.where(g == n_groups - 1, 2, 1))

    cache = pl.BlockSpec((1, 1, N_HEADS_NA, past, HEAD_DIM), lambda b, g: (b, layer, 0, 0, 0))
    return pl.pallas_call(
        functools.partial(_lat_na_body, rows=rows),
        grid=(batch, n_groups),
        in_specs=[pl.BlockSpec((seq, 3 * WIDTH_NA), lambda b, g: (b, 0)),
                  cache, cache,
                  pl.BlockSpec((1, N_HEADS_NA, nq, NA_WROWS * GRID_W),
                               lambda b, g: (group_class(g), 0, 0, 0))],
        out_specs=pl.BlockSpec((nq, WIDTH_NA), lambda b, g: (b * n_groups + g, 0)),
        out_shape=jax.ShapeDtypeStruct((batch * seq, WIDTH_NA), BF16),
        scratch_shapes=[pltpu.VMEM((N_HEADS_NA, past, HEAD_DIM), BF16),
                        pltpu.VMEM((N_HEADS_NA, past, HEAD_DIM), BF16)],
        compiler_params=_params(("arbitrary", "arbitrary"), 58),
        name="lat_na",
    )(z, cache_k, cache_v, bias_tab)


def _rope_tables(seq):
    t = jnp.arange(seq)
    n_freq = HEAD_DIM // 4
    inv = ROPE_BASE ** (-jnp.arange(n_freq, dtype=F32) / n_freq)
    ang = jnp.concatenate([(t // GRID_W).astype(F32)[:, None] * inv,
                           (t % GRID_W).astype(F32)[:, None] * inv], axis=-1)
    cos = jnp.repeat(jnp.cos(ang), 2, axis=-1)
    sign = jnp.tile(jnp.asarray([-1.0, 1.0], F32), HEAD_DIM // 2)
    sin = jnp.repeat(jnp.sin(ang), 2, axis=-1) * sign
    return jnp.tile(cos, (1, N_HEADS_SWA)), jnp.tile(sin, (1, N_HEADS_SWA))


def _rope(x, cos, sin_signed):
    width = x.shape[-1]
    lane = lax.broadcasted_iota(jnp.int32, x.shape, x.ndim - 1)
    partner = jnp.where((lane & 1) == 0, pltpu.roll(x, width - 1, x.ndim - 1), pltpu.roll(x, 1, x.ndim - 1))
    return x * cos + partner * sin_signed


def _lat_swa_body(sink_ref, z_ref, kc_ref, vc_ref, cos_ref, sin_ref, o_ref, kr_scr, *, seq):
    d = HEAD_DIM
    blk = SWA_BLOCK
    n = pl.program_id(1)

    @pl.when(n == 0)
    def _():
        kr_scr[...] = _rope(z_ref[:, WIDTH_SWA:WIDTH_SWA + WIDTH_KV_SWA],
                            cos_ref[:, :WIDTH_KV_SWA], sin_ref[:, :WIDTH_KV_SWA])

    qrows = pl.ds(pl.multiple_of(n * blk, blk), blk)
    kstart = jnp.clip((n - 1) * blk, 0, seq - 3 * blk)
    krows = pl.ds(pl.multiple_of(kstart, blk), 3 * blk)
    q_all = _rope(z_ref[qrows, 0:WIDTH_SWA], cos_ref[qrows, :], sin_ref[qrows, :])
    qpos = n * blk + lax.broadcasted_iota(jnp.int32, (blk, 3 * blk), 0)
    kpos = kstart + lax.broadcasted_iota(jnp.int32, (blk, 3 * blk), 1)
    valid = jnp.abs(qpos - kpos) <= SWA_RADIUS
    group = N_HEADS_SWA // N_KV_SWA
    for g in range(N_KV_SWA):
        kl = kr_scr[krows, g * d:(g + 1) * d]
        vl = z_ref[krows, WIDTH_SWA + WIDTH_KV_SWA + g * d:WIDTH_SWA + WIDTH_KV_SWA + (g + 1) * d]
        for jq in range(group):
            h = g * group + jq
            q = q_all[:, h * d:(h + 1) * d]
            s_loc = jnp.where(valid, _mm_nt(q, kl) * SCALE, MASKED)
            s_ctx = _mm_nt(q, kc_ref[0, 0, g]) * SCALE
            m = jnp.maximum(jnp.maximum(jnp.max(s_loc, axis=-1, keepdims=True),
                                        jnp.max(s_ctx, axis=-1, keepdims=True)), sink_ref[h])
            e_loc = jnp.exp(s_loc - m)
            e_ctx = jnp.exp(s_ctx - m)
            den = (jnp.sum(e_loc, axis=-1, keepdims=True) + jnp.sum(e_ctx, axis=-1, keepdims=True)
                   + jnp.exp(sink_ref[h] - m))
            o = (_mm(e_ctx, vc_ref[0, 0, g]) + _mm(e_loc, vl)) / den
            o_ref[:, h * d:(h + 1) * d] = o.astype(o_ref.dtype)


def _lat_swa(z, cache_k, cache_v, swa_sink, cos, sin, layer, batch, seq):
    nb = seq // SWA_BLOCK
    past = cache_k.shape[3]
    width = WIDTH_SWA + 2 * WIDTH_KV_SWA
    assert OFF_SWA % width == 0 and seq >= 3 * SWA_BLOCK
    cache = pl.BlockSpec((1, 1, N_KV_SWA, past, HEAD_DIM), lambda b, n: (b, layer, 0, 0, 0))
    table = pl.BlockSpec((seq, WIDTH_SWA), lambda b, n: (0, 0))
    return pl.pallas_call(
        functools.partial(_lat_swa_body, seq=seq),
        grid=(batch, nb),
        in_specs=[pl.BlockSpec(memory_space=pltpu.SMEM),
                  pl.BlockSpec((seq, width), lambda b, n: (b, OFF_SWA // width)),
                  cache, cache, table, table],
        out_specs=pl.BlockSpec((SWA_BLOCK, WIDTH_SWA), lambda b, n: (b * nb + n, 0)),
        out_shape=jax.ShapeDtypeStruct((batch * seq, WIDTH_SWA), BF16),
        scratch_shapes=[pltpu.VMEM((seq, WIDTH_KV_SWA), F32)],
        compiler_params=_params(("arbitrary", "arbitrary"), 32),
        name="lat_swa",
    )(swa_sink, z, cache_k, cache_v, cos, sin)


def _lat_ret_body(decay_ref, z_ref, ng_ref, s0_ref, o_ref, tab_scr, d_scr, u_scr, s_scr, *, seq):
    _retention(z_ref, 0, decay_ref, ng_ref, s0_ref, o_ref, None, tab_scr, d_scr, u_scr, s_scr, seq)


def _lat_ret(z, ret_decay, ret_norm_g, state, layer, batch, seq):
    width = 4 * WIDTH_RET
    assert OFF_RET % width == 0
    return pl.pallas_call(
        functools.partial(_lat_ret_body, seq=seq),
        grid=(batch,),
        in_specs=[pl.BlockSpec(memory_space=pltpu.SMEM),
                  pl.BlockSpec((seq, width), lambda b: (b, OFF_RET // width)),
                  pl.BlockSpec((1, WIDTH_RET), lambda b: (0, 0)),
                  pl.BlockSpec((1, 1, 2, N_HEADS_RET, HEAD_DIM, HEAD_DIM),
                               lambda b: (b, layer, 0, 0, 0, 0))],
        out_specs=pl.BlockSpec((seq, WIDTH_RET), lambda b: (b, 0)),
        out_shape=jax.ShapeDtypeStruct((batch * seq, WIDTH_RET), BF16),
        scratch_shapes=_ret_scratch(seq),
        compiler_params=_params(("arbitrary",), 40),
        name="lat_ret",
    )(ret_decay, z, ret_norm_g.reshape(1, WIDTH_RET), state)


def _merge_body(ona_ref, oswa_ref, oret_ref, zg_ref, x_ref, mod_ref, g2_ref,
                wna_ref, wswa_ref, wret_ref, wout_ref, rw_ref, rb_ref,
                x1_ref, h2_ref, comb_ref, wna_scr, wswa_scr, wret_scr, wout_scr, rw_scr):
    @pl.when(pl.program_id(0) == 0)
    def _():
        wna_scr[...] = wna_ref[0].astype(BF16)
        wswa_scr[...] = wswa_ref[0].astype(BF16)
        wret_scr[...] = wret_ref[0].astype(BF16)
        wout_scr[...] = wout_ref[0].astype(BF16)
        rw = rw_ref[0]
        rw_hi = rw.astype(BF16)
        rw_scr[0] = rw_hi
        rw_scr[1] = (rw - rw_hi.astype(F32)).astype(BF16)

    D = D_MODEL
    m = mod_ref[0]
    z = (_sigmoid(zg_ref[:, 0:D]) * jnp.dot(ona_ref[...], wna_scr[...], preferred_element_type=F32)
         + _sigmoid(zg_ref[:, D:2 * D]) * jnp.dot(oswa_ref[...], wswa_scr[...], preferred_element_type=F32)
         + _sigmoid(zg_ref[:, 2 * D:3 * D]) * jnp.dot(oret_ref[...], wret_scr[...], preferred_element_type=F32))
    y = jnp.dot(z.astype(BF16), wout_scr[...], preferred_element_type=F32)
    x1 = x_ref[...] + m[:, 2 * D:3 * D] * y
    x1_ref[...] = x1
    h2 = x1 * lax.rsqrt(jnp.mean(x1 * x1, axis=-1, keepdims=True) + EPS) * g2_ref[0]
    h2 = h2 * (1.0 + m[:, 4 * D:5 * D]) + m[:, 3 * D:4 * D]
    h2_ref[...] = h2.astype(BF16)

    h_hi = h2.astype(BF16)
    h_lo = (h2 - h_hi.astype(F32)).astype(BF16)
    logit = (jnp.dot(h_hi, rw_scr[0], preferred_element_type=F32)
             + jnp.dot(h_hi, rw_scr[1], preferred_element_type=F32)
             + jnp.dot(h_lo, rw_scr[0], preferred_element_type=F32)) + rb_ref[0]
    lane = lax.broadcasted_iota(jnp.int32, logit.shape, 1).astype(F32)
    neg = jnp.float32(-jnp.inf)
    big = jnp.float32(ROUTER_LANES)
    is_group = lane < N_GROUPS
    is_expert = (lane >= N_GROUPS) & (lane < N_GROUPS + N_EXPERTS)
    gl = jnp.where(is_group, logit, neg)
    gmax = jnp.max(gl, axis=-1, keepdims=True)
    gsel = jnp.min(jnp.where(gl == gmax, lane, big), axis=-1, keepdims=True)
    p_group = 1.0 / jnp.sum(jnp.where(is_group, jnp.exp(gl - gmax), 0.0), axis=-1, keepdims=True)
    eidx = lane - N_GROUPS
    in_group = is_expert & (jnp.floor(eidx / EXPERTS_PER_GROUP) == gsel)
    el = jnp.where(in_group, logit, neg)
    top1 = jnp.max(el, axis=-1, keepdims=True)
    sel1 = jnp.min(jnp.where(el == top1, lane, big), axis=-1, keepdims=True)
    el2 = jnp.where(lane == sel1, neg, el)
    top2 = jnp.max(el2, axis=-1, keepdims=True)
    sel2 = jnp.min(jnp.where(el2 == top2, lane, big), axis=-1, keepdims=True)
    e2 = jnp.exp(top2 - top1)
    w1 = p_group / (1.0 + e2)
    w2 = p_group * e2 / (1.0 + e2)
    comb = jnp.where(lane == sel1, w1, 0.0) + jnp.where(lane == sel2, w2, 0.0)
    comb_ref[...] = comb[:, N_GROUPS:N_GROUPS + N_EXPERTS]


def _merge(o_na, o_swa, o_ret, z, x, mod_rows, norm2_g, w_br_na, w_br_swa, w_br_ret, w_out,
           router_w, router_b, layer):
    n, d = x.shape
    n_cond = mod_rows.shape[0]
    per_cond = n // n_cond
    tm = 512
    gate_width = 3 * d
    assert OFF_GATES % gate_width == 0
    row = lambda w: pl.BlockSpec((tm, w), lambda i: (i, 0))
    weight = lambda k: pl.BlockSpec((1, k, d), lambda i: (layer, 0, 0))
    return pl.pallas_call(
        _merge_body,
        grid=(n // tm,),
        in_specs=[row(WIDTH_NA), row(WIDTH_SWA), row(WIDTH_RET),
                  pl.BlockSpec((tm, gate_width), lambda i: (i, OFF_GATES // gate_width)),
                  row(d),
                  pl.BlockSpec((1, 1, 6 * d), lambda i: (i * tm // per_cond, 0, 0)),
                  pl.BlockSpec((1, 1, d), lambda i: (layer, 0, 0)),
                  weight(WIDTH_NA), weight(WIDTH_SWA), weight(WIDTH_RET), weight(d),
                  pl.BlockSpec((1, d, ROUTER_LANES), lambda i: (layer, 0, 0)),
                  pl.BlockSpec((1, 1, ROUTER_LANES), lambda i: (layer, 0, 0))],
        out_specs=[row(d), row(d), row(N_EXPERTS)],
        out_shape=[jax.ShapeDtypeStruct((n, d), F32),
                   jax.ShapeDtypeStruct((n, d), BF16),
                   jax.ShapeDtypeStruct((n, N_EXPERTS), F32)],
        scratch_shapes=[pltpu.VMEM((WIDTH_NA, d), BF16), pltpu.VMEM((WIDTH_SWA, d), BF16),
                        pltpu.VMEM((WIDTH_RET, d), BF16), pltpu.VMEM((d, d), BF16),
                        pltpu.VMEM((2, d, ROUTER_LANES), BF16)],
        compiler_params=_params(("arbitrary",), 56),
        name="merge_router",
    )(o_na, o_swa, o_ret, z, x, mod_rows, norm2_g.reshape(norm2_g.shape[0], 1, d),
      w_br_na, w_br_swa, w_br_ret, w_out, router_w, router_b)


def _moe_body(h_ref, comb_ref, x1_ref, mod_ref, wg_ref, wu_ref, wd_ref, fg_ref, o_ref, *, final):
    g = pl.program_id(1)
    h = h_ref[...]
    comb = comb_ref[...]
    lane = lax.broadcasted_iota(jnp.int32, comb.shape, 1)
    y = None
    for e in range(EXPERTS_PER_GROUP):
        hg = jnp.dot(h, wg_ref[0, e].astype(BF16), preferred_element_type=F32)
        hu = jnp.dot(h, wu_ref[0, e].astype(BF16), preferred_element_type=F32)
        c = jnp.sum(jnp.where(lane == g * EXPERTS_PER_GROUP + e, comb, 0.0), axis=-1, keepdims=True)
        a = _silu(hg) * hu * c
        t = jnp.dot(a.astype(BF16), wd_ref[0, e].astype(BF16), preferred_element_type=F32)
        y = t if y is None else y + t

    @pl.when(g == 0)
    def _():
        o_ref[...] = y

    @pl.when(g > 0)
    def _():
        o_ref[...] += y

    @pl.when(g == N_GROUPS - 1)
    def _():
        x2 = x1_ref[...] + mod_ref[0][:, 5 * D_MODEL:6 * D_MODEL] * o_ref[...]
        if final:
            x2 = x2 * lax.rsqrt(jnp.mean(x2 * x2, axis=-1, keepdims=True) + EPS) * fg_ref[...]
        o_ref[...] = x2


def _moe(h2, comb, x1, mod_rows, w_gate, w_up, w_down, final_g, layer, final):
    n, d = x1.shape
    n_cond = mod_rows.shape[0]
    per_cond = n // n_cond
    tm = 1024
    return pl.pallas_call(
        functools.partial(_moe_body, final=final),
        grid=(n // tm, N_GROUPS),
        in_specs=[pl.BlockSpec((tm, d), lambda i, g: (i, 0)),
                  pl.BlockSpec((tm, N_EXPERTS), lambda i, g: (i, 0)),
                  pl.BlockSpec((tm, d), lambda i, g: (i, 0)),
                  pl.BlockSpec((1, 1, 6 * d), lambda i, g: (i * tm // per_cond, 0, 0)),
                  pl.BlockSpec((1, EXPERTS_PER_GROUP, d, D_EXPERT), lambda i, g: (layer, g, 0, 0)),
                  pl.BlockSpec((1, EXPERTS_PER_GROUP, d, D_EXPERT), lambda i, g: (layer, g, 0, 0)),
                  pl.BlockSpec((1, EXPERTS_PER_GROUP, D_EXPERT, d), lambda i, g: (layer, g, 0, 0)),
                  pl.BlockSpec((1, d), lambda i, g: (0, 0))],
        out_specs=pl.BlockSpec((tm, d), lambda i, g: (i, 0)),
        out_shape=jax.ShapeDtypeStruct((n, d), F32),
        compiler_params=_params(("arbitrary", "arbitrary"), 56),
        name="moe",
    )(h2, comb, x1, mod_rows, w_gate, w_up, w_down, final_g.reshape(1, d))


def kernel(x_prompt, x_sample, c, cache_na_k, cache_na_v, cache_swa_k, cache_swa_v, state_ret, c_ctx, ada_w, ada_b, norm1_g, norm2_g, w_in, na_rpb, swa_sink, ret_decay, ret_norm_g, w_br_na, w_br_swa, w_br_ret, w_out, router_group_w, router_group_b, router_expert_w, router_expert_b, w_gate, w_up, w_down, final_norm_g):
    batch, seq, d = x_prompt.shape
    dec_batch, dec_seq, _ = x_sample.shape
    depth = ada_w.shape[0]
    assert d == D_MODEL and dec_batch + 1 <= 8

    cond = jnp.concatenate([c_ctx[None], c, jnp.zeros((8 - 1 - dec_batch, d), F32)], axis=0)
    mod = _modulation(cond, ada_w, ada_b)
    router_w = jnp.concatenate([router_group_w, router_expert_w], axis=-1)
    router_w = jnp.pad(router_w, ((0, 0), (0, 0), (0, ROUTER_LANES - router_w.shape[-1])))
    router_b = jnp.concatenate([router_group_b, router_expert_b], axis=-1)
    router_b = jnp.pad(router_b, ((0, 0), (0, ROUTER_LANES - router_b.shape[-1])))[:, None, :]
    cos, sin = _rope_tables(dec_seq)

    xp = x_prompt.reshape(batch * seq, d)
    xs = x_sample.reshape(dec_batch * dec_seq, d)
    caches = None
    for l in range(depth):
        final = l == depth - 1
        mod_ctx = mod[l, 0:1][:, None, :]
        mod_lat = mod[l, 1:1 + dec_batch][:, None, :]
        merge_w = (norm2_g, w_br_na, w_br_swa, w_br_ret, w_out, router_w, router_b, l)
        moe_w = (w_gate, w_up, w_down, final_norm_g, l, final)

        z = _in_proj(xp, mod_ctx, norm1_g, w_in, l)
        o_na, o_swa, o_ret, *caches = _ctx_mixers(
            z, swa_sink[l], ret_decay[l], ret_norm_g[l], caches, l, depth, batch, seq)
        x1, h2, comb = _merge(o_na, o_swa, o_ret, z, xp, mod_ctx, *merge_w)
        xp = _moe(h2, comb, x1, mod_ctx, *moe_w)

        z = _in_proj(xs, mod_lat, norm1_g, w_in, l)
        o_na = _lat_na(z, cache_na_k, cache_na_v, _na_bias_table(na_rpb[l], dec_seq // GRID_W),
                       l, dec_batch, dec_seq)
        o_swa = _lat_swa(z, cache_swa_k, cache_swa_v, swa_sink[l], cos, sin, l, dec_batch, dec_seq)
        o_ret = _lat_ret(z, ret_decay[l], ret_norm_g[l], state_ret, l, dec_batch, dec_seq)
        x1, h2, comb = _merge(o_na, o_swa, o_ret, z, xs, mod_lat, *merge_w)
        xs = _moe(h2, comb, x1, mod_lat, *moe_w)

    return (xp.reshape(batch, seq, d), xs.reshape(dec_batch, dec_seq, d), *caches)
```

```python
import functools

import numpy as np
import jax
import jax.numpy as jnp
from jax import lax
from jax.experimental import pallas as pl
from jax.experimental.pallas import tpu as pltpu

F32 = jnp.float32
BF16 = jnp.bfloat16

D_MODEL = 1024
HEAD_DIM = 64
GRID_W = 64
N_HEADS_NA = 8
WIN_ROWS = 8
WIN_COLS = 16
N_HEADS_SWA = 4
N_KV_SWA = 2
SWA_RADIUS = 128
SWA_BLOCK = 128
N_HEADS_RET = 4
RET_CHUNK = 128
WIDTH_NA = N_HEADS_NA * HEAD_DIM
WIDTH_SWA = N_HEADS_SWA * HEAD_DIM
WIDTH_KV_SWA = N_KV_SWA * HEAD_DIM
WIDTH_RET = N_HEADS_RET * HEAD_DIM
IN_COLS = 3 * WIDTH_NA + WIDTH_SWA + 2 * WIDTH_KV_SWA + 4 * WIDTH_RET + 3 * D_MODEL
OFF_NA = 0
OFF_SWA = 3 * WIDTH_NA
OFF_RET = OFF_SWA + WIDTH_SWA + 2 * WIDTH_KV_SWA
OFF_GATES = OFF_RET + 4 * WIDTH_RET
N_GROUPS = 4
EXPERTS_PER_GROUP = 4
N_EXPERTS = N_GROUPS * EXPERTS_PER_GROUP
D_EXPERT = D_MODEL // 4
ROPE_BASE = 10000.0
EPS = 1e-6
SCALE = HEAD_DIM ** -0.5
MASKED = -1e30
ROUTER_LANES = 128

MIB = 1024 * 1024


def _mm(a, b):
    return jnp.dot(a.astype(BF16), b.astype(BF16), preferred_element_type=F32)


def _mm_nt(a, b):
    return lax.dot_general(a.astype(BF16), b.astype(BF16), (((1,), (1,)), ((), ())),
                           preferred_element_type=F32)


def _sigmoid(x):
    return 1.0 / (1.0 + jnp.exp(-x))


def _silu(x):
    return x * _sigmoid(x)


def _params(semantics, vmem_mib):
    return pltpu.CompilerParams(dimension_semantics=semantics, vmem_limit_bytes=vmem_mib * MIB)


def _mod_body(cond_ref, w_ref, b_ref, o_ref):
    o_ref[0] = _mm(_silu(cond_ref[...]), w_ref[0]) + b_ref[0]


def _modulation(cond, ada_w, ada_b):
    depth, d, n = ada_w.shape
    tn = 1024
    return pl.pallas_call(
        _mod_body,
        grid=(depth, n // tn),
        in_specs=[pl.BlockSpec((8, d), lambda l, j: (0, 0)),
                  pl.BlockSpec((1, d, tn), lambda l, j: (l, 0, j)),
                  pl.BlockSpec((1, 1, tn), lambda l, j: (l, 0, j))],
        out_specs=pl.BlockSpec((1, 8, tn), lambda l, j: (l, 0, j)),
        out_shape=jax.ShapeDtypeStruct((depth, 8, n), F32),
        compiler_params=_params(("arbitrary", "arbitrary"), 32),
        name="modulation",
    )(cond, ada_w, ada_b.reshape(depth, 1, n))


def _inproj_body(x_ref, mod_ref, g_ref, w_ref, *refs, n_f32):
    zf_ref = refs[0] if n_f32 else None
    zb_ref, h_scr, w_scr = refs[-3:]
    i = pl.program_id(0)
    j = pl.program_id(1)

    @pl.when(i == 0)
    def _():
        w_scr[j] = w_ref[0].astype(BF16)

    @pl.when(j == 0)
    def _():
        x = x_ref[...]
        m = mod_ref[0]
        y = x * lax.rsqrt(jnp.mean(x * x, axis=-1, keepdims=True) + EPS) * g_ref[0]
        h_scr[...] = (y * (1.0 + m[:, D_MODEL:2 * D_MODEL]) + m[:, 0:D_MODEL]).astype(BF16)

    acc = jnp.dot(h_scr[...], w_scr[j], preferred_element_type=F32)
    if n_f32 == 0:
        zb_ref[...] = acc.astype(BF16)
    else:
        @pl.when(j < n_f32)
        def _():
            zf_ref[...] = acc

        @pl.when(j >= n_f32)
        def _():
            zb_ref[...] = acc.astype(BF16)


IN_TILE = 1024


def _in_proj(x, mod_rows, norm_g, w_in, layer, n_f32):
    n, d = x.shape
    n_cond = mod_rows.shape[0]
    tm, tn = 1024, IN_TILE
    nj = IN_COLS // tn
    per_cond = n // n_cond
    out_specs = [pl.BlockSpec((tm, tn), lambda i, j: (i, jnp.maximum(j - n_f32, 0)))]
    out_shape = [jax.ShapeDtypeStruct((n, IN_COLS - n_f32 * tn), BF16)]
    if n_f32:
        out_specs.insert(0, pl.BlockSpec((tm, tn), lambda i, j: (i, jnp.minimum(j, n_f32 - 1))))
        out_shape.insert(0, jax.ShapeDtypeStruct((n, n_f32 * tn), F32))
    return pl.pallas_call(
        functools.partial(_inproj_body, n_f32=n_f32),
        grid=(n // tm, nj),
        in_specs=[pl.BlockSpec((tm, d), lambda i, j: (i, 0)),
                  pl.BlockSpec((1, 1, 6 * d), lambda i, j: (i * tm // per_cond, 0, 0)),
                  pl.BlockSpec((1, 1, d), lambda i, j: (layer, 0, 0)),
                  pl.BlockSpec((1, d, tn), lambda i, j: (layer, 0, jnp.where(i == 0, j, nj - 1)))],
        out_specs=out_specs,
        out_shape=out_shape,
        scratch_shapes=[pltpu.VMEM((tm, d), BF16), pltpu.VMEM((nj, d, tn), BF16)],
        compiler_params=_params(("arbitrary", "arbitrary"), 48),
        name="in_proj",
    )(x, mod_rows, norm_g.reshape(norm_g.shape[0], 1, d), w_in)


def _log_sigmoid(x):
    return jnp.minimum(x, 0.0) - jnp.log1p(jnp.exp(-jnp.abs(x)))


def _retention(z_ref, c0, decay_ref, ng_ref, s0_ref, o_ref, st_ref, tab_scr, d_scr, u_scr, s_scr, n_tok):
    C = RET_CHUNK
    nc = n_tok // C
    d = HEAD_DIM
    pos = lax.broadcasted_iota(jnp.int32, (C, d), 0).astype(F32)
    ri = lax.broadcasted_iota(jnp.int32, (C, C), 0).astype(F32)
    ci = lax.broadcasted_iota(jnp.int32, (C, C), 1).astype(F32)
    diff = ri - ci
    for h in range(N_HEADS_RET):
        lgf = _log_sigmoid(jnp.full((C, C), decay_ref[0, h], F32))
        lgb = _log_sigmoid(jnp.full((C, C), decay_ref[1, h], F32))
        d_scr[h] = (jnp.where(diff >= 0, jnp.exp(lgf * jnp.maximum(diff, 0.0)), 0.0)
                    + jnp.where(diff <= 0, jnp.exp(lgb * jnp.maximum(-diff, 0.0)), 0.0))
        lf = _log_sigmoid(jnp.full((C, d), decay_ref[0, h], F32))
        lb = _log_sigmoid(jnp.full((C, d), decay_ref[1, h], F32))
        hs = slice(h * d, (h + 1) * d)
        tab_scr[0, :, hs] = jnp.exp(lf * (C - 1.0 - pos))
        tab_scr[1, :, hs] = jnp.exp(lb * pos)
        tab_scr[2, :, hs] = jnp.exp(lf * (pos + 1.0))
        tab_scr[3, :, hs] = jnp.exp(lb * (C - pos))
        tab_scr[4, :, hs] = jnp.exp(lf * C)
        tab_scr[5, :, hs] = jnp.exp(lb * C)

    def rows(c):
        return pl.ds(pl.multiple_of(c * C, C), C)

    def part(p):
        return slice(c0 + p * WIDTH_RET, c0 + (p + 1) * WIDTH_RET)

    W = WIDTH_RET
    head_shift = d.bit_length() - 1
    assert d == 1 << head_shift
    same_head = ((lax.broadcasted_iota(jnp.int32, (W, W), 0) >> head_shift)
                 == (lax.broadcasted_iota(jnp.int32, (W, W), 1) >> head_shift))
    lane_head = lax.broadcasted_iota(jnp.int32, (C, W), 1) >> head_shift
    head_mean = jnp.where(same_head, 1.0 / d, 0.0).astype(BF16)

    def upd(c, carry):
        k = z_ref[rows(c), part(1)].astype(F32) * SCALE
        v = z_ref[rows(c), part(2)]
        u_scr[0, c] = jnp.where(same_head, _mm((k * tab_scr[0]).T, v), 0.0)
        u_scr[1, c] = jnp.where(same_head, _mm((k * tab_scr[1]).T, v), 0.0)
        return carry

    lax.fori_loop(0, nc, upd, 0)

    for direction in range(2):
        if s0_ref is None:
            s_init = jnp.zeros((W, W), F32)
        else:
            u_scr[direction, nc] = jnp.zeros((W, W), F32)
            for h in range(N_HEADS_RET):
                u_scr[direction, nc, h * d:(h + 1) * d, h * d:(h + 1) * d] = s0_ref[0, 0, direction, h]
            s_init = u_scr[direction, nc]
        chunk_decay = tab_scr[4 + direction, 0:1, :]

        def scan(t, s):
            c = t if direction == 0 else nc - 1 - t
            s_scr[direction, c] = s.astype(BF16)
            return chunk_decay * s + u_scr[direction, c]

        s_fin = lax.fori_loop(0, nc, scan, s_init)
        if st_ref is not None:
            for h in range(N_HEADS_RET):
                st_ref[0, direction, h] = s_fin[h * d:(h + 1) * d, h * d:(h + 1) * d]

    def out(c, carry):
        q = z_ref[rows(c), part(0)].astype(BF16)
        k = (z_ref[rows(c), part(1)].astype(F32) * SCALE).astype(BF16)
        v = z_ref[rows(c), part(2)].astype(BF16)
        gate = z_ref[rows(c), part(3)].astype(F32)
        o = (jnp.dot(q, s_scr[0, c], preferred_element_type=F32) * tab_scr[2]
             + jnp.dot(q, s_scr[1, c], preferred_element_type=F32) * tab_scr[3])
        intra = [_mm_nt(jnp.where(lane_head == h, q, jnp.zeros_like(q)), k) * d_scr[h]
                 for h in range(N_HEADS_RET)]
        for h in range(N_HEADS_RET):
            o = o + jnp.where(lane_head == h, _mm(intra[h], v), 0.0)
        sq = o * o
        sq_hi = sq.astype(BF16)
        sq_lo = (sq - sq_hi.astype(F32)).astype(BF16)
        ms = (jnp.dot(sq_hi, head_mean, preferred_element_type=F32)
              + jnp.dot(sq_lo, head_mean, preferred_element_type=F32))
        o = o * lax.rsqrt(ms + EPS) * ng_ref[...]
        o_ref[rows(c), :] = (o * _silu(gate)).astype(o_ref.dtype)
        return carry

    lax.fori_loop(0, nc, out, 0)


def _ret_scratch(n_tok):
    nc = n_tok // RET_CHUNK
    return [pltpu.VMEM((6, RET_CHUNK, WIDTH_RET), F32),
            pltpu.VMEM((N_HEADS_RET, RET_CHUNK, RET_CHUNK), F32),
            pltpu.VMEM((2, nc + 1, WIDTH_RET, WIDTH_RET), F32),
            pltpu.VMEM((2, nc, WIDTH_RET, WIDTH_RET), BF16)]


def _ctx_attn_body(sink_ref, decay_ref, z_ref, zr_ref, ng_ref, *refs, seq, n_prev):
    (ona_ref, oswa_ref, oret_ref, kna_ref, vna_ref, kswa_ref, vswa_ref, st_ref,
     tab_scr, d_scr, u_scr, s_scr) = refs[n_prev:]
    d = HEAD_DIM
    low_half = lax.broadcasted_iota(jnp.int32, (seq, 2 * d), 1) < d

    def pair_scores(q2, k2, head_is_low):
        return _mm_nt(jnp.where(low_half == head_is_low, q2, jnp.zeros_like(q2)), k2)

    jobs = []
    for pair in range(N_HEADS_NA // 2):
        cols = lambda part: slice(OFF_NA + part * WIDTH_NA + pair * 2 * d,
                                  OFF_NA + part * WIDTH_NA + (pair + 1) * 2 * d)
        q2 = (z_ref[:, cols(0)] * SCALE).astype(BF16)
        k2 = z_ref[:, cols(1)]
        v2 = z_ref[:, cols(2)]
        for hh in range(2):
            kna_ref[0, 0, 2 * pair + hh] = k2[:, hh * d:(hh + 1) * d]
            vna_ref[0, 0, 2 * pair + hh] = v2[:, hh * d:(hh + 1) * d]
        k2 = k2.astype(BF16)
        v2 = v2.astype(BF16)
        jobs.append([(pair_scores(q2, k2, hh == 0), v2, hh == 0, None) for hh in range(2)]
                    + [ona_ref, pair])
    lo = OFF_SWA + WIDTH_SWA
    k2 = z_ref[:, lo:lo + WIDTH_KV_SWA]
    v2 = z_ref[:, lo + WIDTH_KV_SWA:lo + 2 * WIDTH_KV_SWA]
    for g in range(N_KV_SWA):
        kswa_ref[0, 0, g] = k2[:, g * d:(g + 1) * d]
        vswa_ref[0, 0, g] = v2[:, g * d:(g + 1) * d]
    k2 = k2.astype(BF16)
    v2 = v2.astype(BF16)
    group = N_HEADS_SWA // N_KV_SWA
    for pair in range(N_HEADS_SWA // 2):
        q2 = (z_ref[:, OFF_SWA + pair * 2 * d:OFF_SWA + (pair + 1) * 2 * d] * SCALE).astype(BF16)
        heads = []
        for hh in range(2):
            h = 2 * pair + hh
            g = h // group
            qh = q2[:, hh * d:(hh + 1) * d]
            q_at_g = jnp.concatenate([qh, qh], axis=-1) if g != hh else q2
            heads.append((pair_scores(q_at_g, k2, g == 0), v2, g == 0, sink_ref[h]))
        jobs.append(heads + [oswa_ref, pair])
    probs = []
    for job in jobs:
        for s, v2, v_is_low, sink in job[:2]:
            m = jnp.max(s, axis=-1, keepdims=True)
            if sink is not None:
                m = jnp.maximum(m, sink)
            e = jnp.exp(s - m)
            den = jnp.sum(e, axis=-1, keepdims=True)
            if sink is not None:
                den = den + jnp.exp(sink - m)
            probs.append((e, den))
    for j, job in enumerate(jobs):
        halves = []
        for hh, (s, v2, v_is_low, sink) in enumerate(job[:2]):
            e, den = probs[2 * j + hh]
            o2 = _mm(e, v2) / den
            halves.append(o2[:, :d] if v_is_low else o2[:, d:])
        o_ref, pair = job[2], job[3]
        o_ref[:, pair * 2 * d:(pair + 1) * 2 * d] = jnp.concatenate(halves, axis=-1).astype(o_ref.dtype)
    _retention(zr_ref, 0, decay_ref, ng_ref, None, oret_ref, st_ref.at[0],
               tab_scr, d_scr, u_scr, s_scr, seq)


def _ctx_mixers(z_att, z_rest, swa_sink, ret_decay, ret_norm_g, caches, layer, depth, batch, seq):
    n = batch * seq
    assert z_att.shape[1] == OFF_RET
    smem = pl.BlockSpec(memory_space=pltpu.SMEM)
    kv = lambda heads: pl.BlockSpec((1, 1, heads, seq, HEAD_DIM), lambda b: (b, layer, 0, 0, 0))
    kv_shape = lambda heads: jax.ShapeDtypeStruct((batch, depth, heads, seq, HEAD_DIM), F32)
    in_specs = [smem, smem,
                pl.BlockSpec((seq, OFF_RET), lambda b: (b, 0)),
                pl.BlockSpec((seq, 4 * WIDTH_RET), lambda b: (b, 0)),
                pl.BlockSpec((1, WIDTH_RET), lambda b: (0, 0))]
    args = [swa_sink, ret_decay, z_att, z_rest, ret_norm_g.reshape(1, WIDTH_RET)]
    n_prev = 0 if caches is None else len(caches)
    aliases = {len(args) + i: 3 + i for i in range(n_prev)}
    if caches is not None:
        in_specs += [pl.BlockSpec(memory_space=pl.ANY)] * n_prev
        args += list(caches)
    return pl.pallas_call(
        functools.partial(_ctx_attn_body, seq=seq, n_prev=n_prev),
        grid=(batch,),
        in_specs=in_specs,
        out_specs=[pl.BlockSpec((seq, WIDTH_NA), lambda b: (b, 0)),
                   pl.BlockSpec((seq, WIDTH_SWA), lambda b: (b, 0)),
                   pl.BlockSpec((seq, WIDTH_RET), lambda b: (b, 0)),
                   kv(N_HEADS_NA), kv(N_HEADS_NA), kv(N_KV_SWA), kv(N_KV_SWA),
                   pl.BlockSpec((1, 1, 2, N_HEADS_RET, HEAD_DIM, HEAD_DIM),
                                lambda b: (b, layer, 0, 0, 0, 0))],
        out_shape=[jax.ShapeDtypeStruct((n, WIDTH_NA), BF16),
                   jax.ShapeDtypeStruct((n, WIDTH_SWA), BF16),
                   jax.ShapeDtypeStruct((n, WIDTH_RET), BF16),
                   kv_shape(N_HEADS_NA), kv_shape(N_HEADS_NA), kv_shape(N_KV_SWA), kv_shape(N_KV_SWA),
                   jax.ShapeDtypeStruct((batch, depth, 2, N_HEADS_RET, HEAD_DIM, HEAD_DIM), F32)],
        scratch_shapes=_ret_scratch(seq),
        input_output_aliases=aliases,
        compiler_params=_params(("arbitrary",), 32),
        name="ctx_mixers",
    )(*args)


NA_QROWS = 4
NA_WROWS = NA_QROWS + WIN_ROWS


def _na_bias_table(rpb, rows):
    n_groups = rows // NA_QROWS
    assert rows % NA_QROWS == 0 and n_groups >= 3
    n_heads, n_ro, n_co = rpb.shape
    half = WIN_ROWS // 2
    cc = np.arange(GRID_W)
    col_start = np.clip(cc - WIN_COLS // 2, 0, GRID_W - WIN_COLS)
    valid = (cc[None, :] >= col_start[:, None]) & (cc[None, :] < col_start[:, None] + WIN_COLS)
    span = 2 * GRID_W - 1
    lead = GRID_W - WIN_COLS
    ext = jnp.pad(rpb.astype(F32), ((0, 0), (0, 0), (lead, span + 1 - lead - n_co)))
    band = jnp.tile(ext, (1, 1, GRID_W))[..., :GRID_W * span].reshape(n_heads, n_ro, GRID_W, span)
    band = jnp.where(valid[None, None], band[..., GRID_W - 1:], MASKED)
    outside = jnp.full((n_heads, GRID_W, GRID_W), MASKED, F32)
    tabs = []
    for g in (0, 1, n_groups - 1):
        ws = min(max(NA_QROWS * g - half, 0), rows - NA_WROWS)
        q_blocks = []
        for i in range(NA_QROWS):
            r = NA_QROWS * g + i
            rs = min(max(r - half, 0), rows - WIN_ROWS)
            q_blocks.append(jnp.concatenate(
                [band[:, ws + j - r + WIN_ROWS - 1] if rs <= ws + j < rs + WIN_ROWS else outside
                 for j in range(NA_WROWS)], axis=-1))
        tabs.append(jnp.concatenate(q_blocks, axis=-2))
    return jnp.stack(tabs, axis=0)


def _lat_na_body(z_ref, kc_ref, vc_ref, bias_ref, o_ref, kc_scr, vc_scr, *, rows):
    d = HEAD_DIM
    nq = NA_QROWS * GRID_W
    g = pl.program_id(1)

    @pl.when(g == 0)
    def _():
        kc_scr[...] = kc_ref[0, 0].astype(BF16)
        vc_scr[...] = vc_ref[0, 0].astype(BF16)

    ws = jnp.clip(NA_QROWS * g - WIN_ROWS // 2, 0, rows - NA_WROWS)
    qrows = pl.ds(pl.multiple_of(g * nq, nq), nq)
    wrows = pl.ds(pl.multiple_of(ws * GRID_W, GRID_W), NA_WROWS * GRID_W)
    low_half = lax.broadcasted_iota(jnp.int32, (nq, 2 * d), 1) < d
    for pair in range(N_HEADS_NA // 2):
        cols = slice(pair * 2 * d, (pair + 1) * 2 * d)
        q2 = (z_ref[qrows, cols] * SCALE).astype(BF16)
        kl2 = z_ref[wrows, WIDTH_NA + pair * 2 * d:WIDTH_NA + (pair + 1) * 2 * d].astype(BF16)
        vl2 = z_ref[wrows, 2 * WIDTH_NA + pair * 2 * d:2 * WIDTH_NA + (pair + 1) * 2 * d].astype(BF16)
        for hh in range(2):
            h = 2 * pair + hh
            qm = jnp.where(low_half == (hh == 0), q2, jnp.zeros_like(q2))
            s_loc = _mm_nt(qm, kl2) + bias_ref[0, h]
            s_ctx = _mm_nt(q2[:, hh * d:(hh + 1) * d], kc_scr[h])
            m = jnp.maximum(jnp.max(s_ctx, axis=-1, keepdims=True), jnp.max(s_loc, axis=-1, keepdims=True))
            e_ctx = jnp.exp(s_ctx - m)
            e_loc = jnp.exp(s_loc - m)
            den = jnp.sum(e_ctx, axis=-1, keepdims=True) + jnp.sum(e_loc, axis=-1, keepdims=True)
            o = (_mm(e_ctx, vc_scr[h]) + _mm(e_loc, vl2)[:, hh * d:(hh + 1) * d]) / den
            o_ref[:, h * d:(h + 1) * d] = o.astype(o_ref.dtype)


def _lat_na(z, cache_k, cache_v, bias_tab, layer, batch, seq):
    rows = seq // GRID_W
    n_groups = rows // NA_QROWS
    nq = NA_QROWS * GRID_W
    past = cache_k.shape[3]

    def group_class(g):
        return jnp.where(g == 0, 0, jnp.where(g == n_groups - 1, 2, 1))

    cache = pl.BlockSpec((1, 1, N_HEADS_NA, past, HEAD_DIM), lambda b, g: (b, layer, 0, 0, 0))
    return pl.pallas_call(
        functools.partial(_lat_na_body, rows=rows),
        grid=(batch, n_groups),
        in_specs=[pl.BlockSpec((seq, 3 * WIDTH_NA), lambda b, g: (b, 0)),
                  cache, cache,
                  pl.BlockSpec((1, N_HEADS_NA, nq, NA_WROWS * GRID_W),
                               lambda b, g: (group_class(g), 0, 0, 0))],
        out_specs=pl.BlockSpec((nq, WIDTH_NA), lambda b, g: (b * n_groups + g, 0)),
        out_shape=jax.ShapeDtypeStruct((batch * seq, WIDTH_NA), BF16),
        scratch_shapes=[pltpu.VMEM((N_HEADS_NA, past, HEAD_DIM), BF16),
                        pltpu.VMEM((N_HEADS_NA, past, HEAD_DIM), BF16)],
        compiler_params=_params(("arbitrary", "arbitrary"), 58),
        name="lat_na",
    )(z, cache_k, cache_v, bias_tab)


def _rope_tables(seq):
    t = jnp.arange(seq)
    n_freq = HEAD_DIM // 4
    inv = ROPE_BASE ** (-jnp.arange(n_freq, dtype=F32) / n_freq)
    ang = jnp.concatenate([(t // GRID_W).astype(F32)[:, None] * inv,
                           (t % GRID_W).astype(F32)[:, None] * inv], axis=-1)
    cos = jnp.repeat(jnp.cos(ang), 2, axis=-1)
    sign = jnp.tile(jnp.asarray([-1.0, 1.0], F32), HEAD_DIM // 2)
    sin = jnp.repeat(jnp.sin(ang), 2, axis=-1) * sign
    return jnp.tile(cos, (1, N_HEADS_SWA)), jnp.tile(sin, (1, N_HEADS_SWA))


def _rope(x, cos, sin_signed):
    width = x.shape[-1]
    lane = lax.broadcasted_iota(jnp.int32, x.shape, x.ndim - 1)
    partner = jnp.where((lane & 1) == 0, pltpu.roll(x, width - 1, x.ndim - 1), pltpu.roll(x, 1, x.ndim - 1))
    return x * cos + partner * sin_signed


def _lat_swa_body(sink_ref, z_ref, kc_ref, vc_ref, cos_ref, sin_ref, o_ref, kr_scr, kc_scr, vc_scr, *, seq):
    d = HEAD_DIM
    blk = SWA_BLOCK
    n = pl.program_id(1)
    assert N_KV_SWA == 2

    @pl.when(n == 0)
    def _():
        k = z_ref[:, WIDTH_SWA:WIDTH_SWA + WIDTH_KV_SWA].astype(F32)
        kr_scr[...] = _rope(k, cos_ref[:, :WIDTH_KV_SWA], sin_ref[:, :WIDTH_KV_SWA]).astype(BF16)
        kc_scr[...] = jnp.concatenate([kc_ref[0, 0, 0], kc_ref[0, 0, 1]], axis=-1).astype(BF16)
        vc_scr[...] = jnp.concatenate([vc_ref[0, 0, 0], vc_ref[0, 0, 1]], axis=-1).astype(BF16)

    qrows = pl.ds(pl.multiple_of(n * blk, blk), blk)
    kstart = jnp.clip((n - 1) * blk, 0, seq - 3 * blk)
    krows = pl.ds(pl.multiple_of(kstart, blk), 3 * blk)
    q_all = _rope(z_ref[qrows, 0:WIDTH_SWA].astype(F32), cos_ref[qrows, :], sin_ref[qrows, :]) * SCALE
    q_all = q_all.astype(BF16)
    qpos = n * blk + lax.broadcasted_iota(jnp.int32, (blk, 3 * blk), 0)
    kpos = kstart + lax.broadcasted_iota(jnp.int32, (blk, 3 * blk), 1)
    valid = jnp.abs(qpos - kpos) <= SWA_RADIUS
    kl2 = kr_scr[krows, :]
    vl2 = z_ref[krows, WIDTH_SWA + WIDTH_KV_SWA:WIDTH_SWA + 2 * WIDTH_KV_SWA]
    group = N_HEADS_SWA // N_KV_SWA
    zero = jnp.zeros((blk, d), BF16)
    scores = []
    for h in range(N_HEADS_SWA):
        g = h // group
        qh = q_all[:, h * d:(h + 1) * d]
        q_at_g = jnp.concatenate([qh, zero] if g == 0 else [zero, qh], axis=-1)
        scores.append((jnp.where(valid, _mm_nt(q_at_g, kl2), MASKED), _mm_nt(q_at_g, kc_scr[...])))
    probs = []
    for h, (s_loc, s_ctx) in enumerate(scores):
        m = jnp.maximum(jnp.maximum(jnp.max(s_loc, axis=-1, keepdims=True),
                                    jnp.max(s_ctx, axis=-1, keepdims=True)), sink_ref[h])
        e_loc = jnp.exp(s_loc - m)
        e_ctx = jnp.exp(s_ctx - m)
        den = (jnp.sum(e_loc, axis=-1, keepdims=True) + jnp.sum(e_ctx, axis=-1, keepdims=True)
               + jnp.exp(sink_ref[h] - m))
        probs.append((e_loc, e_ctx, den))
    outs = []
    for h, (e_loc, e_ctx, den) in enumerate(probs):
        g = h // group
        o2 = (_mm(e_ctx, vc_scr[...]) + _mm(e_loc, vl2)) / den
        outs.append(o2[:, g * d:(g + 1) * d])
    o_ref[...] = jnp.concatenate(outs, axis=-1).astype(o_ref.dtype)


def _lat_swa(z, cache_k, cache_v, swa_sink, cos, sin, layer, batch, seq):
    nb = seq // SWA_BLOCK
    past = cache_k.shape[3]
    width = WIDTH_SWA + 2 * WIDTH_KV_SWA
    assert OFF_SWA % width == 0 and seq >= 3 * SWA_BLOCK
    cache = pl.BlockSpec((1, 1, N_KV_SWA, past, HEAD_DIM), lambda b, n: (b, layer, 0, 0, 0))
    table = pl.BlockSpec((seq, WIDTH_SWA), lambda b, n: (0, 0))
    return pl.pallas_call(
        functools.partial(_lat_swa_body, seq=seq),
        grid=(batch, nb),
        in_specs=[pl.BlockSpec(memory_space=pltpu.SMEM),
                  pl.BlockSpec((seq, width), lambda b, n: (b, OFF_SWA // width)),
                  cache, cache, table, table],
        out_specs=pl.BlockSpec((SWA_BLOCK, WIDTH_SWA), lambda b, n: (b * nb + n, 0)),
        out_shape=jax.ShapeDtypeStruct((batch * seq, WIDTH_SWA), BF16),
        scratch_shapes=[pltpu.VMEM((seq, WIDTH_KV_SWA), BF16),
                        pltpu.VMEM((past, WIDTH_KV_SWA), BF16), pltpu.VMEM((past, WIDTH_KV_SWA), BF16)],
        compiler_params=_params(("arbitrary", "arbitrary"), 32),
        name="lat_swa",
    )(swa_sink, z, cache_k, cache_v, cos, sin)


def _lat_ret_body(decay_ref, z_ref, ng_ref, s0_ref, o_ref, tab_scr, d_scr, u_scr, s_scr, *, seq):
    _retention(z_ref, 0, decay_ref, ng_ref, s0_ref, o_ref, None, tab_scr, d_scr, u_scr, s_scr, seq)


def _lat_ret(z, ret_decay, ret_norm_g, state, layer, batch, seq):
    width = 4 * WIDTH_RET
    assert OFF_RET % width == 0
    return pl.pallas_call(
        functools.partial(_lat_ret_body, seq=seq),
        grid=(batch,),
        in_specs=[pl.BlockSpec(memory_space=pltpu.SMEM),
                  pl.BlockSpec((seq, width), lambda b: (b, OFF_RET // width)),
                  pl.BlockSpec((1, WIDTH_RET), lambda b: (0, 0)),
                  pl.BlockSpec((1, 1, 2, N_HEADS_RET, HEAD_DIM, HEAD_DIM),
                               lambda b: (b, layer, 0, 0, 0, 0))],
        out_specs=pl.BlockSpec((seq, WIDTH_RET), lambda b: (b, 0)),
        out_shape=jax.ShapeDtypeStruct((batch * seq, WIDTH_RET), BF16),
        scratch_shapes=_ret_scratch(seq),
        compiler_params=_params(("arbitrary",), 40),
        name="lat_ret",
    )(ret_decay, z, ret_norm_g.reshape(1, WIDTH_RET), state)


def _merge_body(ona_ref, oswa_ref, oret_ref, ga_ref, gs_ref, gr_ref, x_ref, mod_ref, g2_ref,
                wna_ref, wswa_ref, wret_ref, wout_ref, rw_ref, rb_ref,
                x1_ref, h2_ref, comb_ref, wna_scr, wswa_scr, wret_scr, wout_scr, rw_scr):
    @pl.when(pl.program_id(0) == 0)
    def _():
        wna_scr[...] = wna_ref[0].astype(BF16)
        wswa_scr[...] = wswa_ref[0].astype(BF16)
        wret_scr[...] = wret_ref[0].astype(BF16)
        wout_scr[...] = wout_ref[0].astype(BF16)
        rw = rw_ref[0]
        rw_hi = rw.astype(BF16)
        rw_scr[0] = rw_hi
        rw_scr[1] = (rw - rw_hi.astype(F32)).astype(BF16)

    D = D_MODEL
    m = mod_ref[0]
    gate = lambda ref: _sigmoid(ref[...].astype(F32))
    z = (gate(ga_ref) * jnp.dot(ona_ref[...], wna_scr[...], preferred_element_type=F32)
         + gate(gs_ref) * jnp.dot(oswa_ref[...], wswa_scr[...], preferred_element_type=F32)
         + gate(gr_ref) * jnp.dot(oret_ref[...], wret_scr[...], preferred_element_type=F32))
    y = jnp.dot(z.astype(BF16), wout_scr[...], preferred_element_type=F32)
    x1 = x_ref[...] + m[:, 2 * D:3 * D] * y
    x1_ref[...] = x1
    h2 = x1 * lax.rsqrt(jnp.mean(x1 * x1, axis=-1, keepdims=True) + EPS) * g2_ref[0]
    h2 = h2 * (1.0 + m[:, 4 * D:5 * D]) + m[:, 3 * D:4 * D]
    h2_ref[...] = h2.astype(BF16)

    h_hi = h2.astype(BF16)
    h_lo = (h2 - h_hi.astype(F32)).astype(BF16)
    logit = (jnp.dot(h_hi, rw_scr[0], preferred_element_type=F32)
             + jnp.dot(h_hi, rw_scr[1], preferred_element_type=F32)
             + jnp.dot(h_lo, rw_scr[0], preferred_element_type=F32)) + rb_ref[0]
    lane = lax.broadcasted_iota(jnp.int32, logit.shape, 1).astype(F32)
    neg = jnp.float32(-jnp.inf)
    big = jnp.float32(ROUTER_LANES)
    is_group = lane < N_GROUPS
    is_expert = (lane >= N_GROUPS) & (lane < N_GROUPS + N_EXPERTS)
    gl = jnp.where(is_group, logit, neg)
    gmax = jnp.max(gl, axis=-1, keepdims=True)
    gsel = jnp.min(jnp.where(gl == gmax, lane, big), axis=-1, keepdims=True)
    p_group = 1.0 / jnp.sum(jnp.where(is_group, jnp.exp(gl - gmax), 0.0), axis=-1, keepdims=True)
    eidx = lane - N_GROUPS
    in_group = is_expert & (jnp.floor(eidx / EXPERTS_PER_GROUP) == gsel)
    el = jnp.where(in_group, logit, neg)
    top1 = jnp.max(el, axis=-1, keepdims=True)
    sel1 = jnp.min(jnp.where(el == top1, lane, big), axis=-1, keepdims=True)
    el2 = jnp.where(lane == sel1, neg, el)
    top2 = jnp.max(el2, axis=-1, keepdims=True)
    sel2 = jnp.min(jnp.where(el2 == top2, lane, big), axis=-1, keepdims=True)
    e2 = jnp.exp(top2 - top1)
    w1 = p_group / (1.0 + e2)
    w2 = p_group * e2 / (1.0 + e2)
    comb = jnp.where(lane == sel1, w1, 0.0) + jnp.where(lane == sel2, w2, 0.0)
    comb_ref[...] = comb[:, N_GROUPS:N_GROUPS + N_EXPERTS]


def _merge(o_na, o_swa, o_ret, z, gate_tile, x, mod_rows, norm2_g, w_br_na, w_br_swa, w_br_ret, w_out,
           router_w, router_b, layer):
    n, d = x.shape
    n_cond = mod_rows.shape[0]
    per_cond = n // n_cond
    tm = 512
    gate = lambda k: pl.BlockSpec((tm, d), lambda i: (i, gate_tile + k))
    row = lambda w: pl.BlockSpec((tm, w), lambda i: (i, 0))
    weight = lambda k: pl.BlockSpec((1, k, d), lambda i: (layer, 0, 0))
    return pl.pallas_call(
        _merge_body,
        grid=(n // tm,),
        in_specs=[row(WIDTH_NA), row(WIDTH_SWA), row(WIDTH_RET),
                  gate(0), gate(1), gate(2),
                  row(d),
                  pl.BlockSpec((1, 1, 6 * d), lambda i: (i * tm // per_cond, 0, 0)),
                  pl.BlockSpec((1, 1, d), lambda i: (layer, 0, 0)),
                  weight(WIDTH_NA), weight(WIDTH_SWA), weight(WIDTH_RET), weight(d),
                  pl.BlockSpec((1, d, ROUTER_LANES), lambda i: (layer, 0, 0)),
                  pl.BlockSpec((1, 1, ROUTER_LANES), lambda i: (layer, 0, 0))],
        out_specs=[row(d), row(d), row(N_EXPERTS)],
        out_shape=[jax.ShapeDtypeStruct((n, d), F32),
                   jax.ShapeDtypeStruct((n, d), BF16),
                   jax.ShapeDtypeStruct((n, N_EXPERTS), F32)],
        scratch_shapes=[pltpu.VMEM((WIDTH_NA, d), BF16), pltpu.VMEM((WIDTH_SWA, d), BF16),
                        pltpu.VMEM((WIDTH_RET, d), BF16), pltpu.VMEM((d, d), BF16),
                        pltpu.VMEM((2, d, ROUTER_LANES), BF16)],
        compiler_params=_params(("arbitrary",), 56),
        name="merge_router",
    )(o_na, o_swa, o_ret, z, z, z, x, mod_rows, norm2_g.reshape(norm2_g.shape[0], 1, d),
      w_br_na, w_br_swa, w_br_ret, w_out, router_w, router_b)


def _moe_body(h_ref, comb_ref, x1_ref, mod_ref, wg_ref, wu_ref, wd_ref, fg_ref, o_ref, *, final):
    g = pl.program_id(1)
    h = h_ref[...]
    comb = comb_ref[...]
    lane = lax.broadcasted_iota(jnp.int32, comb.shape, 1)
    y = None
    for e in range(EXPERTS_PER_GROUP):
        hg = jnp.dot(h, wg_ref[0, e].astype(BF16), preferred_element_type=F32)
        hu = jnp.dot(h, wu_ref[0, e].astype(BF16), preferred_element_type=F32)
        c = jnp.sum(jnp.where(lane == g * EXPERTS_PER_GROUP + e, comb, 0.0), axis=-1, keepdims=True)
        a = _silu(hg) * hu * c
        t = jnp.dot(a.astype(BF16), wd_ref[0, e].astype(BF16), preferred_element_type=F32)
        y = t if y is None else y + t

    @pl.when(g == 0)
    def _():
        o_ref[...] = y

    @pl.when(g > 0)
    def _():
        o_ref[...] += y

    @pl.when(g == N_GROUPS - 1)
    def _():
        x2 = x1_ref[...] + mod_ref[0][:, 5 * D_MODEL:6 * D_MODEL] * o_ref[...]
        if final:
            x2 = x2 * lax.rsqrt(jnp.mean(x2 * x2, axis=-1, keepdims=True) + EPS) * fg_ref[...]
        o_ref[...] = x2


def _moe(h2, comb, x1, mod_rows, w_gate, w_up, w_down, final_g, layer, final):
    n, d = x1.shape
    n_cond = mod_rows.shape[0]
    per_cond = n // n_cond
    tm = 1024
    return pl.pallas_call(
        functools.partial(_moe_body, final=final),
        grid=(n // tm, N_GROUPS),
        in_specs=[pl.BlockSpec((tm, d), lambda i, g: (i, 0)),
                  pl.BlockSpec((tm, N_EXPERTS), lambda i, g: (i, 0)),
                  pl.BlockSpec((tm, d), lambda i, g: (i, 0)),
                  pl.BlockSpec((1, 1, 6 * d), lambda i, g: (i * tm // per_cond, 0, 0)),
                  pl.BlockSpec((1, EXPERTS_PER_GROUP, d, D_EXPERT), lambda i, g: (layer, g, 0, 0)),
                  pl.BlockSpec((1, EXPERTS_PER_GROUP, d, D_EXPERT), lambda i, g: (layer, g, 0, 0)),
                  pl.BlockSpec((1, EXPERTS_PER_GROUP, D_EXPERT, d), lambda i, g: (layer, g, 0, 0)),
                  pl.BlockSpec((1, d), lambda i, g: (0, 0))],
        out_specs=pl.BlockSpec((tm, d), lambda i, g: (i, 0)),
        out_shape=jax.ShapeDtypeStruct((n, d), F32),
        compiler_params=_params(("arbitrary", "arbitrary"), 56),
        name="moe",
    )(h2, comb, x1, mod_rows, w_gate, w_up, w_down, final_g.reshape(1, d))


def kernel(x_prompt, x_sample, c, cache_na_k, cache_na_v, cache_swa_k, cache_swa_v, state_ret, c_ctx, ada_w, ada_b, norm1_g, norm2_g, w_in, na_rpb, swa_sink, ret_decay, ret_norm_g, w_br_na, w_br_swa, w_br_ret, w_out, router_group_w, router_group_b, router_expert_w, router_expert_b, w_gate, w_up, w_down, final_norm_g):
    batch, seq, d = x_prompt.shape
    dec_batch, dec_seq, _ = x_sample.shape
    depth = ada_w.shape[0]
    assert d == D_MODEL and dec_batch + 1 <= 8
    assert OFF_RET % IN_TILE == 0 and OFF_GATES % d == 0 and (OFF_GATES - OFF_RET) % d == 0

    cond = jnp.concatenate([c_ctx[None], c, jnp.zeros((8 - 1 - dec_batch, d), F32)], axis=0)
    mod = _modulation(cond, ada_w, ada_b)
    router_w = jnp.concatenate([router_group_w, router_expert_w], axis=-1)
    router_w = jnp.pad(router_w, ((0, 0), (0, 0), (0, ROUTER_LANES - router_w.shape[-1])))
    router_b = jnp.concatenate([router_group_b, router_expert_b], axis=-1)
    router_b = jnp.pad(router_b, ((0, 0), (0, ROUTER_LANES - router_b.shape[-1])))[:, None, :]
    cos, sin = _rope_tables(dec_seq)

    xp = x_prompt.reshape(batch * seq, d)
    xs = x_sample.reshape(dec_batch * dec_seq, d)
    caches = None
    for l in range(depth):
        final = l == depth - 1
        mod_ctx = mod[l, 0:1][:, None, :]
        mod_lat = mod[l, 1:1 + dec_batch][:, None, :]
        merge_w = (norm2_g, w_br_na, w_br_swa, w_br_ret, w_out, router_w, router_b, l)
        moe_w = (w_gate, w_up, w_down, final_norm_g, l, final)

        z_att, z_rest = _in_proj(xp, mod_ctx, norm1_g, w_in, l, OFF_RET // IN_TILE)
        o_na, o_swa, o_ret, *caches = _ctx_mixers(
            z_att, z_rest, swa_sink[l], ret_decay[l], ret_norm_g[l], caches, l, depth, batch, seq)
        x1, h2, comb = _merge(o_na, o_swa, o_ret, z_rest, (OFF_GATES - OFF_RET) // d, xp, mod_ctx, *merge_w)
        xp = _moe(h2, comb, x1, mod_ctx, *moe_w)

        z, = _in_proj(xs, mod_lat, norm1_g, w_in, l, 0)
        o_na = _lat_na(z, cache_na_k, cache_na_v, _na_bias_table(na_rpb[l], dec_seq // GRID_W),
                       l, dec_batch, dec_seq)
        o_swa = _lat_swa(z, cache_swa_k, cache_swa_v, swa_sink[l], cos, sin, l, dec_batch, dec_seq)
        o_ret = _lat_ret(z, ret_decay[l], ret_norm_g[l], state_ret, l, dec_batch, dec_seq)
        x1, h2, comb = _merge(o_na, o_swa, o_ret, z, OFF_GATES // d, xs, mod_lat, *merge_w)
        xs = _moe(h2, comb, x1, mod_lat, *moe_w)

    return (xp.reshape(batch, seq, d), xs.reshape(dec_batch, dec_seq, d), *caches)
```

```python
import functools

import numpy as np
import jax
import jax.numpy as jnp
from jax import lax
from jax.experimental import pallas as pl
from jax.experimental.pallas import tpu as pltpu

F32 = jnp.float32
BF16 = jnp.bfloat16

D_MODEL = 1024
HEAD_DIM = 64
GRID_W = 64
N_HEADS_NA = 8
WIN_ROWS = 8
WIN_COLS = 16
N_HEADS_SWA = 4
N_KV_SWA = 2
SWA_RADIUS = 128
SWA_BLOCK = 128
N_HEADS_RET = 4
RET_CHUNK = 128
WIDTH_NA = N_HEADS_NA * HEAD_DIM
WIDTH_SWA = N_HEADS_SWA * HEAD_DIM
WIDTH_KV_SWA = N_KV_SWA * HEAD_DIM
WIDTH_RET = N_HEADS_RET * HEAD_DIM
IN_COLS = 3 * WIDTH_NA + WIDTH_SWA + 2 * WIDTH_KV_SWA + 4 * WIDTH_RET + 3 * D_MODEL
OFF_NA = 0
OFF_SWA = 3 * WIDTH_NA
OFF_RET = OFF_SWA + WIDTH_SWA + 2 * WIDTH_KV_SWA
OFF_GATES = OFF_RET + 4 * WIDTH_RET
N_GROUPS = 4
EXPERTS_PER_GROUP = 4
N_EXPERTS = N_GROUPS * EXPERTS_PER_GROUP
D_EXPERT = D_MODEL // 4
ROPE_BASE = 10000.0
EPS = 1e-6
SCALE = HEAD_DIM ** -0.5
MASKED = -1e30
ROUTER_LANES = 128
ROUTE_GROUP_LANE = 0
ROUTE_EXPERT_LANE = N_GROUPS

MIB = 1024 * 1024


def _mm(a, b):
    return jnp.dot(a.astype(BF16), b.astype(BF16), preferred_element_type=F32)


def _mm_nt(a, b):
    return lax.dot_general(a.astype(BF16), b.astype(BF16), (((1,), (1,)), ((), ())),
                           preferred_element_type=F32)


def _sigmoid(x):
    return 1.0 / (1.0 + jnp.exp(-x))


def _silu(x):
    return x * _sigmoid(x)


def _params(semantics, vmem_mib):
    return pltpu.CompilerParams(dimension_semantics=semantics, vmem_limit_bytes=vmem_mib * MIB)


def _mod_body(cond_ref, w_ref, b_ref, o_ref):
    o_ref[0] = _mm(_silu(cond_ref[...]), w_ref[0]) + b_ref[0]


def _modulation(cond, ada_w, ada_b):
    depth, d, n = ada_w.shape
    tn = 1024
    return pl.pallas_call(
        _mod_body,
        grid=(depth, n // tn),
        in_specs=[pl.BlockSpec((8, d), lambda l, j: (0, 0)),
                  pl.BlockSpec((1, d, tn), lambda l, j: (l, 0, j)),
                  pl.BlockSpec((1, 1, tn), lambda l, j: (l, 0, j))],
        out_specs=pl.BlockSpec((1, 8, tn), lambda l, j: (l, 0, j)),
        out_shape=jax.ShapeDtypeStruct((depth, 8, n), F32),
        compiler_params=_params(("arbitrary", "arbitrary"), 32),
        name="modulation",
    )(cond, ada_w, ada_b.reshape(depth, 1, n))


def _inproj_body(x_ref, mod_ref, g_ref, w_ref, *refs, n_f32):
    zf_ref = refs[0] if n_f32 else None
    zb_ref, h_scr, w_scr = refs[-3:]
    i = pl.program_id(0)
    j = pl.program_id(1)

    @pl.when(i == 0)
    def _():
        w_scr[j] = w_ref[0].astype(BF16)

    @pl.when(j == 0)
    def _():
        x = x_ref[...]
        m = mod_ref[0]
        y = x * lax.rsqrt(jnp.mean(x * x, axis=-1, keepdims=True) + EPS) * g_ref[0]
        h_scr[...] = (y * (1.0 + m[:, D_MODEL:2 * D_MODEL]) + m[:, 0:D_MODEL]).astype(BF16)

    acc = jnp.dot(h_scr[...], w_scr[j], preferred_element_type=F32)
    if n_f32 == 0:
        zb_ref[...] = acc.astype(BF16)
    else:
        @pl.when(j < n_f32)
        def _():
            zf_ref[...] = acc

        @pl.when(j >= n_f32)
        def _():
            zb_ref[...] = acc.astype(BF16)


IN_TILE = 1024


def _in_proj(x, mod_rows, norm_g, w_in, layer, n_f32):
    n, d = x.shape
    n_cond = mod_rows.shape[0]
    tm, tn = 1024, IN_TILE
    nj = IN_COLS // tn
    per_cond = n // n_cond
    out_specs = [pl.BlockSpec((tm, tn), lambda i, j: (i, jnp.maximum(j - n_f32, 0)))]
    out_shape = [jax.ShapeDtypeStruct((n, IN_COLS - n_f32 * tn), BF16)]
    if n_f32:
        out_specs.insert(0, pl.BlockSpec((tm, tn), lambda i, j: (i, jnp.minimum(j, n_f32 - 1))))
        out_shape.insert(0, jax.ShapeDtypeStruct((n, n_f32 * tn), F32))
    return pl.pallas_call(
        functools.partial(_inproj_body, n_f32=n_f32),
        grid=(n // tm, nj),
        in_specs=[pl.BlockSpec((tm, d), lambda i, j: (i, 0)),
                  pl.BlockSpec((1, 1, 6 * d), lambda i, j: (i * tm // per_cond, 0, 0)),
                  pl.BlockSpec((1, 1, d), lambda i, j: (layer, 0, 0)),
                  pl.BlockSpec((1, d, tn), lambda i, j: (layer, 0, jnp.where(i == 0, j, nj - 1)))],
        out_specs=out_specs,
        out_shape=out_shape,
        scratch_shapes=[pltpu.VMEM((tm, d), BF16), pltpu.VMEM((nj, d, tn), BF16)],
        compiler_params=_params(("arbitrary", "arbitrary"), 48),
        name="in_proj",
    )(x, mod_rows, norm_g.reshape(norm_g.shape[0], 1, d), w_in)


def _log_sigmoid(x):
    return jnp.minimum(x, 0.0) - jnp.log1p(jnp.exp(-jnp.abs(x)))


def _retention(z_ref, c0, decay_ref, ng_ref, s0_ref, o_ref, st_ref, tab_scr, d_scr, u_scr, s_scr, n_tok):
    C = RET_CHUNK
    nc = n_tok // C
    d = HEAD_DIM
    pos = lax.broadcasted_iota(jnp.int32, (C, d), 0).astype(F32)
    ri = lax.broadcasted_iota(jnp.int32, (C, C), 0).astype(F32)
    ci = lax.broadcasted_iota(jnp.int32, (C, C), 1).astype(F32)
    diff = ri - ci
    for h in range(N_HEADS_RET):
        lgf = _log_sigmoid(jnp.full((C, C), decay_ref[0, h], F32))
        lgb = _log_sigmoid(jnp.full((C, C), decay_ref[1, h], F32))
        d_scr[h] = (jnp.where(diff >= 0, jnp.exp(lgf * jnp.maximum(diff, 0.0)), 0.0)
                    + jnp.where(diff <= 0, jnp.exp(lgb * jnp.maximum(-diff, 0.0)), 0.0))
        lf = _log_sigmoid(jnp.full((C, d), decay_ref[0, h], F32))
        lb = _log_sigmoid(jnp.full((C, d), decay_ref[1, h], F32))
        hs = slice(h * d, (h + 1) * d)
        tab_scr[0, :, hs] = jnp.exp(lf * (C - 1.0 - pos))
        tab_scr[1, :, hs] = jnp.exp(lb * pos)
        tab_scr[2, :, hs] = jnp.exp(lf * (pos + 1.0))
        tab_scr[3, :, hs] = jnp.exp(lb * (C - pos))
        tab_scr[4, :, hs] = jnp.exp(lf * C)
        tab_scr[5, :, hs] = jnp.exp(lb * C)

    def rows(c):
        return pl.ds(pl.multiple_of(c * C, C), C)

    def part(p):
        return slice(c0 + p * WIDTH_RET, c0 + (p + 1) * WIDTH_RET)

    W = WIDTH_RET
    head_shift = d.bit_length() - 1
    assert d == 1 << head_shift
    same_head = ((lax.broadcasted_iota(jnp.int32, (W, W), 0) >> head_shift)
                 == (lax.broadcasted_iota(jnp.int32, (W, W), 1) >> head_shift))
    lane_head = lax.broadcasted_iota(jnp.int32, (C, W), 1) >> head_shift
    head_mean = jnp.where(same_head, 1.0 / d, 0.0).astype(BF16)

    def upd(c, carry):
        k = z_ref[rows(c), part(1)].astype(F32) * SCALE
        v = z_ref[rows(c), part(2)]
        u_scr[0, c] = jnp.where(same_head, _mm((k * tab_scr[0]).T, v), 0.0)
        u_scr[1, c] = jnp.where(same_head, _mm((k * tab_scr[1]).T, v), 0.0)
        return carry

    lax.fori_loop(0, nc, upd, 0)

    for direction in range(2):
        if s0_ref is None:
            s_init = jnp.zeros((W, W), F32)
        else:
            u_scr[direction, nc] = jnp.zeros((W, W), F32)
            for h in range(N_HEADS_RET):
                u_scr[direction, nc, h * d:(h + 1) * d, h * d:(h + 1) * d] = s0_ref[0, 0, direction, h]
            s_init = u_scr[direction, nc]
        chunk_decay = tab_scr[4 + direction, 0:1, :]

        def scan(t, s):
            c = t if direction == 0 else nc - 1 - t
            s_scr[direction, c] = s.astype(BF16)
            return chunk_decay * s + u_scr[direction, c]

        s_fin = lax.fori_loop(0, nc, scan, s_init)
        if st_ref is not None:
            for h in range(N_HEADS_RET):
                st_ref[0, direction, h] = s_fin[h * d:(h + 1) * d, h * d:(h + 1) * d]

    def out(c, carry):
        q = z_ref[rows(c), part(0)].astype(BF16)
        k = (z_ref[rows(c), part(1)].astype(F32) * SCALE).astype(BF16)
        v = z_ref[rows(c), part(2)].astype(BF16)
        gate = z_ref[rows(c), part(3)].astype(F32)
        o = (jnp.dot(q, s_scr[0, c], preferred_element_type=F32) * tab_scr[2]
             + jnp.dot(q, s_scr[1, c], preferred_element_type=F32) * tab_scr[3])
        intra = [_mm_nt(jnp.where(lane_head == h, q, jnp.zeros_like(q)), k) * d_scr[h]
                 for h in range(N_HEADS_RET)]
        for h in range(N_HEADS_RET):
            o = o + jnp.where(lane_head == h, _mm(intra[h], v), 0.0)
        sq = o * o
        sq_hi = sq.astype(BF16)
        sq_lo = (sq - sq_hi.astype(F32)).astype(BF16)
        ms = (jnp.dot(sq_hi, head_mean, preferred_element_type=F32)
              + jnp.dot(sq_lo, head_mean, preferred_element_type=F32))
        o = o * lax.rsqrt(ms + EPS) * ng_ref[...]
        o_ref[rows(c), :] = (o * _silu(gate)).astype(o_ref.dtype)
        return carry

    lax.fori_loop(0, nc, out, 0)


def _ret_scratch(n_tok):
    nc = n_tok // RET_CHUNK
    return [pltpu.VMEM((6, RET_CHUNK, WIDTH_RET), F32),
            pltpu.VMEM((N_HEADS_RET, RET_CHUNK, RET_CHUNK), F32),
            pltpu.VMEM((2, nc + 1, WIDTH_RET, WIDTH_RET), F32),
            pltpu.VMEM((2, nc, WIDTH_RET, WIDTH_RET), BF16)]


def _ctx_attn_body(sink_ref, decay_ref, z_ref, zr_ref, ng_ref, *refs, seq, n_prev, layer, depth):
    (ona_ref, oswa_ref, oret_ref, kna_ref, vna_ref, kswa_ref, vswa_ref, st_ref,
     tab_scr, d_scr, u_scr, s_scr) = refs[n_prev:]
    cache_refs = (kna_ref, vna_ref, kswa_ref, vswa_ref, st_ref)
    if n_prev == 0:
        for ref in cache_refs:
            for other in range(depth):
                if other != layer:
                    ref[0, other] = jnp.zeros(ref.shape[2:], ref.dtype)
        kna_ref, vna_ref, kswa_ref, vswa_ref, st_ref = (ref.at[:, pl.ds(layer, 1)] for ref in cache_refs)
    d = HEAD_DIM
    low_half = lax.broadcasted_iota(jnp.int32, (seq, 2 * d), 1) < d

    def pair_scores(q2, k2, head_is_low):
        return _mm_nt(jnp.where(low_half == head_is_low, q2, jnp.zeros_like(q2)), k2)

    jobs = []
    for pair in range(N_HEADS_NA // 2):
        cols = lambda part: slice(OFF_NA + part * WIDTH_NA + pair * 2 * d,
                                  OFF_NA + part * WIDTH_NA + (pair + 1) * 2 * d)
        q2 = (z_ref[:, cols(0)] * SCALE).astype(BF16)
        k2 = z_ref[:, cols(1)]
        v2 = z_ref[:, cols(2)]
        for hh in range(2):
            kna_ref[0, 0, 2 * pair + hh] = k2[:, hh * d:(hh + 1) * d]
            vna_ref[0, 0, 2 * pair + hh] = v2[:, hh * d:(hh + 1) * d]
        k2 = k2.astype(BF16)
        v2 = v2.astype(BF16)
        jobs.append([(pair_scores(q2, k2, hh == 0), v2, hh == 0, None) for hh in range(2)]
                    + [ona_ref, pair])
    lo = OFF_SWA + WIDTH_SWA
    k2 = z_ref[:, lo:lo + WIDTH_KV_SWA]
    v2 = z_ref[:, lo + WIDTH_KV_SWA:lo + 2 * WIDTH_KV_SWA]
    for g in range(N_KV_SWA):
        kswa_ref[0, 0, g] = k2[:, g * d:(g + 1) * d]
        vswa_ref[0, 0, g] = v2[:, g * d:(g + 1) * d]
    k2 = k2.astype(BF16)
    v2 = v2.astype(BF16)
    group = N_HEADS_SWA // N_KV_SWA
    for pair in range(N_HEADS_SWA // 2):
        q2 = (z_ref[:, OFF_SWA + pair * 2 * d:OFF_SWA + (pair + 1) * 2 * d] * SCALE).astype(BF16)
        heads = []
        for hh in range(2):
            h = 2 * pair + hh
            g = h // group
            qh = q2[:, hh * d:(hh + 1) * d]
            q_at_g = jnp.concatenate([qh, qh], axis=-1) if g != hh else q2
            heads.append((pair_scores(q_at_g, k2, g == 0), v2, g == 0, sink_ref[h]))
        jobs.append(heads + [oswa_ref, pair])
    probs = []
    for job in jobs:
        for s, v2, v_is_low, sink in job[:2]:
            m = jnp.max(s, axis=-1, keepdims=True)
            if sink is not None:
                m = jnp.maximum(m, sink)
            e = jnp.exp(s - m)
            den = jnp.sum(e, axis=-1, keepdims=True)
            if sink is not None:
                den = den + jnp.exp(sink - m)
            probs.append((e, den))
    for j, job in enumerate(jobs):
        halves = []
        for hh, (s, v2, v_is_low, sink) in enumerate(job[:2]):
            e, den = probs[2 * j + hh]
            o2 = _mm(e, v2) / den
            halves.append(o2[:, :d] if v_is_low else o2[:, d:])
        o_ref, pair = job[2], job[3]
        o_ref[:, pair * 2 * d:(pair + 1) * 2 * d] = jnp.concatenate(halves, axis=-1).astype(o_ref.dtype)
    _retention(zr_ref, 0, decay_ref, ng_ref, None, oret_ref, st_ref.at[0],
               tab_scr, d_scr, u_scr, s_scr, seq)


def _ctx_mixers(z_att, z_rest, swa_sink, ret_decay, ret_norm_g, caches, layer, depth, batch, seq):
    n = batch * seq
    assert z_att.shape[1] == OFF_RET
    smem = pl.BlockSpec(memory_space=pltpu.SMEM)
    first = caches is None
    layers_in_block = depth if first else 1
    block_layer = 0 if first else layer
    kv = lambda heads: pl.BlockSpec((1, layers_in_block, heads, seq, HEAD_DIM),
                                    lambda b: (b, block_layer, 0, 0, 0))
    kv_shape = lambda heads: jax.ShapeDtypeStruct((batch, depth, heads, seq, HEAD_DIM), F32)
    in_specs = [smem, smem,
                pl.BlockSpec((seq, OFF_RET), lambda b: (b, 0)),
                pl.BlockSpec((seq, 4 * WIDTH_RET), lambda b: (b, 0)),
                pl.BlockSpec((1, WIDTH_RET), lambda b: (0, 0))]
    args = [swa_sink, ret_decay, z_att, z_rest, ret_norm_g.reshape(1, WIDTH_RET)]
    n_prev = 0 if caches is None else len(caches)
    aliases = {len(args) + i: 3 + i for i in range(n_prev)}
    if caches is not None:
        in_specs += [pl.BlockSpec(memory_space=pl.ANY)] * n_prev
        args += list(caches)
    return pl.pallas_call(
        functools.partial(_ctx_attn_body, seq=seq, n_prev=n_prev, layer=layer, depth=depth),
        grid=(batch,),
        in_specs=in_specs,
        out_specs=[pl.BlockSpec((seq, WIDTH_NA), lambda b: (b, 0)),
                   pl.BlockSpec((seq, WIDTH_SWA), lambda b: (b, 0)),
                   pl.BlockSpec((seq, WIDTH_RET), lambda b: (b, 0)),
                   kv(N_HEADS_NA), kv(N_HEADS_NA), kv(N_KV_SWA), kv(N_KV_SWA),
                   pl.BlockSpec((1, layers_in_block, 2, N_HEADS_RET, HEAD_DIM, HEAD_DIM),
                                lambda b: (b, block_layer, 0, 0, 0, 0))],
        out_shape=[jax.ShapeDtypeStruct((n, WIDTH_NA), BF16),
                   jax.ShapeDtypeStruct((n, WIDTH_SWA), BF16),
                   jax.ShapeDtypeStruct((n, WIDTH_RET), BF16),
                   kv_shape(N_HEADS_NA), kv_shape(N_HEADS_NA), kv_shape(N_KV_SWA), kv_shape(N_KV_SWA),
                   jax.ShapeDtypeStruct((batch, depth, 2, N_HEADS_RET, HEAD_DIM, HEAD_DIM), F32)],
        scratch_shapes=_ret_scratch(seq),
        input_output_aliases=aliases,
        compiler_params=_params(("arbitrary",), 32),
        name="ctx_mixers",
    )(*args)


NA_QROWS = 4
NA_WROWS = NA_QROWS + WIN_ROWS


def _na_bias_table(rpb, rows):
    n_groups = rows // NA_QROWS
    assert rows % NA_QROWS == 0 and n_groups >= 3
    n_heads, n_ro, n_co = rpb.shape
    half = WIN_ROWS // 2
    cc = np.arange(GRID_W)
    col_start = np.clip(cc - WIN_COLS // 2, 0, GRID_W - WIN_COLS)
    valid = (cc[None, :] >= col_start[:, None]) & (cc[None, :] < col_start[:, None] + WIN_COLS)
    span = 2 * GRID_W - 1
    lead = GRID_W - WIN_COLS
    ext = jnp.pad(rpb.astype(F32), ((0, 0), (0, 0), (lead, span + 1 - lead - n_co)))
    band = jnp.tile(ext, (1, 1, GRID_W))[..., :GRID_W * span].reshape(n_heads, n_ro, GRID_W, span)
    band = jnp.where(valid[None, None], band[..., GRID_W - 1:], MASKED)
    outside = jnp.full((n_heads, GRID_W, GRID_W), MASKED, F32)
    tabs = []
    for g in (0, 1, n_groups - 1):
        ws = min(max(NA_QROWS * g - half, 0), rows - NA_WROWS)
        q_blocks = []
        for i in range(NA_QROWS):
            r = NA_QROWS * g + i
            rs = min(max(r - half, 0), rows - WIN_ROWS)
            q_blocks.append(jnp.concatenate(
                [band[:, ws + j - r + WIN_ROWS - 1] if rs <= ws + j < rs + WIN_ROWS else outside
                 for j in range(NA_WROWS)], axis=-1))
        tabs.append(jnp.concatenate(q_blocks, axis=-2))
    return jnp.stack(tabs, axis=0)


def _lat_na_body(z_ref, kc_ref, vc_ref, bias_ref, o_ref, kc_scr, vc_scr, *, rows):
    d = HEAD_DIM
    nq = NA_QROWS * GRID_W
    g = pl.program_id(1)

    @pl.when(g == 0)
    def _():
        kc_scr[...] = kc_ref[0, 0].astype(BF16)
        vc_scr[...] = vc_ref[0, 0].astype(BF16)

    ws = jnp.clip(NA_QROWS * g - WIN_ROWS // 2, 0, rows - NA_WROWS)
    qrows = pl.ds(pl.multiple_of(g * nq, nq), nq)
    wrows = pl.ds(pl.multiple_of(ws * GRID_W, GRID_W), NA_WROWS * GRID_W)
    low_half = lax.broadcasted_iota(jnp.int32, (nq, 2 * d), 1) < d
    for pair in range(N_HEADS_NA // 2):
        cols = slice(pair * 2 * d, (pair + 1) * 2 * d)
        q2 = (z_ref[qrows, cols] * SCALE).astype(BF16)
        kl2 = z_ref[wrows, WIDTH_NA + pair * 2 * d:WIDTH_NA + (pair + 1) * 2 * d].astype(BF16)
        vl2 = z_ref[wrows, 2 * WIDTH_NA + pair * 2 * d:2 * WIDTH_NA + (pair + 1) * 2 * d].astype(BF16)
        for hh in range(2):
            h = 2 * pair + hh
            qm = jnp.where(low_half == (hh == 0), q2, jnp.zeros_like(q2))
            s_loc = _mm_nt(qm, kl2) + bias_ref[0, h]
            s_ctx = _mm_nt(q2[:, hh * d:(hh + 1) * d], kc_scr[h])
            m = jnp.maximum(jnp.max(s_ctx, axis=-1, keepdims=True), jnp.max(s_loc, axis=-1, keepdims=True))
            e_ctx = jnp.exp(s_ctx - m)
            e_loc = jnp.exp(s_loc - m)
            den = jnp.sum(e_ctx, axis=-1, keepdims=True) + jnp.sum(e_loc, axis=-1, keepdims=True)
            o = (_mm(e_ctx, vc_scr[h]) + _mm(e_loc, vl2)[:, hh * d:(hh + 1) * d]) / den
            o_ref[:, h * d:(h + 1) * d] = o.astype(o_ref.dtype)


def _lat_na(z, cache_k, cache_v, bias_tab, layer, batch, seq):
    rows = seq // GRID_W
    n_groups = rows // NA_QROWS
    nq = NA_QROWS * GRID_W
    past = cache_k.shape[3]

    def group_class(g):
        return jnp.where(g == 0, 0, jnp.where(g == n_groups - 1, 2, 1))

    cache = pl.BlockSpec((1, 1, N_HEADS_NA, past, HEAD_DIM), lambda b, g: (b, layer, 0, 0, 0))
    return pl.pallas_call(
        functools.partial(_lat_na_body, rows=rows),
        grid=(batch, n_groups),
        in_specs=[pl.BlockSpec((seq, 3 * WIDTH_NA), lambda b, g: (b, 0)),
                  cache, cache,
                  pl.BlockSpec((1, N_HEADS_NA, nq, NA_WROWS * GRID_W),
                               lambda b, g: (group_class(g), 0, 0, 0))],
        out_specs=pl.BlockSpec((nq, WIDTH_NA), lambda b, g: (b * n_groups + g, 0)),
        out_shape=jax.ShapeDtypeStruct((batch * seq, WIDTH_NA), BF16),
        scratch_shapes=[pltpu.VMEM((N_HEADS_NA, past, HEAD_DIM), BF16),
                        pltpu.VMEM((N_HEADS_NA, past, HEAD_DIM), BF16)],
        compiler_params=_params(("arbitrary", "arbitrary"), 58),
        name="lat_na",
    )(z, cache_k, cache_v, bias_tab)


def _rope_tables(seq):
    t = jnp.arange(seq)
    n_freq = HEAD_DIM // 4
    inv = ROPE_BASE ** (-jnp.arange(n_freq, dtype=F32) / n_freq)
    ang = jnp.concatenate([(t // GRID_W).astype(F32)[:, None] * inv,
                           (t % GRID_W).astype(F32)[:, None] * inv], axis=-1)
    cos = jnp.repeat(jnp.cos(ang), 2, axis=-1)
    sign = jnp.tile(jnp.asarray([-1.0, 1.0], F32), HEAD_DIM // 2)
    sin = jnp.repeat(jnp.sin(ang), 2, axis=-1) * sign
    return jnp.tile(cos, (1, N_HEADS_SWA)), jnp.tile(sin, (1, N_HEADS_SWA))


def _rope(x, cos, sin_signed):
    width = x.shape[-1]
    lane = lax.broadcasted_iota(jnp.int32, x.shape, x.ndim - 1)
    partner = jnp.where((lane & 1) == 0, pltpu.roll(x, width - 1, x.ndim - 1), pltpu.roll(x, 1, x.ndim - 1))
    return x * cos + partner * sin_signed


def _lat_swa_body(sink_ref, z_ref, kc_ref, vc_ref, cos_ref, sin_ref, o_ref, kr_scr, kc_scr, vc_scr, *, seq):
    d = HEAD_DIM
    blk = SWA_BLOCK
    n = pl.program_id(1)
    assert N_KV_SWA == 2

    @pl.when(n == 0)
    def _():
        k = z_ref[:, WIDTH_SWA:WIDTH_SWA + WIDTH_KV_SWA].astype(F32)
        kr_scr[...] = _rope(k, cos_ref[:, :WIDTH_KV_SWA], sin_ref[:, :WIDTH_KV_SWA]).astype(BF16)
        kc_scr[...] = jnp.concatenate([kc_ref[0, 0, 0], kc_ref[0, 0, 1]], axis=-1).astype(BF16)
        vc_scr[...] = jnp.concatenate([vc_ref[0, 0, 0], vc_ref[0, 0, 1]], axis=-1).astype(BF16)

    qrows = pl.ds(pl.multiple_of(n * blk, blk), blk)
    kstart = jnp.clip((n - 1) * blk, 0, seq - 3 * blk)
    krows = pl.ds(pl.multiple_of(kstart, blk), 3 * blk)
    q_all = _rope(z_ref[qrows, 0:WIDTH_SWA].astype(F32), cos_ref[qrows, :], sin_ref[qrows, :]) * SCALE
    q_all = q_all.astype(BF16)
    qpos = n * blk + lax.broadcasted_iota(jnp.int32, (blk, 3 * blk), 0)
    kpos = kstart + lax.broadcasted_iota(jnp.int32, (blk, 3 * blk), 1)
    valid = jnp.abs(qpos - kpos) <= SWA_RADIUS
    kl2 = kr_scr[krows, :]
    vl2 = z_ref[krows, WIDTH_SWA + WIDTH_KV_SWA:WIDTH_SWA + 2 * WIDTH_KV_SWA]
    group = N_HEADS_SWA // N_KV_SWA
    zero = jnp.zeros((blk, d), BF16)
    scores = []
    for h in range(N_HEADS_SWA):
        g = h // group
        qh = q_all[:, h * d:(h + 1) * d]
        q_at_g = jnp.concatenate([qh, zero] if g == 0 else [zero, qh], axis=-1)
        scores.append((jnp.where(valid, _mm_nt(q_at_g, kl2), MASKED), _mm_nt(q_at_g, kc_scr[...])))
    probs = []
    for h, (s_loc, s_ctx) in enumerate(scores):
        m = jnp.maximum(jnp.maximum(jnp.max(s_loc, axis=-1, keepdims=True),
                                    jnp.max(s_ctx, axis=-1, keepdims=True)), sink_ref[h])
        e_loc = jnp.exp(s_loc - m)
        e_ctx = jnp.exp(s_ctx - m)
        den = (jnp.sum(e_loc, axis=-1, keepdims=True) + jnp.sum(e_ctx, axis=-1, keepdims=True)
               + jnp.exp(sink_ref[h] - m))
        probs.append((e_loc, e_ctx, den))
    outs = []
    for h, (e_loc, e_ctx, den) in enumerate(probs):
        g = h // group
        o2 = (_mm(e_ctx, vc_scr[...]) + _mm(e_loc, vl2)) / den
        outs.append(o2[:, g * d:(g + 1) * d])
    o_ref[...] = jnp.concatenate(outs, axis=-1).astype(o_ref.dtype)


def _lat_swa(z, cache_k, cache_v, swa_sink, cos, sin, layer, batch, seq):
    nb = seq // SWA_BLOCK
    past = cache_k.shape[3]
    width = WIDTH_SWA + 2 * WIDTH_KV_SWA
    assert OFF_SWA % width == 0 and seq >= 3 * SWA_BLOCK
    cache = pl.BlockSpec((1, 1, N_KV_SWA, past, HEAD_DIM), lambda b, n: (b, layer, 0, 0, 0))
    table = pl.BlockSpec((seq, WIDTH_SWA), lambda b, n: (0, 0))
    return pl.pallas_call(
        functools.partial(_lat_swa_body, seq=seq),
        grid=(batch, nb),
        in_specs=[pl.BlockSpec(memory_space=pltpu.SMEM),
                  pl.BlockSpec((seq, width), lambda b, n: (b, OFF_SWA // width)),
                  cache, cache, table, table],
        out_specs=pl.BlockSpec((SWA_BLOCK, WIDTH_SWA), lambda b, n: (b * nb + n, 0)),
        out_shape=jax.ShapeDtypeStruct((batch * seq, WIDTH_SWA), BF16),
        scratch_shapes=[pltpu.VMEM((seq, WIDTH_KV_SWA), BF16),
                        pltpu.VMEM((past, WIDTH_KV_SWA), BF16), pltpu.VMEM((past, WIDTH_KV_SWA), BF16)],
        compiler_params=_params(("arbitrary", "arbitrary"), 32),
        name="lat_swa",
    )(swa_sink, z, cache_k, cache_v, cos, sin)


def _lat_ret_body(decay_ref, z_ref, ng_ref, s0_ref, o_ref, tab_scr, d_scr, u_scr, s_scr, *, seq):
    _retention(z_ref, 0, decay_ref, ng_ref, s0_ref, o_ref, None, tab_scr, d_scr, u_scr, s_scr, seq)


def _lat_ret(z, ret_decay, ret_norm_g, state, layer, batch, seq):
    width = 4 * WIDTH_RET
    assert OFF_RET % width == 0
    return pl.pallas_call(
        functools.partial(_lat_ret_body, seq=seq),
        grid=(batch,),
        in_specs=[pl.BlockSpec(memory_space=pltpu.SMEM),
                  pl.BlockSpec((seq, width), lambda b: (b, OFF_RET // width)),
                  pl.BlockSpec((1, WIDTH_RET), lambda b: (0, 0)),
                  pl.BlockSpec((1, 1, 2, N_HEADS_RET, HEAD_DIM, HEAD_DIM),
                               lambda b: (b, layer, 0, 0, 0, 0))],
        out_specs=pl.BlockSpec((seq, WIDTH_RET), lambda b: (b, 0)),
        out_shape=jax.ShapeDtypeStruct((batch * seq, WIDTH_RET), BF16),
        scratch_shapes=_ret_scratch(seq),
        compiler_params=_params(("arbitrary",), 40),
        name="lat_ret",
    )(ret_decay, z, ret_norm_g.reshape(1, WIDTH_RET), state)


def _merge_body(ona_ref, oswa_ref, oret_ref, ga_ref, gs_ref, gr_ref, x_ref, mod_ref, g2_ref,
                wna_ref, wswa_ref, wret_ref, wout_ref, rw_ref, rb_ref,
                x1_ref, hx_ref, wna_scr, wswa_scr, wret_scr, wout_scr, rw_scr):
    @pl.when(pl.program_id(0) == 0)
    def _():
        wna_scr[...] = wna_ref[0].astype(BF16)
        wswa_scr[...] = wswa_ref[0].astype(BF16)
        wret_scr[...] = wret_ref[0].astype(BF16)
        wout_scr[...] = wout_ref[0].astype(BF16)
        rw = rw_ref[0]
        rw_hi = rw.astype(BF16)
        rw_scr[0] = rw_hi
        rw_scr[1] = (rw - rw_hi.astype(F32)).astype(BF16)

    D = D_MODEL
    m = mod_ref[0]
    gate = lambda ref: _sigmoid(ref[...].astype(F32))
    z = (gate(ga_ref) * jnp.dot(ona_ref[...], wna_scr[...], preferred_element_type=F32)
         + gate(gs_ref) * jnp.dot(oswa_ref[...], wswa_scr[...], preferred_element_type=F32)
         + gate(gr_ref) * jnp.dot(oret_ref[...], wret_scr[...], preferred_element_type=F32))
    y = jnp.dot(z.astype(BF16), wout_scr[...], preferred_element_type=F32)
    x1 = x_ref[...] + m[:, 2 * D:3 * D] * y
    x1_ref[...] = x1
    h2 = x1 * lax.rsqrt(jnp.mean(x1 * x1, axis=-1, keepdims=True) + EPS) * g2_ref[0]
    h2 = h2 * (1.0 + m[:, 4 * D:5 * D]) + m[:, 3 * D:4 * D]
    hx_ref[:, 0:D] = h2

    h_hi = h2.astype(BF16)
    h_lo = (h2 - h_hi.astype(F32)).astype(BF16)
    logit = (jnp.dot(h_hi, rw_scr[0], preferred_element_type=F32)
             + jnp.dot(h_hi, rw_scr[1], preferred_element_type=F32)
             + jnp.dot(h_lo, rw_scr[0], preferred_element_type=F32)) + rb_ref[0]
    lane = lax.broadcasted_iota(jnp.int32, logit.shape, 1).astype(F32)
    neg = jnp.float32(-jnp.inf)
    big = jnp.float32(ROUTER_LANES)
    is_group = lane < N_GROUPS
    is_expert = (lane >= N_GROUPS) & (lane < N_GROUPS + N_EXPERTS)
    gl = jnp.where(is_group, logit, neg)
    gmax = jnp.max(gl, axis=-1, keepdims=True)
    gsel = jnp.min(jnp.where(gl == gmax, lane, big), axis=-1, keepdims=True)
    p_group = 1.0 / jnp.sum(jnp.where(is_group, jnp.exp(gl - gmax), 0.0), axis=-1, keepdims=True)
    eidx = lane - N_GROUPS
    in_group = is_expert & (jnp.floor(eidx / EXPERTS_PER_GROUP) == gsel)
    el = jnp.where(in_group, logit, neg)
    top1 = jnp.max(el, axis=-1, keepdims=True)
    sel1 = jnp.min(jnp.where(el == top1, lane, big), axis=-1, keepdims=True)
    el2 = jnp.where(lane == sel1, neg, el)
    top2 = jnp.max(el2, axis=-1, keepdims=True)
    sel2 = jnp.min(jnp.where(el2 == top2, lane, big), axis=-1, keepdims=True)
    e2 = jnp.exp(top2 - top1)
    w1 = p_group / (1.0 + e2)
    w2 = p_group * e2 / (1.0 + e2)
    comb = jnp.where(lane == sel1, w1, 0.0) + jnp.where(lane == sel2, w2, 0.0)
    hx_ref[:, D:D + ROUTER_LANES] = comb + jnp.where(lane == ROUTE_GROUP_LANE, gsel, 0.0)


def _merge(o_na, o_swa, o_ret, z, gate_tile, x, mod_rows, norm2_g, w_br_na, w_br_swa, w_br_ret, w_out,
           router_w, router_b, layer):
    n, d = x.shape
    n_cond = mod_rows.shape[0]
    per_cond = n // n_cond
    tm = 512
    gate = lambda k: pl.BlockSpec((tm, d), lambda i: (i, gate_tile + k))
    row = lambda w: pl.BlockSpec((tm, w), lambda i: (i, 0))
    weight = lambda k: pl.BlockSpec((1, k, d), lambda i: (layer, 0, 0))
    return pl.pallas_call(
        _merge_body,
        grid=(n // tm,),
        in_specs=[row(WIDTH_NA), row(WIDTH_SWA), row(WIDTH_RET),
                  gate(0), gate(1), gate(2),
                  row(d),
                  pl.BlockSpec((1, 1, 6 * d), lambda i: (i * tm // per_cond, 0, 0)),
                  pl.BlockSpec((1, 1, d), lambda i: (layer, 0, 0)),
                  weight(WIDTH_NA), weight(WIDTH_SWA), weight(WIDTH_RET), weight(d),
                  pl.BlockSpec((1, d, ROUTER_LANES), lambda i: (layer, 0, 0)),
                  pl.BlockSpec((1, 1, ROUTER_LANES), lambda i: (layer, 0, 0))],
        out_specs=[row(d), row(d + ROUTER_LANES)],
        out_shape=[jax.ShapeDtypeStruct((n, d), F32),
                   jax.ShapeDtypeStruct((n, d + ROUTER_LANES), F32)],
        scratch_shapes=[pltpu.VMEM((WIDTH_NA, d), BF16), pltpu.VMEM((WIDTH_SWA, d), BF16),
                        pltpu.VMEM((WIDTH_RET, d), BF16), pltpu.VMEM((d, d), BF16),
                        pltpu.VMEM((2, d, ROUTER_LANES), BF16)],
        compiler_params=_params(("arbitrary",), 56),
        name="merge_router",
    )(o_na, o_swa, o_ret, z, z, z, x, mod_rows, norm2_g.reshape(norm2_g.shape[0], 1, d),
      w_br_na, w_br_swa, w_br_ret, w_out, router_w, router_b)


MOE_TILE = 512
FIN_TILE = 512


def _route_plan(hx, n_tiles):
    n = hx.shape[0]
    g = hx[:, D_MODEL + ROUTE_GROUP_LANE].astype(jnp.int32)
    onehot = (g[:, None] == jnp.arange(N_GROUPS, dtype=jnp.int32)[None, :]).astype(jnp.int32)
    csum = jnp.cumsum(onehot, axis=0)
    counts = csum[-1]
    ends = jnp.cumsum(counts)
    dest = jnp.sum(onehot * ((ends - counts)[None, :] + csum - 1), axis=1)
    order = jnp.zeros((n,), jnp.int32).at[dest].set(jnp.arange(n, dtype=jnp.int32), unique_indices=True)
    first = jnp.arange(n_tiles, dtype=jnp.int32) * MOE_TILE
    tile_lo = jnp.sum((first[:, None] >= ends[None, :]).astype(jnp.int32), axis=1)
    tile_hi = jnp.sum(((first + MOE_TILE - 1)[:, None] >= ends[None, :]).astype(jnp.int32), axis=1)
    return order, dest, tile_lo, tile_hi


def _row_copies(src_hbm, idx_ref, base, buf, sem, n_rows, start):
    def body(i, carry):
        copy = pltpu.make_async_copy(src_hbm.at[pl.ds(idx_ref[base + i], 1)], buf.at[pl.ds(i, 1)], sem)
        if start:
            copy.start()
        else:
            copy.wait()
        return carry

    lax.fori_loop(0, n_rows, body, 0, unroll=8)


def _moe_body(order_ref, lo_ref, hi_ref, hx_hbm, wg_hbm, wu_hbm, wd_hbm, o_ref,
              hbuf, wg_scr, wu_scr, wd_scr, row_sem, w_sem, *, layer, n_tiles):
    t = pl.program_id(0)
    slot = lax.rem(t, 2)

    def rows_of(tile, slot_, start):
        _row_copies(hx_hbm, order_ref, tile * MOE_TILE, hbuf.at[slot_], row_sem.at[slot_], MOE_TILE, start)

    @pl.when(t == 0)
    def _():
        copies = [pltpu.make_async_copy(src.at[layer], dst, w_sem.at[k])
                  for k, (src, dst) in enumerate(((wg_hbm, wg_scr), (wu_hbm, wu_scr), (wd_hbm, wd_scr)))]
        for copy in copies:
            copy.start()
        rows_of(0, 0, True)
        for copy in copies:
            copy.wait()

    @pl.when(t + 1 < n_tiles)
    def _():
        rows_of(t + 1, 1 - slot, True)

    rows_of(t, slot, False)
    hx = hbuf[slot]
    h = hx[:, 0:D_MODEL].astype(BF16)
    route = hx[:, D_MODEL:D_MODEL + ROUTER_LANES]
    lane = lax.broadcasted_iota(jnp.int32, route.shape, 1)
    o_ref[...] = jnp.zeros_like(o_ref)
    for g in range(N_GROUPS):
        @pl.when((lo_ref[t] <= g) & (g <= hi_ref[t]))
        def _():
            y = None
            for e in range(g * EXPERTS_PER_GROUP, (g + 1) * EXPERTS_PER_GROUP):
                hg = jnp.dot(h, wg_scr[e], preferred_element_type=F32)
                hu = jnp.dot(h, wu_scr[e], preferred_element_type=F32)
                c = jnp.sum(jnp.where(lane == ROUTE_EXPERT_LANE + e, route, 0.0), axis=-1, keepdims=True)
                a = _silu(hg) * hu * c
                part = jnp.dot(a.astype(BF16), wd_scr[e], preferred_element_type=F32)
                y = part if y is None else y + part
            o_ref[...] += y


def _moe_sorted(hx, order, tile_lo, tile_hi, w_gate, w_up, w_down, layer):
    n = hx.shape[0]
    d = D_MODEL
    n_tiles = n // MOE_TILE
    any_space = pl.BlockSpec(memory_space=pl.ANY)
    return pl.pallas_call(
        functools.partial(_moe_body, layer=layer, n_tiles=n_tiles),
        grid_spec=pltpu.PrefetchScalarGridSpec(
            num_scalar_prefetch=3,
            grid=(n_tiles,),
            in_specs=[any_space, any_space, any_space, any_space],
            out_specs=pl.BlockSpec((MOE_TILE, d), lambda t, *_: (t, 0)),
            scratch_shapes=[pltpu.VMEM((2, MOE_TILE, d + ROUTER_LANES), F32),
                            pltpu.VMEM((N_EXPERTS, d, D_EXPERT), BF16),
                            pltpu.VMEM((N_EXPERTS, d, D_EXPERT), BF16),
                            pltpu.VMEM((N_EXPERTS, D_EXPERT, d), BF16),
                            pltpu.SemaphoreType.DMA((2,)),
                            pltpu.SemaphoreType.DMA((3,))]),
        out_shape=jax.ShapeDtypeStruct((n, d), F32),
        compiler_params=_params(("arbitrary",), 56),
        name="moe_sorted",
    )(order, tile_lo, tile_hi, hx, w_gate, w_up, w_down)


def _moe_finish_body(dest_ref, y_hbm, x1_ref, mod_ref, fg_ref, o_ref, ybuf, row_sem, *, final, n_tiles):
    t = pl.program_id(0)
    slot = lax.rem(t, 2)

    def rows_of(tile, slot_, start):
        _row_copies(y_hbm, dest_ref, tile * FIN_TILE, ybuf.at[slot_], row_sem.at[slot_], FIN_TILE, start)

    @pl.when(t == 0)
    def _():
        rows_of(0, 0, True)

    @pl.when(t + 1 < n_tiles)
    def _():
        rows_of(t + 1, 1 - slot, True)

    rows_of(t, slot, False)
    x2 = x1_ref[...] + mod_ref[0][:, 5 * D_MODEL:6 * D_MODEL] * ybuf[slot]
    if final:
        x2 = x2 * lax.rsqrt(jnp.mean(x2 * x2, axis=-1, keepdims=True) + EPS) * fg_ref[...]
    o_ref[...] = x2


def _moe_finish(y_sorted, dest, x1, mod_rows, final_g, final):
    n, d = x1.shape
    n_cond = mod_rows.shape[0]
    per_cond = n // n_cond
    n_tiles = n // FIN_TILE
    return pl.pallas_call(
        functools.partial(_moe_finish_body, final=final, n_tiles=n_tiles),
        grid_spec=pltpu.PrefetchScalarGridSpec(
            num_scalar_prefetch=1,
            grid=(n_tiles,),
            in_specs=[pl.BlockSpec(memory_space=pl.ANY),
                      pl.BlockSpec((FIN_TILE, d), lambda t, *_: (t, 0)),
                      pl.BlockSpec((1, 1, 6 * d), lambda t, *_: (t * FIN_TILE // per_cond, 0, 0)),
                      pl.BlockSpec((1, d), lambda t, *_: (0, 0))],
            out_specs=pl.BlockSpec((FIN_TILE, d), lambda t, *_: (t, 0)),
            scratch_shapes=[pltpu.VMEM((2, FIN_TILE, d), F32), pltpu.SemaphoreType.DMA((2,))]),
        out_shape=jax.ShapeDtypeStruct((n, d), F32),
        compiler_params=_params(("arbitrary",), 32),
        name="moe_finish",
    )(dest, y_sorted, x1, mod_rows, final_g.reshape(1, d))


def _moe(hx, x1, mod_rows, w_gate, w_up, w_down, final_g, layer, final):
    order, dest, tile_lo, tile_hi = _route_plan(hx, hx.shape[0] // MOE_TILE)
    y_sorted = _moe_sorted(hx, order, tile_lo, tile_hi, w_gate, w_up, w_down, layer)
    return _moe_finish(y_sorted, dest, x1, mod_rows, final_g, final)


def kernel(x_prompt, x_sample, c, cache_na_k, cache_na_v, cache_swa_k, cache_swa_v, state_ret, c_ctx, ada_w, ada_b, norm1_g, norm2_g, w_in, na_rpb, swa_sink, ret_decay, ret_norm_g, w_br_na, w_br_swa, w_br_ret, w_out, router_group_w, router_group_b, router_expert_w, router_expert_b, w_gate, w_up, w_down, final_norm_g):
    batch, seq, d = x_prompt.shape
    dec_batch, dec_seq, _ = x_sample.shape
    depth = ada_w.shape[0]
    assert d == D_MODEL and dec_batch + 1 <= 8
    assert OFF_RET % IN_TILE == 0 and OFF_GATES % d == 0 and (OFF_GATES - OFF_RET) % d == 0

    cond = jnp.concatenate([c_ctx[None], c, jnp.zeros((8 - 1 - dec_batch, d), F32)], axis=0)
    mod = _modulation(cond, ada_w, ada_b)
    router_w = jnp.concatenate([router_group_w, router_expert_w], axis=-1)
    router_w = jnp.pad(router_w, ((0, 0), (0, 0), (0, ROUTER_LANES - router_w.shape[-1])))
    router_b = jnp.concatenate([router_group_b, router_expert_b], axis=-1)
    router_b = jnp.pad(router_b, ((0, 0), (0, ROUTER_LANES - router_b.shape[-1])))[:, None, :]
    cos, sin = _rope_tables(dec_seq)
    moe_weights = (w_gate.astype(BF16), w_up.astype(BF16), w_down.astype(BF16))

    xp = x_prompt.reshape(batch * seq, d)
    xs = x_sample.reshape(dec_batch * dec_seq, d)
    caches = None
    for l in range(depth):
        final = l == depth - 1
        mod_ctx = mod[l, 0:1][:, None, :]
        mod_lat = mod[l, 1:1 + dec_batch][:, None, :]
        merge_w = (norm2_g, w_br_na, w_br_swa, w_br_ret, w_out, router_w, router_b, l)
        moe_w = (*moe_weights, final_norm_g, l, final)

        z_att, z_rest = _in_proj(xp, mod_ctx, norm1_g, w_in, l, OFF_RET // IN_TILE)
        o_na, o_swa, o_ret, *caches = _ctx_mixers(
            z_att, z_rest, swa_sink[l], ret_decay[l], ret_norm_g[l], caches, l, depth, batch, seq)
        x1, hx = _merge(o_na, o_swa, o_ret, z_rest, (OFF_GATES - OFF_RET) // d, xp, mod_ctx, *merge_w)
        xp = _moe(hx, x1, mod_ctx, *moe_w)

        z, = _in_proj(xs, mod_lat, norm1_g, w_in, l, 0)
        o_na = _lat_na(z, cache_na_k, cache_na_v, _na_bias_table(na_rpb[l], dec_seq // GRID_W),
                       l, dec_batch, dec_seq)
        o_swa = _lat_swa(z, cache_swa_k, cache_swa_v, swa_sink[l], cos, sin, l, dec_batch, dec_seq)
        o_ret = _lat_ret(z, ret_decay[l], ret_norm_g[l], state_ret, l, dec_batch, dec_seq)
        x1, hx = _merge(o_na, o_swa, o_ret, z, OFF_GATES // d, xs, mod_lat, *merge_w)
        xs = _moe(hx, x1, mod_lat, *moe_w)

    return (xp.reshape(batch, seq, d), xs.reshape(dec_batch, dec_seq, d), *caches)
```

```python
import functools

import numpy as np
import jax
import jax.numpy as jnp
from jax import lax
from jax.experimental import pallas as pl
from jax.experimental.pallas import tpu as pltpu

F32 = jnp.float32
BF16 = jnp.bfloat16

D_MODEL = 1024
HEAD_DIM = 64
GRID_W = 64
N_HEADS_NA = 8
WIN_ROWS = 8
WIN_COLS = 16
N_HEADS_SWA = 4
N_KV_SWA = 2
SWA_RADIUS = 128
SWA_BLOCK = 128
N_HEADS_RET = 4
RET_CHUNK = 128
WIDTH_NA = N_HEADS_NA * HEAD_DIM
WIDTH_SWA = N_HEADS_SWA * HEAD_DIM
WIDTH_KV_SWA = N_KV_SWA * HEAD_DIM
WIDTH_RET = N_HEADS_RET * HEAD_DIM
IN_COLS = 3 * WIDTH_NA + WIDTH_SWA + 2 * WIDTH_KV_SWA + 4 * WIDTH_RET + 3 * D_MODEL
OFF_NA = 0
OFF_SWA = 3 * WIDTH_NA
OFF_RET = OFF_SWA + WIDTH_SWA + 2 * WIDTH_KV_SWA
OFF_GATES = OFF_RET + 4 * WIDTH_RET
N_GROUPS = 4
EXPERTS_PER_GROUP = 4
N_EXPERTS = N_GROUPS * EXPERTS_PER_GROUP
D_EXPERT = D_MODEL // 4
ROPE_BASE = 10000.0
EPS = 1e-6
SCALE = HEAD_DIM ** -0.5
MASKED = -1e30
ROUTER_LANES = 128
ROUTE_GROUP_LANE = 0
ROUTE_EXPERT_LANE = N_GROUPS

MIB = 1024 * 1024


def _mm(a, b):
    return jnp.dot(a.astype(BF16), b.astype(BF16), preferred_element_type=F32)


def _mm_nt(a, b):
    return lax.dot_general(a.astype(BF16), b.astype(BF16), (((1,), (1,)), ((), ())),
                           preferred_element_type=F32)


def _sigmoid(x):
    return 1.0 / (1.0 + jnp.exp(-x))


def _silu(x):
    return x * _sigmoid(x)


def _params(semantics, vmem_mib):
    return pltpu.CompilerParams(dimension_semantics=semantics, vmem_limit_bytes=vmem_mib * MIB)


def _mod_body(cond_ref, w_ref, b_ref, o_ref):
    o_ref[0] = _mm(_silu(cond_ref[...]), w_ref[0]) + b_ref[0]


def _modulation(cond, ada_w, ada_b):
    depth, d, n = ada_w.shape
    tn = 1024
    return pl.pallas_call(
        _mod_body,
        grid=(depth, n // tn),
        in_specs=[pl.BlockSpec((8, d), lambda l, j: (0, 0)),
                  pl.BlockSpec((1, d, tn), lambda l, j: (l, 0, j)),
                  pl.BlockSpec((1, 1, tn), lambda l, j: (l, 0, j))],
        out_specs=pl.BlockSpec((1, 8, tn), lambda l, j: (l, 0, j)),
        out_shape=jax.ShapeDtypeStruct((depth, 8, n), F32),
        compiler_params=_params(("arbitrary", "arbitrary"), 32),
        name="modulation",
    )(cond, ada_w, ada_b.reshape(depth, 1, n))


def _inproj_body(x_ref, mod_ref, g_ref, w_ref, *refs, n_f32):
    zf_ref = refs[0] if n_f32 else None
    zb_ref, h_scr, w_scr = refs[-3:]
    i = pl.program_id(0)
    j = pl.program_id(1)

    @pl.when(i == 0)
    def _():
        w_scr[j] = w_ref[0].astype(BF16)

    @pl.when(j == 0)
    def _():
        x = x_ref[...]
        m = mod_ref[0]
        y = x * lax.rsqrt(jnp.mean(x * x, axis=-1, keepdims=True) + EPS) * g_ref[0]
        h_scr[...] = (y * (1.0 + m[:, D_MODEL:2 * D_MODEL]) + m[:, 0:D_MODEL]).astype(BF16)

    acc = jnp.dot(h_scr[...], w_scr[j], preferred_element_type=F32)
    if n_f32 == 0:
        zb_ref[...] = acc.astype(BF16)
    else:
        @pl.when(j < n_f32)
        def _():
            zf_ref[...] = acc

        @pl.when(j >= n_f32)
        def _():
            zb_ref[...] = acc.astype(BF16)


IN_TILE = 1024


def _in_proj(x, mod_rows, norm_g, w_in, layer, n_f32):
    n, d = x.shape
    n_cond = mod_rows.shape[0]
    tm, tn = 1024, IN_TILE
    nj = IN_COLS // tn
    per_cond = n // n_cond
    out_specs = [pl.BlockSpec((tm, tn), lambda i, j: (i, jnp.maximum(j - n_f32, 0)))]
    out_shape = [jax.ShapeDtypeStruct((n, IN_COLS - n_f32 * tn), BF16)]
    if n_f32:
        out_specs.insert(0, pl.BlockSpec((tm, tn), lambda i, j: (i, jnp.minimum(j, n_f32 - 1))))
        out_shape.insert(0, jax.ShapeDtypeStruct((n, n_f32 * tn), F32))
    return pl.pallas_call(
        functools.partial(_inproj_body, n_f32=n_f32),
        grid=(n // tm, nj),
        in_specs=[pl.BlockSpec((tm, d), lambda i, j: (i, 0)),
                  pl.BlockSpec((1, 1, 6 * d), lambda i, j: (i * tm // per_cond, 0, 0)),
                  pl.BlockSpec((1, 1, d), lambda i, j: (layer, 0, 0)),
                  pl.BlockSpec((1, d, tn), lambda i, j: (layer, 0, jnp.where(i == 0, j, nj - 1)))],
        out_specs=out_specs,
        out_shape=out_shape,
        scratch_shapes=[pltpu.VMEM((tm, d), BF16), pltpu.VMEM((nj, d, tn), BF16)],
        compiler_params=_params(("arbitrary", "arbitrary"), 48),
        name="in_proj",
    )(x, mod_rows, norm_g.reshape(norm_g.shape[0], 1, d), w_in)


def _log_sigmoid(x):
    return jnp.minimum(x, 0.0) - jnp.log1p(jnp.exp(-jnp.abs(x)))


def _retention(z_ref, c0, decay_ref, ng_ref, s0_ref, o_ref, st_ref, tab_scr, d_scr, u_scr, s_scr, n_tok):
    C = RET_CHUNK
    nc = n_tok // C
    d = HEAD_DIM
    pos = lax.broadcasted_iota(jnp.int32, (C, d), 0).astype(F32)
    ri = lax.broadcasted_iota(jnp.int32, (C, C), 0).astype(F32)
    ci = lax.broadcasted_iota(jnp.int32, (C, C), 1).astype(F32)
    diff = ri - ci
    for h in range(N_HEADS_RET):
        lgf = _log_sigmoid(jnp.full((C, C), decay_ref[0, h], F32))
        lgb = _log_sigmoid(jnp.full((C, C), decay_ref[1, h], F32))
        d_scr[h] = (jnp.where(diff >= 0, jnp.exp(lgf * jnp.maximum(diff, 0.0)), 0.0)
                    + jnp.where(diff <= 0, jnp.exp(lgb * jnp.maximum(-diff, 0.0)), 0.0))
        lf = _log_sigmoid(jnp.full((C, d), decay_ref[0, h], F32))
        lb = _log_sigmoid(jnp.full((C, d), decay_ref[1, h], F32))
        hs = slice(h * d, (h + 1) * d)
        tab_scr[0, :, hs] = jnp.exp(lf * (C - 1.0 - pos))
        tab_scr[1, :, hs] = jnp.exp(lb * pos)
        tab_scr[2, :, hs] = jnp.exp(lf * (pos + 1.0))
        tab_scr[3, :, hs] = jnp.exp(lb * (C - pos))
        tab_scr[4, :, hs] = jnp.exp(lf * C)
        tab_scr[5, :, hs] = jnp.exp(lb * C)

    def rows(c):
        return pl.ds(pl.multiple_of(c * C, C), C)

    def part(p):
        return slice(c0 + p * WIDTH_RET, c0 + (p + 1) * WIDTH_RET)

    W = WIDTH_RET
    head_shift = d.bit_length() - 1
    assert d == 1 << head_shift
    same_head = ((lax.broadcasted_iota(jnp.int32, (W, W), 0) >> head_shift)
                 == (lax.broadcasted_iota(jnp.int32, (W, W), 1) >> head_shift))
    lane_head = lax.broadcasted_iota(jnp.int32, (C, W), 1) >> head_shift
    head_mean = jnp.where(same_head, 1.0 / d, 0.0).astype(BF16)

    def upd(c, carry):
        k = z_ref[rows(c), part(1)].astype(F32) * SCALE
        v = z_ref[rows(c), part(2)]
        u_scr[0, c] = jnp.where(same_head, _mm((k * tab_scr[0]).T, v), 0.0)
        u_scr[1, c] = jnp.where(same_head, _mm((k * tab_scr[1]).T, v), 0.0)
        return carry

    lax.fori_loop(0, nc, upd, 0)

    for direction in range(2):
        if s0_ref is None:
            s_init = jnp.zeros((W, W), F32)
        else:
            u_scr[direction, nc] = jnp.zeros((W, W), F32)
            for h in range(N_HEADS_RET):
                u_scr[direction, nc, h * d:(h + 1) * d, h * d:(h + 1) * d] = s0_ref[0, 0, direction, h]
            s_init = u_scr[direction, nc]
        chunk_decay = tab_scr[4 + direction, 0:1, :]

        def scan(t, s):
            c = t if direction == 0 else nc - 1 - t
            s_scr[direction, c] = s.astype(BF16)
            return chunk_decay * s + u_scr[direction, c]

        s_fin = lax.fori_loop(0, nc, scan, s_init)
        if st_ref is not None:
            for h in range(N_HEADS_RET):
                st_ref[0, direction, h] = s_fin[h * d:(h + 1) * d, h * d:(h + 1) * d]

    def out(c, carry):
        q = z_ref[rows(c), part(0)].astype(BF16)
        k = (z_ref[rows(c), part(1)].astype(F32) * SCALE).astype(BF16)
        v = z_ref[rows(c), part(2)].astype(BF16)
        gate = z_ref[rows(c), part(3)].astype(F32)
        o = (jnp.dot(q, s_scr[0, c], preferred_element_type=F32) * tab_scr[2]
             + jnp.dot(q, s_scr[1, c], preferred_element_type=F32) * tab_scr[3])
        intra = [_mm_nt(jnp.where(lane_head == h, q, jnp.zeros_like(q)), k) * d_scr[h]
                 for h in range(N_HEADS_RET)]
        for h in range(N_HEADS_RET):
            o = o + jnp.where(lane_head == h, _mm(intra[h], v), 0.0)
        sq = o * o
        sq_hi = sq.astype(BF16)
        sq_lo = (sq - sq_hi.astype(F32)).astype(BF16)
        ms = (jnp.dot(sq_hi, head_mean, preferred_element_type=F32)
              + jnp.dot(sq_lo, head_mean, preferred_element_type=F32))
        o = o * lax.rsqrt(ms + EPS) * ng_ref[...]
        o_ref[rows(c), :] = (o * _silu(gate)).astype(o_ref.dtype)
        return carry

    lax.fori_loop(0, nc, out, 0)


def _ret_scratch(n_tok):
    nc = n_tok // RET_CHUNK
    return [pltpu.VMEM((6, RET_CHUNK, WIDTH_RET), F32),
            pltpu.VMEM((N_HEADS_RET, RET_CHUNK, RET_CHUNK), F32),
            pltpu.VMEM((2, nc + 1, WIDTH_RET, WIDTH_RET), F32),
            pltpu.VMEM((2, nc, WIDTH_RET, WIDTH_RET), BF16)]


def _ctx_attn_body(sink_ref, decay_ref, z_ref, zr_ref, ng_ref, *refs, seq, n_prev, layer, depth):
    (ona_ref, oswa_ref, oret_ref, kna_ref, vna_ref, kswa_ref, vswa_ref, st_ref,
     tab_scr, d_scr, u_scr, s_scr) = refs[n_prev:]
    cache_refs = (kna_ref, vna_ref, kswa_ref, vswa_ref, st_ref)
    if n_prev == 0:
        for ref in cache_refs:
            for other in range(depth):
                if other != layer:
                    ref[0, other] = jnp.zeros(ref.shape[2:], ref.dtype)
        kna_ref, vna_ref, kswa_ref, vswa_ref, st_ref = (ref.at[:, pl.ds(layer, 1)] for ref in cache_refs)
    d = HEAD_DIM
    low_half = lax.broadcasted_iota(jnp.int32, (seq, 2 * d), 1) < d

    def pair_scores(q2, k2, head_is_low):
        return _mm_nt(jnp.where(low_half == head_is_low, q2, jnp.zeros_like(q2)), k2)

    jobs = []
    for pair in range(N_HEADS_NA // 2):
        cols = lambda part: slice(OFF_NA + part * WIDTH_NA + pair * 2 * d,
                                  OFF_NA + part * WIDTH_NA + (pair + 1) * 2 * d)
        q2 = (z_ref[:, cols(0)] * SCALE).astype(BF16)
        k2 = z_ref[:, cols(1)]
        v2 = z_ref[:, cols(2)]
        for hh in range(2):
            kna_ref[0, 0, 2 * pair + hh] = k2[:, hh * d:(hh + 1) * d]
            vna_ref[0, 0, 2 * pair + hh] = v2[:, hh * d:(hh + 1) * d]
        k2 = k2.astype(BF16)
        v2 = v2.astype(BF16)
        jobs.append([(pair_scores(q2, k2, hh == 0), v2, hh == 0, None) for hh in range(2)]
                    + [ona_ref, pair])
    lo = OFF_SWA + WIDTH_SWA
    k2 = z_ref[:, lo:lo + WIDTH_KV_SWA]
    v2 = z_ref[:, lo + WIDTH_KV_SWA:lo + 2 * WIDTH_KV_SWA]
    for g in range(N_KV_SWA):
        kswa_ref[0, 0, g] = k2[:, g * d:(g + 1) * d]
        vswa_ref[0, 0, g] = v2[:, g * d:(g + 1) * d]
    k2 = k2.astype(BF16)
    v2 = v2.astype(BF16)
    group = N_HEADS_SWA // N_KV_SWA
    for pair in range(N_HEADS_SWA // 2):
        q2 = (z_ref[:, OFF_SWA + pair * 2 * d:OFF_SWA + (pair + 1) * 2 * d] * SCALE).astype(BF16)
        heads = []
        for hh in range(2):
            h = 2 * pair + hh
            g = h // group
            qh = q2[:, hh * d:(hh + 1) * d]
            q_at_g = jnp.concatenate([qh, qh], axis=-1) if g != hh else q2
            heads.append((pair_scores(q_at_g, k2, g == 0), v2, g == 0, sink_ref[h]))
        jobs.append(heads + [oswa_ref, pair])
    probs = []
    for job in jobs:
        for s, v2, v_is_low, sink in job[:2]:
            m = jnp.max(s, axis=-1, keepdims=True)
            if sink is not None:
                m = jnp.maximum(m, sink)
            e = jnp.exp(s - m)
            den = jnp.sum(e, axis=-1, keepdims=True)
            if sink is not None:
                den = den + jnp.exp(sink - m)
            probs.append((e, den))
    for j, job in enumerate(jobs):
        halves = []
        for hh, (s, v2, v_is_low, sink) in enumerate(job[:2]):
            e, den = probs[2 * j + hh]
            o2 = _mm(e, v2) / den
            halves.append(o2[:, :d] if v_is_low else o2[:, d:])
        o_ref, pair = job[2], job[3]
        o_ref[:, pair * 2 * d:(pair + 1) * 2 * d] = jnp.concatenate(halves, axis=-1).astype(o_ref.dtype)
    _retention(zr_ref, 0, decay_ref, ng_ref, None, oret_ref, st_ref.at[0],
               tab_scr, d_scr, u_scr, s_scr, seq)


def _ctx_mixers(z_att, z_rest, swa_sink, ret_decay, ret_norm_g, caches, layer, depth, batch, seq):
    n = batch * seq
    assert z_att.shape[1] == OFF_RET
    smem = pl.BlockSpec(memory_space=pltpu.SMEM)
    first = caches is None
    layers_in_block = depth if first else 1
    block_layer = 0 if first else layer
    kv = lambda heads: pl.BlockSpec((1, layers_in_block, heads, seq, HEAD_DIM),
                                    lambda b: (b, block_layer, 0, 0, 0))
    kv_shape = lambda heads: jax.ShapeDtypeStruct((batch, depth, heads, seq, HEAD_DIM), F32)
    in_specs = [smem, smem,
                pl.BlockSpec((seq, OFF_RET), lambda b: (b, 0)),
                pl.BlockSpec((seq, 4 * WIDTH_RET), lambda b: (b, 0)),
                pl.BlockSpec((1, WIDTH_RET), lambda b: (0, 0))]
    args = [swa_sink, ret_decay, z_att, z_rest, ret_norm_g.reshape(1, WIDTH_RET)]
    n_prev = 0 if caches is None else len(caches)
    aliases = {len(args) + i: 3 + i for i in range(n_prev)}
    if caches is not None:
        in_specs += [pl.BlockSpec(memory_space=pl.ANY)] * n_prev
        args += list(caches)
    return pl.pallas_call(
        functools.partial(_ctx_attn_body, seq=seq, n_prev=n_prev, layer=layer, depth=depth),
        grid=(batch,),
        in_specs=in_specs,
        out_specs=[pl.BlockSpec((seq, WIDTH_NA), lambda b: (b, 0)),
                   pl.BlockSpec((seq, WIDTH_SWA), lambda b: (b, 0)),
                   pl.BlockSpec((seq, WIDTH_RET), lambda b: (b, 0)),
                   kv(N_HEADS_NA), kv(N_HEADS_NA), kv(N_KV_SWA), kv(N_KV_SWA),
                   pl.BlockSpec((1, layers_in_block, 2, N_HEADS_RET, HEAD_DIM, HEAD_DIM),
                                lambda b: (b, block_layer, 0, 0, 0, 0))],
        out_shape=[jax.ShapeDtypeStruct((n, WIDTH_NA), BF16),
                   jax.ShapeDtypeStruct((n, WIDTH_SWA), BF16),
                   jax.ShapeDtypeStruct((n, WIDTH_RET), BF16),
                   kv_shape(N_HEADS_NA), kv_shape(N_HEADS_NA), kv_shape(N_KV_SWA), kv_shape(N_KV_SWA),
                   jax.ShapeDtypeStruct((batch, depth, 2, N_HEADS_RET, HEAD_DIM, HEAD_DIM), F32)],
        scratch_shapes=_ret_scratch(seq),
        input_output_aliases=aliases,
        compiler_params=_params(("arbitrary",), 32),
        name="ctx_mixers",
    )(*args)


NA_QROWS = 4
NA_WROWS = NA_QROWS + WIN_ROWS


def _na_bias_table(rpb, rows):
    n_groups = rows // NA_QROWS
    assert rows % NA_QROWS == 0 and n_groups >= 3
    n_heads, n_ro, n_co = rpb.shape
    half = WIN_ROWS // 2
    cc = np.arange(GRID_W)
    col_start = np.clip(cc - WIN_COLS // 2, 0, GRID_W - WIN_COLS)
    valid = (cc[None, :] >= col_start[:, None]) & (cc[None, :] < col_start[:, None] + WIN_COLS)
    span = 2 * GRID_W - 1
    lead = GRID_W - WIN_COLS
    ext = jnp.pad(rpb.astype(F32), ((0, 0), (0, 0), (lead, span + 1 - lead - n_co)))
    band = jnp.tile(ext, (1, 1, GRID_W))[..., :GRID_W * span].reshape(n_heads, n_ro, GRID_W, span)
    band = jnp.where(valid[None, None], band[..., GRID_W - 1:], MASKED)
    outside = jnp.full((n_heads, GRID_W, GRID_W), MASKED, F32)
    tabs = []
    for g in (0, 1, n_groups - 1):
        ws = min(max(NA_QROWS * g - half, 0), rows - NA_WROWS)
        q_blocks = []
        for i in range(NA_QROWS):
            r = NA_QROWS * g + i
            rs = min(max(r - half, 0), rows - WIN_ROWS)
            q_blocks.append(jnp.concatenate(
                [band[:, ws + j - r + WIN_ROWS - 1] if rs <= ws + j < rs + WIN_ROWS else outside
                 for j in range(NA_WROWS)], axis=-1))
        tabs.append(jnp.concatenate(q_blocks, axis=-2))
    return jnp.stack(tabs, axis=0)


def _lat_na_body(z_ref, kc_ref, vc_ref, bias_ref, o_ref, kc_scr, vc_scr, *, rows):
    d = HEAD_DIM
    nq = NA_QROWS * GRID_W
    g = pl.program_id(1)

    @pl.when(g == 0)
    def _():
        kc_scr[...] = kc_ref[0, 0].astype(BF16)
        vc_scr[...] = vc_ref[0, 0].astype(BF16)

    ws = jnp.clip(NA_QROWS * g - WIN_ROWS // 2, 0, rows - NA_WROWS)
    qrows = pl.ds(pl.multiple_of(g * nq, nq), nq)
    wrows = pl.ds(pl.multiple_of(ws * GRID_W, GRID_W), NA_WROWS * GRID_W)
    low_half = lax.broadcasted_iota(jnp.int32, (nq, 2 * d), 1) < d
    for pair in range(N_HEADS_NA // 2):
        cols = slice(pair * 2 * d, (pair + 1) * 2 * d)
        q2 = (z_ref[qrows, cols] * SCALE).astype(BF16)
        kl2 = z_ref[wrows, WIDTH_NA + pair * 2 * d:WIDTH_NA + (pair + 1) * 2 * d].astype(BF16)
        vl2 = z_ref[wrows, 2 * WIDTH_NA + pair * 2 * d:2 * WIDTH_NA + (pair + 1) * 2 * d].astype(BF16)
        for hh in range(2):
            h = 2 * pair + hh
            qm = jnp.where(low_half == (hh == 0), q2, jnp.zeros_like(q2))
            s_loc = _mm_nt(qm, kl2) + bias_ref[0, h]
            s_ctx = _mm_nt(q2[:, hh * d:(hh + 1) * d], kc_scr[h])
            m = jnp.maximum(jnp.max(s_ctx, axis=-1, keepdims=True), jnp.max(s_loc, axis=-1, keepdims=True))
            e_ctx = jnp.exp(s_ctx - m)
            e_loc = jnp.exp(s_loc - m)
            den = jnp.sum(e_ctx, axis=-1, keepdims=True) + jnp.sum(e_loc, axis=-1, keepdims=True)
            o = (_mm(e_ctx, vc_scr[h]) + _mm(e_loc, vl2)[:, hh * d:(hh + 1) * d]) / den
            o_ref[:, h * d:(h + 1) * d] = o.astype(o_ref.dtype)


def _lat_na(z, cache_k, cache_v, bias_tab, layer, batch, seq):
    rows = seq // GRID_W
    n_groups = rows // NA_QROWS
    nq = NA_QROWS * GRID_W
    past = cache_k.shape[3]

    def group_class(g):
        return jnp.where(g == 0, 0, jnp.where(g == n_groups - 1, 2, 1))

    cache = pl.BlockSpec((1, 1, N_HEADS_NA, past, HEAD_DIM), lambda b, g: (b, layer, 0, 0, 0))
    return pl.pallas_call(
        functools.partial(_lat_na_body, rows=rows),
        grid=(batch, n_groups),
        in_specs=[pl.BlockSpec((seq, 3 * WIDTH_NA), lambda b, g: (b, 0)),
                  cache, cache,
                  pl.BlockSpec((1, N_HEADS_NA, nq, NA_WROWS * GRID_W),
                               lambda b, g: (group_class(g), 0, 0, 0))],
        out_specs=pl.BlockSpec((nq, WIDTH_NA), lambda b, g: (b * n_groups + g, 0)),
        out_shape=jax.ShapeDtypeStruct((batch * seq, WIDTH_NA), BF16),
        scratch_shapes=[pltpu.VMEM((N_HEADS_NA, past, HEAD_DIM), BF16),
                        pltpu.VMEM((N_HEADS_NA, past, HEAD_DIM), BF16)],
        compiler_params=_params(("arbitrary", "arbitrary"), 58),
        name="lat_na",
    )(z, cache_k, cache_v, bias_tab)


def _rope_tables(seq):
    t = jnp.arange(seq)
    n_freq = HEAD_DIM // 4
    inv = ROPE_BASE ** (-jnp.arange(n_freq, dtype=F32) / n_freq)
    ang = jnp.concatenate([(t // GRID_W).astype(F32)[:, None] * inv,
                           (t % GRID_W).astype(F32)[:, None] * inv], axis=-1)
    cos = jnp.repeat(jnp.cos(ang), 2, axis=-1)
    sign = jnp.tile(jnp.asarray([-1.0, 1.0], F32), HEAD_DIM // 2)
    sin = jnp.repeat(jnp.sin(ang), 2, axis=-1) * sign
    return jnp.tile(cos, (1, N_HEADS_SWA)), jnp.tile(sin, (1, N_HEADS_SWA))


def _rope(x, cos, sin_signed):
    width = x.shape[-1]
    lane = lax.broadcasted_iota(jnp.int32, x.shape, x.ndim - 1)
    partner = jnp.where((lane & 1) == 0, pltpu.roll(x, width - 1, x.ndim - 1), pltpu.roll(x, 1, x.ndim - 1))
    return x * cos + partner * sin_signed


def _lat_swa_body(sink_ref, z_ref, kc_ref, vc_ref, cos_ref, sin_ref, o_ref, kr_scr, kc_scr, vc_scr, *, seq):
    d = HEAD_DIM
    blk = SWA_BLOCK
    n = pl.program_id(1)
    assert N_KV_SWA == 2

    @pl.when(n == 0)
    def _():
        k = z_ref[:, WIDTH_SWA:WIDTH_SWA + WIDTH_KV_SWA].astype(F32)
        kr_scr[...] = _rope(k, cos_ref[:, :WIDTH_KV_SWA], sin_ref[:, :WIDTH_KV_SWA]).astype(BF16)
        kc_scr[...] = jnp.concatenate([kc_ref[0, 0, 0], kc_ref[0, 0, 1]], axis=-1).astype(BF16)
        vc_scr[...] = jnp.concatenate([vc_ref[0, 0, 0], vc_ref[0, 0, 1]], axis=-1).astype(BF16)

    qrows = pl.ds(pl.multiple_of(n * blk, blk), blk)
    kstart = jnp.clip((n - 1) * blk, 0, seq - 3 * blk)
    krows = pl.ds(pl.multiple_of(kstart, blk), 3 * blk)
    q_all = _rope(z_ref[qrows, 0:WIDTH_SWA].astype(F32), cos_ref[qrows, :], sin_ref[qrows, :]) * SCALE
    q_all = q_all.astype(BF16)
    qpos = n * blk + lax.broadcasted_iota(jnp.int32, (blk, 3 * blk), 0)
    kpos = kstart + lax.broadcasted_iota(jnp.int32, (blk, 3 * blk), 1)
    valid = jnp.abs(qpos - kpos) <= SWA_RADIUS
    kl2 = kr_scr[krows, :]
    vl2 = z_ref[krows, WIDTH_SWA + WIDTH_KV_SWA:WIDTH_SWA + 2 * WIDTH_KV_SWA]
    group = N_HEADS_SWA // N_KV_SWA
    zero = jnp.zeros((blk, d), BF16)
    scores = []
    for h in range(N_HEADS_SWA):
        g = h // group
        qh = q_all[:, h * d:(h + 1) * d]
        q_at_g = jnp.concatenate([qh, zero] if g == 0 else [zero, qh], axis=-1)
        scores.append((jnp.where(valid, _mm_nt(q_at_g, kl2), MASKED), _mm_nt(q_at_g, kc_scr[...])))
    probs = []
    for h, (s_loc, s_ctx) in enumerate(scores):
        m = jnp.maximum(jnp.maximum(jnp.max(s_loc, axis=-1, keepdims=True),
                                    jnp.max(s_ctx, axis=-1, keepdims=True)), sink_ref[h])
        e_loc = jnp.exp(s_loc - m)
        e_ctx = jnp.exp(s_ctx - m)
        den = (jnp.sum(e_loc, axis=-1, keepdims=True) + jnp.sum(e_ctx, axis=-1, keepdims=True)
               + jnp.exp(sink_ref[h] - m))
        probs.append((e_loc, e_ctx, den))
    outs = []
    for h, (e_loc, e_ctx, den) in enumerate(probs):
        g = h // group
        o2 = (_mm(e_ctx, vc_scr[...]) + _mm(e_loc, vl2)) / den
        outs.append(o2[:, g * d:(g + 1) * d])
    o_ref[...] = jnp.concatenate(outs, axis=-1).astype(o_ref.dtype)


def _lat_swa(z, cache_k, cache_v, swa_sink, cos, sin, layer, batch, seq):
    nb = seq // SWA_BLOCK
    past = cache_k.shape[3]
    width = WIDTH_SWA + 2 * WIDTH_KV_SWA
    assert OFF_SWA % width == 0 and seq >= 3 * SWA_BLOCK
    cache = pl.BlockSpec((1, 1, N_KV_SWA, past, HEAD_DIM), lambda b, n: (b, layer, 0, 0, 0))
    table = pl.BlockSpec((seq, WIDTH_SWA), lambda b, n: (0, 0))
    return pl.pallas_call(
        functools.partial(_lat_swa_body, seq=seq),
        grid=(batch, nb),
        in_specs=[pl.BlockSpec(memory_space=pltpu.SMEM),
                  pl.BlockSpec((seq, width), lambda b, n: (b, OFF_SWA // width)),
                  cache, cache, table, table],
        out_specs=pl.BlockSpec((SWA_BLOCK, WIDTH_SWA), lambda b, n: (b * nb + n, 0)),
        out_shape=jax.ShapeDtypeStruct((batch * seq, WIDTH_SWA), BF16),
        scratch_shapes=[pltpu.VMEM((seq, WIDTH_KV_SWA), BF16),
                        pltpu.VMEM((past, WIDTH_KV_SWA), BF16), pltpu.VMEM((past, WIDTH_KV_SWA), BF16)],
        compiler_params=_params(("arbitrary", "arbitrary"), 32),
        name="lat_swa",
    )(swa_sink, z, cache_k, cache_v, cos, sin)


def _lat_ret_body(decay_ref, z_ref, ng_ref, s0_ref, o_ref, tab_scr, d_scr, u_scr, s_scr, *, seq):
    _retention(z_ref, 0, decay_ref, ng_ref, s0_ref, o_ref, None, tab_scr, d_scr, u_scr, s_scr, seq)


def _lat_ret(z, ret_decay, ret_norm_g, state, layer, batch, seq):
    width = 4 * WIDTH_RET
    assert OFF_RET % width == 0
    return pl.pallas_call(
        functools.partial(_lat_ret_body, seq=seq),
        grid=(batch,),
        in_specs=[pl.BlockSpec(memory_space=pltpu.SMEM),
                  pl.BlockSpec((seq, width), lambda b: (b, OFF_RET // width)),
                  pl.BlockSpec((1, WIDTH_RET), lambda b: (0, 0)),
                  pl.BlockSpec((1, 1, 2, N_HEADS_RET, HEAD_DIM, HEAD_DIM),
                               lambda b: (b, layer, 0, 0, 0, 0))],
        out_specs=pl.BlockSpec((seq, WIDTH_RET), lambda b: (b, 0)),
        out_shape=jax.ShapeDtypeStruct((batch * seq, WIDTH_RET), BF16),
        scratch_shapes=_ret_scratch(seq),
        compiler_params=_params(("arbitrary",), 40),
        name="lat_ret",
    )(ret_decay, z, ret_norm_g.reshape(1, WIDTH_RET), state)


def _merge_body(ona_ref, oswa_ref, oret_ref, ga_ref, gs_ref, gr_ref, x_ref, mod_ref, g2_ref,
                wna_ref, wswa_ref, wret_ref, wout_ref, rw_ref, rb_ref,
                x1_ref, hx_ref, wna_scr, wswa_scr, wret_scr, wout_scr, rw_scr):
    @pl.when(pl.program_id(0) == 0)
    def _():
        wna_scr[...] = wna_ref[0].astype(BF16)
        wswa_scr[...] = wswa_ref[0].astype(BF16)
        wret_scr[...] = wret_ref[0].astype(BF16)
        wout_scr[...] = wout_ref[0].astype(BF16)
        rw = rw_ref[0]
        rw_hi = rw.astype(BF16)
        rw_scr[0] = rw_hi
        rw_scr[1] = (rw - rw_hi.astype(F32)).astype(BF16)

    D = D_MODEL
    m = mod_ref[0]
    gate = lambda ref: _sigmoid(ref[...].astype(F32))
    z = (gate(ga_ref) * jnp.dot(ona_ref[...], wna_scr[...], preferred_element_type=F32)
         + gate(gs_ref) * jnp.dot(oswa_ref[...], wswa_scr[...], preferred_element_type=F32)
         + gate(gr_ref) * jnp.dot(oret_ref[...], wret_scr[...], preferred_element_type=F32))
    y = jnp.dot(z.astype(BF16), wout_scr[...], preferred_element_type=F32)
    x1 = x_ref[...] + m[:, 2 * D:3 * D] * y
    x1_ref[...] = x1
    h2 = x1 * lax.rsqrt(jnp.mean(x1 * x1, axis=-1, keepdims=True) + EPS) * g2_ref[0]
    h2 = h2 * (1.0 + m[:, 4 * D:5 * D]) + m[:, 3 * D:4 * D]
    hx_ref[:, 0:D] = h2

    h_hi = h2.astype(BF16)
    h_lo = (h2 - h_hi.astype(F32)).astype(BF16)
    logit = (jnp.dot(h_hi, rw_scr[0], preferred_element_type=F32)
             + jnp.dot(h_hi, rw_scr[1], preferred_element_type=F32)
             + jnp.dot(h_lo, rw_scr[0], preferred_element_type=F32)) + rb_ref[0]
    lane = lax.broadcasted_iota(jnp.int32, logit.shape, 1).astype(F32)
    neg = jnp.float32(-jnp.inf)
    big = jnp.float32(ROUTER_LANES)
    is_group = lane < N_GROUPS
    is_expert = (lane >= N_GROUPS) & (lane < N_GROUPS + N_EXPERTS)
    gl = jnp.where(is_group, logit, neg)
    gmax = jnp.max(gl, axis=-1, keepdims=True)
    gsel = jnp.min(jnp.where(gl == gmax, lane, big), axis=-1, keepdims=True)
    p_group = 1.0 / jnp.sum(jnp.where(is_group, jnp.exp(gl - gmax), 0.0), axis=-1, keepdims=True)
    eidx = lane - N_GROUPS
    in_group = is_expert & (jnp.floor(eidx / EXPERTS_PER_GROUP) == gsel)
    el = jnp.where(in_group, logit, neg)
    top1 = jnp.max(el, axis=-1, keepdims=True)
    sel1 = jnp.min(jnp.where(el == top1, lane, big), axis=-1, keepdims=True)
    el2 = jnp.where(lane == sel1, neg, el)
    top2 = jnp.max(el2, axis=-1, keepdims=True)
    sel2 = jnp.min(jnp.where(el2 == top2, lane, big), axis=-1, keepdims=True)
    e2 = jnp.exp(top2 - top1)
    w1 = p_group / (1.0 + e2)
    w2 = p_group * e2 / (1.0 + e2)
    comb = jnp.where(lane == sel1, w1, 0.0) + jnp.where(lane == sel2, w2, 0.0)
    hx_ref[:, D:D + ROUTER_LANES] = comb + jnp.where(lane == ROUTE_GROUP_LANE, gsel, 0.0)


def _merge(o_na, o_swa, o_ret, z, gate_tile, x, mod_rows, norm2_g, w_br_na, w_br_swa, w_br_ret, w_out,
           router_w, router_b, layer):
    n, d = x.shape
    n_cond = mod_rows.shape[0]
    per_cond = n // n_cond
    tm = 512
    gate = lambda k: pl.BlockSpec((tm, d), lambda i: (i, gate_tile + k))
    row = lambda w: pl.BlockSpec((tm, w), lambda i: (i, 0))
    weight = lambda k: pl.BlockSpec((1, k, d), lambda i: (layer, 0, 0))
    return pl.pallas_call(
        _merge_body,
        grid=(n // tm,),
        in_specs=[row(WIDTH_NA), row(WIDTH_SWA), row(WIDTH_RET),
                  gate(0), gate(1), gate(2),
                  row(d),
                  pl.BlockSpec((1, 1, 6 * d), lambda i: (i * tm // per_cond, 0, 0)),
                  pl.BlockSpec((1, 1, d), lambda i: (layer, 0, 0)),
                  weight(WIDTH_NA), weight(WIDTH_SWA), weight(WIDTH_RET), weight(d),
                  pl.BlockSpec((1, d, ROUTER_LANES), lambda i: (layer, 0, 0)),
                  pl.BlockSpec((1, 1, ROUTER_LANES), lambda i: (layer, 0, 0))],
        out_specs=[row(d), row(d + ROUTER_LANES)],
        out_shape=[jax.ShapeDtypeStruct((n, d), F32),
                   jax.ShapeDtypeStruct((n, d + ROUTER_LANES), F32)],
        scratch_shapes=[pltpu.VMEM((WIDTH_NA, d), BF16), pltpu.VMEM((WIDTH_SWA, d), BF16),
                        pltpu.VMEM((WIDTH_RET, d), BF16), pltpu.VMEM((d, d), BF16),
                        pltpu.VMEM((2, d, ROUTER_LANES), BF16)],
        compiler_params=_params(("arbitrary",), 56),
        name="merge_router",
    )(o_na, o_swa, o_ret, z, z, z, x, mod_rows, norm2_g.reshape(norm2_g.shape[0], 1, d),
      w_br_na, w_br_swa, w_br_ret, w_out, router_w, router_b)


MOE_TILE = 512
FIN_TILE = 512


def _route_plan(hx, n_tiles):
    n = hx.shape[0]
    g = hx[:, D_MODEL + ROUTE_GROUP_LANE].astype(jnp.int32)
    onehot = (g[:, None] == jnp.arange(N_GROUPS, dtype=jnp.int32)[None, :]).astype(jnp.int32)
    csum = jnp.cumsum(onehot, axis=0)
    counts = csum[-1]
    ends = jnp.cumsum(counts)
    dest = jnp.sum(onehot * ((ends - counts)[None, :] + csum - 1), axis=1)
    first = jnp.arange(n_tiles, dtype=jnp.int32) * MOE_TILE
    tile_lo = jnp.sum((first[:, None] >= ends[None, :]).astype(jnp.int32), axis=1)
    tile_hi = jnp.sum(((first + MOE_TILE - 1)[:, None] >= ends[None, :]).astype(jnp.int32), axis=1)
    return dest, tile_lo, tile_hi


def _row_copies(src_hbm, idx_ref, base, buf, sem, n_rows, start):
    def body(i, carry):
        copy = pltpu.make_async_copy(src_hbm.at[pl.ds(idx_ref[base + i], 1)], buf.at[pl.ds(i, 1)], sem)
        if start:
            copy.start()
        else:
            copy.wait()
        return carry

    lax.fori_loop(0, n_rows, body, 0, unroll=8)


def _moe_body(dest_ref, lo_ref, hi_ref, hx_hbm, wg_hbm, wu_hbm, wd_hbm, o_ref,
              order_scr, hbuf, wg_scr, wu_scr, wd_scr, row_sem, w_sem, *, layer, n_tiles, n_tok):
    t = pl.program_id(0)
    slot = lax.rem(t, 2)
    nxt = jnp.minimum(t + 1, n_tiles - 1)

    def row_copy(tile, i, slot_):
        return pltpu.make_async_copy(hx_hbm.at[pl.ds(order_scr[tile * MOE_TILE + i], 1)],
                                     hbuf.at[slot_, pl.ds(i, 1)], row_sem.at[slot_])

    def for_rows(fn):
        def body(i, carry):
            fn(i)
            return carry
        lax.fori_loop(0, MOE_TILE, body, 0, unroll=8)

    @pl.when(t == 0)
    def _():
        copies = [pltpu.make_async_copy(src.at[layer], dst, w_sem.at[k])
                  for k, (src, dst) in enumerate(((wg_hbm, wg_scr), (wu_hbm, wu_scr), (wd_hbm, wd_scr)))]
        for copy in copies:
            copy.start()

        def invert(i, carry):
            order_scr[dest_ref[i]] = i
            return carry
        lax.fori_loop(0, n_tok, invert, 0, unroll=8)
        for_rows(lambda i: row_copy(0, i, 0).start())
        for copy in copies:
            copy.wait()

    for_rows(lambda i: row_copy(t, i, slot).wait())
    hx = hbuf[slot]
    h = hx[:, 0:D_MODEL].astype(BF16)
    route = hx[:, D_MODEL:D_MODEL + ROUTER_LANES]
    lane = lax.broadcasted_iota(jnp.int32, route.shape, 1)
    rows_per_expert = MOE_TILE // EXPERTS_PER_GROUP
    for k in range(N_GROUPS):
        def group(k=k):
            first = (lo_ref[t] + k) * EXPERTS_PER_GROUP
            y = None
            for j in range(EXPERTS_PER_GROUP):
                e = first + j
                hg = jnp.dot(h, wg_scr[e], preferred_element_type=F32)
                hu = jnp.dot(h, wu_scr[e], preferred_element_type=F32)
                c = jnp.sum(jnp.where(lane == ROUTE_EXPERT_LANE + e, route, 0.0), axis=-1, keepdims=True)
                a = _silu(hg) * hu * c
                part = jnp.dot(a.astype(BF16), wd_scr[e], preferred_element_type=F32)
                y = part if y is None else y + part
                if k == 0:
                    for i in range(j * rows_per_expert, (j + 1) * rows_per_expert):
                        row_copy(nxt, i, 1 - slot).start()
            if k == 0:
                o_ref[...] = y
            else:
                o_ref[...] += y

        if k == 0:
            group()
        else:
            pl.when(lo_ref[t] + k <= hi_ref[t])(group)

    @pl.when(t == n_tiles - 1)
    def _():
        for_rows(lambda i: row_copy(nxt, i, 1 - slot).wait())


def _moe_sorted(hx, dest, tile_lo, tile_hi, w_gate, w_up, w_down, layer):
    n = hx.shape[0]
    d = D_MODEL
    n_tiles = n // MOE_TILE
    any_space = pl.BlockSpec(memory_space=pl.ANY)
    return pl.pallas_call(
        functools.partial(_moe_body, layer=layer, n_tiles=n_tiles, n_tok=n),
        grid_spec=pltpu.PrefetchScalarGridSpec(
            num_scalar_prefetch=3,
            grid=(n_tiles,),
            in_specs=[any_space, any_space, any_space, any_space],
            out_specs=pl.BlockSpec((MOE_TILE, d), lambda t, *_: (t, 0)),
            scratch_shapes=[pltpu.SMEM((n,), jnp.int32),
                            pltpu.VMEM((2, MOE_TILE, d + ROUTER_LANES), F32),
                            pltpu.VMEM((N_EXPERTS, d, D_EXPERT), BF16),
                            pltpu.VMEM((N_EXPERTS, d, D_EXPERT), BF16),
                            pltpu.VMEM((N_EXPERTS, D_EXPERT, d), BF16),
                            pltpu.SemaphoreType.DMA((2,)),
                            pltpu.SemaphoreType.DMA((3,))]),
        out_shape=jax.ShapeDtypeStruct((n, d), F32),
        compiler_params=_params(("arbitrary",), 56),
        name="moe_sorted",
    )(dest, tile_lo, tile_hi, hx, w_gate, w_up, w_down)


def _moe_finish_body(dest_ref, y_hbm, x1_ref, mod_ref, fg_ref, o_ref, ybuf, row_sem, *, final, n_tiles):
    t = pl.program_id(0)
    slot = lax.rem(t, 2)

    def rows_of(tile, slot_, start):
        _row_copies(y_hbm, dest_ref, tile * FIN_TILE, ybuf.at[slot_], row_sem.at[slot_], FIN_TILE, start)

    @pl.when(t == 0)
    def _():
        rows_of(0, 0, True)

    @pl.when(t + 1 < n_tiles)
    def _():
        rows_of(t + 1, 1 - slot, True)

    rows_of(t, slot, False)
    x2 = x1_ref[...] + mod_ref[0][:, 5 * D_MODEL:6 * D_MODEL] * ybuf[slot]
    if final:
        x2 = x2 * lax.rsqrt(jnp.mean(x2 * x2, axis=-1, keepdims=True) + EPS) * fg_ref[...]
    o_ref[...] = x2


def _moe_finish(y_sorted, dest, x1, mod_rows, final_g, final):
    n, d = x1.shape
    n_cond = mod_rows.shape[0]
    per_cond = n // n_cond
    n_tiles = n // FIN_TILE
    return pl.pallas_call(
        functools.partial(_moe_finish_body, final=final, n_tiles=n_tiles),
        grid_spec=pltpu.PrefetchScalarGridSpec(
            num_scalar_prefetch=1,
            grid=(n_tiles,),
            in_specs=[pl.BlockSpec(memory_space=pl.ANY),
                      pl.BlockSpec((FIN_TILE, d), lambda t, *_: (t, 0)),
                      pl.BlockSpec((1, 1, 6 * d), lambda t, *_: (t * FIN_TILE // per_cond, 0, 0)),
                      pl.BlockSpec((1, d), lambda t, *_: (0, 0))],
            out_specs=pl.BlockSpec((FIN_TILE, d), lambda t, *_: (t, 0)),
            scratch_shapes=[pltpu.VMEM((2, FIN_TILE, d), F32), pltpu.SemaphoreType.DMA((2,))]),
        out_shape=jax.ShapeDtypeStruct((n, d), F32),
        compiler_params=_params(("arbitrary",), 32),
        name="moe_finish",
    )(dest, y_sorted, x1, mod_rows, final_g.reshape(1, d))


def _moe(hx, x1, mod_rows, w_gate, w_up, w_down, final_g, layer, final):
    dest, tile_lo, tile_hi = _route_plan(hx, hx.shape[0] // MOE_TILE)
    y_sorted = _moe_sorted(hx, dest, tile_lo, tile_hi, w_gate, w_up, w_down, layer)
    return _moe_finish(y_sorted, dest, x1, mod_rows, final_g, final)


def kernel(x_prompt, x_sample, c, cache_na_k, cache_na_v, cache_swa_k, cache_swa_v, state_ret, c_ctx, ada_w, ada_b, norm1_g, norm2_g, w_in, na_rpb, swa_sink, ret_decay, ret_norm_g, w_br_na, w_br_swa, w_br_ret, w_out, router_group_w, router_group_b, router_expert_w, router_expert_b, w_gate, w_up, w_down, final_norm_g):
    batch, seq, d = x_prompt.shape
    dec_batch, dec_seq, _ = x_sample.shape
    depth = ada_w.shape[0]
    assert d == D_MODEL and dec_batch + 1 <= 8
    assert OFF_RET % IN_TILE == 0 and OFF_GATES % d == 0 and (OFF_GATES - OFF_RET) % d == 0

    cond = jnp.concatenate([c_ctx[None], c, jnp.zeros((8 - 1 - dec_batch, d), F32)], axis=0)
    mod = _modulation(cond, ada_w, ada_b)
    router_w = jnp.concatenate([router_group_w, router_expert_w], axis=-1)
    router_w = jnp.pad(router_w, ((0, 0), (0, 0), (0, ROUTER_LANES - router_w.shape[-1])))
    router_b = jnp.concatenate([router_group_b, router_expert_b], axis=-1)
    router_b = jnp.pad(router_b, ((0, 0), (0, ROUTER_LANES - router_b.shape[-1])))[:, None, :]
    cos, sin = _rope_tables(dec_seq)
    moe_weights = (w_gate.astype(BF16), w_up.astype(BF16), w_down.astype(BF16))

    xp = x_prompt.reshape(batch * seq, d)
    xs = x_sample.reshape(dec_batch * dec_seq, d)
    caches = None
    for l in range(depth):
        final = l == depth - 1
        mod_ctx = mod[l, 0:1][:, None, :]
        mod_lat = mod[l, 1:1 + dec_batch][:, None, :]
        merge_w = (norm2_g, w_br_na, w_br_swa, w_br_ret, w_out, router_w, router_b, l)
        moe_w = (*moe_weights, final_norm_g, l, final)

        z_att, z_rest = _in_proj(xp, mod_ctx, norm1_g, w_in, l, OFF_RET // IN_TILE)
        o_na, o_swa, o_ret, *caches = _ctx_mixers(
            z_att, z_rest, swa_sink[l], ret_decay[l], ret_norm_g[l], caches, l, depth, batch, seq)
        x1, hx = _merge(o_na, o_swa, o_ret, z_rest, (OFF_GATES - OFF_RET) // d, xp, mod_ctx, *merge_w)
        xp = _moe(hx, x1, mod_ctx, *moe_w)

        z, = _in_proj(xs, mod_lat, norm1_g, w_in, l, 0)
        o_na = _lat_na(z, cache_na_k, cache_na_v, _na_bias_table(na_rpb[l], dec_seq // GRID_W),
                       l, dec_batch, dec_seq)
        o_swa = _lat_swa(z, cache_swa_k, cache_swa_v, swa_sink[l], cos, sin, l, dec_batch, dec_seq)
        o_ret = _lat_ret(z, ret_decay[l], ret_norm_g[l], state_ret, l, dec_batch, dec_seq)
        x1, hx = _merge(o_na, o_swa, o_ret, z, OFF_GATES // d, xs, mod_lat, *merge_w)
        xs = _moe(hx, x1, mod_lat, *moe_w)

    return (xp.reshape(batch, seq, d), xs.reshape(dec_batch, dec_seq, d), *caches)
```

```python
import functools

import numpy as np
import jax
import jax.numpy as jnp
from jax import lax
from jax.experimental import pallas as pl
from jax.experimental.pallas import tpu as pltpu

F32 = jnp.float32
BF16 = jnp.bfloat16

D_MODEL = 1024
HEAD_DIM = 64
GRID_W = 64
N_HEADS_NA = 8
WIN_ROWS = 8
WIN_COLS = 16
N_HEADS_SWA = 4
N_KV_SWA = 2
SWA_RADIUS = 128
SWA_BLOCK = 128
N_HEADS_RET = 4
RET_CHUNK = 128
WIDTH_NA = N_HEADS_NA * HEAD_DIM
WIDTH_SWA = N_HEADS_SWA * HEAD_DIM
WIDTH_KV_SWA = N_KV_SWA * HEAD_DIM
WIDTH_RET = N_HEADS_RET * HEAD_DIM
IN_COLS = 3 * WIDTH_NA + WIDTH_SWA + 2 * WIDTH_KV_SWA + 4 * WIDTH_RET + 3 * D_MODEL
OFF_NA = 0
OFF_SWA = 3 * WIDTH_NA
OFF_RET = OFF_SWA + WIDTH_SWA + 2 * WIDTH_KV_SWA
OFF_GATES = OFF_RET + 4 * WIDTH_RET
N_GROUPS = 4
EXPERTS_PER_GROUP = 4
N_EXPERTS = N_GROUPS * EXPERTS_PER_GROUP
D_EXPERT = D_MODEL // 4
ROPE_BASE = 10000.0
EPS = 1e-6
SCALE = HEAD_DIM ** -0.5
MASKED = -1e30
ROUTER_LANES = 128
ROUTE_GROUP_LANE = 0
ROUTE_EXPERT_LANE = N_GROUPS

MIB = 1024 * 1024


def _mm(a, b):
    return jnp.dot(a.astype(BF16), b.astype(BF16), preferred_element_type=F32)


def _mm_nt(a, b):
    return lax.dot_general(a.astype(BF16), b.astype(BF16), (((1,), (1,)), ((), ())),
                           preferred_element_type=F32)


def _sigmoid(x):
    return 1.0 / (1.0 + jnp.exp(-x))


def _silu(x):
    return x * _sigmoid(x)


def _params(semantics, vmem_mib):
    return pltpu.CompilerParams(dimension_semantics=semantics, vmem_limit_bytes=vmem_mib * MIB)


def _mod_body(cond_ref, w_ref, b_ref, o_ref):
    o_ref[0] = _mm(_silu(cond_ref[...]), w_ref[0]) + b_ref[0]


def _modulation(cond, ada_w, ada_b):
    depth, d, n = ada_w.shape
    tn = 1024
    return pl.pallas_call(
        _mod_body,
        grid=(depth, n // tn),
        in_specs=[pl.BlockSpec((8, d), lambda l, j: (0, 0)),
                  pl.BlockSpec((1, d, tn), lambda l, j: (l, 0, j)),
                  pl.BlockSpec((1, 1, tn), lambda l, j: (l, 0, j))],
        out_specs=pl.BlockSpec((1, 8, tn), lambda l, j: (l, 0, j)),
        out_shape=jax.ShapeDtypeStruct((depth, 8, n), F32),
        compiler_params=_params(("arbitrary", "arbitrary"), 32),
        name="modulation",
    )(cond, ada_w, ada_b.reshape(depth, 1, n))


def _inproj_body(x_ref, mod_ref, g_ref, w_ref, *refs, n_f32):
    zf_ref = refs[0] if n_f32 else None
    zb_ref, h_scr, w_scr = refs[-3:]
    i = pl.program_id(0)
    j = pl.program_id(1)

    @pl.when(i == 0)
    def _():
        w_scr[j] = w_ref[0].astype(BF16)

    @pl.when(j == 0)
    def _():
        x = x_ref[...]
        m = mod_ref[0]
        y = x * lax.rsqrt(jnp.mean(x * x, axis=-1, keepdims=True) + EPS) * g_ref[0]
        h_scr[...] = (y * (1.0 + m[:, D_MODEL:2 * D_MODEL]) + m[:, 0:D_MODEL]).astype(BF16)

    acc = jnp.dot(h_scr[...], w_scr[j], preferred_element_type=F32)
    if n_f32 == 0:
        zb_ref[...] = acc.astype(BF16)
    else:
        @pl.when(j < n_f32)
        def _():
            zf_ref[...] = acc

        @pl.when(j >= n_f32)
        def _():
            zb_ref[...] = acc.astype(BF16)


IN_TILE = 1024


def _in_proj(x, mod_rows, norm_g, w_in, layer, n_f32):
    n, d = x.shape
    n_cond = mod_rows.shape[0]
    tm, tn = 1024, IN_TILE
    nj = IN_COLS // tn
    per_cond = n // n_cond
    out_specs = [pl.BlockSpec((tm, tn), lambda i, j: (i, jnp.maximum(j - n_f32, 0)))]
    out_shape = [jax.ShapeDtypeStruct((n, IN_COLS - n_f32 * tn), BF16)]
    if n_f32:
        out_specs.insert(0, pl.BlockSpec((tm, tn), lambda i, j: (i, jnp.minimum(j, n_f32 - 1))))
        out_shape.insert(0, jax.ShapeDtypeStruct((n, n_f32 * tn), F32))
    return pl.pallas_call(
        functools.partial(_inproj_body, n_f32=n_f32),
        grid=(n // tm, nj),
        in_specs=[pl.BlockSpec((tm, d), lambda i, j: (i, 0)),
                  pl.BlockSpec((1, 1, 6 * d), lambda i, j: (i * tm // per_cond, 0, 0)),
                  pl.BlockSpec((1, 1, d), lambda i, j: (layer, 0, 0)),
                  pl.BlockSpec((1, d, tn), lambda i, j: (layer, 0, jnp.where(i == 0, j, nj - 1)))],
        out_specs=out_specs,
        out_shape=out_shape,
        scratch_shapes=[pltpu.VMEM((tm, d), BF16), pltpu.VMEM((nj, d, tn), BF16)],
        compiler_params=_params(("arbitrary", "arbitrary"), 48),
        name="in_proj",
    )(x, mod_rows, norm_g.reshape(norm_g.shape[0], 1, d), w_in)


def _log_sigmoid(x):
    return jnp.minimum(x, 0.0) - jnp.log1p(jnp.exp(-jnp.abs(x)))


def _retention(z_ref, c0, decay_ref, ng_ref, s0_ref, o_ref, st_ref, tab_scr, d_scr, u_scr, s_scr, n_tok):
    C = RET_CHUNK
    nc = n_tok // C
    d = HEAD_DIM
    pos = lax.broadcasted_iota(jnp.int32, (C, d), 0).astype(F32)
    ri = lax.broadcasted_iota(jnp.int32, (C, C), 0).astype(F32)
    ci = lax.broadcasted_iota(jnp.int32, (C, C), 1).astype(F32)
    diff = ri - ci
    for h in range(N_HEADS_RET):
        lgf = _log_sigmoid(jnp.full((C, C), decay_ref[0, h], F32))
        lgb = _log_sigmoid(jnp.full((C, C), decay_ref[1, h], F32))
        d_scr[h] = (jnp.where(diff >= 0, jnp.exp(lgf * jnp.maximum(diff, 0.0)), 0.0)
                    + jnp.where(diff <= 0, jnp.exp(lgb * jnp.maximum(-diff, 0.0)), 0.0))
        lf = _log_sigmoid(jnp.full((C, d), decay_ref[0, h], F32))
        lb = _log_sigmoid(jnp.full((C, d), decay_ref[1, h], F32))
        hs = slice(h * d, (h + 1) * d)
        tab_scr[0, :, hs] = jnp.exp(lf * (C - 1.0 - pos))
        tab_scr[1, :, hs] = jnp.exp(lb * pos)
        tab_scr[2, :, hs] = jnp.exp(lf * (pos + 1.0))
        tab_scr[3, :, hs] = jnp.exp(lb * (C - pos))
        tab_scr[4, :, hs] = jnp.exp(lf * C)
        tab_scr[5, :, hs] = jnp.exp(lb * C)

    def rows(c):
        return pl.ds(pl.multiple_of(c * C, C), C)

    def part(p):
        return slice(c0 + p * WIDTH_RET, c0 + (p + 1) * WIDTH_RET)

    W = WIDTH_RET
    head_shift = d.bit_length() - 1
    assert d == 1 << head_shift
    same_head = ((lax.broadcasted_iota(jnp.int32, (W, W), 0) >> head_shift)
                 == (lax.broadcasted_iota(jnp.int32, (W, W), 1) >> head_shift))
    lane_head = lax.broadcasted_iota(jnp.int32, (C, W), 1) >> head_shift
    head_mean = jnp.where(same_head, 1.0 / d, 0.0).astype(BF16)

    def upd(c, carry):
        k = z_ref[rows(c), part(1)].astype(F32) * SCALE
        v = z_ref[rows(c), part(2)]
        u_scr[0, c] = jnp.where(same_head, _mm((k * tab_scr[0]).T, v), 0.0)
        u_scr[1, c] = jnp.where(same_head, _mm((k * tab_scr[1]).T, v), 0.0)
        return carry

    lax.fori_loop(0, nc, upd, 0)

    for direction in range(2):
        if s0_ref is None:
            s_init = jnp.zeros((W, W), F32)
        else:
            u_scr[direction, nc] = jnp.zeros((W, W), F32)
            for h in range(N_HEADS_RET):
                u_scr[direction, nc, h * d:(h + 1) * d, h * d:(h + 1) * d] = s0_ref[0, 0, direction, h]
            s_init = u_scr[direction, nc]
        chunk_decay = tab_scr[4 + direction, 0:1, :]

        def scan(t, s):
            c = t if direction == 0 else nc - 1 - t
            s_scr[direction, c] = s.astype(BF16)
            return chunk_decay * s + u_scr[direction, c]

        s_fin = lax.fori_loop(0, nc, scan, s_init)
        if st_ref is not None:
            for h in range(N_HEADS_RET):
                st_ref[0, direction, h] = s_fin[h * d:(h + 1) * d, h * d:(h + 1) * d]

    def out(c, carry):
        q = z_ref[rows(c), part(0)].astype(BF16)
        k = (z_ref[rows(c), part(1)].astype(F32) * SCALE).astype(BF16)
        v = z_ref[rows(c), part(2)].astype(BF16)
        gate = z_ref[rows(c), part(3)].astype(F32)
        o = (jnp.dot(q, s_scr[0, c], preferred_element_type=F32) * tab_scr[2]
             + jnp.dot(q, s_scr[1, c], preferred_element_type=F32) * tab_scr[3])
        intra = [_mm_nt(jnp.where(lane_head == h, q, jnp.zeros_like(q)), k) * d_scr[h]
                 for h in range(N_HEADS_RET)]
        for h in range(N_HEADS_RET):
            o = o + jnp.where(lane_head == h, _mm(intra[h], v), 0.0)
        sq = o * o
        sq_hi = sq.astype(BF16)
        sq_lo = (sq - sq_hi.astype(F32)).astype(BF16)
        ms = (jnp.dot(sq_hi, head_mean, preferred_element_type=F32)
              + jnp.dot(sq_lo, head_mean, preferred_element_type=F32))
        o = o * lax.rsqrt(ms + EPS) * ng_ref[...]
        o_ref[rows(c), :] = (o * _silu(gate)).astype(o_ref.dtype)
        return carry

    lax.fori_loop(0, nc, out, 0)


def _ret_scratch(n_tok):
    nc = n_tok // RET_CHUNK
    return [pltpu.VMEM((6, RET_CHUNK, WIDTH_RET), F32),
            pltpu.VMEM((N_HEADS_RET, RET_CHUNK, RET_CHUNK), F32),
            pltpu.VMEM((2, nc + 1, WIDTH_RET, WIDTH_RET), F32),
            pltpu.VMEM((2, nc, WIDTH_RET, WIDTH_RET), BF16)]


def _ctx_attn_body(sink_ref, decay_ref, z_ref, zr_ref, ng_ref, *refs, seq, n_prev, layer, depth):
    (ona_ref, oswa_ref, oret_ref, kna_ref, vna_ref, kswa_ref, vswa_ref, st_ref,
     tab_scr, d_scr, u_scr, s_scr) = refs[n_prev:]
    cache_refs = (kna_ref, vna_ref, kswa_ref, vswa_ref, st_ref)
    if n_prev == 0:
        for ref in cache_refs:
            for other in range(depth):
                if other != layer:
                    ref[0, other] = jnp.zeros(ref.shape[2:], ref.dtype)
        kna_ref, vna_ref, kswa_ref, vswa_ref, st_ref = (ref.at[:, pl.ds(layer, 1)] for ref in cache_refs)
    d = HEAD_DIM
    low_half = lax.broadcasted_iota(jnp.int32, (seq, 2 * d), 1) < d

    def pair_scores(q2, k2, head_is_low):
        return _mm_nt(jnp.where(low_half == head_is_low, q2, jnp.zeros_like(q2)), k2)

    jobs = []
    for pair in range(N_HEADS_NA // 2):
        cols = lambda part: slice(OFF_NA + part * WIDTH_NA + pair * 2 * d,
                                  OFF_NA + part * WIDTH_NA + (pair + 1) * 2 * d)
        q2 = (z_ref[:, cols(0)] * SCALE).astype(BF16)
        k2 = z_ref[:, cols(1)]
        v2 = z_ref[:, cols(2)]
        for hh in range(2):
            kna_ref[0, 0, 2 * pair + hh] = k2[:, hh * d:(hh + 1) * d]
            vna_ref[0, 0, 2 * pair + hh] = v2[:, hh * d:(hh + 1) * d]
        k2 = k2.astype(BF16)
        v2 = v2.astype(BF16)
        jobs.append([(pair_scores(q2, k2, hh == 0), v2, hh == 0, None) for hh in range(2)]
                    + [ona_ref, pair])
    lo = OFF_SWA + WIDTH_SWA
    k2 = z_ref[:, lo:lo + WIDTH_KV_SWA]
    v2 = z_ref[:, lo + WIDTH_KV_SWA:lo + 2 * WIDTH_KV_SWA]
    for g in range(N_KV_SWA):
        kswa_ref[0, 0, g] = k2[:, g * d:(g + 1) * d]
        vswa_ref[0, 0, g] = v2[:, g * d:(g + 1) * d]
    k2 = k2.astype(BF16)
    v2 = v2.astype(BF16)
    group = N_HEADS_SWA // N_KV_SWA
    for pair in range(N_HEADS_SWA // 2):
        q2 = (z_ref[:, OFF_SWA + pair * 2 * d:OFF_SWA + (pair + 1) * 2 * d] * SCALE).astype(BF16)
        heads = []
        for hh in range(2):
            h = 2 * pair + hh
            g = h // group
            qh = q2[:, hh * d:(hh + 1) * d]
            q_at_g = jnp.concatenate([qh, qh], axis=-1) if g != hh else q2
            heads.append((pair_scores(q_at_g, k2, g == 0), v2, g == 0, sink_ref[h]))
        jobs.append(heads + [oswa_ref, pair])
    probs = []
    for job in jobs:
        for s, v2, v_is_low, sink in job[:2]:
            m = jnp.max(s, axis=-1, keepdims=True)
            if sink is not None:
                m = jnp.maximum(m, sink)
            e = jnp.exp(s - m)
            den = jnp.sum(e, axis=-1, keepdims=True)
            if sink is not None:
                den = den + jnp.exp(sink - m)
            probs.append((e, den))
    for j, job in enumerate(jobs):
        halves = []
        for hh, (s, v2, v_is_low, sink) in enumerate(job[:2]):
            e, den = probs[2 * j + hh]
            o2 = _mm(e, v2) / den
            halves.append(o2[:, :d] if v_is_low else o2[:, d:])
        o_ref, pair = job[2], job[3]
        o_ref[:, pair * 2 * d:(pair + 1) * 2 * d] = jnp.concatenate(halves, axis=-1).astype(o_ref.dtype)
    _retention(zr_ref, 0, decay_ref, ng_ref, None, oret_ref, st_ref.at[0],
               tab_scr, d_scr, u_scr, s_scr, seq)


def _ctx_mixers(z_att, z_rest, swa_sink, ret_decay, ret_norm_g, caches, layer, depth, batch, seq):
    n = batch * seq
    assert z_att.shape[1] == OFF_RET
    smem = pl.BlockSpec(memory_space=pltpu.SMEM)
    first = caches is None
    layers_in_block = depth if first else 1
    block_layer = 0 if first else layer
    kv = lambda heads: pl.BlockSpec((1, layers_in_block, heads, seq, HEAD_DIM),
                                    lambda b: (b, block_layer, 0, 0, 0))
    kv_shape = lambda heads: jax.ShapeDtypeStruct((batch, depth, heads, seq, HEAD_DIM), F32)
    in_specs = [smem, smem,
                pl.BlockSpec((seq, OFF_RET), lambda b: (b, 0)),
                pl.BlockSpec((seq, 4 * WIDTH_RET), lambda b: (b, 0)),
                pl.BlockSpec((1, WIDTH_RET), lambda b: (0, 0))]
    args = [swa_sink, ret_decay, z_att, z_rest, ret_norm_g.reshape(1, WIDTH_RET)]
    n_prev = 0 if caches is None else len(caches)
    aliases = {len(args) + i: 3 + i for i in range(n_prev)}
    if caches is not None:
        in_specs += [pl.BlockSpec(memory_space=pl.ANY)] * n_prev
        args += list(caches)
    return pl.pallas_call(
        functools.partial(_ctx_attn_body, seq=seq, n_prev=n_prev, layer=layer, depth=depth),
        grid=(batch,),
        in_specs=in_specs,
        out_specs=[pl.BlockSpec((seq, WIDTH_NA), lambda b: (b, 0)),
                   pl.BlockSpec((seq, WIDTH_SWA), lambda b: (b, 0)),
                   pl.BlockSpec((seq, WIDTH_RET), lambda b: (b, 0)),
                   kv(N_HEADS_NA), kv(N_HEADS_NA), kv(N_KV_SWA), kv(N_KV_SWA),
                   pl.BlockSpec((1, layers_in_block, 2, N_HEADS_RET, HEAD_DIM, HEAD_DIM),
                                lambda b: (b, block_layer, 0, 0, 0, 0))],
        out_shape=[jax.ShapeDtypeStruct((n, WIDTH_NA), BF16),
                   jax.ShapeDtypeStruct((n, WIDTH_SWA), BF16),
                   jax.ShapeDtypeStruct((n, WIDTH_RET), BF16),
                   kv_shape(N_HEADS_NA), kv_shape(N_HEADS_NA), kv_shape(N_KV_SWA), kv_shape(N_KV_SWA),
                   jax.ShapeDtypeStruct((batch, depth, 2, N_HEADS_RET, HEAD_DIM, HEAD_DIM), F32)],
        scratch_shapes=_ret_scratch(seq),
        input_output_aliases=aliases,
        compiler_params=_params(("arbitrary",), 32),
        name="ctx_mixers",
    )(*args)


NA_QROWS = 4
NA_WROWS = NA_QROWS + WIN_ROWS


def _na_bias_table(rpb, rows):
    n_groups = rows // NA_QROWS
    assert rows % NA_QROWS == 0 and n_groups >= 3
    depth, n_heads, n_ro, n_co = rpb.shape
    half = WIN_ROWS // 2
    cc = np.arange(GRID_W)
    col_start = np.clip(cc - WIN_COLS // 2, 0, GRID_W - WIN_COLS)
    valid = (cc[None, :] >= col_start[:, None]) & (cc[None, :] < col_start[:, None] + WIN_COLS)
    span = 2 * GRID_W - 1
    lead = GRID_W - WIN_COLS
    ext = jnp.pad(rpb.astype(F32), ((0, 0), (0, 0), (0, 0), (lead, span + 1 - lead - n_co)))
    band = jnp.tile(ext, (1, 1, 1, GRID_W))[..., :GRID_W * span].reshape(depth, n_heads, n_ro, GRID_W, span)
    band = jnp.where(valid, band[..., GRID_W - 1:], MASKED)
    band = jnp.concatenate([band, jnp.full((depth, n_heads, 1, GRID_W, GRID_W), MASKED, F32)], axis=2)
    slab = np.full((3, NA_QROWS, NA_WROWS), n_ro, np.int32)
    for cls, g in enumerate((0, 1, n_groups - 1)):
        ws = min(max(NA_QROWS * g - half, 0), rows - NA_WROWS)
        for i in range(NA_QROWS):
            r = NA_QROWS * g + i
            rs = min(max(r - half, 0), rows - WIN_ROWS)
            for j in range(NA_WROWS):
                if rs <= ws + j < rs + WIN_ROWS:
                    slab[cls, i, j] = ws + j - r + WIN_ROWS - 1
    tab = jnp.take(band, slab.reshape(-1), axis=2)
    tab = tab.reshape(depth, n_heads, 3, NA_QROWS, NA_WROWS, GRID_W, GRID_W)
    tab = tab.transpose(0, 2, 1, 3, 5, 4, 6)
    return tab.reshape(depth, 3, n_heads, NA_QROWS * GRID_W, NA_WROWS * GRID_W)


def _lat_na_body(z_ref, kc_ref, vc_ref, bias_ref, o_ref, kc_scr, vc_scr, *, rows):
    d = HEAD_DIM
    nq = NA_QROWS * GRID_W
    g = pl.program_id(1)

    @pl.when(g == 0)
    def _():
        kc_scr[...] = kc_ref[0, 0].astype(BF16)
        vc_scr[...] = vc_ref[0, 0].astype(BF16)

    ws = jnp.clip(NA_QROWS * g - WIN_ROWS // 2, 0, rows - NA_WROWS)
    qrows = pl.ds(pl.multiple_of(g * nq, nq), nq)
    wrows = pl.ds(pl.multiple_of(ws * GRID_W, GRID_W), NA_WROWS * GRID_W)
    low_half = lax.broadcasted_iota(jnp.int32, (nq, 2 * d), 1) < d
    for pair in range(N_HEADS_NA // 2):
        cols = slice(pair * 2 * d, (pair + 1) * 2 * d)
        q2 = (z_ref[qrows, cols] * SCALE).astype(BF16)
        kl2 = z_ref[wrows, WIDTH_NA + pair * 2 * d:WIDTH_NA + (pair + 1) * 2 * d].astype(BF16)
        vl2 = z_ref[wrows, 2 * WIDTH_NA + pair * 2 * d:2 * WIDTH_NA + (pair + 1) * 2 * d].astype(BF16)
        for hh in range(2):
            h = 2 * pair + hh
            qm = jnp.where(low_half == (hh == 0), q2, jnp.zeros_like(q2))
            s_loc = _mm_nt(qm, kl2) + bias_ref[0, 0, h]
            s_ctx = _mm_nt(q2[:, hh * d:(hh + 1) * d], kc_scr[h])
            m = jnp.maximum(jnp.max(s_ctx, axis=-1, keepdims=True), jnp.max(s_loc, axis=-1, keepdims=True))
            e_ctx = jnp.exp(s_ctx - m)
            e_loc = jnp.exp(s_loc - m)
            den = jnp.sum(e_ctx, axis=-1, keepdims=True) + jnp.sum(e_loc, axis=-1, keepdims=True)
            o = (_mm(e_ctx, vc_scr[h]) + _mm(e_loc, vl2)[:, hh * d:(hh + 1) * d]) / den
            o_ref[:, h * d:(h + 1) * d] = o.astype(o_ref.dtype)


def _lat_na(z, cache_k, cache_v, bias_tab, layer, batch, seq):
    rows = seq // GRID_W
    n_groups = rows // NA_QROWS
    nq = NA_QROWS * GRID_W
    past = cache_k.shape[3]

    def group_class(g):
        return jnp.where(g == 0, 0, jnp.where(g == n_groups - 1, 2, 1))

    cache = pl.BlockSpec((1, 1, N_HEADS_NA, past, HEAD_DIM), lambda b, g: (b, layer, 0, 0, 0))
    return pl.pallas_call(
        functools.partial(_lat_na_body, rows=rows),
        grid=(batch, n_groups),
        in_specs=[pl.BlockSpec((seq, 3 * WIDTH_NA), lambda b, g: (b, 0)),
                  cache, cache,
                  pl.BlockSpec((1, 1, N_HEADS_NA, nq, NA_WROWS * GRID_W),
                               lambda b, g: (layer, group_class(g), 0, 0, 0))],
        out_specs=pl.BlockSpec((nq, WIDTH_NA), lambda b, g: (b * n_groups + g, 0)),
        out_shape=jax.ShapeDtypeStruct((batch * seq, WIDTH_NA), BF16),
        scratch_shapes=[pltpu.VMEM((N_HEADS_NA, past, HEAD_DIM), BF16),
                        pltpu.VMEM((N_HEADS_NA, past, HEAD_DIM), BF16)],
        compiler_params=_params(("arbitrary", "arbitrary"), 58),
        name="lat_na",
    )(z, cache_k, cache_v, bias_tab)


def _rope_tables(seq):
    t = jnp.arange(seq)
    n_freq = HEAD_DIM // 4
    inv = ROPE_BASE ** (-jnp.arange(n_freq, dtype=F32) / n_freq)
    ang = jnp.concatenate([(t // GRID_W).astype(F32)[:, None] * inv,
                           (t % GRID_W).astype(F32)[:, None] * inv], axis=-1)
    cos = jnp.repeat(jnp.cos(ang), 2, axis=-1)
    sign = jnp.tile(jnp.asarray([-1.0, 1.0], F32), HEAD_DIM // 2)
    sin = jnp.repeat(jnp.sin(ang), 2, axis=-1) * sign
    return jnp.tile(cos, (1, N_HEADS_SWA)), jnp.tile(sin, (1, N_HEADS_SWA))


def _rope(x, cos, sin_signed):
    width = x.shape[-1]
    lane = lax.broadcasted_iota(jnp.int32, x.shape, x.ndim - 1)
    partner = jnp.where((lane & 1) == 0, pltpu.roll(x, width - 1, x.ndim - 1), pltpu.roll(x, 1, x.ndim - 1))
    return x * cos + partner * sin_signed


def _lat_swa_body(sink_ref, z_ref, kc_ref, vc_ref, cos_ref, sin_ref, o_ref, kr_scr, kc_scr, vc_scr, *, seq):
    d = HEAD_DIM
    blk = SWA_BLOCK
    n = pl.program_id(1)
    assert N_KV_SWA == 2

    @pl.when(n == 0)
    def _():
        k = z_ref[:, WIDTH_SWA:WIDTH_SWA + WIDTH_KV_SWA].astype(F32)
        kr_scr[...] = _rope(k, cos_ref[:, :WIDTH_KV_SWA], sin_ref[:, :WIDTH_KV_SWA]).astype(BF16)
        kc_scr[...] = jnp.concatenate([kc_ref[0, 0, 0], kc_ref[0, 0, 1]], axis=-1).astype(BF16)
        vc_scr[...] = jnp.concatenate([vc_ref[0, 0, 0], vc_ref[0, 0, 1]], axis=-1).astype(BF16)

    qrows = pl.ds(pl.multiple_of(n * blk, blk), blk)
    kstart = jnp.clip((n - 1) * blk, 0, seq - 3 * blk)
    krows = pl.ds(pl.multiple_of(kstart, blk), 3 * blk)
    q_all = _rope(z_ref[qrows, 0:WIDTH_SWA].astype(F32), cos_ref[qrows, :], sin_ref[qrows, :]) * SCALE
    q_all = q_all.astype(BF16)
    qpos = n * blk + lax.broadcasted_iota(jnp.int32, (blk, 3 * blk), 0)
    kpos = kstart + lax.broadcasted_iota(jnp.int32, (blk, 3 * blk), 1)
    valid = jnp.abs(qpos - kpos) <= SWA_RADIUS
    kl2 = kr_scr[krows, :]
    vl2 = z_ref[krows, WIDTH_SWA + WIDTH_KV_SWA:WIDTH_SWA + 2 * WIDTH_KV_SWA]
    group = N_HEADS_SWA // N_KV_SWA
    zero = jnp.zeros((blk, d), BF16)
    scores = []
    for h in range(N_HEADS_SWA):
        g = h // group
        qh = q_all[:, h * d:(h + 1) * d]
        q_at_g = jnp.concatenate([qh, zero] if g == 0 else [zero, qh], axis=-1)
        scores.append((jnp.where(valid, _mm_nt(q_at_g, kl2), MASKED), _mm_nt(q_at_g, kc_scr[...])))
    probs = []
    for h, (s_loc, s_ctx) in enumerate(scores):
        m = jnp.maximum(jnp.maximum(jnp.max(s_loc, axis=-1, keepdims=True),
                                    jnp.max(s_ctx, axis=-1, keepdims=True)), sink_ref[h])
        e_loc = jnp.exp(s_loc - m)
        e_ctx = jnp.exp(s_ctx - m)
        den = (jnp.sum(e_loc, axis=-1, keepdims=True) + jnp.sum(e_ctx, axis=-1, keepdims=True)
               + jnp.exp(sink_ref[h] - m))
        probs.append((e_loc, e_ctx, den))
    outs = []
    for h, (e_loc, e_ctx, den) in enumerate(probs):
        g = h // group
        o2 = (_mm(e_ctx, vc_scr[...]) + _mm(e_loc, vl2)) / den
        outs.append(o2[:, g * d:(g + 1) * d])
    o_ref[...] = jnp.concatenate(outs, axis=-1).astype(o_ref.dtype)


def _lat_swa(z, cache_k, cache_v, swa_sink, cos, sin, layer, batch, seq):
    nb = seq // SWA_BLOCK
    past = cache_k.shape[3]
    width = WIDTH_SWA + 2 * WIDTH_KV_SWA
    assert OFF_SWA % width == 0 and seq >= 3 * SWA_BLOCK
    cache = pl.BlockSpec((1, 1, N_KV_SWA, past, HEAD_DIM), lambda b, n: (b, layer, 0, 0, 0))
    table = pl.BlockSpec((seq, WIDTH_SWA), lambda b, n: (0, 0))
    return pl.pallas_call(
        functools.partial(_lat_swa_body, seq=seq),
        grid=(batch, nb),
        in_specs=[pl.BlockSpec(memory_space=pltpu.SMEM),
                  pl.BlockSpec((seq, width), lambda b, n: (b, OFF_SWA // width)),
                  cache, cache, table, table],
        out_specs=pl.BlockSpec((SWA_BLOCK, WIDTH_SWA), lambda b, n: (b * nb + n, 0)),
        out_shape=jax.ShapeDtypeStruct((batch * seq, WIDTH_SWA), BF16),
        scratch_shapes=[pltpu.VMEM((seq, WIDTH_KV_SWA), BF16),
                        pltpu.VMEM((past, WIDTH_KV_SWA), BF16), pltpu.VMEM((past, WIDTH_KV_SWA), BF16)],
        compiler_params=_params(("arbitrary", "arbitrary"), 32),
        name="lat_swa",
    )(swa_sink, z, cache_k, cache_v, cos, sin)


def _lat_ret_body(decay_ref, z_ref, ng_ref, s0_ref, o_ref, tab_scr, d_scr, u_scr, s_scr, *, seq):
    _retention(z_ref, 0, decay_ref, ng_ref, s0_ref, o_ref, None, tab_scr, d_scr, u_scr, s_scr, seq)


def _lat_ret(z, ret_decay, ret_norm_g, state, layer, batch, seq):
    width = 4 * WIDTH_RET
    assert OFF_RET % width == 0
    return pl.pallas_call(
        functools.partial(_lat_ret_body, seq=seq),
        grid=(batch,),
        in_specs=[pl.BlockSpec(memory_space=pltpu.SMEM),
                  pl.BlockSpec((seq, width), lambda b: (b, OFF_RET // width)),
                  pl.BlockSpec((1, WIDTH_RET), lambda b: (0, 0)),
                  pl.BlockSpec((1, 1, 2, N_HEADS_RET, HEAD_DIM, HEAD_DIM),
                               lambda b: (b, layer, 0, 0, 0, 0))],
        out_specs=pl.BlockSpec((seq, WIDTH_RET), lambda b: (b, 0)),
        out_shape=jax.ShapeDtypeStruct((batch * seq, WIDTH_RET), BF16),
        scratch_shapes=_ret_scratch(seq),
        compiler_params=_params(("arbitrary",), 40),
        name="lat_ret",
    )(ret_decay, z, ret_norm_g.reshape(1, WIDTH_RET), state)


def _merge_body(ona_ref, oswa_ref, oret_ref, ga_ref, gs_ref, gr_ref, x_ref, mod_ref, g2_ref,
                wna_ref, wswa_ref, wret_ref, wout_ref, rw_ref, rb_ref,
                x1_ref, hx_ref, wna_scr, wswa_scr, wret_scr, wout_scr, rw_scr):
    @pl.when(pl.program_id(0) == 0)
    def _():
        wna_scr[...] = wna_ref[0].astype(BF16)
        wswa_scr[...] = wswa_ref[0].astype(BF16)
        wret_scr[...] = wret_ref[0].astype(BF16)
        wout_scr[...] = wout_ref[0].astype(BF16)
        rw = rw_ref[0]
        rw_hi = rw.astype(BF16)
        rw_scr[0] = rw_hi
        rw_scr[1] = (rw - rw_hi.astype(F32)).astype(BF16)

    D = D_MODEL
    m = mod_ref[0]
    sub = x_ref.shape[0] // MERGE_SPLIT
    parts = [pl.ds(s * sub, sub) for s in range(MERGE_SPLIT)]
    gate = lambda ref, rows: _sigmoid(ref[rows, :].astype(F32))
    zs = [(gate(ga_ref, r) * jnp.dot(ona_ref[r, :], wna_scr[...], preferred_element_type=F32)
           + gate(gs_ref, r) * jnp.dot(oswa_ref[r, :], wswa_scr[...], preferred_element_type=F32)
           + gate(gr_ref, r) * jnp.dot(oret_ref[r, :], wret_scr[...], preferred_element_type=F32)
           ).astype(BF16) for r in parts]
    ys = [jnp.dot(z, wout_scr[...], preferred_element_type=F32) for z in zs]
    logits = []
    for r, y in zip(parts, ys):
        x1 = x_ref[r, :] + m[:, 2 * D:3 * D] * y
        x1_ref[r, :] = x1
        h2 = x1 * lax.rsqrt(jnp.mean(x1 * x1, axis=-1, keepdims=True) + EPS) * g2_ref[0]
        h2 = h2 * (1.0 + m[:, 4 * D:5 * D]) + m[:, 3 * D:4 * D]
        hx_ref[r, 0:D] = h2
        h_hi = h2.astype(BF16)
        h_lo = (h2 - h_hi.astype(F32)).astype(BF16)
        logits.append((jnp.dot(h_hi, rw_scr[0], preferred_element_type=F32)
                       + jnp.dot(h_hi, rw_scr[1], preferred_element_type=F32)
                       + jnp.dot(h_lo, rw_scr[0], preferred_element_type=F32)) + rb_ref[0])
    for r, logit in zip(parts, logits):
        hx_ref[r, D:D + ROUTER_LANES] = _route(logit)


MERGE_SPLIT = 2


def _route(logit):
    lane = lax.broadcasted_iota(jnp.int32, logit.shape, 1).astype(F32)
    neg = jnp.float32(-jnp.inf)
    big = jnp.float32(ROUTER_LANES)
    is_group = lane < N_GROUPS
    is_expert = (lane >= N_GROUPS) & (lane < N_GROUPS + N_EXPERTS)
    gl = jnp.where(is_group, logit, neg)
    gmax = jnp.max(gl, axis=-1, keepdims=True)
    gsel = jnp.min(jnp.where(gl == gmax, lane, big), axis=-1, keepdims=True)
    p_group = 1.0 / jnp.sum(jnp.where(is_group, jnp.exp(gl - gmax), 0.0), axis=-1, keepdims=True)
    eidx = lane - N_GROUPS
    in_group = is_expert & (jnp.floor(eidx / EXPERTS_PER_GROUP) == gsel)
    el = jnp.where(in_group, logit, neg)
    top1 = jnp.max(el, axis=-1, keepdims=True)
    sel1 = jnp.min(jnp.where(el == top1, lane, big), axis=-1, keepdims=True)
    el2 = jnp.where(lane == sel1, neg, el)
    top2 = jnp.max(el2, axis=-1, keepdims=True)
    sel2 = jnp.min(jnp.where(el2 == top2, lane, big), axis=-1, keepdims=True)
    e2 = jnp.exp(top2 - top1)
    w1 = p_group / (1.0 + e2)
    w2 = p_group * e2 / (1.0 + e2)
    comb = jnp.where(lane == sel1, w1, 0.0) + jnp.where(lane == sel2, w2, 0.0)
    return comb + jnp.where(lane == ROUTE_GROUP_LANE, gsel, 0.0)


def _merge(o_na, o_swa, o_ret, z, gate_tile, x, mod_rows, norm2_g, w_br_na, w_br_swa, w_br_ret, w_out,
           router_w, router_b, layer):
    n, d = x.shape
    n_cond = mod_rows.shape[0]
    per_cond = n // n_cond
    tm = 512
    gate = lambda k: pl.BlockSpec((tm, d), lambda i: (i, gate_tile + k))
    row = lambda w: pl.BlockSpec((tm, w), lambda i: (i, 0))
    weight = lambda k: pl.BlockSpec((1, k, d), lambda i: (layer, 0, 0))
    return pl.pallas_call(
        _merge_body,
        grid=(n // tm,),
        in_specs=[row(WIDTH_NA), row(WIDTH_SWA), row(WIDTH_RET),
                  gate(0), gate(1), gate(2),
                  row(d),
                  pl.BlockSpec((1, 1, 6 * d), lambda i: (i * tm // per_cond, 0, 0)),
                  pl.BlockSpec((1, 1, d), lambda i: (layer, 0, 0)),
                  weight(WIDTH_NA), weight(WIDTH_SWA), weight(WIDTH_RET), weight(d),
                  pl.BlockSpec((1, d, ROUTER_LANES), lambda i: (layer, 0, 0)),
                  pl.BlockSpec((1, 1, ROUTER_LANES), lambda i: (layer, 0, 0))],
        out_specs=[row(d), row(d + ROUTER_LANES)],
        out_shape=[jax.ShapeDtypeStruct((n, d), F32),
                   jax.ShapeDtypeStruct((n, d + ROUTER_LANES), F32)],
        scratch_shapes=[pltpu.VMEM((WIDTH_NA, d), BF16), pltpu.VMEM((WIDTH_SWA, d), BF16),
                        pltpu.VMEM((WIDTH_RET, d), BF16), pltpu.VMEM((d, d), BF16),
                        pltpu.VMEM((2, d, ROUTER_LANES), BF16)],
        compiler_params=_params(("arbitrary",), 56),
        name="merge_router",
    )(o_na, o_swa, o_ret, z, z, z, x, mod_rows, norm2_g.reshape(norm2_g.shape[0], 1, d),
      w_br_na, w_br_swa, w_br_ret, w_out, router_w, router_b)


MOE_TILE = 512
FIN_TILE = 512


def _route_plan(hx, n_tiles):
    n = hx.shape[0]
    g = hx[:, D_MODEL + ROUTE_GROUP_LANE].astype(jnp.int32)
    onehot = (g[:, None] == jnp.arange(N_GROUPS, dtype=jnp.int32)[None, :]).astype(jnp.int32)
    csum = jnp.cumsum(onehot, axis=0)
    counts = csum[-1]
    ends = jnp.cumsum(counts)
    dest = jnp.sum(onehot * ((ends - counts)[None, :] + csum - 1), axis=1)
    first = jnp.arange(n_tiles, dtype=jnp.int32) * MOE_TILE
    tile_lo = jnp.sum((first[:, None] >= ends[None, :]).astype(jnp.int32), axis=1)
    tile_hi = jnp.sum(((first + MOE_TILE - 1)[:, None] >= ends[None, :]).astype(jnp.int32), axis=1)
    return dest, tile_lo, tile_hi


def _row_copies(src_hbm, idx_ref, base, buf, sem, n_rows, start):
    def body(i, carry):
        copy = pltpu.make_async_copy(src_hbm.at[pl.ds(idx_ref[base + i], 1)], buf.at[pl.ds(i, 1)], sem)
        if start:
            copy.start()
        else:
            copy.wait()
        return carry

    lax.fori_loop(0, n_rows, body, 0, unroll=8)


def _moe_body(dest_ref, lo_ref, hi_ref, hx_hbm, wg_hbm, wu_hbm, wd_hbm, o_ref,
              order_scr, hbuf, wg_scr, wu_scr, wd_scr, row_sem, w_sem, *, layer, n_tiles, n_tok):
    t = pl.program_id(0)
    slot = lax.rem(t, 2)
    nxt = jnp.minimum(t + 1, n_tiles - 1)

    def row_copy(tile, i, slot_):
        return pltpu.make_async_copy(hx_hbm.at[pl.ds(order_scr[tile * MOE_TILE + i], 1)],
                                     hbuf.at[slot_, pl.ds(i, 1)], row_sem.at[slot_])

    def for_rows(fn):
        def body(i, carry):
            fn(i)
            return carry
        lax.fori_loop(0, MOE_TILE, body, 0, unroll=8)

    @pl.when(t == 0)
    def _():
        copies = [pltpu.make_async_copy(src.at[layer], dst, w_sem.at[k])
                  for k, (src, dst) in enumerate(((wg_hbm, wg_scr), (wu_hbm, wu_scr), (wd_hbm, wd_scr)))]
        for copy in copies:
            copy.start()

        def invert(i, carry):
            order_scr[dest_ref[i]] = i
            return carry
        lax.fori_loop(0, n_tok, invert, 0, unroll=8)
        for_rows(lambda i: row_copy(0, i, 0).start())
        for copy in copies:
            copy.wait()

    for_rows(lambda i: row_copy(t, i, slot).wait())
    hx = hbuf[slot]
    h = hx[:, 0:D_MODEL].astype(BF16)
    route = hx[:, D_MODEL:D_MODEL + ROUTER_LANES]
    lane = lax.broadcasted_iota(jnp.int32, route.shape, 1)
    rows_per_expert = MOE_TILE // EXPERTS_PER_GROUP
    for k in range(N_GROUPS):
        def group(k=k):
            first = (lo_ref[t] + k) * EXPERTS_PER_GROUP
            y = None
            for j in range(EXPERTS_PER_GROUP):
                e = first + j
                hg = jnp.dot(h, wg_scr[e], preferred_element_type=F32)
                hu = jnp.dot(h, wu_scr[e], preferred_element_type=F32)
                c = jnp.sum(jnp.where(lane == ROUTE_EXPERT_LANE + e, route, 0.0), axis=-1, keepdims=True)
                a = _silu(hg) * hu * c
                part = jnp.dot(a.astype(BF16), wd_scr[e], preferred_element_type=F32)
                y = part if y is None else y + part
                if k == 0:
                    for i in range(j * rows_per_expert, (j + 1) * rows_per_expert):
                        row_copy(nxt, i, 1 - slot).start()
            if k == 0:
                o_ref[...] = y
            else:
                o_ref[...] += y

        if k == 0:
            group()
        else:
            pl.when(lo_ref[t] + k <= hi_ref[t])(group)

    @pl.when(t == n_tiles - 1)
    def _():
        for_rows(lambda i: row_copy(nxt, i, 1 - slot).wait())


def _moe_sorted(hx, dest, tile_lo, tile_hi, w_gate, w_up, w_down, layer):
    n = hx.shape[0]
    d = D_MODEL
    n_tiles = n // MOE_TILE
    any_space = pl.BlockSpec(memory_space=pl.ANY)
    return pl.pallas_call(
        functools.partial(_moe_body, layer=layer, n_tiles=n_tiles, n_tok=n),
        grid_spec=pltpu.PrefetchScalarGridSpec(
            num_scalar_prefetch=3,
            grid=(n_tiles,),
            in_specs=[any_space, any_space, any_space, any_space],
            out_specs=pl.BlockSpec((MOE_TILE, d), lambda t, *_: (t, 0)),
            scratch_shapes=[pltpu.SMEM((n,), jnp.int32),
                            pltpu.VMEM((2, MOE_TILE, d + ROUTER_LANES), F32),
                            pltpu.VMEM((N_EXPERTS, d, D_EXPERT), BF16),
                            pltpu.VMEM((N_EXPERTS, d, D_EXPERT), BF16),
                            pltpu.VMEM((N_EXPERTS, D_EXPERT, d), BF16),
                            pltpu.SemaphoreType.DMA((2,)),
                            pltpu.SemaphoreType.DMA((3,))]),
        out_shape=jax.ShapeDtypeStruct((n, d), F32),
        compiler_params=_params(("arbitrary",), 56),
        name="moe_sorted",
    )(dest, tile_lo, tile_hi, hx, w_gate, w_up, w_down)


def _moe_finish_body(dest_ref, y_hbm, x1_ref, mod_ref, fg_ref, o_ref, ybuf, row_sem, *, final, n_tiles):
    t = pl.program_id(0)
    slot = lax.rem(t, 2)

    def rows_of(tile, slot_, start):
        _row_copies(y_hbm, dest_ref, tile * FIN_TILE, ybuf.at[slot_], row_sem.at[slot_], FIN_TILE, start)

    @pl.when(t == 0)
    def _():
        rows_of(0, 0, True)

    @pl.when(t + 1 < n_tiles)
    def _():
        rows_of(t + 1, 1 - slot, True)

    rows_of(t, slot, False)
    x2 = x1_ref[...] + mod_ref[0][:, 5 * D_MODEL:6 * D_MODEL] * ybuf[slot]
    if final:
        x2 = x2 * lax.rsqrt(jnp.mean(x2 * x2, axis=-1, keepdims=True) + EPS) * fg_ref[...]
    o_ref[...] = x2


def _moe_finish(y_sorted, dest, x1, mod_rows, final_g, final):
    n, d = x1.shape
    n_cond = mod_rows.shape[0]
    per_cond = n // n_cond
    n_tiles = n // FIN_TILE
    return pl.pallas_call(
        functools.partial(_moe_finish_body, final=final, n_tiles=n_tiles),
        grid_spec=pltpu.PrefetchScalarGridSpec(
            num_scalar_prefetch=1,
            grid=(n_tiles,),
            in_specs=[pl.BlockSpec(memory_space=pl.ANY),
                      pl.BlockSpec((FIN_TILE, d), lambda t, *_: (t, 0)),
                      pl.BlockSpec((1, 1, 6 * d), lambda t, *_: (t * FIN_TILE // per_cond, 0, 0)),
                      pl.BlockSpec((1, d), lambda t, *_: (0, 0))],
            out_specs=pl.BlockSpec((FIN_TILE, d), lambda t, *_: (t, 0)),
            scratch_shapes=[pltpu.VMEM((2, FIN_TILE, d), F32), pltpu.SemaphoreType.DMA((2,))]),
        out_shape=jax.ShapeDtypeStruct((n, d), F32),
        compiler_params=_params(("arbitrary",), 32),
        name="moe_finish",
    )(dest, y_sorted, x1, mod_rows, final_g.reshape(1, d))


def _moe(hx, x1, mod_rows, w_gate, w_up, w_down, final_g, layer, final):
    dest, tile_lo, tile_hi = _route_plan(hx, hx.shape[0] // MOE_TILE)
    y_sorted = _moe_sorted(hx, dest, tile_lo, tile_hi, w_gate, w_up, w_down, layer)
    return _moe_finish(y_sorted, dest, x1, mod_rows, final_g, final)


def kernel(x_prompt, x_sample, c, cache_na_k, cache_na_v, cache_swa_k, cache_swa_v, state_ret, c_ctx, ada_w, ada_b, norm1_g, norm2_g, w_in, na_rpb, swa_sink, ret_decay, ret_norm_g, w_br_na, w_br_swa, w_br_ret, w_out, router_group_w, router_group_b, router_expert_w, router_expert_b, w_gate, w_up, w_down, final_norm_g):
    batch, seq, d = x_prompt.shape
    dec_batch, dec_seq, _ = x_sample.shape
    depth = ada_w.shape[0]
    assert d == D_MODEL and dec_batch + 1 <= 8
    assert OFF_RET % IN_TILE == 0 and OFF_GATES % d == 0 and (OFF_GATES - OFF_RET) % d == 0

    cond = jnp.concatenate([c_ctx[None], c, jnp.zeros((8 - 1 - dec_batch, d), F32)], axis=0)
    mod = _modulation(cond, ada_w, ada_b)
    router_w = jnp.concatenate([router_group_w, router_expert_w], axis=-1)
    router_w = jnp.pad(router_w, ((0, 0), (0, 0), (0, ROUTER_LANES - router_w.shape[-1])))
    router_b = jnp.concatenate([router_group_b, router_expert_b], axis=-1)
    router_b = jnp.pad(router_b, ((0, 0), (0, ROUTER_LANES - router_b.shape[-1])))[:, None, :]
    cos, sin = _rope_tables(dec_seq)
    na_bias = _na_bias_table(na_rpb, dec_seq // GRID_W)
    moe_weights = (w_gate.astype(BF16), w_up.astype(BF16), w_down.astype(BF16))

    xp = x_prompt.reshape(batch * seq, d)
    xs = x_sample.reshape(dec_batch * dec_seq, d)
    caches = None
    for l in range(depth):
        final = l == depth - 1
        mod_ctx = mod[l, 0:1][:, None, :]
        mod_lat = mod[l, 1:1 + dec_batch][:, None, :]
        merge_w = (norm2_g, w_br_na, w_br_swa, w_br_ret, w_out, router_w, router_b, l)
        moe_w = (*moe_weights, final_norm_g, l, final)

        z_att, z_rest = _in_proj(xp, mod_ctx, norm1_g, w_in, l, OFF_RET // IN_TILE)
        o_na, o_swa, o_ret, *caches = _ctx_mixers(
            z_att, z_rest, swa_sink[l], ret_decay[l], ret_norm_g[l], caches, l, depth, batch, seq)
        x1, hx = _merge(o_na, o_swa, o_ret, z_rest, (OFF_GATES - OFF_RET) // d, xp, mod_ctx, *merge_w)
        xp = _moe(hx, x1, mod_ctx, *moe_w)

        z, = _in_proj(xs, mod_lat, norm1_g, w_in, l, 0)
        o_na = _lat_na(z, cache_na_k, cache_na_v, na_bias, l, dec_batch, dec_seq)
        o_swa = _lat_swa(z, cache_swa_k, cache_swa_v, swa_sink[l], cos, sin, l, dec_batch, dec_seq)
        o_ret = _lat_ret(z, ret_decay[l], ret_norm_g[l], state_ret, l, dec_batch, dec_seq)
        x1, hx = _merge(o_na, o_swa, o_ret, z, OFF_GATES // d, xs, mod_lat, *merge_w)
        xs = _moe(hx, x1, mod_lat, *moe_w)

    return (xp.reshape(batch, seq, d), xs.reshape(dec_batch, dec_seq, d), *caches)
```

```python
import functools

import numpy as np
import jax
import jax.numpy as jnp
from jax import lax
from jax.experimental import pallas as pl
from jax.experimental.pallas import tpu as pltpu

F32 = jnp.float32
BF16 = jnp.bfloat16

D_MODEL = 1024
HEAD_DIM = 64
GRID_W = 64
N_HEADS_NA = 8
WIN_ROWS = 8
WIN_COLS = 16
N_HEADS_SWA = 4
N_KV_SWA = 2
SWA_RADIUS = 128
SWA_BLOCK = 128
N_HEADS_RET = 4
RET_CHUNK = 128
WIDTH_NA = N_HEADS_NA * HEAD_DIM
WIDTH_SWA = N_HEADS_SWA * HEAD_DIM
WIDTH_KV_SWA = N_KV_SWA * HEAD_DIM
WIDTH_RET = N_HEADS_RET * HEAD_DIM
IN_COLS = 3 * WIDTH_NA + WIDTH_SWA + 2 * WIDTH_KV_SWA + 4 * WIDTH_RET + 3 * D_MODEL
OFF_NA = 0
OFF_SWA = 3 * WIDTH_NA
OFF_RET = OFF_SWA + WIDTH_SWA + 2 * WIDTH_KV_SWA
OFF_GATES = OFF_RET + 4 * WIDTH_RET
N_GROUPS = 4
EXPERTS_PER_GROUP = 4
N_EXPERTS = N_GROUPS * EXPERTS_PER_GROUP
D_EXPERT = D_MODEL // 4
ROPE_BASE = 10000.0
EPS = 1e-6
SCALE = HEAD_DIM ** -0.5
MASKED = -1e30
ROUTER_LANES = 128
ROUTE_GROUP_LANE = 0
ROUTE_EXPERT_LANE = N_GROUPS
ROUTE_COND_LANE = N_GROUPS + N_EXPERTS
HX_X1 = D_MODEL + ROUTER_LANES
HX_WIDTH = HX_X1 + D_MODEL

MIB = 1024 * 1024


def _mm(a, b):
    return jnp.dot(a.astype(BF16), b.astype(BF16), preferred_element_type=F32)


def _mm_nt(a, b):
    return lax.dot_general(a.astype(BF16), b.astype(BF16), (((1,), (1,)), ((), ())),
                           preferred_element_type=F32)


def _sigmoid(x):
    return 1.0 / (1.0 + jnp.exp(-x))


def _silu(x):
    return x * _sigmoid(x)


def _params(semantics, vmem_mib):
    return pltpu.CompilerParams(dimension_semantics=semantics, vmem_limit_bytes=vmem_mib * MIB)


def _mod_body(cond_ref, w_ref, b_ref, o_ref):
    o_ref[0] = _mm(_silu(cond_ref[...]), w_ref[0]) + b_ref[0]


def _modulation(cond, ada_w, ada_b):
    depth, d, n = ada_w.shape
    tn = 1024
    return pl.pallas_call(
        _mod_body,
        grid=(depth, n // tn),
        in_specs=[pl.BlockSpec((8, d), lambda l, j: (0, 0)),
                  pl.BlockSpec((1, d, tn), lambda l, j: (l, 0, j)),
                  pl.BlockSpec((1, 1, tn), lambda l, j: (l, 0, j))],
        out_specs=pl.BlockSpec((1, 8, tn), lambda l, j: (l, 0, j)),
        out_shape=jax.ShapeDtypeStruct((depth, 8, n), F32),
        compiler_params=_params(("arbitrary", "arbitrary"), 32),
        name="modulation",
    )(cond, ada_w, ada_b.reshape(depth, 1, n))


def _inproj_body(x_ref, mod_ref, g_ref, w_ref, *refs, n_f32):
    zf_ref = refs[0] if n_f32 else None
    zb_ref, h_scr, w_scr = refs[-3:]
    i = pl.program_id(0)
    j = pl.program_id(1)

    @pl.when(i == 0)
    def _():
        w_scr[j] = w_ref[0].astype(BF16)

    @pl.when(j == 0)
    def _():
        x = x_ref[...]
        m = mod_ref[0]
        y = x * lax.rsqrt(jnp.mean(x * x, axis=-1, keepdims=True) + EPS) * g_ref[0]
        h_scr[...] = (y * (1.0 + m[:, D_MODEL:2 * D_MODEL]) + m[:, 0:D_MODEL]).astype(BF16)

    acc = jnp.dot(h_scr[...], w_scr[j], preferred_element_type=F32)
    if n_f32 == 0:
        zb_ref[...] = acc.astype(BF16)
    else:
        @pl.when(j < n_f32)
        def _():
            zf_ref[...] = acc

        @pl.when(j >= n_f32)
        def _():
            zb_ref[...] = acc.astype(BF16)


IN_TILE = 1024


def _in_proj(x, mod_rows, norm_g, w_in, layer, n_f32):
    n, d = x.shape
    n_cond = mod_rows.shape[0]
    tm, tn = 1024, IN_TILE
    nj = IN_COLS // tn
    per_cond = n // n_cond
    out_specs = [pl.BlockSpec((tm, tn), lambda i, j: (i, jnp.maximum(j - n_f32, 0)))]
    out_shape = [jax.ShapeDtypeStruct((n, IN_COLS - n_f32 * tn), BF16)]
    if n_f32:
        out_specs.insert(0, pl.BlockSpec((tm, tn), lambda i, j: (i, jnp.minimum(j, n_f32 - 1))))
        out_shape.insert(0, jax.ShapeDtypeStruct((n, n_f32 * tn), F32))
    return pl.pallas_call(
        functools.partial(_inproj_body, n_f32=n_f32),
        grid=(n // tm, nj),
        in_specs=[pl.BlockSpec((tm, d), lambda i, j: (i, 0)),
                  pl.BlockSpec((1, 1, 6 * d), lambda i, j: (i * tm // per_cond, 0, 0)),
                  pl.BlockSpec((1, 1, d), lambda i, j: (layer, 0, 0)),
                  pl.BlockSpec((1, d, tn), lambda i, j: (layer, 0, jnp.where(i == 0, j, nj - 1)))],
        out_specs=out_specs,
        out_shape=out_shape,
        scratch_shapes=[pltpu.VMEM((tm, d), BF16), pltpu.VMEM((nj, d, tn), BF16)],
        compiler_params=_params(("arbitrary", "arbitrary"), 48),
        name="in_proj",
    )(x, mod_rows, norm_g.reshape(norm_g.shape[0], 1, d), w_in)


def _log_sigmoid(x):
    return jnp.minimum(x, 0.0) - jnp.log1p(jnp.exp(-jnp.abs(x)))


def _retention(z_ref, c0, decay_ref, ng_ref, s0_ref, o_ref, st_ref, tab_scr, d_scr, u_scr, s_scr, n_tok):
    C = RET_CHUNK
    nc = n_tok // C
    d = HEAD_DIM
    pos = lax.broadcasted_iota(jnp.int32, (C, d), 0).astype(F32)
    ri = lax.broadcasted_iota(jnp.int32, (C, C), 0).astype(F32)
    ci = lax.broadcasted_iota(jnp.int32, (C, C), 1).astype(F32)
    diff = ri - ci
    for h in range(N_HEADS_RET):
        lgf = _log_sigmoid(jnp.full((C, C), decay_ref[0, h], F32))
        lgb = _log_sigmoid(jnp.full((C, C), decay_ref[1, h], F32))
        d_scr[h] = (jnp.where(diff >= 0, jnp.exp(lgf * jnp.maximum(diff, 0.0)), 0.0)
                    + jnp.where(diff <= 0, jnp.exp(lgb * jnp.maximum(-diff, 0.0)), 0.0))
        lf = _log_sigmoid(jnp.full((C, d), decay_ref[0, h], F32))
        lb = _log_sigmoid(jnp.full((C, d), decay_ref[1, h], F32))
        hs = slice(h * d, (h + 1) * d)
        tab_scr[0, :, hs] = jnp.exp(lf * (C - 1.0 - pos))
        tab_scr[1, :, hs] = jnp.exp(lb * pos)
        tab_scr[2, :, hs] = jnp.exp(lf * (pos + 1.0))
        tab_scr[3, :, hs] = jnp.exp(lb * (C - pos))
        tab_scr[4, :, hs] = jnp.exp(lf * C)
        tab_scr[5, :, hs] = jnp.exp(lb * C)

    def rows(c):
        return pl.ds(pl.multiple_of(c * C, C), C)

    def part(p):
        return slice(c0 + p * WIDTH_RET, c0 + (p + 1) * WIDTH_RET)

    W = WIDTH_RET
    head_shift = d.bit_length() - 1
    assert d == 1 << head_shift
    same_head = ((lax.broadcasted_iota(jnp.int32, (W, W), 0) >> head_shift)
                 == (lax.broadcasted_iota(jnp.int32, (W, W), 1) >> head_shift))
    lane_head = lax.broadcasted_iota(jnp.int32, (C, W), 1) >> head_shift
    head_mean = jnp.where(same_head, 1.0 / d, 0.0).astype(BF16)

    def upd(c, carry):
        k = z_ref[rows(c), part(1)].astype(F32) * SCALE
        v = z_ref[rows(c), part(2)]
        u_scr[0, c] = jnp.where(same_head, _mm((k * tab_scr[0]).T, v), 0.0)
        u_scr[1, c] = jnp.where(same_head, _mm((k * tab_scr[1]).T, v), 0.0)
        return carry

    lax.fori_loop(0, nc, upd, 0)

    for direction in range(2):
        if s0_ref is None:
            s_init = jnp.zeros((W, W), F32)
        else:
            u_scr[direction, nc] = jnp.zeros((W, W), F32)
            for h in range(N_HEADS_RET):
                u_scr[direction, nc, h * d:(h + 1) * d, h * d:(h + 1) * d] = s0_ref[0, 0, direction, h]
            s_init = u_scr[direction, nc]
        chunk_decay = tab_scr[4 + direction, 0:1, :]

        def scan(t, s):
            c = t if direction == 0 else nc - 1 - t
            s_scr[direction, c] = s.astype(BF16)
            return chunk_decay * s + u_scr[direction, c]

        s_fin = lax.fori_loop(0, nc, scan, s_init)
        if st_ref is not None:
            for h in range(N_HEADS_RET):
                st_ref[0, direction, h] = s_fin[h * d:(h + 1) * d, h * d:(h + 1) * d]

    def out(c, carry):
        q = z_ref[rows(c), part(0)].astype(BF16)
        k = (z_ref[rows(c), part(1)].astype(F32) * SCALE).astype(BF16)
        v = z_ref[rows(c), part(2)].astype(BF16)
        gate = z_ref[rows(c), part(3)].astype(F32)
        o = (jnp.dot(q, s_scr[0, c], preferred_element_type=F32) * tab_scr[2]
             + jnp.dot(q, s_scr[1, c], preferred_element_type=F32) * tab_scr[3])
        intra = [_mm_nt(jnp.where(lane_head == h, q, jnp.zeros_like(q)), k) * d_scr[h]
                 for h in range(N_HEADS_RET)]
        for h in range(N_HEADS_RET):
            o = o + jnp.where(lane_head == h, _mm(intra[h], v), 0.0)
        sq = o * o
        sq_hi = sq.astype(BF16)
        sq_lo = (sq - sq_hi.astype(F32)).astype(BF16)
        ms = (jnp.dot(sq_hi, head_mean, preferred_element_type=F32)
              + jnp.dot(sq_lo, head_mean, preferred_element_type=F32))
        o = o * lax.rsqrt(ms + EPS) * ng_ref[...]
        o_ref[rows(c), :] = (o * _silu(gate)).astype(o_ref.dtype)
        return carry

    lax.fori_loop(0, nc, out, 0)


def _ret_scratch(n_tok):
    nc = n_tok // RET_CHUNK
    return [pltpu.VMEM((6, RET_CHUNK, WIDTH_RET), F32),
            pltpu.VMEM((N_HEADS_RET, RET_CHUNK, RET_CHUNK), F32),
            pltpu.VMEM((2, nc + 1, WIDTH_RET, WIDTH_RET), F32),
            pltpu.VMEM((2, nc, WIDTH_RET, WIDTH_RET), BF16)]


def _ctx_attn_body(sink_ref, decay_ref, z_ref, zr_ref, ng_ref, *refs, seq, n_prev, layer, depth):
    (ona_ref, oswa_ref, oret_ref, kna_ref, vna_ref, kswa_ref, vswa_ref, st_ref,
     tab_scr, d_scr, u_scr, s_scr) = refs[n_prev:]
    cache_refs = (kna_ref, vna_ref, kswa_ref, vswa_ref, st_ref)
    if n_prev == 0:
        for ref in cache_refs:
            for other in range(depth):
                if other != layer:
                    ref[0, other] = jnp.zeros(ref.shape[2:], ref.dtype)
        kna_ref, vna_ref, kswa_ref, vswa_ref, st_ref = (ref.at[:, pl.ds(layer, 1)] for ref in cache_refs)
    d = HEAD_DIM
    low_half = lax.broadcasted_iota(jnp.int32, (seq, 2 * d), 1) < d

    def pair_scores(q2, k2, head_is_low):
        return _mm_nt(jnp.where(low_half == head_is_low, q2, jnp.zeros_like(q2)), k2)

    jobs = []
    for pair in range(N_HEADS_NA // 2):
        cols = lambda part: slice(OFF_NA + part * WIDTH_NA + pair * 2 * d,
                                  OFF_NA + part * WIDTH_NA + (pair + 1) * 2 * d)
        q2 = (z_ref[:, cols(0)] * SCALE).astype(BF16)
        k2 = z_ref[:, cols(1)]
        v2 = z_ref[:, cols(2)]
        for hh in range(2):
            kna_ref[0, 0, 2 * pair + hh] = k2[:, hh * d:(hh + 1) * d]
            vna_ref[0, 0, 2 * pair + hh] = v2[:, hh * d:(hh + 1) * d]
        k2 = k2.astype(BF16)
        v2 = v2.astype(BF16)
        jobs.append([(pair_scores(q2, k2, hh == 0), v2, hh == 0, None) for hh in range(2)]
                    + [ona_ref, pair])
    lo = OFF_SWA + WIDTH_SWA
    k2 = z_ref[:, lo:lo + WIDTH_KV_SWA]
    v2 = z_ref[:, lo + WIDTH_KV_SWA:lo + 2 * WIDTH_KV_SWA]
    for g in range(N_KV_SWA):
        kswa_ref[0, 0, g] = k2[:, g * d:(g + 1) * d]
        vswa_ref[0, 0, g] = v2[:, g * d:(g + 1) * d]
    k2 = k2.astype(BF16)
    v2 = v2.astype(BF16)
    group = N_HEADS_SWA // N_KV_SWA
    for pair in range(N_HEADS_SWA // 2):
        q2 = (z_ref[:, OFF_SWA + pair * 2 * d:OFF_SWA + (pair + 1) * 2 * d] * SCALE).astype(BF16)
        heads = []
        for hh in range(2):
            h = 2 * pair + hh
            g = h // group
            qh = q2[:, hh * d:(hh + 1) * d]
            q_at_g = jnp.concatenate([qh, qh], axis=-1) if g != hh else q2
            heads.append((pair_scores(q_at_g, k2, g == 0), v2, g == 0, sink_ref[h]))
        jobs.append(heads + [oswa_ref, pair])
    probs = []
    for job in jobs:
        for s, v2, v_is_low, sink in job[:2]:
            m = jnp.max(s, axis=-1, keepdims=True)
            if sink is not None:
                m = jnp.maximum(m, sink)
            e = jnp.exp(s - m)
            den = jnp.sum(e, axis=-1, keepdims=True)
            if sink is not None:
                den = den + jnp.exp(sink - m)
            probs.append((e, den))
    for j, job in enumerate(jobs):
        halves = []
        for hh, (s, v2, v_is_low, sink) in enumerate(job[:2]):
            e, den = probs[2 * j + hh]
            o2 = _mm(e, v2) / den
            halves.append(o2[:, :d] if v_is_low else o2[:, d:])
        o_ref, pair = job[2], job[3]
        o_ref[:, pair * 2 * d:(pair + 1) * 2 * d] = jnp.concatenate(halves, axis=-1).astype(o_ref.dtype)
    _retention(zr_ref, 0, decay_ref, ng_ref, None, oret_ref, st_ref.at[0],
               tab_scr, d_scr, u_scr, s_scr, seq)


def _ctx_mixers(z_att, z_rest, swa_sink, ret_decay, ret_norm_g, caches, layer, depth, batch, seq):
    n = batch * seq
    assert z_att.shape[1] == OFF_RET
    smem = pl.BlockSpec(memory_space=pltpu.SMEM)
    first = caches is None
    layers_in_block = depth if first else 1
    block_layer = 0 if first else layer
    kv = lambda heads: pl.BlockSpec((1, layers_in_block, heads, seq, HEAD_DIM),
                                    lambda b: (b, block_layer, 0, 0, 0))
    kv_shape = lambda heads: jax.ShapeDtypeStruct((batch, depth, heads, seq, HEAD_DIM), F32)
    in_specs = [smem, smem,
                pl.BlockSpec((seq, OFF_RET), lambda b: (b, 0)),
                pl.BlockSpec((seq, 4 * WIDTH_RET), lambda b: (b, 0)),
                pl.BlockSpec((1, WIDTH_RET), lambda b: (0, 0))]
    args = [swa_sink, ret_decay, z_att, z_rest, ret_norm_g.reshape(1, WIDTH_RET)]
    n_prev = 0 if caches is None else len(caches)
    aliases = {len(args) + i: 3 + i for i in range(n_prev)}
    if caches is not None:
        in_specs += [pl.BlockSpec(memory_space=pl.ANY)] * n_prev
        args += list(caches)
    return pl.pallas_call(
        functools.partial(_ctx_attn_body, seq=seq, n_prev=n_prev, layer=layer, depth=depth),
        grid=(batch,),
        in_specs=in_specs,
        out_specs=[pl.BlockSpec((seq, WIDTH_NA), lambda b: (b, 0)),
                   pl.BlockSpec((seq, WIDTH_SWA), lambda b: (b, 0)),
                   pl.BlockSpec((seq, WIDTH_RET), lambda b: (b, 0)),
                   kv(N_HEADS_NA), kv(N_HEADS_NA), kv(N_KV_SWA), kv(N_KV_SWA),
                   pl.BlockSpec((1, layers_in_block, 2, N_HEADS_RET, HEAD_DIM, HEAD_DIM),
                                lambda b: (b, block_layer, 0, 0, 0, 0))],
        out_shape=[jax.ShapeDtypeStruct((n, WIDTH_NA), BF16),
                   jax.ShapeDtypeStruct((n, WIDTH_SWA), BF16),
                   jax.ShapeDtypeStruct((n, WIDTH_RET), BF16),
                   kv_shape(N_HEADS_NA), kv_shape(N_HEADS_NA), kv_shape(N_KV_SWA), kv_shape(N_KV_SWA),
                   jax.ShapeDtypeStruct((batch, depth, 2, N_HEADS_RET, HEAD_DIM, HEAD_DIM), F32)],
        scratch_shapes=_ret_scratch(seq),
        input_output_aliases=aliases,
        compiler_params=_params(("arbitrary",), 32),
        name="ctx_mixers",
    )(*args)


NA_QROWS = 4
NA_WROWS = NA_QROWS + WIN_ROWS


def _na_bias_table(rpb, rows):
    n_groups = rows // NA_QROWS
    assert rows % NA_QROWS == 0 and n_groups >= 3
    n_heads, n_ro, n_co = rpb.shape
    half = WIN_ROWS // 2
    cc = np.arange(GRID_W)
    col_start = np.clip(cc - WIN_COLS // 2, 0, GRID_W - WIN_COLS)
    valid = (cc[None, :] >= col_start[:, None]) & (cc[None, :] < col_start[:, None] + WIN_COLS)
    span = 2 * GRID_W - 1
    lead = GRID_W - WIN_COLS
    ext = jnp.pad(rpb.astype(F32), ((0, 0), (0, 0), (lead, span + 1 - lead - n_co)))
    band = jnp.tile(ext, (1, 1, GRID_W))[..., :GRID_W * span].reshape(n_heads, n_ro, GRID_W, span)
    band = jnp.where(valid[None, None], band[..., GRID_W - 1:], MASKED)
    outside = jnp.full((n_heads, GRID_W, GRID_W), MASKED, F32)
    tabs = []
    for g in (0, 1, n_groups - 1):
        ws = min(max(NA_QROWS * g - half, 0), rows - NA_WROWS)
        q_blocks = []
        for i in range(NA_QROWS):
            r = NA_QROWS * g + i
            rs = min(max(r - half, 0), rows - WIN_ROWS)
            q_blocks.append(jnp.concatenate(
                [band[:, ws + j - r + WIN_ROWS - 1] if rs <= ws + j < rs + WIN_ROWS else outside
                 for j in range(NA_WROWS)], axis=-1))
        tabs.append(jnp.concatenate(q_blocks, axis=-2))
    return jnp.stack(tabs, axis=0)


def _lat_na_body(z_ref, kc_ref, vc_ref, bias_ref, o_ref, kc_scr, vc_scr, *, rows):
    d = HEAD_DIM
    nq = NA_QROWS * GRID_W
    g = pl.program_id(1)

    @pl.when(g == 0)
    def _():
        kc_scr[...] = kc_ref[0, 0].astype(BF16)
        vc_scr[...] = vc_ref[0, 0].astype(BF16)

    ws = jnp.clip(NA_QROWS * g - WIN_ROWS // 2, 0, rows - NA_WROWS)
    qrows = pl.ds(pl.multiple_of(g * nq, nq), nq)
    wrows = pl.ds(pl.multiple_of(ws * GRID_W, GRID_W), NA_WROWS * GRID_W)
    low_half = lax.broadcasted_iota(jnp.int32, (nq, 2 * d), 1) < d
    for pair in range(N_HEADS_NA // 2):
        cols = slice(pair * 2 * d, (pair + 1) * 2 * d)
        q2 = (z_ref[qrows, cols] * SCALE).astype(BF16)
        kl2 = z_ref[wrows, WIDTH_NA + pair * 2 * d:WIDTH_NA + (pair + 1) * 2 * d].astype(BF16)
        vl2 = z_ref[wrows, 2 * WIDTH_NA + pair * 2 * d:2 * WIDTH_NA + (pair + 1) * 2 * d].astype(BF16)
        for hh in range(2):
            h = 2 * pair + hh
            qm = jnp.where(low_half == (hh == 0), q2, jnp.zeros_like(q2))
            s_loc = _mm_nt(qm, kl2) + bias_ref[0, h]
            s_ctx = _mm_nt(q2[:, hh * d:(hh + 1) * d], kc_scr[h])
            m = jnp.maximum(jnp.max(s_ctx, axis=-1, keepdims=True), jnp.max(s_loc, axis=-1, keepdims=True))
            e_ctx = jnp.exp(s_ctx - m)
            e_loc = jnp.exp(s_loc - m)
            den = jnp.sum(e_ctx, axis=-1, keepdims=True) + jnp.sum(e_loc, axis=-1, keepdims=True)
            o = (_mm(e_ctx, vc_scr[h]) + _mm(e_loc, vl2)[:, hh * d:(hh + 1) * d]) / den
            o_ref[:, h * d:(h + 1) * d] = o.astype(o_ref.dtype)


def _lat_na(z, cache_k, cache_v, bias_tab, layer, batch, seq):
    rows = seq // GRID_W
    n_groups = rows // NA_QROWS
    nq = NA_QROWS * GRID_W
    past = cache_k.shape[3]

    def group_class(g):
        return jnp.where(g == 0, 0, jnp.where(g == n_groups - 1, 2, 1))

    cache = pl.BlockSpec((1, 1, N_HEADS_NA, past, HEAD_DIM), lambda b, g: (b, layer, 0, 0, 0))
    return pl.pallas_call(
        functools.partial(_lat_na_body, rows=rows),
        grid=(batch, n_groups),
        in_specs=[pl.BlockSpec((seq, 3 * WIDTH_NA), lambda b, g: (b, 0)),
                  cache, cache,
                  pl.BlockSpec((1, N_HEADS_NA, nq, NA_WROWS * GRID_W),
                               lambda b, g: (group_class(g), 0, 0, 0))],
        out_specs=pl.BlockSpec((nq, WIDTH_NA), lambda b, g: (b * n_groups + g, 0)),
        out_shape=jax.ShapeDtypeStruct((batch * seq, WIDTH_NA), BF16),
        scratch_shapes=[pltpu.VMEM((N_HEADS_NA, past, HEAD_DIM), BF16),
                        pltpu.VMEM((N_HEADS_NA, past, HEAD_DIM), BF16)],
        compiler_params=_params(("arbitrary", "arbitrary"), 58),
        name="lat_na",
    )(z, cache_k, cache_v, bias_tab)


def _rope_tables(seq):
    t = jnp.arange(seq)
    n_freq = HEAD_DIM // 4
    inv = ROPE_BASE ** (-jnp.arange(n_freq, dtype=F32) / n_freq)
    ang = jnp.concatenate([(t // GRID_W).astype(F32)[:, None] * inv,
                           (t % GRID_W).astype(F32)[:, None] * inv], axis=-1)
    cos = jnp.repeat(jnp.cos(ang), 2, axis=-1)
    sign = jnp.tile(jnp.asarray([-1.0, 1.0], F32), HEAD_DIM // 2)
    sin = jnp.repeat(jnp.sin(ang), 2, axis=-1) * sign
    return jnp.tile(cos, (1, N_HEADS_SWA)), jnp.tile(sin, (1, N_HEADS_SWA))


def _rope(x, cos, sin_signed):
    width = x.shape[-1]
    lane = lax.broadcasted_iota(jnp.int32, x.shape, x.ndim - 1)
    partner = jnp.where((lane & 1) == 0, pltpu.roll(x, width - 1, x.ndim - 1), pltpu.roll(x, 1, x.ndim - 1))
    return x * cos + partner * sin_signed


def _lat_swa_body(sink_ref, z_ref, kc_ref, vc_ref, cos_ref, sin_ref, o_ref, kr_scr, kc_scr, vc_scr, *, seq):
    d = HEAD_DIM
    blk = SWA_BLOCK
    n = pl.program_id(1)
    assert N_KV_SWA == 2

    @pl.when(n == 0)
    def _():
        k = z_ref[:, WIDTH_SWA:WIDTH_SWA + WIDTH_KV_SWA].astype(F32)
        kr_scr[...] = _rope(k, cos_ref[:, :WIDTH_KV_SWA], sin_ref[:, :WIDTH_KV_SWA]).astype(BF16)
        kc_scr[...] = jnp.concatenate([kc_ref[0, 0, 0], kc_ref[0, 0, 1]], axis=-1).astype(BF16)
        vc_scr[...] = jnp.concatenate([vc_ref[0, 0, 0], vc_ref[0, 0, 1]], axis=-1).astype(BF16)

    qrows = pl.ds(pl.multiple_of(n * blk, blk), blk)
    kstart = jnp.clip((n - 1) * blk, 0, seq - 3 * blk)
    krows = pl.ds(pl.multiple_of(kstart, blk), 3 * blk)
    q_all = _rope(z_ref[qrows, 0:WIDTH_SWA].astype(F32), cos_ref[qrows, :], sin_ref[qrows, :]) * SCALE
    q_all = q_all.astype(BF16)
    qpos = n * blk + lax.broadcasted_iota(jnp.int32, (blk, 3 * blk), 0)
    kpos = kstart + lax.broadcasted_iota(jnp.int32, (blk, 3 * blk), 1)
    valid = jnp.abs(qpos - kpos) <= SWA_RADIUS
    kl2 = kr_scr[krows, :]
    vl2 = z_ref[krows, WIDTH_SWA + WIDTH_KV_SWA:WIDTH_SWA + 2 * WIDTH_KV_SWA]
    group = N_HEADS_SWA // N_KV_SWA
    zero = jnp.zeros((blk, d), BF16)
    scores = []
    for h in range(N_HEADS_SWA):
        g = h // group
        qh = q_all[:, h * d:(h + 1) * d]
        q_at_g = jnp.concatenate([qh, zero] if g == 0 else [zero, qh], axis=-1)
        scores.append((jnp.where(valid, _mm_nt(q_at_g, kl2), MASKED), _mm_nt(q_at_g, kc_scr[...])))
    probs = []
    for h, (s_loc, s_ctx) in enumerate(scores):
        m = jnp.maximum(jnp.maximum(jnp.max(s_loc, axis=-1, keepdims=True),
                                    jnp.max(s_ctx, axis=-1, keepdims=True)), sink_ref[h])
        e_loc = jnp.exp(s_loc - m)
        e_ctx = jnp.exp(s_ctx - m)
        den = (jnp.sum(e_loc, axis=-1, keepdims=True) + jnp.sum(e_ctx, axis=-1, keepdims=True)
               + jnp.exp(sink_ref[h] - m))
        probs.append((e_loc, e_ctx, den))
    outs = []
    for h, (e_loc, e_ctx, den) in enumerate(probs):
        g = h // group
        o2 = (_mm(e_ctx, vc_scr[...]) + _mm(e_loc, vl2)) / den
        outs.append(o2[:, g * d:(g + 1) * d])
    o_ref[...] = jnp.concatenate(outs, axis=-1).astype(o_ref.dtype)


def _lat_swa(z, cache_k, cache_v, swa_sink, cos, sin, layer, batch, seq):
    nb = seq // SWA_BLOCK
    past = cache_k.shape[3]
    width = WIDTH_SWA + 2 * WIDTH_KV_SWA
    assert OFF_SWA % width == 0 and seq >= 3 * SWA_BLOCK
    cache = pl.BlockSpec((1, 1, N_KV_SWA, past, HEAD_DIM), lambda b, n: (b, layer, 0, 0, 0))
    table = pl.BlockSpec((seq, WIDTH_SWA), lambda b, n: (0, 0))
    return pl.pallas_call(
        functools.partial(_lat_swa_body, seq=seq),
        grid=(batch, nb),
        in_specs=[pl.BlockSpec(memory_space=pltpu.SMEM),
                  pl.BlockSpec((seq, width), lambda b, n: (b, OFF_SWA // width)),
                  cache, cache, table, table],
        out_specs=pl.BlockSpec((SWA_BLOCK, WIDTH_SWA), lambda b, n: (b * nb + n, 0)),
        out_shape=jax.ShapeDtypeStruct((batch * seq, WIDTH_SWA), BF16),
        scratch_shapes=[pltpu.VMEM((seq, WIDTH_KV_SWA), BF16),
                        pltpu.VMEM((past, WIDTH_KV_SWA), BF16), pltpu.VMEM((past, WIDTH_KV_SWA), BF16)],
        compiler_params=_params(("arbitrary", "arbitrary"), 32),
        name="lat_swa",
    )(swa_sink, z, cache_k, cache_v, cos, sin)


def _lat_ret_body(decay_ref, z_ref, ng_ref, s0_ref, o_ref, tab_scr, d_scr, u_scr, s_scr, *, seq):
    _retention(z_ref, 0, decay_ref, ng_ref, s0_ref, o_ref, None, tab_scr, d_scr, u_scr, s_scr, seq)


def _lat_ret(z, ret_decay, ret_norm_g, state, layer, batch, seq):
    width = 4 * WIDTH_RET
    assert OFF_RET % width == 0
    return pl.pallas_call(
        functools.partial(_lat_ret_body, seq=seq),
        grid=(batch,),
        in_specs=[pl.BlockSpec(memory_space=pltpu.SMEM),
                  pl.BlockSpec((seq, width), lambda b: (b, OFF_RET // width)),
                  pl.BlockSpec((1, WIDTH_RET), lambda b: (0, 0)),
                  pl.BlockSpec((1, 1, 2, N_HEADS_RET, HEAD_DIM, HEAD_DIM),
                               lambda b: (b, layer, 0, 0, 0, 0))],
        out_specs=pl.BlockSpec((seq, WIDTH_RET), lambda b: (b, 0)),
        out_shape=jax.ShapeDtypeStruct((batch * seq, WIDTH_RET), BF16),
        scratch_shapes=_ret_scratch(seq),
        compiler_params=_params(("arbitrary",), 40),
        name="lat_ret",
    )(ret_decay, z, ret_norm_g.reshape(1, WIDTH_RET), state)


def _merge_body(ona_ref, oswa_ref, oret_ref, ga_ref, gs_ref, gr_ref, x_ref, mod_ref, g2_ref,
                wna_ref, wswa_ref, wret_ref, wout_ref, rw_ref, rb_ref,
                hx_ref, wna_scr, wswa_scr, wret_scr, wout_scr, rw_scr, *, per_cond):
    @pl.when(pl.program_id(0) == 0)
    def _():
        wna_scr[...] = wna_ref[0].astype(BF16)
        wswa_scr[...] = wswa_ref[0].astype(BF16)
        wret_scr[...] = wret_ref[0].astype(BF16)
        wout_scr[...] = wout_ref[0].astype(BF16)
        rw = rw_ref[0]
        rw_hi = rw.astype(BF16)
        rw_scr[0] = rw_hi
        rw_scr[1] = (rw - rw_hi.astype(F32)).astype(BF16)

    D = D_MODEL
    m = mod_ref[0]
    sub = x_ref.shape[0] // MERGE_SPLIT
    parts = [pl.ds(s * sub, sub) for s in range(MERGE_SPLIT)]
    gate = lambda ref, rows: _sigmoid(ref[rows, :].astype(F32))
    zs = [(gate(ga_ref, r) * jnp.dot(ona_ref[r, :], wna_scr[...], preferred_element_type=F32)
           + gate(gs_ref, r) * jnp.dot(oswa_ref[r, :], wswa_scr[...], preferred_element_type=F32)
           + gate(gr_ref, r) * jnp.dot(oret_ref[r, :], wret_scr[...], preferred_element_type=F32)
           ).astype(BF16) for r in parts]
    ys = [jnp.dot(z, wout_scr[...], preferred_element_type=F32) for z in zs]
    logits = []
    for r, y in zip(parts, ys):
        x1 = x_ref[r, :] + m[:, 2 * D:3 * D] * y
        hx_ref[r, HX_X1:HX_X1 + D] = x1
        h2 = x1 * lax.rsqrt(jnp.mean(x1 * x1, axis=-1, keepdims=True) + EPS) * g2_ref[0]
        h2 = h2 * (1.0 + m[:, 4 * D:5 * D]) + m[:, 3 * D:4 * D]
        hx_ref[r, 0:D] = h2
        h_hi = h2.astype(BF16)
        h_lo = (h2 - h_hi.astype(F32)).astype(BF16)
        logits.append((jnp.dot(h_hi, rw_scr[0], preferred_element_type=F32)
                       + jnp.dot(h_hi, rw_scr[1], preferred_element_type=F32)
                       + jnp.dot(h_lo, rw_scr[0], preferred_element_type=F32)) + rb_ref[0])
    for r, logit in zip(parts, logits):
        route = _route(logit)
        cond = (pl.program_id(0) * x_ref.shape[0] // per_cond).astype(F32)
        lane = lax.broadcasted_iota(jnp.int32, route.shape, 1)
        hx_ref[r, D:D + ROUTER_LANES] = jnp.where(lane == ROUTE_COND_LANE, cond, route)


MERGE_SPLIT = 2


def _route(logit):
    lane = lax.broadcasted_iota(jnp.int32, logit.shape, 1).astype(F32)
    neg = jnp.float32(-jnp.inf)
    big = jnp.float32(ROUTER_LANES)
    is_group = lane < N_GROUPS
    is_expert = (lane >= N_GROUPS) & (lane < N_GROUPS + N_EXPERTS)
    gl = jnp.where(is_group, logit, neg)
    gmax = jnp.max(gl, axis=-1, keepdims=True)
    gsel = jnp.min(jnp.where(gl == gmax, lane, big), axis=-1, keepdims=True)
    p_group = 1.0 / jnp.sum(jnp.where(is_group, jnp.exp(gl - gmax), 0.0), axis=-1, keepdims=True)
    eidx = lane - N_GROUPS
    in_group = is_expert & (jnp.floor(eidx / EXPERTS_PER_GROUP) == gsel)
    el = jnp.where(in_group, logit, neg)
    top1 = jnp.max(el, axis=-1, keepdims=True)
    sel1 = jnp.min(jnp.where(el == top1, lane, big), axis=-1, keepdims=True)
    el2 = jnp.where(lane == sel1, neg, el)
    top2 = jnp.max(el2, axis=-1, keepdims=True)
    sel2 = jnp.min(jnp.where(el2 == top2, lane, big), axis=-1, keepdims=True)
    e2 = jnp.exp(top2 - top1)
    w1 = p_group / (1.0 + e2)
    w2 = p_group * e2 / (1.0 + e2)
    comb = jnp.where(lane == sel1, w1, 0.0) + jnp.where(lane == sel2, w2, 0.0)
    return comb + jnp.where(lane == ROUTE_GROUP_LANE, gsel, 0.0)


def _merge(o_na, o_swa, o_ret, z, gate_tile, x, mod_rows, norm2_g, w_br_na, w_br_swa, w_br_ret, w_out,
           router_w, router_b, layer):
    n, d = x.shape
    n_cond = mod_rows.shape[0]
    per_cond = n // n_cond
    tm = 512
    gate = lambda k: pl.BlockSpec((tm, d), lambda i: (i, gate_tile + k))
    row = lambda w: pl.BlockSpec((tm, w), lambda i: (i, 0))
    weight = lambda k: pl.BlockSpec((1, k, d), lambda i: (layer, 0, 0))
    return pl.pallas_call(
        functools.partial(_merge_body, per_cond=per_cond),
        grid=(n // tm,),
        in_specs=[row(WIDTH_NA), row(WIDTH_SWA), row(WIDTH_RET),
                  gate(0), gate(1), gate(2),
                  row(d),
                  pl.BlockSpec((1, 1, 6 * d), lambda i: (i * tm // per_cond, 0, 0)),
                  pl.BlockSpec((1, 1, d), lambda i: (layer, 0, 0)),
                  weight(WIDTH_NA), weight(WIDTH_SWA), weight(WIDTH_RET), weight(d),
                  pl.BlockSpec((1, d, ROUTER_LANES), lambda i: (layer, 0, 0)),
                  pl.BlockSpec((1, 1, ROUTER_LANES), lambda i: (layer, 0, 0))],
        out_specs=row(HX_WIDTH),
        out_shape=jax.ShapeDtypeStruct((n, HX_WIDTH), F32),
        scratch_shapes=[pltpu.VMEM((WIDTH_NA, d), BF16), pltpu.VMEM((WIDTH_SWA, d), BF16),
                        pltpu.VMEM((WIDTH_RET, d), BF16), pltpu.VMEM((d, d), BF16),
                        pltpu.VMEM((2, d, ROUTER_LANES), BF16)],
        compiler_params=_params(("arbitrary",), 56),
        name="merge_router",
    )(o_na, o_swa, o_ret, z, z, z, x, mod_rows, norm2_g.reshape(norm2_g.shape[0], 1, d),
      w_br_na, w_br_swa, w_br_ret, w_out, router_w, router_b)


MOE_TILE = 512


def _route_plan(hx, n_tiles):
    n = hx.shape[0]
    g = hx[:, D_MODEL + ROUTE_GROUP_LANE].astype(jnp.int32)
    onehot = (g[:, None] == jnp.arange(N_GROUPS, dtype=jnp.int32)[None, :]).astype(jnp.int32)
    csum = jnp.cumsum(onehot, axis=0)
    counts = csum[-1]
    ends = jnp.cumsum(counts)
    dest = jnp.sum(onehot * ((ends - counts)[None, :] + csum - 1), axis=1)
    first = jnp.arange(n_tiles, dtype=jnp.int32) * MOE_TILE
    tile_lo = jnp.sum((first[:, None] >= ends[None, :]).astype(jnp.int32), axis=1)
    tile_hi = jnp.sum(((first + MOE_TILE - 1)[:, None] >= ends[None, :]).astype(jnp.int32), axis=1)
    return dest, tile_lo, tile_hi


def _moe_body(dest_ref, lo_ref, hi_ref, hx_hbm, wg_hbm, wu_hbm, wd_hbm, mod_ref, fg_ref, out_hbm,
              order_scr, hbuf, obuf, y_scr, wg_scr, wu_scr, wd_scr, in_sem, out_sem, w_sem,
              *, layer, n_tiles, n_tok, n_cond, final):
    t = pl.program_id(0)
    slot = lax.rem(t, 2)
    nxt = jnp.minimum(t + 1, n_tiles - 1)

    def in_copy(tile, i, slot_):
        return pltpu.make_async_copy(hx_hbm.at[pl.ds(order_scr[tile * MOE_TILE + i], 1)],
                                     hbuf.at[slot_, pl.ds(i, 1)], in_sem.at[slot_])

    def out_copy(tile, i, slot_):
        return pltpu.make_async_copy(obuf.at[slot_, pl.ds(i, 1)],
                                     out_hbm.at[pl.ds(order_scr[tile * MOE_TILE + i], 1)], out_sem.at[slot_])

    def for_rows(fn):
        def body(i, carry):
            fn(i)
            return carry
        lax.fori_loop(0, MOE_TILE, body, 0, unroll=8)

    @pl.when(t == 0)
    def _():
        copies = [pltpu.make_async_copy(src.at[layer], dst, w_sem.at[k])
                  for k, (src, dst) in enumerate(((wg_hbm, wg_scr), (wu_hbm, wu_scr), (wd_hbm, wd_scr)))]
        for copy in copies:
            copy.start()

        def invert(i, carry):
            order_scr[dest_ref[i]] = i
            return carry
        lax.fori_loop(0, n_tok, invert, 0, unroll=8)
        for_rows(lambda i: in_copy(0, i, 0).start())
        for copy in copies:
            copy.wait()

    for_rows(lambda i: in_copy(t, i, slot).wait())
    hx = hbuf[slot]
    h = hx[:, 0:D_MODEL].astype(BF16)
    route = hx[:, D_MODEL:D_MODEL + ROUTER_LANES]
    lane = lax.broadcasted_iota(jnp.int32, route.shape, 1)
    rows_per_expert = MOE_TILE // EXPERTS_PER_GROUP

    def group(k, write_back_previous):
        first = (lo_ref[t] + k) * EXPERTS_PER_GROUP
        y = None
        for j in range(EXPERTS_PER_GROUP):
            e = first + j
            hg = jnp.dot(h, wg_scr[e], preferred_element_type=F32)
            hu = jnp.dot(h, wu_scr[e], preferred_element_type=F32)
            c = jnp.sum(jnp.where(lane == ROUTE_EXPERT_LANE + e, route, 0.0), axis=-1, keepdims=True)
            a = _silu(hg) * hu * c
            part = jnp.dot(a.astype(BF16), wd_scr[e], preferred_element_type=F32)
            y = part if y is None else y + part
            if k == 0:
                for i in range(j * rows_per_expert, (j + 1) * rows_per_expert):
                    in_copy(nxt, i, 1 - slot).start()
                    if write_back_previous:
                        out_copy(t - 1, i, 1 - slot).start()
        if k == 0:
            y_scr[...] = y
        else:
            y_scr[...] += y

    pl.when(t == 0)(functools.partial(group, 0, False))
    pl.when(t > 0)(functools.partial(group, 0, True))
    for k in range(1, N_GROUPS):
        pl.when(lo_ref[t] + k <= hi_ref[t])(functools.partial(group, k, False))

    cond = route[:, ROUTE_COND_LANE:ROUTE_COND_LANE + 1]
    gate2 = jnp.zeros((1, D_MODEL), F32)
    for c in range(n_cond):
        gate2 = jnp.where(cond == c, mod_ref[c][:, 5 * D_MODEL:6 * D_MODEL], gate2)
    x2 = hx[:, HX_X1:HX_X1 + D_MODEL] + gate2 * y_scr[...]
    if final:
        x2 = x2 * lax.rsqrt(jnp.mean(x2 * x2, axis=-1, keepdims=True) + EPS) * fg_ref[...]
    obuf[slot] = x2

    @pl.when(t > 0)
    def _():
        for_rows(lambda i: out_copy(t - 1, i, 1 - slot).wait())

    @pl.when(t == n_tiles - 1)
    def _():
        for_rows(lambda i: in_copy(nxt, i, 1 - slot).wait())
        for_rows(lambda i: out_copy(t, i, slot).start())
        for_rows(lambda i: out_copy(t, i, slot).wait())


def _moe(hx, mod_rows, w_gate, w_up, w_down, final_g, layer, final):
    n = hx.shape[0]
    d = D_MODEL
    n_tiles = n // MOE_TILE
    n_cond = mod_rows.shape[0]
    dest, tile_lo, tile_hi = _route_plan(hx, n_tiles)
    any_space = pl.BlockSpec(memory_space=pl.ANY)
    return pl.pallas_call(
        functools.partial(_moe_body, layer=layer, n_tiles=n_tiles, n_tok=n, n_cond=n_cond, final=final),
        grid_spec=pltpu.PrefetchScalarGridSpec(
            num_scalar_prefetch=3,
            grid=(n_tiles,),
            in_specs=[any_space, any_space, any_space, any_space,
                      pl.BlockSpec((n_cond, 1, 6 * d), lambda t, *_: (0, 0, 0)),
                      pl.BlockSpec((1, d), lambda t, *_: (0, 0))],
            out_specs=any_space,
            scratch_shapes=[pltpu.SMEM((n,), jnp.int32),
                            pltpu.VMEM((2, MOE_TILE, HX_WIDTH), F32),
                            pltpu.VMEM((2, MOE_TILE, d), F32),
                            pltpu.VMEM((MOE_TILE, d), F32),
                            pltpu.VMEM((N_EXPERTS, d, D_EXPERT), BF16),
                            pltpu.VMEM((N_EXPERTS, d, D_EXPERT), BF16),
                            pltpu.VMEM((N_EXPERTS, D_EXPERT, d), BF16),
                            pltpu.SemaphoreType.DMA((2,)),
                            pltpu.SemaphoreType.DMA((2,)),
                            pltpu.SemaphoreType.DMA((3,))]),
        out_shape=jax.ShapeDtypeStruct((n, d), F32),
        compiler_params=_params(("arbitrary",), 56),
        name="moe",
    )(dest, tile_lo, tile_hi, hx, w_gate, w_up, w_down, mod_rows, final_g.reshape(1, d))


def kernel(x_prompt, x_sample, c, cache_na_k, cache_na_v, cache_swa_k, cache_swa_v, state_ret, c_ctx, ada_w, ada_b, norm1_g, norm2_g, w_in, na_rpb, swa_sink, ret_decay, ret_norm_g, w_br_na, w_br_swa, w_br_ret, w_out, router_group_w, router_group_b, router_expert_w, router_expert_b, w_gate, w_up, w_down, final_norm_g):
    batch, seq, d = x_prompt.shape
    dec_batch, dec_seq, _ = x_sample.shape
    depth = ada_w.shape[0]
    assert d == D_MODEL and dec_batch + 1 <= 8
    assert OFF_RET % IN_TILE == 0 and OFF_GATES % d == 0 and (OFF_GATES - OFF_RET) % d == 0

    cond = jnp.concatenate([c_ctx[None], c, jnp.zeros((8 - 1 - dec_batch, d), F32)], axis=0)
    mod = _modulation(cond, ada_w, ada_b)
    router_w = jnp.concatenate([router_group_w, router_expert_w], axis=-1)
    router_w = jnp.pad(router_w, ((0, 0), (0, 0), (0, ROUTER_LANES - router_w.shape[-1])))
    router_b = jnp.concatenate([router_group_b, router_expert_b], axis=-1)
    router_b = jnp.pad(router_b, ((0, 0), (0, ROUTER_LANES - router_b.shape[-1])))[:, None, :]
    cos, sin = _rope_tables(dec_seq)
    moe_weights = (w_gate.astype(BF16), w_up.astype(BF16), w_down.astype(BF16))

    xp = x_prompt.reshape(batch * seq, d)
    xs = x_sample.reshape(dec_batch * dec_seq, d)
    caches = None
    for l in range(depth):
        final = l == depth - 1
        mod_ctx = mod[l, 0:1][:, None, :]
        mod_lat = mod[l, 1:1 + dec_batch][:, None, :]
        merge_w = (norm2_g, w_br_na, w_br_swa, w_br_ret, w_out, router_w, router_b, l)
        moe_w = (*moe_weights, final_norm_g, l, final)

        z_att, z_rest = _in_proj(xp, mod_ctx, norm1_g, w_in, l, OFF_RET // IN_TILE)
        o_na, o_swa, o_ret, *caches = _ctx_mixers(
            z_att, z_rest, swa_sink[l], ret_decay[l], ret_norm_g[l], caches, l, depth, batch, seq)
        hx = _merge(o_na, o_swa, o_ret, z_rest, (OFF_GATES - OFF_RET) // d, xp, mod_ctx, *merge_w)
        xp = _moe(hx, mod_ctx, *moe_w)

        z, = _in_proj(xs, mod_lat, norm1_g, w_in, l, 0)
        o_na = _lat_na(z, cache_na_k, cache_na_v, _na_bias_table(na_rpb[l], dec_seq // GRID_W),
                       l, dec_batch, dec_seq)
        o_swa = _lat_swa(z, cache_swa_k, cache_swa_v, swa_sink[l], cos, sin, l, dec_batch, dec_seq)
        o_ret = _lat_ret(z, ret_decay[l], ret_norm_g[l], state_ret, l, dec_batch, dec_seq)
        hx = _merge(o_na, o_swa, o_ret, z, OFF_GATES // d, xs, mod_lat, *merge_w)
        xs = _moe(hx, mod_lat, *moe_w)

    return (xp.reshape(batch, seq, d), xs.reshape(dec_batch, dec_seq, d), *caches)
```

```python
import functools

import numpy as np
import jax
import jax.numpy as jnp
from jax import lax
from jax.experimental import pallas as pl
from jax.experimental.pallas import tpu as pltpu

F32 = jnp.float32
BF16 = jnp.bfloat16

D_MODEL = 1024
HEAD_DIM = 64
GRID_W = 64
N_HEADS_NA = 8
WIN_ROWS = 8
WIN_COLS = 16
N_HEADS_SWA = 4
N_KV_SWA = 2
SWA_RADIUS = 128
SWA_BLOCK = 128
N_HEADS_RET = 4
RET_CHUNK = 128
WIDTH_NA = N_HEADS_NA * HEAD_DIM
WIDTH_SWA = N_HEADS_SWA * HEAD_DIM
WIDTH_KV_SWA = N_KV_SWA * HEAD_DIM
WIDTH_RET = N_HEADS_RET * HEAD_DIM
IN_COLS = 3 * WIDTH_NA + WIDTH_SWA + 2 * WIDTH_KV_SWA + 4 * WIDTH_RET + 3 * D_MODEL
OFF_NA = 0
OFF_SWA = 3 * WIDTH_NA
OFF_RET = OFF_SWA + WIDTH_SWA + 2 * WIDTH_KV_SWA
OFF_GATES = OFF_RET + 4 * WIDTH_RET
N_GROUPS = 4
EXPERTS_PER_GROUP = 4
N_EXPERTS = N_GROUPS * EXPERTS_PER_GROUP
D_EXPERT = D_MODEL // 4
ROPE_BASE = 10000.0
EPS = 1e-6
SCALE = HEAD_DIM ** -0.5
MASKED = -1e30
ROUTER_LANES = 128
ROUTE_GROUP_LANE = 0
ROUTE_EXPERT_LANE = N_GROUPS
ROUTE_COND_LANE = N_GROUPS + N_EXPERTS
HX_X1 = D_MODEL + ROUTER_LANES
HX_WIDTH = HX_X1 + D_MODEL

MIB = 1024 * 1024


def _mm(a, b):
    return jnp.dot(a.astype(BF16), b.astype(BF16), preferred_element_type=F32)


def _mm_nt(a, b):
    return lax.dot_general(a.astype(BF16), b.astype(BF16), (((1,), (1,)), ((), ())),
                           preferred_element_type=F32)


def _sigmoid(x):
    return 1.0 / (1.0 + jnp.exp(-x))


def _silu(x):
    return x * _sigmoid(x)


def _params(semantics, vmem_mib):
    return pltpu.CompilerParams(dimension_semantics=semantics, vmem_limit_bytes=vmem_mib * MIB)


def _mod_body(cond_ref, w_ref, b_ref, o_ref):
    o_ref[0] = _mm(_silu(cond_ref[...]), w_ref[0]) + b_ref[0]


def _modulation(cond, ada_w, ada_b):
    depth, d, n = ada_w.shape
    tn = 1024
    return pl.pallas_call(
        _mod_body,
        grid=(depth, n // tn),
        in_specs=[pl.BlockSpec((8, d), lambda l, j: (0, 0)),
                  pl.BlockSpec((1, d, tn), lambda l, j: (l, 0, j)),
                  pl.BlockSpec((1, 1, tn), lambda l, j: (l, 0, j))],
        out_specs=pl.BlockSpec((1, 8, tn), lambda l, j: (l, 0, j)),
        out_shape=jax.ShapeDtypeStruct((depth, 8, n), F32),
        compiler_params=_params(("arbitrary", "arbitrary"), 32),
        name="modulation",
    )(cond, ada_w, ada_b.reshape(depth, 1, n))


def _inproj_body(x_ref, mod_ref, g_ref, w_ref, *refs, n_f32):
    zf_ref = refs[0] if n_f32 else None
    zb_ref, h_scr, w_scr = refs[-3:]
    i = pl.program_id(0)
    j = pl.program_id(1)

    @pl.when(i == 0)
    def _():
        w_scr[j] = w_ref[0].astype(BF16)

    @pl.when(j == 0)
    def _():
        x = x_ref[...]
        m = mod_ref[0]
        y = x * lax.rsqrt(jnp.mean(x * x, axis=-1, keepdims=True) + EPS) * g_ref[0]
        h_scr[...] = (y * (1.0 + m[:, D_MODEL:2 * D_MODEL]) + m[:, 0:D_MODEL]).astype(BF16)

    acc = jnp.dot(h_scr[...], w_scr[j], preferred_element_type=F32)
    if n_f32 == 0:
        zb_ref[...] = acc.astype(BF16)
    else:
        @pl.when(j < n_f32)
        def _():
            zf_ref[...] = acc

        @pl.when(j >= n_f32)
        def _():
            zb_ref[...] = acc.astype(BF16)


IN_TILE = 1024


def _in_proj(x, mod_rows, norm_g, w_in, layer, n_f32):
    n, d = x.shape
    n_cond = mod_rows.shape[0]
    tm, tn = 1024, IN_TILE
    nj = IN_COLS // tn
    per_cond = n // n_cond
    out_specs = [pl.BlockSpec((tm, tn), lambda i, j: (i, jnp.maximum(j - n_f32, 0)))]
    out_shape = [jax.ShapeDtypeStruct((n, IN_COLS - n_f32 * tn), BF16)]
    if n_f32:
        out_specs.insert(0, pl.BlockSpec((tm, tn), lambda i, j: (i, jnp.minimum(j, n_f32 - 1))))
        out_shape.insert(0, jax.ShapeDtypeStruct((n, n_f32 * tn), F32))
    return pl.pallas_call(
        functools.partial(_inproj_body, n_f32=n_f32),
        grid=(n // tm, nj),
        in_specs=[pl.BlockSpec((tm, d), lambda i, j: (i, 0)),
                  pl.BlockSpec((1, 1, 6 * d), lambda i, j: (i * tm // per_cond, 0, 0)),
                  pl.BlockSpec((1, 1, d), lambda i, j: (layer, 0, 0)),
                  pl.BlockSpec((1, d, tn), lambda i, j: (layer, 0, jnp.where(i == 0, j, nj - 1)))],
        out_specs=out_specs,
        out_shape=out_shape,
        scratch_shapes=[pltpu.VMEM((tm, d), BF16), pltpu.VMEM((nj, d, tn), BF16)],
        compiler_params=_params(("arbitrary", "arbitrary"), 48),
        name="in_proj",
    )(x, mod_rows, norm_g.reshape(norm_g.shape[0], 1, d), w_in)


def _log_sigmoid(x):
    return jnp.minimum(x, 0.0) - jnp.log1p(jnp.exp(-jnp.abs(x)))


def _retention(z_ref, c0, decay_ref, ng_ref, s0_ref, o_ref, st_ref, tab_scr, d_scr, u_scr, s_scr, n_tok):
    C = RET_CHUNK
    nc = n_tok // C
    d = HEAD_DIM
    @pl.when(pl.program_id(0) == 0)
    def _():
        pos = lax.broadcasted_iota(jnp.int32, (C, d), 0).astype(F32)
        ri = lax.broadcasted_iota(jnp.int32, (C, C), 0).astype(F32)
        ci = lax.broadcasted_iota(jnp.int32, (C, C), 1).astype(F32)
        diff = ri - ci
        for h in range(N_HEADS_RET):
            lgf = _log_sigmoid(jnp.full((C, C), decay_ref[0, h], F32))
            lgb = _log_sigmoid(jnp.full((C, C), decay_ref[1, h], F32))
            d_scr[h] = (jnp.where(diff >= 0, jnp.exp(lgf * jnp.maximum(diff, 0.0)), 0.0)
                        + jnp.where(diff <= 0, jnp.exp(lgb * jnp.maximum(-diff, 0.0)), 0.0))
            lf = _log_sigmoid(jnp.full((C, d), decay_ref[0, h], F32))
            lb = _log_sigmoid(jnp.full((C, d), decay_ref[1, h], F32))
            hs = slice(h * d, (h + 1) * d)
            tab_scr[0, :, hs] = jnp.exp(lf * (C - 1.0 - pos))
            tab_scr[1, :, hs] = jnp.exp(lb * pos)
            tab_scr[2, :, hs] = jnp.exp(lf * (pos + 1.0))
            tab_scr[3, :, hs] = jnp.exp(lb * (C - pos))
            tab_scr[4, :, hs] = jnp.exp(lf * C)
            tab_scr[5, :, hs] = jnp.exp(lb * C)

    def rows(c):
        return pl.ds(pl.multiple_of(c * C, C), C)

    def part(p):
        return slice(c0 + p * WIDTH_RET, c0 + (p + 1) * WIDTH_RET)

    W = WIDTH_RET
    head_shift = d.bit_length() - 1
    assert d == 1 << head_shift
    same_head = ((lax.broadcasted_iota(jnp.int32, (W, W), 0) >> head_shift)
                 == (lax.broadcasted_iota(jnp.int32, (W, W), 1) >> head_shift))
    lane_head = lax.broadcasted_iota(jnp.int32, (C, W), 1) >> head_shift
    head_mean = jnp.where(same_head, 1.0 / d, 0.0).astype(BF16)

    def upd(c, carry):
        k = z_ref[rows(c), part(1)].astype(F32) * SCALE
        v = z_ref[rows(c), part(2)]
        u_scr[0, c] = jnp.where(same_head, _mm((k * tab_scr[0]).T, v), 0.0)
        u_scr[1, c] = jnp.where(same_head, _mm((k * tab_scr[1]).T, v), 0.0)
        return carry

    lax.fori_loop(0, nc, upd, 0)

    for direction in range(2):
        if s0_ref is None:
            s_init = jnp.zeros((W, W), F32)
        else:
            u_scr[direction, nc] = jnp.zeros((W, W), F32)
            for h in range(N_HEADS_RET):
                u_scr[direction, nc, h * d:(h + 1) * d, h * d:(h + 1) * d] = s0_ref[0, 0, direction, h]
            s_init = u_scr[direction, nc]
        chunk_decay = tab_scr[4 + direction, 0:1, :]

        def scan(t, s):
            c = t if direction == 0 else nc - 1 - t
            s_scr[direction, c] = s.astype(BF16)
            return chunk_decay * s + u_scr[direction, c]

        s_fin = lax.fori_loop(0, nc, scan, s_init)
        if st_ref is not None:
            for h in range(N_HEADS_RET):
                st_ref[0, direction, h] = s_fin[h * d:(h + 1) * d, h * d:(h + 1) * d]

    def out(c, carry):
        q = z_ref[rows(c), part(0)].astype(BF16)
        k = (z_ref[rows(c), part(1)].astype(F32) * SCALE).astype(BF16)
        v = z_ref[rows(c), part(2)].astype(BF16)
        gate = z_ref[rows(c), part(3)].astype(F32)
        o = (jnp.dot(q, s_scr[0, c], preferred_element_type=F32) * tab_scr[2]
             + jnp.dot(q, s_scr[1, c], preferred_element_type=F32) * tab_scr[3])
        intra = [_mm_nt(jnp.where(lane_head == h, q, jnp.zeros_like(q)), k) * d_scr[h]
                 for h in range(N_HEADS_RET)]
        for h in range(N_HEADS_RET):
            o = o + jnp.where(lane_head == h, _mm(intra[h], v), 0.0)
        sq = o * o
        sq_hi = sq.astype(BF16)
        sq_lo = (sq - sq_hi.astype(F32)).astype(BF16)
        ms = (jnp.dot(sq_hi, head_mean, preferred_element_type=F32)
              + jnp.dot(sq_lo, head_mean, preferred_element_type=F32))
        o = o * lax.rsqrt(ms + EPS) * ng_ref[...]
        o_ref[rows(c), :] = (o * _silu(gate)).astype(o_ref.dtype)
        return carry

    lax.fori_loop(0, nc, out, 0)


def _ret_scratch(n_tok):
    nc = n_tok // RET_CHUNK
    return [pltpu.VMEM((6, RET_CHUNK, WIDTH_RET), F32),
            pltpu.VMEM((N_HEADS_RET, RET_CHUNK, RET_CHUNK), F32),
            pltpu.VMEM((2, nc + 1, WIDTH_RET, WIDTH_RET), F32),
            pltpu.VMEM((2, nc, WIDTH_RET, WIDTH_RET), BF16)]


def _ctx_attn_body(sink_ref, decay_ref, z_ref, zr_ref, ng_ref, *refs, seq, n_prev, layer, depth):
    (ona_ref, oswa_ref, oret_ref, kna_ref, vna_ref, kswa_ref, vswa_ref, st_ref,
     tab_scr, d_scr, u_scr, s_scr) = refs[n_prev:]
    cache_refs = (kna_ref, vna_ref, kswa_ref, vswa_ref, st_ref)
    if n_prev == 0:
        for ref in cache_refs:
            for other in range(depth):
                if other != layer:
                    ref[0, other] = jnp.zeros(ref.shape[2:], ref.dtype)
        kna_ref, vna_ref, kswa_ref, vswa_ref, st_ref = (ref.at[:, pl.ds(layer, 1)] for ref in cache_refs)
    d = HEAD_DIM
    low_half = lax.broadcasted_iota(jnp.int32, (seq, 2 * d), 1) < d

    def pair_scores(q2, k2, head_is_low):
        return _mm_nt(jnp.where(low_half == head_is_low, q2, jnp.zeros_like(q2)), k2)

    jobs = []
    for pair in range(N_HEADS_NA // 2):
        cols = lambda part: slice(OFF_NA + part * WIDTH_NA + pair * 2 * d,
                                  OFF_NA + part * WIDTH_NA + (pair + 1) * 2 * d)
        q2 = (z_ref[:, cols(0)] * SCALE).astype(BF16)
        k2 = z_ref[:, cols(1)]
        v2 = z_ref[:, cols(2)]
        for hh in range(2):
            kna_ref[0, 0, 2 * pair + hh] = k2[:, hh * d:(hh + 1) * d]
            vna_ref[0, 0, 2 * pair + hh] = v2[:, hh * d:(hh + 1) * d]
        k2 = k2.astype(BF16)
        v2 = v2.astype(BF16)
        jobs.append([(pair_scores(q2, k2, hh == 0), v2, hh == 0, None) for hh in range(2)]
                    + [ona_ref, pair])
    lo = OFF_SWA + WIDTH_SWA
    k2 = z_ref[:, lo:lo + WIDTH_KV_SWA]
    v2 = z_ref[:, lo + WIDTH_KV_SWA:lo + 2 * WIDTH_KV_SWA]
    for g in range(N_KV_SWA):
        kswa_ref[0, 0, g] = k2[:, g * d:(g + 1) * d]
        vswa_ref[0, 0, g] = v2[:, g * d:(g + 1) * d]
    k2 = k2.astype(BF16)
    v2 = v2.astype(BF16)
    group = N_HEADS_SWA // N_KV_SWA
    for pair in range(N_HEADS_SWA // 2):
        q2 = (z_ref[:, OFF_SWA + pair * 2 * d:OFF_SWA + (pair + 1) * 2 * d] * SCALE).astype(BF16)
        heads = []
        for hh in range(2):
            h = 2 * pair + hh
            g = h // group
            qh = q2[:, hh * d:(hh + 1) * d]
            q_at_g = jnp.concatenate([qh, qh], axis=-1) if g != hh else q2
            heads.append((pair_scores(q_at_g, k2, g == 0), v2, g == 0, sink_ref[h]))
        jobs.append(heads + [oswa_ref, pair])
    probs = []
    for job in jobs:
        for s, v2, v_is_low, sink in job[:2]:
            m = jnp.max(s, axis=-1, keepdims=True)
            if sink is not None:
                m = jnp.maximum(m, sink)
            e = jnp.exp(s - m)
            den = jnp.sum(e, axis=-1, keepdims=True)
            if sink is not None:
                den = den + jnp.exp(sink - m)
            probs.append((e, den))
    for j, job in enumerate(jobs):
        halves = []
        for hh, (s, v2, v_is_low, sink) in enumerate(job[:2]):
            e, den = probs[2 * j + hh]
            o2 = _mm(e, v2) / den
            halves.append(o2[:, :d] if v_is_low else o2[:, d:])
        o_ref, pair = job[2], job[3]
        o_ref[:, pair * 2 * d:(pair + 1) * 2 * d] = jnp.concatenate(halves, axis=-1).astype(o_ref.dtype)
    _retention(zr_ref, 0, decay_ref, ng_ref, None, oret_ref, st_ref.at[0],
               tab_scr, d_scr, u_scr, s_scr, seq)


def _ctx_mixers(z_att, z_rest, swa_sink, ret_decay, ret_norm_g, caches, layer, depth, batch, seq):
    n = batch * seq
    assert z_att.shape[1] == OFF_RET
    smem = pl.BlockSpec(memory_space=pltpu.SMEM)
    first = caches is None
    layers_in_block = depth if first else 1
    block_layer = 0 if first else layer
    kv = lambda heads: pl.BlockSpec((1, layers_in_block, heads, seq, HEAD_DIM),
                                    lambda b: (b, block_layer, 0, 0, 0))
    kv_shape = lambda heads: jax.ShapeDtypeStruct((batch, depth, heads, seq, HEAD_DIM), F32)
    in_specs = [smem, smem,
                pl.BlockSpec((seq, OFF_RET), lambda b: (b, 0)),
                pl.BlockSpec((seq, 4 * WIDTH_RET), lambda b: (b, 0)),
                pl.BlockSpec((1, WIDTH_RET), lambda b: (0, 0))]
    args = [swa_sink, ret_decay, z_att, z_rest, ret_norm_g.reshape(1, WIDTH_RET)]
    n_prev = 0 if caches is None else len(caches)
    aliases = {len(args) + i: 3 + i for i in range(n_prev)}
    if caches is not None:
        in_specs += [pl.BlockSpec(memory_space=pl.ANY)] * n_prev
        args += list(caches)
    return pl.pallas_call(
        functools.partial(_ctx_attn_body, seq=seq, n_prev=n_prev, layer=layer, depth=depth),
        grid=(batch,),
        in_specs=in_specs,
        out_specs=[pl.BlockSpec((seq, WIDTH_NA), lambda b: (b, 0)),
                   pl.BlockSpec((seq, WIDTH_SWA), lambda b: (b, 0)),
                   pl.BlockSpec((seq, WIDTH_RET), lambda b: (b, 0)),
                   kv(N_HEADS_NA), kv(N_HEADS_NA), kv(N_KV_SWA), kv(N_KV_SWA),
                   pl.BlockSpec((1, layers_in_block, 2, N_HEADS_RET, HEAD_DIM, HEAD_DIM),
                                lambda b: (b, block_layer, 0, 0, 0, 0))],
        out_shape=[jax.ShapeDtypeStruct((n, WIDTH_NA), BF16),
                   jax.ShapeDtypeStruct((n, WIDTH_SWA), BF16),
                   jax.ShapeDtypeStruct((n, WIDTH_RET), BF16),
                   kv_shape(N_HEADS_NA), kv_shape(N_HEADS_NA), kv_shape(N_KV_SWA), kv_shape(N_KV_SWA),
                   jax.ShapeDtypeStruct((batch, depth, 2, N_HEADS_RET, HEAD_DIM, HEAD_DIM), F32)],
        scratch_shapes=_ret_scratch(seq),
        input_output_aliases=aliases,
        compiler_params=_params(("arbitrary",), 32),
        name="ctx_mixers",
    )(*args)


NA_QROWS = 4
NA_WROWS = NA_QROWS + WIN_ROWS


def _na_bias_table(rpb, rows):
    n_groups = rows // NA_QROWS
    assert rows % NA_QROWS == 0 and n_groups >= 3
    n_heads, n_ro, n_co = rpb.shape
    half = WIN_ROWS // 2
    cc = np.arange(GRID_W)
    col_start = np.clip(cc - WIN_COLS // 2, 0, GRID_W - WIN_COLS)
    valid = (cc[None, :] >= col_start[:, None]) & (cc[None, :] < col_start[:, None] + WIN_COLS)
    span = 2 * GRID_W - 1
    lead = GRID_W - WIN_COLS
    ext = jnp.pad(rpb.astype(F32), ((0, 0), (0, 0), (lead, span + 1 - lead - n_co)))
    band = jnp.tile(ext, (1, 1, GRID_W))[..., :GRID_W * span].reshape(n_heads, n_ro, GRID_W, span)
    band = jnp.where(valid[None, None], band[..., GRID_W - 1:], MASKED)
    outside = jnp.full((n_heads, GRID_W, GRID_W), MASKED, F32)
    tabs = []
    for g in (0, 1, n_groups - 1):
        ws = min(max(NA_QROWS * g - half, 0), rows - NA_WROWS)
        q_blocks = []
        for i in range(NA_QROWS):
            r = NA_QROWS * g + i
            rs = min(max(r - half, 0), rows - WIN_ROWS)
            q_blocks.append(jnp.concatenate(
                [band[:, ws + j - r + WIN_ROWS - 1] if rs <= ws + j < rs + WIN_ROWS else outside
                 for j in range(NA_WROWS)], axis=-1))
        tabs.append(jnp.concatenate(q_blocks, axis=-2))
    return jnp.stack(tabs, axis=0)


def _lat_na_body(z_ref, kc_ref, vc_ref, bias_ref, o_ref, kc_scr, vc_scr, *, rows):
    d = HEAD_DIM
    nq = NA_QROWS * GRID_W
    g = pl.program_id(1)

    @pl.when(g == 0)
    def _():
        kc_scr[...] = kc_ref[0, 0].astype(BF16)
        vc_scr[...] = vc_ref[0, 0].astype(BF16)

    ws = jnp.clip(NA_QROWS * g - WIN_ROWS // 2, 0, rows - NA_WROWS)
    qrows = pl.ds(pl.multiple_of(g * nq, nq), nq)
    wrows = pl.ds(pl.multiple_of(ws * GRID_W, GRID_W), NA_WROWS * GRID_W)
    low_half = lax.broadcasted_iota(jnp.int32, (nq, 2 * d), 1) < d
    for pair in range(N_HEADS_NA // 2):
        cols = slice(pair * 2 * d, (pair + 1) * 2 * d)
        q2 = (z_ref[qrows, cols] * SCALE).astype(BF16)
        kl2 = z_ref[wrows, WIDTH_NA + pair * 2 * d:WIDTH_NA + (pair + 1) * 2 * d].astype(BF16)
        vl2 = z_ref[wrows, 2 * WIDTH_NA + pair * 2 * d:2 * WIDTH_NA + (pair + 1) * 2 * d].astype(BF16)
        for hh in range(2):
            h = 2 * pair + hh
            qm = jnp.where(low_half == (hh == 0), q2, jnp.zeros_like(q2))
            s_loc = _mm_nt(qm, kl2) + bias_ref[0, h]
            s_ctx = _mm_nt(q2[:, hh * d:(hh + 1) * d], kc_scr[h])
            m = jnp.maximum(jnp.max(s_ctx, axis=-1, keepdims=True), jnp.max(s_loc, axis=-1, keepdims=True))
            e_ctx = jnp.exp(s_ctx - m)
            e_loc = jnp.exp(s_loc - m)
            den = jnp.sum(e_ctx, axis=-1, keepdims=True) + jnp.sum(e_loc, axis=-1, keepdims=True)
            o = (_mm(e_ctx, vc_scr[h]) + _mm(e_loc, vl2)[:, hh * d:(hh + 1) * d]) / den
            o_ref[:, h * d:(h + 1) * d] = o.astype(o_ref.dtype)


def _lat_na(z, cache_k, cache_v, bias_tab, layer, batch, seq):
    rows = seq // GRID_W
    n_groups = rows // NA_QROWS
    nq = NA_QROWS * GRID_W
    past = cache_k.shape[3]

    def group_class(g):
        return jnp.where(g == 0, 0, jnp.where(g == n_groups - 1, 2, 1))

    cache = pl.BlockSpec((1, 1, N_HEADS_NA, past, HEAD_DIM), lambda b, g: (b, layer, 0, 0, 0))
    return pl.pallas_call(
        functools.partial(_lat_na_body, rows=rows),
        grid=(batch, n_groups),
        in_specs=[pl.BlockSpec((seq, 3 * WIDTH_NA), lambda b, g: (b, 0)),
                  cache, cache,
                  pl.BlockSpec((1, N_HEADS_NA, nq, NA_WROWS * GRID_W),
                               lambda b, g: (group_class(g), 0, 0, 0))],
        out_specs=pl.BlockSpec((nq, WIDTH_NA), lambda b, g: (b * n_groups + g, 0)),
        out_shape=jax.ShapeDtypeStruct((batch * seq, WIDTH_NA), BF16),
        scratch_shapes=[pltpu.VMEM((N_HEADS_NA, past, HEAD_DIM), BF16),
                        pltpu.VMEM((N_HEADS_NA, past, HEAD_DIM), BF16)],
        compiler_params=_params(("arbitrary", "arbitrary"), 58),
        name="lat_na",
    )(z, cache_k, cache_v, bias_tab)


def _rope_tables(seq):
    t = jnp.arange(seq)
    n_freq = HEAD_DIM // 4
    inv = ROPE_BASE ** (-jnp.arange(n_freq, dtype=F32) / n_freq)
    ang = jnp.concatenate([(t // GRID_W).astype(F32)[:, None] * inv,
                           (t % GRID_W).astype(F32)[:, None] * inv], axis=-1)
    cos = jnp.repeat(jnp.cos(ang), 2, axis=-1)
    sign = jnp.tile(jnp.asarray([-1.0, 1.0], F32), HEAD_DIM // 2)
    sin = jnp.repeat(jnp.sin(ang), 2, axis=-1) * sign
    return jnp.tile(cos, (1, N_HEADS_SWA)), jnp.tile(sin, (1, N_HEADS_SWA))


def _rope(x, cos, sin_signed):
    width = x.shape[-1]
    lane = lax.broadcasted_iota(jnp.int32, x.shape, x.ndim - 1)
    partner = jnp.where((lane & 1) == 0, pltpu.roll(x, width - 1, x.ndim - 1), pltpu.roll(x, 1, x.ndim - 1))
    return x * cos + partner * sin_signed


def _lat_swa_body(sink_ref, z_ref, kc_ref, vc_ref, cos_ref, sin_ref, o_ref, kr_scr, kc_scr, vc_scr, *, seq):
    d = HEAD_DIM
    blk = SWA_BLOCK
    n = pl.program_id(1)
    assert N_KV_SWA == 2

    @pl.when(n == 0)
    def _():
        k = z_ref[:, WIDTH_SWA:WIDTH_SWA + WIDTH_KV_SWA].astype(F32)
        kr_scr[...] = _rope(k, cos_ref[:, :WIDTH_KV_SWA], sin_ref[:, :WIDTH_KV_SWA]).astype(BF16)
        kc_scr[...] = jnp.concatenate([kc_ref[0, 0, 0], kc_ref[0, 0, 1]], axis=-1).astype(BF16)
        vc_scr[...] = jnp.concatenate([vc_ref[0, 0, 0], vc_ref[0, 0, 1]], axis=-1).astype(BF16)

    qrows = pl.ds(pl.multiple_of(n * blk, blk), blk)
    kstart = jnp.clip((n - 1) * blk, 0, seq - 3 * blk)
    krows = pl.ds(pl.multiple_of(kstart, blk), 3 * blk)
    q_all = _rope(z_ref[qrows, 0:WIDTH_SWA].astype(F32), cos_ref[qrows, :], sin_ref[qrows, :]) * SCALE
    q_all = q_all.astype(BF16)
    qpos = n * blk + lax.broadcasted_iota(jnp.int32, (blk, 3 * blk), 0)
    kpos = kstart + lax.broadcasted_iota(jnp.int32, (blk, 3 * blk), 1)
    valid = jnp.abs(qpos - kpos) <= SWA_RADIUS
    kl2 = kr_scr[krows, :]
    vl2 = z_ref[krows, WIDTH_SWA + WIDTH_KV_SWA:WIDTH_SWA + 2 * WIDTH_KV_SWA]
    group = N_HEADS_SWA // N_KV_SWA
    zero = jnp.zeros((blk, d), BF16)
    scores = []
    for h in range(N_HEADS_SWA):
        g = h // group
        qh = q_all[:, h * d:(h + 1) * d]
        q_at_g = jnp.concatenate([qh, zero] if g == 0 else [zero, qh], axis=-1)
        scores.append((jnp.where(valid, _mm_nt(q_at_g, kl2), MASKED), _mm_nt(q_at_g, kc_scr[...])))
    probs = []
    for h, (s_loc, s_ctx) in enumerate(scores):
        m = jnp.maximum(jnp.maximum(jnp.max(s_loc, axis=-1, keepdims=True),
                                    jnp.max(s_ctx, axis=-1, keepdims=True)), sink_ref[h])
        e_loc = jnp.exp(s_loc - m)
        e_ctx = jnp.exp(s_ctx - m)
        den = (jnp.sum(e_loc, axis=-1, keepdims=True) + jnp.sum(e_ctx, axis=-1, keepdims=True)
               + jnp.exp(sink_ref[h] - m))
        probs.append((e_loc, e_ctx, den))
    outs = []
    for h, (e_loc, e_ctx, den) in enumerate(probs):
        g = h // group
        o2 = (_mm(e_ctx, vc_scr[...]) + _mm(e_loc, vl2)) / den
        outs.append(o2[:, g * d:(g + 1) * d])
    o_ref[...] = jnp.concatenate(outs, axis=-1).astype(o_ref.dtype)


def _lat_swa(z, cache_k, cache_v, swa_sink, cos, sin, layer, batch, seq):
    nb = seq // SWA_BLOCK
    past = cache_k.shape[3]
    width = WIDTH_SWA + 2 * WIDTH_KV_SWA
    assert OFF_SWA % width == 0 and seq >= 3 * SWA_BLOCK
    cache = pl.BlockSpec((1, 1, N_KV_SWA, past, HEAD_DIM), lambda b, n: (b, layer, 0, 0, 0))
    table = pl.BlockSpec((seq, WIDTH_SWA), lambda b, n: (0, 0))
    return pl.pallas_call(
        functools.partial(_lat_swa_body, seq=seq),
        grid=(batch, nb),
        in_specs=[pl.BlockSpec(memory_space=pltpu.SMEM),
                  pl.BlockSpec((seq, width), lambda b, n: (b, OFF_SWA // width)),
                  cache, cache, table, table],
        out_specs=pl.BlockSpec((SWA_BLOCK, WIDTH_SWA), lambda b, n: (b * nb + n, 0)),
        out_shape=jax.ShapeDtypeStruct((batch * seq, WIDTH_SWA), BF16),
        scratch_shapes=[pltpu.VMEM((seq, WIDTH_KV_SWA), BF16),
                        pltpu.VMEM((past, WIDTH_KV_SWA), BF16), pltpu.VMEM((past, WIDTH_KV_SWA), BF16)],
        compiler_params=_params(("arbitrary", "arbitrary"), 32),
        name="lat_swa",
    )(swa_sink, z, cache_k, cache_v, cos, sin)


def _lat_ret_body(decay_ref, z_ref, ng_ref, s0_ref, o_ref, tab_scr, d_scr, u_scr, s_scr, *, seq):
    _retention(z_ref, 0, decay_ref, ng_ref, s0_ref, o_ref, None, tab_scr, d_scr, u_scr, s_scr, seq)


def _lat_ret(z, ret_decay, ret_norm_g, state, layer, batch, seq):
    width = 4 * WIDTH_RET
    assert OFF_RET % width == 0
    return pl.pallas_call(
        functools.partial(_lat_ret_body, seq=seq),
        grid=(batch,),
        in_specs=[pl.BlockSpec(memory_space=pltpu.SMEM),
                  pl.BlockSpec((seq, width), lambda b: (b, OFF_RET // width)),
                  pl.BlockSpec((1, WIDTH_RET), lambda b: (0, 0)),
                  pl.BlockSpec((1, 1, 2, N_HEADS_RET, HEAD_DIM, HEAD_DIM),
                               lambda b: (b, layer, 0, 0, 0, 0))],
        out_specs=pl.BlockSpec((seq, WIDTH_RET), lambda b: (b, 0)),
        out_shape=jax.ShapeDtypeStruct((batch * seq, WIDTH_RET), BF16),
        scratch_shapes=_ret_scratch(seq),
        compiler_params=_params(("arbitrary",), 40),
        name="lat_ret",
    )(ret_decay, z, ret_norm_g.reshape(1, WIDTH_RET), state)


def _merge_body(ona_ref, oswa_ref, oret_ref, ga_ref, gs_ref, gr_ref, x_ref, mod_ref, g2_ref,
                wna_ref, wswa_ref, wret_ref, wout_ref, rw_ref, rb_ref,
                hx_ref, wna_scr, wswa_scr, wret_scr, wout_scr, rw_scr, *, per_cond):
    @pl.when(pl.program_id(0) == 0)
    def _():
        wna_scr[...] = wna_ref[0].astype(BF16)
        wswa_scr[...] = wswa_ref[0].astype(BF16)
        wret_scr[...] = wret_ref[0].astype(BF16)
        wout_scr[...] = wout_ref[0].astype(BF16)
        rw = rw_ref[0]
        rw_hi = rw.astype(BF16)
        rw_scr[0] = rw_hi
        rw_scr[1] = (rw - rw_hi.astype(F32)).astype(BF16)

    D = D_MODEL
    m = mod_ref[0]
    sub = x_ref.shape[0] // MERGE_SPLIT
    parts = [pl.ds(s * sub, sub) for s in range(MERGE_SPLIT)]
    gate = lambda ref, rows: _sigmoid(ref[rows, :].astype(F32))
    zs = [(gate(ga_ref, r) * jnp.dot(ona_ref[r, :], wna_scr[...], preferred_element_type=F32)
           + gate(gs_ref, r) * jnp.dot(oswa_ref[r, :], wswa_scr[...], preferred_element_type=F32)
           + gate(gr_ref, r) * jnp.dot(oret_ref[r, :], wret_scr[...], preferred_element_type=F32)
           ).astype(BF16) for r in parts]
    ys = [jnp.dot(z, wout_scr[...], preferred_element_type=F32) for z in zs]
    logits = []
    for r, y in zip(parts, ys):
        x1 = x_ref[r, :] + m[:, 2 * D:3 * D] * y
        hx_ref[r, HX_X1:HX_X1 + D] = x1
        h2 = x1 * lax.rsqrt(jnp.mean(x1 * x1, axis=-1, keepdims=True) + EPS) * g2_ref[0]
        h2 = h2 * (1.0 + m[:, 4 * D:5 * D]) + m[:, 3 * D:4 * D]
        hx_ref[r, 0:D] = h2
        h_hi = h2.astype(BF16)
        h_lo = (h2 - h_hi.astype(F32)).astype(BF16)
        logits.append((jnp.dot(h_hi, rw_scr[0], preferred_element_type=F32)
                       + jnp.dot(h_hi, rw_scr[1], preferred_element_type=F32)
                       + jnp.dot(h_lo, rw_scr[0], preferred_element_type=F32)) + rb_ref[0])
    for r, logit in zip(parts, logits):
        route = _route(logit)
        cond = (pl.program_id(0) * x_ref.shape[0] // per_cond).astype(F32)
        lane = lax.broadcasted_iota(jnp.int32, route.shape, 1)
        hx_ref[r, D:D + ROUTER_LANES] = jnp.where(lane == ROUTE_COND_LANE, cond, route)


MERGE_SPLIT = 2


def _route(logit):
    lane = lax.broadcasted_iota(jnp.int32, logit.shape, 1).astype(F32)
    neg = jnp.float32(-jnp.inf)
    big = jnp.float32(ROUTER_LANES)
    is_group = lane < N_GROUPS
    is_expert = (lane >= N_GROUPS) & (lane < N_GROUPS + N_EXPERTS)
    gl = jnp.where(is_group, logit, neg)
    gmax = jnp.max(gl, axis=-1, keepdims=True)
    gsel = jnp.min(jnp.where(gl == gmax, lane, big), axis=-1, keepdims=True)
    p_group = 1.0 / jnp.sum(jnp.where(is_group, jnp.exp(gl - gmax), 0.0), axis=-1, keepdims=True)
    eidx = lane - N_GROUPS
    in_group = is_expert & (jnp.floor(eidx / EXPERTS_PER_GROUP) == gsel)
    el = jnp.where(in_group, logit, neg)
    top1 = jnp.max(el, axis=-1, keepdims=True)
    sel1 = jnp.min(jnp.where(el == top1, lane, big), axis=-1, keepdims=True)
    el2 = jnp.where(lane == sel1, neg, el)
    top2 = jnp.max(el2, axis=-1, keepdims=True)
    sel2 = jnp.min(jnp.where(el2 == top2, lane, big), axis=-1, keepdims=True)
    e2 = jnp.exp(top2 - top1)
    w1 = p_group / (1.0 + e2)
    w2 = p_group * e2 / (1.0 + e2)
    comb = jnp.where(lane == sel1, w1, 0.0) + jnp.where(lane == sel2, w2, 0.0)
    return comb + jnp.where(lane == ROUTE_GROUP_LANE, gsel, 0.0)


def _merge(o_na, o_swa, o_ret, z, gate_tile, x, mod_rows, norm2_g, w_br_na, w_br_swa, w_br_ret, w_out,
           router_w, router_b, layer):
    n, d = x.shape
    n_cond = mod_rows.shape[0]
    per_cond = n // n_cond
    tm = 512
    gate = lambda k: pl.BlockSpec((tm, d), lambda i: (i, gate_tile + k))
    row = lambda w: pl.BlockSpec((tm, w), lambda i: (i, 0))
    weight = lambda k: pl.BlockSpec((1, k, d), lambda i: (layer, 0, 0))
    return pl.pallas_call(
        functools.partial(_merge_body, per_cond=per_cond),
        grid=(n // tm,),
        in_specs=[row(WIDTH_NA), row(WIDTH_SWA), row(WIDTH_RET),
                  gate(0), gate(1), gate(2),
                  row(d),
                  pl.BlockSpec((1, 1, 6 * d), lambda i: (i * tm // per_cond, 0, 0)),
                  pl.BlockSpec((1, 1, d), lambda i: (layer, 0, 0)),
                  weight(WIDTH_NA), weight(WIDTH_SWA), weight(WIDTH_RET), weight(d),
                  pl.BlockSpec((1, d, ROUTER_LANES), lambda i: (layer, 0, 0)),
                  pl.BlockSpec((1, 1, ROUTER_LANES), lambda i: (layer, 0, 0))],
        out_specs=row(HX_WIDTH),
        out_shape=jax.ShapeDtypeStruct((n, HX_WIDTH), F32),
        scratch_shapes=[pltpu.VMEM((WIDTH_NA, d), BF16), pltpu.VMEM((WIDTH_SWA, d), BF16),
                        pltpu.VMEM((WIDTH_RET, d), BF16), pltpu.VMEM((d, d), BF16),
                        pltpu.VMEM((2, d, ROUTER_LANES), BF16)],
        compiler_params=_params(("arbitrary",), 56),
        name="merge_router",
    )(o_na, o_swa, o_ret, z, z, z, x, mod_rows, norm2_g.reshape(norm2_g.shape[0], 1, d),
      w_br_na, w_br_swa, w_br_ret, w_out, router_w, router_b)


MOE_TILE = 512


def _route_plan(hx, n_tiles):
    n = hx.shape[0]
    g = hx[:, D_MODEL + ROUTE_GROUP_LANE].astype(jnp.int32)
    onehot = (g[:, None] == jnp.arange(N_GROUPS, dtype=jnp.int32)[None, :]).astype(jnp.int32)
    csum = jnp.cumsum(onehot, axis=0)
    counts = csum[-1]
    ends = jnp.cumsum(counts)
    dest = jnp.sum(onehot * ((ends - counts)[None, :] + csum - 1), axis=1)
    first = jnp.arange(n_tiles, dtype=jnp.int32) * MOE_TILE
    tile_lo = jnp.sum((first[:, None] >= ends[None, :]).astype(jnp.int32), axis=1)
    tile_hi = jnp.sum(((first + MOE_TILE - 1)[:, None] >= ends[None, :]).astype(jnp.int32), axis=1)
    return dest, tile_lo, tile_hi


def _moe_body(dest_ref, lo_ref, hi_ref, hx_hbm, wg_hbm, wu_hbm, wd_hbm, mod_ref, fg_ref, out_hbm,
              order_scr, hbuf, obuf, y_scr, wg_scr, wu_scr, wd_scr, in_sem, out_sem, w_sem,
              *, layer, n_tiles, n_tok, n_cond, final):
    step = pl.program_id(0)
    last_step = pl.num_programs(0) - 1
    rows_per_expert = MOE_TILE // EXPERTS_PER_GROUP

    def in_copy(tile, i, slot):
        return pltpu.make_async_copy(hx_hbm.at[pl.ds(order_scr[tile * MOE_TILE + i], 1)],
                                     hbuf.at[slot, pl.ds(i, 1)], in_sem.at[slot])

    def out_copy(tile, i, slot):
        return pltpu.make_async_copy(obuf.at[slot, pl.ds(i, 1)],
                                     out_hbm.at[pl.ds(order_scr[tile * MOE_TILE + i], 1)], out_sem.at[slot])

    def for_rows(fn):
        def body(i, carry):
            fn(i)
            return carry
        lax.fori_loop(0, MOE_TILE, body, 0, unroll=8)

    @pl.when(step == 0)
    def _():
        copies = [pltpu.make_async_copy(src.at[layer], dst, w_sem.at[k])
                  for k, (src, dst) in enumerate(((wg_hbm, wg_scr), (wu_hbm, wu_scr), (wd_hbm, wd_scr)))]
        for copy in copies:
            copy.start()

        def invert(i, carry):
            order_scr[dest_ref[i]] = i
            return carry
        lax.fori_loop(0, n_tok, invert, 0, unroll=8)
        for_rows(lambda i: in_copy(0, i, 0).start())
        for copy in copies:
            copy.wait()

    def run_tile(tile, slot, nxt, write_back):
        other = 1 - slot
        for_rows(lambda i: in_copy(tile, i, slot).wait())
        hx = hbuf[slot]
        h = hx[:, 0:D_MODEL].astype(BF16)
        route = hx[:, D_MODEL:D_MODEL + ROUTER_LANES]
        lane = lax.broadcasted_iota(jnp.int32, route.shape, 1)

        def group(k, copy_back):
            first = (lo_ref[tile] + k) * EXPERTS_PER_GROUP
            y = None
            for j in range(EXPERTS_PER_GROUP):
                e = first + j
                hg = jnp.dot(h, wg_scr[e], preferred_element_type=F32)
                hu = jnp.dot(h, wu_scr[e], preferred_element_type=F32)
                c = jnp.sum(jnp.where(lane == ROUTE_EXPERT_LANE + e, route, 0.0), axis=-1, keepdims=True)
                a = _silu(hg) * hu * c
                part = jnp.dot(a.astype(BF16), wd_scr[e], preferred_element_type=F32)
                y = part if y is None else y + part
                if k == 0:
                    for i in range(j * rows_per_expert, (j + 1) * rows_per_expert):
                        in_copy(nxt, i, other).start()
                        if copy_back:
                            out_copy(tile - 1, i, other).start()
            if k == 0:
                y_scr[...] = y
            else:
                y_scr[...] += y

        if write_back is True:
            group(0, True)
        else:
            pl.when(write_back)(functools.partial(group, 0, True))
            pl.when(jnp.logical_not(write_back))(functools.partial(group, 0, False))
        for k in range(1, N_GROUPS):
            pl.when(lo_ref[tile] + k <= hi_ref[tile])(functools.partial(group, k, False))

        cond = route[:, ROUTE_COND_LANE:ROUTE_COND_LANE + 1]
        gate2 = jnp.zeros((1, D_MODEL), F32)
        for c in range(n_cond):
            gate2 = jnp.where(cond == c, mod_ref[c][:, 5 * D_MODEL:6 * D_MODEL], gate2)
        x2 = hx[:, HX_X1:HX_X1 + D_MODEL] + gate2 * y_scr[...]
        if final:
            x2 = x2 * lax.rsqrt(jnp.mean(x2 * x2, axis=-1, keepdims=True) + EPS) * fg_ref[...]
        obuf[slot] = x2
        if write_back is True:
            for_rows(lambda i: out_copy(tile - 1, i, other).wait())
        else:
            @pl.when(write_back)
            def _():
                for_rows(lambda i: out_copy(tile - 1, i, other).wait())

    first_tile = 2 * step
    second_tile = first_tile + 1
    after = jnp.minimum(second_tile + 1, n_tiles - 1)
    run_tile(first_tile, 0, second_tile, step > 0)
    run_tile(second_tile, 1, after, True)

    @pl.when(step == last_step)
    def _():
        for_rows(lambda i: in_copy(after, i, 0).wait())
        for_rows(lambda i: out_copy(second_tile, i, 1).start())
        for_rows(lambda i: out_copy(second_tile, i, 1).wait())


def _moe(hx, mod_rows, w_gate, w_up, w_down, final_g, layer, final):
    n = hx.shape[0]
    d = D_MODEL
    n_tiles = n // MOE_TILE
    assert n_tiles % 2 == 0
    n_cond = mod_rows.shape[0]
    dest, tile_lo, tile_hi = _route_plan(hx, n_tiles)
    any_space = pl.BlockSpec(memory_space=pl.ANY)
    return pl.pallas_call(
        functools.partial(_moe_body, layer=layer, n_tiles=n_tiles, n_tok=n, n_cond=n_cond, final=final),
        grid_spec=pltpu.PrefetchScalarGridSpec(
            num_scalar_prefetch=3,
            grid=(n_tiles // 2,),
            in_specs=[any_space, any_space, any_space, any_space,
                      pl.BlockSpec((n_cond, 1, 6 * d), lambda t, *_: (0, 0, 0)),
                      pl.BlockSpec((1, d), lambda t, *_: (0, 0))],
            out_specs=any_space,
            scratch_shapes=[pltpu.SMEM((n,), jnp.int32),
                            pltpu.VMEM((2, MOE_TILE, HX_WIDTH), F32),
                            pltpu.VMEM((2, MOE_TILE, d), F32),
                            pltpu.VMEM((MOE_TILE, d), F32),
                            pltpu.VMEM((N_EXPERTS, d, D_EXPERT), BF16),
                            pltpu.VMEM((N_EXPERTS, d, D_EXPERT), BF16),
                            pltpu.VMEM((N_EXPERTS, D_EXPERT, d), BF16),
                            pltpu.SemaphoreType.DMA((2,)),
                            pltpu.SemaphoreType.DMA((2,)),
                            pltpu.SemaphoreType.DMA((3,))]),
        out_shape=jax.ShapeDtypeStruct((n, d), F32),
        compiler_params=_params(("arbitrary",), 56),
        name="moe",
    )(dest, tile_lo, tile_hi, hx, w_gate, w_up, w_down, mod_rows, final_g.reshape(1, d))


def kernel(x_prompt, x_sample, c, cache_na_k, cache_na_v, cache_swa_k, cache_swa_v, state_ret, c_ctx, ada_w, ada_b, norm1_g, norm2_g, w_in, na_rpb, swa_sink, ret_decay, ret_norm_g, w_br_na, w_br_swa, w_br_ret, w_out, router_group_w, router_group_b, router_expert_w, router_expert_b, w_gate, w_up, w_down, final_norm_g):
    batch, seq, d = x_prompt.shape
    dec_batch, dec_seq, _ = x_sample.shape
    depth = ada_w.shape[0]
    assert d == D_MODEL and dec_batch + 1 <= 8
    assert OFF_RET % IN_TILE == 0 and OFF_GATES % d == 0 and (OFF_GATES - OFF_RET) % d == 0

    cond = jnp.concatenate([c_ctx[None], c, jnp.zeros((8 - 1 - dec_batch, d), F32)], axis=0)
    mod = _modulation(cond, ada_w, ada_b)
    router_w = jnp.concatenate([router_group_w, router_expert_w], axis=-1)
    router_w = jnp.pad(router_w, ((0, 0), (0, 0), (0, ROUTER_LANES - router_w.shape[-1])))
    router_b = jnp.concatenate([router_group_b, router_expert_b], axis=-1)
    router_b = jnp.pad(router_b, ((0, 0), (0, ROUTER_LANES - router_b.shape[-1])))[:, None, :]
    cos, sin = _rope_tables(dec_seq)
    moe_weights = (w_gate.astype(BF16), w_up.astype(BF16), w_down.astype(BF16))

    xp = x_prompt.reshape(batch * seq, d)
    xs = x_sample.reshape(dec_batch * dec_seq, d)
    caches = None
    for l in range(depth):
        final = l == depth - 1
        mod_ctx = mod[l, 0:1][:, None, :]
        mod_lat = mod[l, 1:1 + dec_batch][:, None, :]
        merge_w = (norm2_g, w_br_na, w_br_swa, w_br_ret, w_out, router_w, router_b, l)
        moe_w = (*moe_weights, final_norm_g, l, final)

        z_att, z_rest = _in_proj(xp, mod_ctx, norm1_g, w_in, l, OFF_RET // IN_TILE)
        o_na, o_swa, o_ret, *caches = _ctx_mixers(
            z_att, z_rest, swa_sink[l], ret_decay[l], ret_norm_g[l], caches, l, depth, batch, seq)
        hx = _merge(o_na, o_swa, o_ret, z_rest, (OFF_GATES - OFF_RET) // d, xp, mod_ctx, *merge_w)
        xp = _moe(hx, mod_ctx, *moe_w)

        z, = _in_proj(xs, mod_lat, norm1_g, w_in, l, 0)
        o_na = _lat_na(z, cache_na_k, cache_na_v, _na_bias_table(na_rpb[l], dec_seq // GRID_W),
                       l, dec_batch, dec_seq)
        o_swa = _lat_swa(z, cache_swa_k, cache_swa_v, swa_sink[l], cos, sin, l, dec_batch, dec_seq)
        o_ret = _lat_ret(z, ret_decay[l], ret_norm_g[l], state_ret, l, dec_batch, dec_seq)
        hx = _merge(o_na, o_swa, o_ret, z, OFF_GATES // d, xs, mod_lat, *merge_w)
        xs = _moe(hx, mod_lat, *moe_w)

    return (xp.reshape(batch, seq, d), xs.reshape(dec_batch, dec_seq, d), *caches)
```

```python
import functools

import numpy as np
import jax
import jax.numpy as jnp
from jax import lax
from jax.experimental import pallas as pl
from jax.experimental.pallas import tpu as pltpu

F32 = jnp.float32
BF16 = jnp.bfloat16

D_MODEL = 1024
HEAD_DIM = 64
GRID_W = 64
N_HEADS_NA = 8
WIN_ROWS = 8
WIN_COLS = 16
N_HEADS_SWA = 4
N_KV_SWA = 2
SWA_RADIUS = 128
SWA_BLOCK = 128
N_HEADS_RET = 4
RET_CHUNK = 128
WIDTH_NA = N_HEADS_NA * HEAD_DIM
WIDTH_SWA = N_HEADS_SWA * HEAD_DIM
WIDTH_KV_SWA = N_KV_SWA * HEAD_DIM
WIDTH_RET = N_HEADS_RET * HEAD_DIM
IN_COLS = 3 * WIDTH_NA + WIDTH_SWA + 2 * WIDTH_KV_SWA + 4 * WIDTH_RET + 3 * D_MODEL
OFF_NA = 0
OFF_SWA = 3 * WIDTH_NA
OFF_RET = OFF_SWA + WIDTH_SWA + 2 * WIDTH_KV_SWA
OFF_GATES = OFF_RET + 4 * WIDTH_RET
N_GROUPS = 4
EXPERTS_PER_GROUP = 4
N_EXPERTS = N_GROUPS * EXPERTS_PER_GROUP
D_EXPERT = D_MODEL // 4
ROPE_BASE = 10000.0
EPS = 1e-6
SCALE = HEAD_DIM ** -0.5
MASKED = -1e30
ROUTER_LANES = 128
ROUTE_GROUP_LANE = 0
ROUTE_EXPERT_LANE = N_GROUPS
ROUTE_COND_LANE = N_GROUPS + N_EXPERTS
HX_X1 = D_MODEL + ROUTER_LANES
HX_WIDTH = HX_X1 + D_MODEL

MIB = 1024 * 1024


def _mm(a, b):
    return jnp.dot(a.astype(BF16), b.astype(BF16), preferred_element_type=F32)


def _mm_nt(a, b):
    return lax.dot_general(a.astype(BF16), b.astype(BF16), (((1,), (1,)), ((), ())),
                           preferred_element_type=F32)


def _sigmoid(x):
    return 1.0 / (1.0 + jnp.exp(-x))


def _silu(x):
    return x * _sigmoid(x)


def _params(semantics, vmem_mib):
    return pltpu.CompilerParams(dimension_semantics=semantics, vmem_limit_bytes=vmem_mib * MIB)


def _mod_body(cond_ref, w_ref, b_ref, o_ref):
    o_ref[0] = _mm(_silu(cond_ref[...]), w_ref[0]) + b_ref[0]


def _modulation(cond, ada_w, ada_b):
    depth, d, n = ada_w.shape
    tn = 1024
    return pl.pallas_call(
        _mod_body,
        grid=(depth, n // tn),
        in_specs=[pl.BlockSpec((8, d), lambda l, j: (0, 0)),
                  pl.BlockSpec((1, d, tn), lambda l, j: (l, 0, j)),
                  pl.BlockSpec((1, 1, tn), lambda l, j: (l, 0, j))],
        out_specs=pl.BlockSpec((1, 8, tn), lambda l, j: (l, 0, j)),
        out_shape=jax.ShapeDtypeStruct((depth, 8, n), F32),
        compiler_params=_params(("arbitrary", "arbitrary"), 32),
        name="modulation",
    )(cond, ada_w, ada_b.reshape(depth, 1, n))


def _inproj_body(x_ref, mod_ref, g_ref, w_ref, *refs, n_f32):
    zf_ref = refs[0] if n_f32 else None
    zb_ref, h_scr, w_scr = refs[-3:]
    i = pl.program_id(0)
    j = pl.program_id(1)

    @pl.when(i == 0)
    def _():
        w_scr[j] = w_ref[0].astype(BF16)

    @pl.when(j == 0)
    def _():
        x = x_ref[...]
        m = mod_ref[0]
        y = x * lax.rsqrt(jnp.mean(x * x, axis=-1, keepdims=True) + EPS) * g_ref[0]
        h_scr[...] = (y * (1.0 + m[:, D_MODEL:2 * D_MODEL]) + m[:, 0:D_MODEL]).astype(BF16)

    acc = jnp.dot(h_scr[...], w_scr[j], preferred_element_type=F32)
    if n_f32 == 0:
        zb_ref[...] = acc.astype(BF16)
    else:
        @pl.when(j < n_f32)
        def _():
            zf_ref[...] = acc

        @pl.when(j >= n_f32)
        def _():
            zb_ref[...] = acc.astype(BF16)


IN_TILE = 1024


def _in_proj(x, mod_rows, norm_g, w_in, layer, n_f32):
    n, d = x.shape
    n_cond = mod_rows.shape[0]
    tm, tn = 1024, IN_TILE
    nj = IN_COLS // tn
    per_cond = n // n_cond
    out_specs = [pl.BlockSpec((tm, tn), lambda i, j: (i, jnp.maximum(j - n_f32, 0)))]
    out_shape = [jax.ShapeDtypeStruct((n, IN_COLS - n_f32 * tn), BF16)]
    if n_f32:
        out_specs.insert(0, pl.BlockSpec((tm, tn), lambda i, j: (i, jnp.minimum(j, n_f32 - 1))))
        out_shape.insert(0, jax.ShapeDtypeStruct((n, n_f32 * tn), F32))
    return pl.pallas_call(
        functools.partial(_inproj_body, n_f32=n_f32),
        grid=(n // tm, nj),
        in_specs=[pl.BlockSpec((tm, d), lambda i, j: (i, 0)),
                  pl.BlockSpec((1, 1, 6 * d), lambda i, j: (i * tm // per_cond, 0, 0)),
                  pl.BlockSpec((1, 1, d), lambda i, j: (layer, 0, 0)),
                  pl.BlockSpec((1, d, tn), lambda i, j: (layer, 0, jnp.where(i == 0, j, nj - 1)))],
        out_specs=out_specs,
        out_shape=out_shape,
        scratch_shapes=[pltpu.VMEM((tm, d), BF16), pltpu.VMEM((nj, d, tn), BF16)],
        compiler_params=_params(("arbitrary", "arbitrary"), 48),
        name="in_proj",
    )(x, mod_rows, norm_g.reshape(norm_g.shape[0], 1, d), w_in)


def _log_sigmoid(x):
    return jnp.minimum(x, 0.0) - jnp.log1p(jnp.exp(-jnp.abs(x)))


def _retention(z_ref, c0, decay_ref, ng_ref, s0_ref, o_ref, st_ref, tab_scr, d_scr, u_scr, s_scr, n_tok):
    C = RET_CHUNK
    nc = n_tok // C
    d = HEAD_DIM
    @pl.when(pl.program_id(0) == 0)
    def _():
        pos = lax.broadcasted_iota(jnp.int32, (C, d), 0).astype(F32)
        ri = lax.broadcasted_iota(jnp.int32, (C, C), 0).astype(F32)
        ci = lax.broadcasted_iota(jnp.int32, (C, C), 1).astype(F32)
        diff = ri - ci
        for h in range(N_HEADS_RET):
            lgf = _log_sigmoid(jnp.full((C, C), decay_ref[0, h], F32))
            lgb = _log_sigmoid(jnp.full((C, C), decay_ref[1, h], F32))
            d_scr[h] = (jnp.where(diff >= 0, jnp.exp(lgf * jnp.maximum(diff, 0.0)), 0.0)
                        + jnp.where(diff <= 0, jnp.exp(lgb * jnp.maximum(-diff, 0.0)), 0.0))
            lf = _log_sigmoid(jnp.full((C, d), decay_ref[0, h], F32))
            lb = _log_sigmoid(jnp.full((C, d), decay_ref[1, h], F32))
            hs = slice(h * d, (h + 1) * d)
            tab_scr[0, :, hs] = jnp.exp(lf * (C - 1.0 - pos))
            tab_scr[1, :, hs] = jnp.exp(lb * pos)
            tab_scr[2, :, hs] = jnp.exp(lf * (pos + 1.0))
            tab_scr[3, :, hs] = jnp.exp(lb * (C - pos))
            tab_scr[4, :, hs] = jnp.exp(lf * C)
            tab_scr[5, :, hs] = jnp.exp(lb * C)

    def rows(c):
        return pl.ds(pl.multiple_of(c * C, C), C)

    def part(p):
        return slice(c0 + p * WIDTH_RET, c0 + (p + 1) * WIDTH_RET)

    W = WIDTH_RET
    head_shift = d.bit_length() - 1
    assert d == 1 << head_shift
    same_head = ((lax.broadcasted_iota(jnp.int32, (W, W), 0) >> head_shift)
                 == (lax.broadcasted_iota(jnp.int32, (W, W), 1) >> head_shift))
    lane_head = lax.broadcasted_iota(jnp.int32, (C, W), 1) >> head_shift
    head_mean = jnp.where(same_head, 1.0 / d, 0.0).astype(BF16)

    def upd(c, carry):
        k = z_ref[rows(c), part(1)].astype(F32) * SCALE
        v = z_ref[rows(c), part(2)]
        u_scr[0, c] = jnp.where(same_head, _mm((k * tab_scr[0]).T, v), 0.0)
        u_scr[1, c] = jnp.where(same_head, _mm((k * tab_scr[1]).T, v), 0.0)
        return carry

    lax.fori_loop(0, nc, upd, 0)

    for direction in range(2):
        if s0_ref is None:
            s_init = jnp.zeros((W, W), F32)
        else:
            u_scr[direction, nc] = jnp.zeros((W, W), F32)
            for h in range(N_HEADS_RET):
                u_scr[direction, nc, h * d:(h + 1) * d, h * d:(h + 1) * d] = s0_ref[0, 0, direction, h]
            s_init = u_scr[direction, nc]
        chunk_decay = tab_scr[4 + direction, 0:1, :]

        def scan(t, s):
            c = t if direction == 0 else nc - 1 - t
            s_scr[direction, c] = s.astype(BF16)
            return chunk_decay * s + u_scr[direction, c]

        s_fin = lax.fori_loop(0, nc, scan, s_init)
        if st_ref is not None:
            for h in range(N_HEADS_RET):
                st_ref[0, direction, h] = s_fin[h * d:(h + 1) * d, h * d:(h + 1) * d]

    def out(c, carry):
        q = z_ref[rows(c), part(0)].astype(BF16)
        k = (z_ref[rows(c), part(1)].astype(F32) * SCALE).astype(BF16)
        v = z_ref[rows(c), part(2)].astype(BF16)
        gate = z_ref[rows(c), part(3)].astype(F32)
        o = (jnp.dot(q, s_scr[0, c], preferred_element_type=F32) * tab_scr[2]
             + jnp.dot(q, s_scr[1, c], preferred_element_type=F32) * tab_scr[3])
        intra = [_mm_nt(jnp.where(lane_head == h, q, jnp.zeros_like(q)), k) * d_scr[h]
                 for h in range(N_HEADS_RET)]
        for h in range(N_HEADS_RET):
            o = o + jnp.where(lane_head == h, _mm(intra[h], v), 0.0)
        sq = o * o
        sq_hi = sq.astype(BF16)
        sq_lo = (sq - sq_hi.astype(F32)).astype(BF16)
        ms = (jnp.dot(sq_hi, head_mean, preferred_element_type=F32)
              + jnp.dot(sq_lo, head_mean, preferred_element_type=F32))
        o = o * lax.rsqrt(ms + EPS) * ng_ref[...]
        o_ref[rows(c), :] = (o * _silu(gate)).astype(o_ref.dtype)
        return carry

    lax.fori_loop(0, nc, out, 0)


def _ret_scratch(n_tok):
    nc = n_tok // RET_CHUNK
    return [pltpu.VMEM((6, RET_CHUNK, WIDTH_RET), F32),
            pltpu.VMEM((N_HEADS_RET, RET_CHUNK, RET_CHUNK), F32),
            pltpu.VMEM((2, nc + 1, WIDTH_RET, WIDTH_RET), F32),
            pltpu.VMEM((2, nc, WIDTH_RET, WIDTH_RET), BF16)]


def _ctx_attn_body(sink_ref, decay_ref, z_ref, zr_ref, ng_ref, *refs, seq, n_prev, layer, depth):
    (ona_ref, oswa_ref, oret_ref, kna_ref, vna_ref, kswa_ref, vswa_ref, st_ref,
     tab_scr, d_scr, u_scr, s_scr) = refs[n_prev:]
    cache_refs = (kna_ref, vna_ref, kswa_ref, vswa_ref, st_ref)
    if n_prev == 0:
        for ref in cache_refs:
            for other in range(depth):
                if other != layer:
                    ref[0, other] = jnp.zeros(ref.shape[2:], ref.dtype)
        kna_ref, vna_ref, kswa_ref, vswa_ref, st_ref = (ref.at[:, pl.ds(layer, 1)] for ref in cache_refs)
    d = HEAD_DIM
    low_half = lax.broadcasted_iota(jnp.int32, (seq, 2 * d), 1) < d

    def pair_scores(q2, k2, head_is_low):
        return _mm_nt(jnp.where(low_half == head_is_low, q2, jnp.zeros_like(q2)), k2)

    jobs = []
    for pair in range(N_HEADS_NA // 2):
        cols = lambda part: slice(OFF_NA + part * WIDTH_NA + pair * 2 * d,
                                  OFF_NA + part * WIDTH_NA + (pair + 1) * 2 * d)
        q2 = (z_ref[:, cols(0)] * SCALE).astype(BF16)
        k2 = z_ref[:, cols(1)]
        v2 = z_ref[:, cols(2)]
        for hh in range(2):
            kna_ref[0, 0, 2 * pair + hh] = k2[:, hh * d:(hh + 1) * d]
            vna_ref[0, 0, 2 * pair + hh] = v2[:, hh * d:(hh + 1) * d]
        k2 = k2.astype(BF16)
        v2 = v2.astype(BF16)
        jobs.append([(pair_scores(q2, k2, hh == 0), v2, hh == 0, None) for hh in range(2)]
                    + [ona_ref, pair])
    lo = OFF_SWA + WIDTH_SWA
    k2 = z_ref[:, lo:lo + WIDTH_KV_SWA]
    v2 = z_ref[:, lo + WIDTH_KV_SWA:lo + 2 * WIDTH_KV_SWA]
    for g in range(N_KV_SWA):
        kswa_ref[0, 0, g] = k2[:, g * d:(g + 1) * d]
        vswa_ref[0, 0, g] = v2[:, g * d:(g + 1) * d]
    k2 = k2.astype(BF16)
    v2 = v2.astype(BF16)
    group = N_HEADS_SWA // N_KV_SWA
    for pair in range(N_HEADS_SWA // 2):
        q2 = (z_ref[:, OFF_SWA + pair * 2 * d:OFF_SWA + (pair + 1) * 2 * d] * SCALE).astype(BF16)
        heads = []
        for hh in range(2):
            h = 2 * pair + hh
            g = h // group
            qh = q2[:, hh * d:(hh + 1) * d]
            q_at_g = jnp.concatenate([qh, qh], axis=-1) if g != hh else q2
            heads.append((pair_scores(q_at_g, k2, g == 0), v2, g == 0, sink_ref[h]))
        jobs.append(heads + [oswa_ref, pair])
    probs = []
    for job in jobs:
        for s, v2, v_is_low, sink in job[:2]:
            m = jnp.max(s, axis=-1, keepdims=True)
            if sink is not None:
                m = jnp.maximum(m, sink)
            e = jnp.exp(s - m)
            den = jnp.sum(e, axis=-1, keepdims=True)
            if sink is not None:
                den = den + jnp.exp(sink - m)
            probs.append((e, den))
    for j, job in enumerate(jobs):
        halves = []
        for hh, (s, v2, v_is_low, sink) in enumerate(job[:2]):
            e, den = probs[2 * j + hh]
            o2 = _mm(e, v2) / den
            halves.append(o2[:, :d] if v_is_low else o2[:, d:])
        o_ref, pair = job[2], job[3]
        o_ref[:, pair * 2 * d:(pair + 1) * 2 * d] = jnp.concatenate(halves, axis=-1).astype(o_ref.dtype)
    _retention(zr_ref, 0, decay_ref, ng_ref, None, oret_ref, st_ref.at[0],
               tab_scr, d_scr, u_scr, s_scr, seq)


def _ctx_mixers(z_att, z_rest, swa_sink, ret_decay, ret_norm_g, caches, layer, depth, batch, seq):
    n = batch * seq
    assert z_att.shape[1] == OFF_RET
    smem = pl.BlockSpec(memory_space=pltpu.SMEM)
    first = caches is None
    layers_in_block = depth if first else 1
    block_layer = 0 if first else layer
    kv = lambda heads: pl.BlockSpec((1, layers_in_block, heads, seq, HEAD_DIM),
                                    lambda b: (b, block_layer, 0, 0, 0))
    kv_shape = lambda heads: jax.ShapeDtypeStruct((batch, depth, heads, seq, HEAD_DIM), F32)
    in_specs = [smem, smem,
                pl.BlockSpec((seq, OFF_RET), lambda b: (b, 0)),
                pl.BlockSpec((seq, 4 * WIDTH_RET), lambda b: (b, 0)),
                pl.BlockSpec((1, WIDTH_RET), lambda b: (0, 0))]
    args = [swa_sink, ret_decay, z_att, z_rest, ret_norm_g.reshape(1, WIDTH_RET)]
    n_prev = 0 if caches is None else len(caches)
    aliases = {len(args) + i: 3 + i for i in range(n_prev)}
    if caches is not None:
        in_specs += [pl.BlockSpec(memory_space=pl.ANY)] * n_prev
        args += list(caches)
    return pl.pallas_call(
        functools.partial(_ctx_attn_body, seq=seq, n_prev=n_prev, layer=layer, depth=depth),
        grid=(batch,),
        in_specs=in_specs,
        out_specs=[pl.BlockSpec((seq, WIDTH_NA), lambda b: (b, 0)),
                   pl.BlockSpec((seq, WIDTH_SWA), lambda b: (b, 0)),
                   pl.BlockSpec((seq, WIDTH_RET), lambda b: (b, 0)),
                   kv(N_HEADS_NA), kv(N_HEADS_NA), kv(N_KV_SWA), kv(N_KV_SWA),
                   pl.BlockSpec((1, layers_in_block, 2, N_HEADS_RET, HEAD_DIM, HEAD_DIM),
                                lambda b: (b, block_layer, 0, 0, 0, 0))],
        out_shape=[jax.ShapeDtypeStruct((n, WIDTH_NA), BF16),
                   jax.ShapeDtypeStruct((n, WIDTH_SWA), BF16),
                   jax.ShapeDtypeStruct((n, WIDTH_RET), BF16),
                   kv_shape(N_HEADS_NA), kv_shape(N_HEADS_NA), kv_shape(N_KV_SWA), kv_shape(N_KV_SWA),
                   jax.ShapeDtypeStruct((batch, depth, 2, N_HEADS_RET, HEAD_DIM, HEAD_DIM), F32)],
        scratch_shapes=_ret_scratch(seq),
        input_output_aliases=aliases,
        compiler_params=_params(("arbitrary",), 32),
        name="ctx_mixers",
    )(*args)


NA_QROWS = 4
NA_WROWS = NA_QROWS + WIN_ROWS


def _na_bias_table(rpb, rows):
    n_groups = rows // NA_QROWS
    assert rows % NA_QROWS == 0 and n_groups >= 3
    n_heads, n_ro, n_co = rpb.shape
    half = WIN_ROWS // 2
    cc = np.arange(GRID_W)
    col_start = np.clip(cc - WIN_COLS // 2, 0, GRID_W - WIN_COLS)
    valid = (cc[None, :] >= col_start[:, None]) & (cc[None, :] < col_start[:, None] + WIN_COLS)
    span = 2 * GRID_W - 1
    lead = GRID_W - WIN_COLS
    ext = jnp.pad(rpb.astype(F32), ((0, 0), (0, 0), (lead, span + 1 - lead - n_co)))
    band = jnp.tile(ext, (1, 1, GRID_W))[..., :GRID_W * span].reshape(n_heads, n_ro, GRID_W, span)
    band = jnp.where(valid[None, None], band[..., GRID_W - 1:], MASKED)
    outside = jnp.full((n_heads, GRID_W, GRID_W), MASKED, F32)
    tabs = []
    for g in (0, 1, n_groups - 1):
        ws = min(max(NA_QROWS * g - half, 0), rows - NA_WROWS)
        q_blocks = []
        for i in range(NA_QROWS):
            r = NA_QROWS * g + i
            rs = min(max(r - half, 0), rows - WIN_ROWS)
            q_blocks.append(jnp.concatenate(
                [band[:, ws + j - r + WIN_ROWS - 1] if rs <= ws + j < rs + WIN_ROWS else outside
                 for j in range(NA_WROWS)], axis=-1))
        tabs.append(jnp.concatenate(q_blocks, axis=-2))
    return jnp.stack(tabs, axis=0)


def _lat_na_body(z_ref, kc_ref, vc_ref, bias_ref, o_ref, kc_scr, vc_scr, *, rows):
    d = HEAD_DIM
    nq = NA_QROWS * GRID_W
    g = pl.program_id(1)

    @pl.when(g == 0)
    def _():
        kc_scr[...] = kc_ref[0, 0].astype(BF16)
        vc_scr[...] = vc_ref[0, 0].astype(BF16)

    ws = jnp.clip(NA_QROWS * g - WIN_ROWS // 2, 0, rows - NA_WROWS)
    qrows = pl.ds(pl.multiple_of(g * nq, nq), nq)
    wrows = pl.ds(pl.multiple_of(ws * GRID_W, GRID_W), NA_WROWS * GRID_W)
    low_half = lax.broadcasted_iota(jnp.int32, (nq, 2 * d), 1) < d
    for pair in range(N_HEADS_NA // 2):
        cols = slice(pair * 2 * d, (pair + 1) * 2 * d)
        q2 = (z_ref[qrows, cols] * SCALE).astype(BF16)
        kl2 = z_ref[wrows, WIDTH_NA + pair * 2 * d:WIDTH_NA + (pair + 1) * 2 * d].astype(BF16)
        vl2 = z_ref[wrows, 2 * WIDTH_NA + pair * 2 * d:2 * WIDTH_NA + (pair + 1) * 2 * d].astype(BF16)
        for hh in range(2):
            h = 2 * pair + hh
            qm = jnp.where(low_half == (hh == 0), q2, jnp.zeros_like(q2))
            s_loc = _mm_nt(qm, kl2) + bias_ref[0, h]
            s_ctx = _mm_nt(q2[:, hh * d:(hh + 1) * d], kc_scr[h])
            m = jnp.maximum(jnp.max(s_ctx, axis=-1, keepdims=True), jnp.max(s_loc, axis=-1, keepdims=True))
            e_ctx = jnp.exp(s_ctx - m)
            e_loc = jnp.exp(s_loc - m)
            den = jnp.sum(e_ctx, axis=-1, keepdims=True) + jnp.sum(e_loc, axis=-1, keepdims=True)
            o = (_mm(e_ctx, vc_scr[h]) + _mm(e_loc, vl2)[:, hh * d:(hh + 1) * d]) / den
            o_ref[:, h * d:(h + 1) * d] = o.astype(o_ref.dtype)


def _lat_na(z, cache_k, cache_v, bias_tab, layer, batch, seq):
    rows = seq // GRID_W
    n_groups = rows // NA_QROWS
    nq = NA_QROWS * GRID_W
    past = cache_k.shape[3]

    def group_class(g):
        return jnp.where(g == 0, 0, jnp.where(g == n_groups - 1, 2, 1))

    cache = pl.BlockSpec((1, 1, N_HEADS_NA, past, HEAD_DIM), lambda b, g: (b, layer, 0, 0, 0))
    return pl.pallas_call(
        functools.partial(_lat_na_body, rows=rows),
        grid=(batch, n_groups),
        in_specs=[pl.BlockSpec((seq, 3 * WIDTH_NA), lambda b, g: (b, 0)),
                  cache, cache,
                  pl.BlockSpec((1, N_HEADS_NA, nq, NA_WROWS * GRID_W),
                               lambda b, g: (group_class(g), 0, 0, 0))],
        out_specs=pl.BlockSpec((nq, WIDTH_NA), lambda b, g: (b * n_groups + g, 0)),
        out_shape=jax.ShapeDtypeStruct((batch * seq, WIDTH_NA), BF16),
        scratch_shapes=[pltpu.VMEM((N_HEADS_NA, past, HEAD_DIM), BF16),
                        pltpu.VMEM((N_HEADS_NA, past, HEAD_DIM), BF16)],
        compiler_params=_params(("arbitrary", "arbitrary"), 58),
        name="lat_na",
    )(z, cache_k, cache_v, bias_tab)


def _rope_tables(seq):
    t = jnp.arange(seq)
    n_freq = HEAD_DIM // 4
    inv = ROPE_BASE ** (-jnp.arange(n_freq, dtype=F32) / n_freq)
    ang = jnp.concatenate([(t // GRID_W).astype(F32)[:, None] * inv,
                           (t % GRID_W).astype(F32)[:, None] * inv], axis=-1)
    cos = jnp.repeat(jnp.cos(ang), 2, axis=-1)
    sign = jnp.tile(jnp.asarray([-1.0, 1.0], F32), HEAD_DIM // 2)
    sin = jnp.repeat(jnp.sin(ang), 2, axis=-1) * sign
    return jnp.tile(cos, (1, N_HEADS_SWA)), jnp.tile(sin, (1, N_HEADS_SWA))


def _rope(x, cos, sin_signed):
    width = x.shape[-1]
    lane = lax.broadcasted_iota(jnp.int32, x.shape, x.ndim - 1)
    partner = jnp.where((lane & 1) == 0, pltpu.roll(x, width - 1, x.ndim - 1), pltpu.roll(x, 1, x.ndim - 1))
    return x * cos + partner * sin_signed


def _lat_swa_body(sink_ref, z_ref, kc_ref, vc_ref, cos_ref, sin_ref, o_ref, kr_scr, kc_scr, vc_scr, *, seq):
    d = HEAD_DIM
    blk = SWA_BLOCK
    n = pl.program_id(1)
    assert N_KV_SWA == 2

    @pl.when(n == 0)
    def _():
        k = z_ref[:, WIDTH_SWA:WIDTH_SWA + WIDTH_KV_SWA].astype(F32)
        kr_scr[...] = _rope(k, cos_ref[:, :WIDTH_KV_SWA], sin_ref[:, :WIDTH_KV_SWA]).astype(BF16)
        kc_scr[...] = jnp.concatenate([kc_ref[0, 0, 0], kc_ref[0, 0, 1]], axis=-1).astype(BF16)
        vc_scr[...] = jnp.concatenate([vc_ref[0, 0, 0], vc_ref[0, 0, 1]], axis=-1).astype(BF16)

    qrows = pl.ds(pl.multiple_of(n * blk, blk), blk)
    kstart = jnp.clip((n - 1) * blk, 0, seq - 3 * blk)
    krows = pl.ds(pl.multiple_of(kstart, blk), 3 * blk)
    q_all = _rope(z_ref[qrows, 0:WIDTH_SWA].astype(F32), cos_ref[qrows, :], sin_ref[qrows, :]) * SCALE
    q_all = q_all.astype(BF16)
    qpos = n * blk + lax.broadcasted_iota(jnp.int32, (blk, 3 * blk), 0)
    kpos = kstart + lax.broadcasted_iota(jnp.int32, (blk, 3 * blk), 1)
    valid = jnp.abs(qpos - kpos) <= SWA_RADIUS
    kl2 = kr_scr[krows, :]
    vl2 = z_ref[krows, WIDTH_SWA + WIDTH_KV_SWA:WIDTH_SWA + 2 * WIDTH_KV_SWA]
    group = N_HEADS_SWA // N_KV_SWA
    zero = jnp.zeros((blk, d), BF16)
    scores = []
    for h in range(N_HEADS_SWA):
        g = h // group
        qh = q_all[:, h * d:(h + 1) * d]
        q_at_g = jnp.concatenate([qh, zero] if g == 0 else [zero, qh], axis=-1)
        scores.append((jnp.where(valid, _mm_nt(q_at_g, kl2), MASKED), _mm_nt(q_at_g, kc_scr[...])))
    probs = []
    for h, (s_loc, s_ctx) in enumerate(scores):
        m = jnp.maximum(jnp.maximum(jnp.max(s_loc, axis=-1, keepdims=True),
                                    jnp.max(s_ctx, axis=-1, keepdims=True)), sink_ref[h])
        e_loc = jnp.exp(s_loc - m)
        e_ctx = jnp.exp(s_ctx - m)
        den = (jnp.sum(e_loc, axis=-1, keepdims=True) + jnp.sum(e_ctx, axis=-1, keepdims=True)
               + jnp.exp(sink_ref[h] - m))
        probs.append((e_loc, e_ctx, den))
    outs = []
    for h, (e_loc, e_ctx, den) in enumerate(probs):
        g = h // group
        o2 = (_mm(e_ctx, vc_scr[...]) + _mm(e_loc, vl2)) / den
        outs.append(o2[:, g * d:(g + 1) * d])
    o_ref[...] = jnp.concatenate(outs, axis=-1).astype(o_ref.dtype)


def _lat_swa(z, cache_k, cache_v, swa_sink, cos, sin, layer, batch, seq):
    nb = seq // SWA_BLOCK
    past = cache_k.shape[3]
    width = WIDTH_SWA + 2 * WIDTH_KV_SWA
    assert OFF_SWA % width == 0 and seq >= 3 * SWA_BLOCK
    cache = pl.BlockSpec((1, 1, N_KV_SWA, past, HEAD_DIM), lambda b, n: (b, layer, 0, 0, 0))
    table = pl.BlockSpec((seq, WIDTH_SWA), lambda b, n: (0, 0))
    return pl.pallas_call(
        functools.partial(_lat_swa_body, seq=seq),
        grid=(batch, nb),
        in_specs=[pl.BlockSpec(memory_space=pltpu.SMEM),
                  pl.BlockSpec((seq, width), lambda b, n: (b, OFF_SWA // width)),
                  cache, cache, table, table],
        out_specs=pl.BlockSpec((SWA_BLOCK, WIDTH_SWA), lambda b, n: (b * nb + n, 0)),
        out_shape=jax.ShapeDtypeStruct((batch * seq, WIDTH_SWA), BF16),
        scratch_shapes=[pltpu.VMEM((seq, WIDTH_KV_SWA), BF16),
                        pltpu.VMEM((past, WIDTH_KV_SWA), BF16), pltpu.VMEM((past, WIDTH_KV_SWA), BF16)],
        compiler_params=_params(("arbitrary", "arbitrary"), 32),
        name="lat_swa",
    )(swa_sink, z, cache_k, cache_v, cos, sin)


def _lat_ret_body(decay_ref, z_ref, ng_ref, s0_ref, o_ref, tab_scr, d_scr, u_scr, s_scr, *, seq):
    _retention(z_ref, 0, decay_ref, ng_ref, s0_ref, o_ref, None, tab_scr, d_scr, u_scr, s_scr, seq)


def _lat_ret(z, ret_decay, ret_norm_g, state, layer, batch, seq):
    width = 4 * WIDTH_RET
    assert OFF_RET % width == 0
    return pl.pallas_call(
        functools.partial(_lat_ret_body, seq=seq),
        grid=(batch,),
        in_specs=[pl.BlockSpec(memory_space=pltpu.SMEM),
                  pl.BlockSpec((seq, width), lambda b: (b, OFF_RET // width)),
                  pl.BlockSpec((1, WIDTH_RET), lambda b: (0, 0)),
                  pl.BlockSpec((1, 1, 2, N_HEADS_RET, HEAD_DIM, HEAD_DIM),
                               lambda b: (b, layer, 0, 0, 0, 0))],
        out_specs=pl.BlockSpec((seq, WIDTH_RET), lambda b: (b, 0)),
        out_shape=jax.ShapeDtypeStruct((batch * seq, WIDTH_RET), BF16),
        scratch_shapes=_ret_scratch(seq),
        compiler_params=_params(("arbitrary",), 40),
        name="lat_ret",
    )(ret_decay, z, ret_norm_g.reshape(1, WIDTH_RET), state)


def _merge_body(ona_ref, oswa_ref, oret_ref, ga_ref, gs_ref, gr_ref, x_ref, mod_ref, g2_ref,
                wna_ref, wswa_ref, wret_ref, wout_ref, rw_ref, rb_ref,
                hx_ref, wna_scr, wswa_scr, wret_scr, wout_scr, rw_scr, *, per_cond):
    @pl.when(pl.program_id(0) == 0)
    def _():
        wna_scr[...] = wna_ref[0].astype(BF16)
        wswa_scr[...] = wswa_ref[0].astype(BF16)
        wret_scr[...] = wret_ref[0].astype(BF16)
        wout_scr[...] = wout_ref[0].astype(BF16)
        rw = rw_ref[0]
        rw_hi = rw.astype(BF16)
        rw_scr[0] = rw_hi
        rw_scr[1] = (rw - rw_hi.astype(F32)).astype(BF16)

    D = D_MODEL
    m = mod_ref[0]
    sub = x_ref.shape[0] // MERGE_SPLIT
    parts = [pl.ds(s * sub, sub) for s in range(MERGE_SPLIT)]
    gate = lambda ref, rows: _sigmoid(ref[rows, :].astype(F32))
    zs = [(gate(ga_ref, r) * jnp.dot(ona_ref[r, :], wna_scr[...], preferred_element_type=F32)
           + gate(gs_ref, r) * jnp.dot(oswa_ref[r, :], wswa_scr[...], preferred_element_type=F32)
           + gate(gr_ref, r) * jnp.dot(oret_ref[r, :], wret_scr[...], preferred_element_type=F32)
           ).astype(BF16) for r in parts]
    ys = [jnp.dot(z, wout_scr[...], preferred_element_type=F32) for z in zs]
    logits = []
    for r, y in zip(parts, ys):
        x1 = x_ref[r, :] + m[:, 2 * D:3 * D] * y
        hx_ref[r, HX_X1:HX_X1 + D] = x1
        h2 = x1 * lax.rsqrt(jnp.mean(x1 * x1, axis=-1, keepdims=True) + EPS) * g2_ref[0]
        h2 = h2 * (1.0 + m[:, 4 * D:5 * D]) + m[:, 3 * D:4 * D]
        hx_ref[r, 0:D] = h2
        h_hi = h2.astype(BF16)
        h_lo = (h2 - h_hi.astype(F32)).astype(BF16)
        logits.append((jnp.dot(h_hi, rw_scr[0], preferred_element_type=F32)
                       + jnp.dot(h_hi, rw_scr[1], preferred_element_type=F32)
                       + jnp.dot(h_lo, rw_scr[0], preferred_element_type=F32)) + rb_ref[0])
    for r, logit in zip(parts, logits):
        route = _route(logit)
        cond = (pl.program_id(0) * x_ref.shape[0] // per_cond).astype(F32)
        lane = lax.broadcasted_iota(jnp.int32, route.shape, 1)
        hx_ref[r, D:D + ROUTER_LANES] = jnp.where(lane == ROUTE_COND_LANE, cond, route)


MERGE_SPLIT = 2


def _route(logit):
    lane = lax.broadcasted_iota(jnp.int32, logit.shape, 1).astype(F32)
    neg = jnp.float32(-jnp.inf)
    big = jnp.float32(ROUTER_LANES)
    is_group = lane < N_GROUPS
    is_expert = (lane >= N_GROUPS) & (lane < N_GROUPS + N_EXPERTS)
    gl = jnp.where(is_group, logit, neg)
    gmax = jnp.max(gl, axis=-1, keepdims=True)
    gsel = jnp.min(jnp.where(gl == gmax, lane, big), axis=-1, keepdims=True)
    p_group = 1.0 / jnp.sum(jnp.where(is_group, jnp.exp(gl - gmax), 0.0), axis=-1, keepdims=True)
    eidx = lane - N_GROUPS
    in_group = is_expert & (jnp.floor(eidx / EXPERTS_PER_GROUP) == gsel)
    el = jnp.where(in_group, logit, neg)
    top1 = jnp.max(el, axis=-1, keepdims=True)
    sel1 = jnp.min(jnp.where(el == top1, lane, big), axis=-1, keepdims=True)
    el2 = jnp.where(lane == sel1, neg, el)
    top2 = jnp.max(el2, axis=-1, keepdims=True)
    sel2 = jnp.min(jnp.where(el2 == top2, lane, big), axis=-1, keepdims=True)
    e2 = jnp.exp(top2 - top1)
    w1 = p_group / (1.0 + e2)
    w2 = p_group * e2 / (1.0 + e2)
    comb = jnp.where(lane == sel1, w1, 0.0) + jnp.where(lane == sel2, w2, 0.0)
    return comb + jnp.where(lane == ROUTE_GROUP_LANE, gsel, 0.0)


def _merge(o_na, o_swa, o_ret, z, gate_tile, x, mod_rows, norm2_g, w_br_na, w_br_swa, w_br_ret, w_out,
           router_w, router_b, layer):
    n, d = x.shape
    n_cond = mod_rows.shape[0]
    per_cond = n // n_cond
    tm = 512
    gate = lambda k: pl.BlockSpec((tm, d), lambda i: (i, gate_tile + k))
    row = lambda w: pl.BlockSpec((tm, w), lambda i: (i, 0))
    weight = lambda k: pl.BlockSpec((1, k, d), lambda i: (layer, 0, 0))
    return pl.pallas_call(
        functools.partial(_merge_body, per_cond=per_cond),
        grid=(n // tm,),
        in_specs=[row(WIDTH_NA), row(WIDTH_SWA), row(WIDTH_RET),
                  gate(0), gate(1), gate(2),
                  row(d),
                  pl.BlockSpec((1, 1, 6 * d), lambda i: (i * tm // per_cond, 0, 0)),
                  pl.BlockSpec((1, 1, d), lambda i: (layer, 0, 0)),
                  weight(WIDTH_NA), weight(WIDTH_SWA), weight(WIDTH_RET), weight(d),
                  pl.BlockSpec((1, d, ROUTER_LANES), lambda i: (layer, 0, 0)),
                  pl.BlockSpec((1, 1, ROUTER_LANES), lambda i: (layer, 0, 0))],
        out_specs=row(HX_WIDTH),
        out_shape=jax.ShapeDtypeStruct((n, HX_WIDTH), F32),
        scratch_shapes=[pltpu.VMEM((WIDTH_NA, d), BF16), pltpu.VMEM((WIDTH_SWA, d), BF16),
                        pltpu.VMEM((WIDTH_RET, d), BF16), pltpu.VMEM((d, d), BF16),
                        pltpu.VMEM((2, d, ROUTER_LANES), BF16)],
        compiler_params=_params(("arbitrary",), 56),
        name="merge_router",
    )(o_na, o_swa, o_ret, z, z, z, x, mod_rows, norm2_g.reshape(norm2_g.shape[0], 1, d),
      w_br_na, w_br_swa, w_br_ret, w_out, router_w, router_b)


MOE_TILE = 512


def _route_plan(hx, n_tiles):
    n = hx.shape[0]
    g = hx[:, D_MODEL + ROUTE_GROUP_LANE].astype(jnp.int32)
    onehot = (g[:, None] == jnp.arange(N_GROUPS, dtype=jnp.int32)[None, :]).astype(jnp.int32)
    csum = jnp.cumsum(onehot, axis=0)
    counts = csum[-1]
    ends = jnp.cumsum(counts)
    dest = jnp.sum(onehot * ((ends - counts)[None, :] + csum - 1), axis=1)
    first = jnp.arange(n_tiles, dtype=jnp.int32) * MOE_TILE
    tile_lo = jnp.sum((first[:, None] >= ends[None, :]).astype(jnp.int32), axis=1)
    tile_hi = jnp.sum(((first + MOE_TILE - 1)[:, None] >= ends[None, :]).astype(jnp.int32), axis=1)
    return dest, tile_lo, tile_hi


def _moe_body(dest_ref, lo_ref, hi_ref, hx_hbm, wg_hbm, wu_hbm, wd_hbm, mod_ref, fg_ref, out_hbm,
              order_scr, hbuf, obuf, y_scr, wg_scr, wu_scr, wd_scr, in_sem, out_sem, w_sem,
              *, layer, n_tiles, n_tok, n_cond, final):
    step = pl.program_id(0)
    last_step = pl.num_programs(0) - 1
    rows_per_expert = MOE_TILE // EXPERTS_PER_GROUP

    def in_copy(tile, i, slot):
        return pltpu.make_async_copy(hx_hbm.at[pl.ds(order_scr[tile * MOE_TILE + i], 1)],
                                     hbuf.at[slot, pl.ds(i, 1)], in_sem.at[slot])

    def out_copy(tile, i, slot):
        return pltpu.make_async_copy(obuf.at[slot, pl.ds(i, 1)],
                                     out_hbm.at[pl.ds(order_scr[tile * MOE_TILE + i], 1)], out_sem.at[slot])

    def for_rows(fn):
        def body(i, carry):
            fn(i)
            return carry
        lax.fori_loop(0, MOE_TILE, body, 0, unroll=8)

    @pl.when(step == 0)
    def _():
        copies = [pltpu.make_async_copy(src.at[layer], dst, w_sem.at[k])
                  for k, (src, dst) in enumerate(((wg_hbm, wg_scr), (wu_hbm, wu_scr), (wd_hbm, wd_scr)))]
        for copy in copies:
            copy.start()

        def invert(i, carry):
            order_scr[dest_ref[i]] = i
            return carry
        lax.fori_loop(0, n_tok, invert, 0, unroll=8)
        for_rows(lambda i: in_copy(0, i, 0).start())
        for copy in copies:
            copy.wait()

    def run_tile(tile, slot, nxt, write_back):
        other = 1 - slot
        for_rows(lambda i: in_copy(tile, i, slot).wait())
        hx = hbuf[slot]
        h = hx[:, 0:D_MODEL].astype(BF16)
        route = hx[:, D_MODEL:D_MODEL + ROUTER_LANES]
        lane = lax.broadcasted_iota(jnp.int32, route.shape, 1)

        def group(k, copy_back):
            first = (lo_ref[tile] + k) * EXPERTS_PER_GROUP
            y = None
            for j in range(EXPERTS_PER_GROUP):
                e = first + j
                hg = jnp.dot(h, wg_scr[e], preferred_element_type=F32)
                hu = jnp.dot(h, wu_scr[e], preferred_element_type=F32)
                c = jnp.sum(jnp.where(lane == ROUTE_EXPERT_LANE + e, route, 0.0), axis=-1, keepdims=True)
                a = _silu(hg) * hu * c
                part = jnp.dot(a.astype(BF16), wd_scr[e], preferred_element_type=F32)
                y = part if y is None else y + part
                if k == 0:
                    for i in range(j * rows_per_expert, (j + 1) * rows_per_expert):
                        in_copy(nxt, i, other).start(priority=i % 2)
                        if copy_back:
                            out_copy(tile - 1, i, other).start(priority=i % 2)
            if k == 0:
                y_scr[...] = y
            else:
                y_scr[...] += y

        if write_back is True:
            group(0, True)
        else:
            pl.when(write_back)(functools.partial(group, 0, True))
            pl.when(jnp.logical_not(write_back))(functools.partial(group, 0, False))
        for k in range(1, N_GROUPS):
            pl.when(lo_ref[tile] + k <= hi_ref[tile])(functools.partial(group, k, False))

        cond = route[:, ROUTE_COND_LANE:ROUTE_COND_LANE + 1]
        gate2 = jnp.zeros((1, D_MODEL), F32)
        for c in range(n_cond):
            gate2 = jnp.where(cond == c, mod_ref[c][:, 5 * D_MODEL:6 * D_MODEL], gate2)
        x2 = hx[:, HX_X1:HX_X1 + D_MODEL] + gate2 * y_scr[...]
        if final:
            x2 = x2 * lax.rsqrt(jnp.mean(x2 * x2, axis=-1, keepdims=True) + EPS) * fg_ref[...]
        obuf[slot] = x2
        if write_back is True:
            for_rows(lambda i: out_copy(tile - 1, i, other).wait())
        else:
            @pl.when(write_back)
            def _():
                for_rows(lambda i: out_copy(tile - 1, i, other).wait())

    first_tile = 2 * step
    second_tile = first_tile + 1
    after = jnp.minimum(second_tile + 1, n_tiles - 1)
    run_tile(first_tile, 0, second_tile, step > 0)
    run_tile(second_tile, 1, after, True)

    @pl.when(step == last_step)
    def _():
        for_rows(lambda i: in_copy(after, i, 0).wait())
        for_rows(lambda i: out_copy(second_tile, i, 1).start())
        for_rows(lambda i: out_copy(second_tile, i, 1).wait())


def _moe(hx, mod_rows, w_gate, w_up, w_down, final_g, layer, final):
    n = hx.shape[0]
    d = D_MODEL
    n_tiles = n // MOE_TILE
    assert n_tiles % 2 == 0
    n_cond = mod_rows.shape[0]
    dest, tile_lo, tile_hi = _route_plan(hx, n_tiles)
    any_space = pl.BlockSpec(memory_space=pl.ANY)
    return pl.pallas_call(
        functools.partial(_moe_body, layer=layer, n_tiles=n_tiles, n_tok=n, n_cond=n_cond, final=final),
        grid_spec=pltpu.PrefetchScalarGridSpec(
            num_scalar_prefetch=3,
            grid=(n_tiles // 2,),
            in_specs=[any_space, any_space, any_space, any_space,
                      pl.BlockSpec((n_cond, 1, 6 * d), lambda t, *_: (0, 0, 0)),
                      pl.BlockSpec((1, d), lambda t, *_: (0, 0))],
            out_specs=any_space,
            scratch_shapes=[pltpu.SMEM((n,), jnp.int32),
                            pltpu.VMEM((2, MOE_TILE, HX_WIDTH), F32),
                            pltpu.VMEM((2, MOE_TILE, d), F32),
                            pltpu.VMEM((MOE_TILE, d), F32),
                            pltpu.VMEM((N_EXPERTS, d, D_EXPERT), BF16),
                            pltpu.VMEM((N_EXPERTS, d, D_EXPERT), BF16),
                            pltpu.VMEM((N_EXPERTS, D_EXPERT, d), BF16),
                            pltpu.SemaphoreType.DMA((2,)),
                            pltpu.SemaphoreType.DMA((2,)),
                            pltpu.SemaphoreType.DMA((3,))]),
        out_shape=jax.ShapeDtypeStruct((n, d), F32),
        compiler_params=_params(("arbitrary",), 56),
        name="moe",
    )(dest, tile_lo, tile_hi, hx, w_gate, w_up, w_down, mod_rows, final_g.reshape(1, d))


def kernel(x_prompt, x_sample, c, cache_na_k, cache_na_v, cache_swa_k, cache_swa_v, state_ret, c_ctx, ada_w, ada_b, norm1_g, norm2_g, w_in, na_rpb, swa_sink, ret_decay, ret_norm_g, w_br_na, w_br_swa, w_br_ret, w_out, router_group_w, router_group_b, router_expert_w, router_expert_b, w_gate, w_up, w_down, final_norm_g):
    batch, seq, d = x_prompt.shape
    dec_batch, dec_seq, _ = x_sample.shape
    depth = ada_w.shape[0]
    assert d == D_MODEL and dec_batch + 1 <= 8
    assert OFF_RET % IN_TILE == 0 and OFF_GATES % d == 0 and (OFF_GATES - OFF_RET) % d == 0

    cond = jnp.concatenate([c_ctx[None], c, jnp.zeros((8 - 1 - dec_batch, d), F32)], axis=0)
    mod = _modulation(cond, ada_w, ada_b)
    router_w = jnp.concatenate([router_group_w, router_expert_w], axis=-1)
    router_w = jnp.pad(router_w, ((0, 0), (0, 0), (0, ROUTER_LANES - router_w.shape[-1])))
    router_b = jnp.concatenate([router_group_b, router_expert_b], axis=-1)
    router_b = jnp.pad(router_b, ((0, 0), (0, ROUTER_LANES - router_b.shape[-1])))[:, None, :]
    cos, sin = _rope_tables(dec_seq)
    moe_weights = (w_gate.astype(BF16), w_up.astype(BF16), w_down.astype(BF16))

    xp = x_prompt.reshape(batch * seq, d)
    xs = x_sample.reshape(dec_batch * dec_seq, d)
    caches = None
    for l in range(depth):
        final = l == depth - 1
        mod_ctx = mod[l, 0:1][:, None, :]
        mod_lat = mod[l, 1:1 + dec_batch][:, None, :]
        merge_w = (norm2_g, w_br_na, w_br_swa, w_br_ret, w_out, router_w, router_b, l)
        moe_w = (*moe_weights, final_norm_g, l, final)

        z_att, z_rest = _in_proj(xp, mod_ctx, norm1_g, w_in, l, OFF_RET // IN_TILE)
        o_na, o_swa, o_ret, *caches = _ctx_mixers(
            z_att, z_rest, swa_sink[l], ret_decay[l], ret_norm_g[l], caches, l, depth, batch, seq)
        hx = _merge(o_na, o_swa, o_ret, z_rest, (OFF_GATES - OFF_RET) // d, xp, mod_ctx, *merge_w)
        xp = _moe(hx, mod_ctx, *moe_w)

        z, = _in_proj(xs, mod_lat, norm1_g, w_in, l, 0)
        o_na = _lat_na(z, cache_na_k, cache_na_v, _na_bias_table(na_rpb[l], dec_seq // GRID_W),
                       l, dec_batch, dec_seq)
        o_swa = _lat_swa(z, cache_swa_k, cache_swa_v, swa_sink[l], cos, sin, l, dec_batch, dec_seq)
        o_ret = _lat_ret(z, ret_decay[l], ret_norm_g[l], state_ret, l, dec_batch, dec_seq)
        hx = _merge(o_na, o_swa, o_ret, z, OFF_GATES // d, xs, mod_lat, *merge_w)
        xs = _moe(hx, mod_lat, *moe_w)

    return (xp.reshape(batch, seq, d), xs.reshape(dec_batch, dec_seq, d), *caches)
```

```python
import functools

import numpy as np
import jax
import jax.numpy as jnp
from jax import lax
from jax.experimental import pallas as pl
from jax.experimental.pallas import tpu as pltpu

F32 = jnp.float32
BF16 = jnp.bfloat16

D_MODEL = 1024
HEAD_DIM = 64
GRID_W = 64
N_HEADS_NA = 8
WIN_ROWS = 8
WIN_COLS = 16
N_HEADS_SWA = 4
N_KV_SWA = 2
SWA_RADIUS = 128
SWA_BLOCK = 128
N_HEADS_RET = 4
RET_CHUNK = 128
WIDTH_NA = N_HEADS_NA * HEAD_DIM
WIDTH_SWA = N_HEADS_SWA * HEAD_DIM
WIDTH_KV_SWA = N_KV_SWA * HEAD_DIM
WIDTH_RET = N_HEADS_RET * HEAD_DIM
IN_COLS = 3 * WIDTH_NA + WIDTH_SWA + 2 * WIDTH_KV_SWA + 4 * WIDTH_RET + 3 * D_MODEL
OFF_NA = 0
OFF_SWA = 3 * WIDTH_NA
OFF_RET = OFF_SWA + WIDTH_SWA + 2 * WIDTH_KV_SWA
OFF_GATES = OFF_RET + 4 * WIDTH_RET
N_GROUPS = 4
EXPERTS_PER_GROUP = 4
N_EXPERTS = N_GROUPS * EXPERTS_PER_GROUP
D_EXPERT = D_MODEL // 4
ROPE_BASE = 10000.0
EPS = 1e-6
SCALE = HEAD_DIM ** -0.5
MASKED = -1e30
ROUTER_LANES = 128
ROUTE_GROUP_LANE = 0
ROUTE_EXPERT_LANE = N_GROUPS
ROUTE_COND_LANE = N_GROUPS + N_EXPERTS
HX_X1 = D_MODEL + ROUTER_LANES
HX_WIDTH = HX_X1 + D_MODEL

MIB = 1024 * 1024


def _mm(a, b):
    return jnp.dot(a.astype(BF16), b.astype(BF16), preferred_element_type=F32)


def _mm_nt(a, b):
    return lax.dot_general(a.astype(BF16), b.astype(BF16), (((1,), (1,)), ((), ())),
                           preferred_element_type=F32)


def _sigmoid(x):
    return 1.0 / (1.0 + jnp.exp(-x))


def _silu(x):
    return x * _sigmoid(x)


def _params(semantics, vmem_mib):
    return pltpu.CompilerParams(dimension_semantics=semantics, vmem_limit_bytes=vmem_mib * MIB)


def _mod_body(cond_ref, w_ref, b_ref, o_ref):
    o_ref[0] = _mm(_silu(cond_ref[...]), w_ref[0]) + b_ref[0]


def _modulation(cond, ada_w, ada_b):
    depth, d, n = ada_w.shape
    tn = 1024
    return pl.pallas_call(
        _mod_body,
        grid=(depth, n // tn),
        in_specs=[pl.BlockSpec((8, d), lambda l, j: (0, 0)),
                  pl.BlockSpec((1, d, tn), lambda l, j: (l, 0, j)),
                  pl.BlockSpec((1, 1, tn), lambda l, j: (l, 0, j))],
        out_specs=pl.BlockSpec((1, 8, tn), lambda l, j: (l, 0, j)),
        out_shape=jax.ShapeDtypeStruct((depth, 8, n), F32),
        compiler_params=_params(("arbitrary", "arbitrary"), 32),
        name="modulation",
    )(cond, ada_w, ada_b.reshape(depth, 1, n))


def _inproj_body(x_ref, mod_ref, g_ref, w_ref, *refs, n_f32):
    zf_ref = refs[0] if n_f32 else None
    zb_ref, h_scr, w_scr = refs[-3:]
    i = pl.program_id(0)
    j = pl.program_id(1)

    @pl.when(i == 0)
    def _():
        w_scr[j] = w_ref[0].astype(BF16)

    @pl.when(j == 0)
    def _():
        x = x_ref[...]
        m = mod_ref[0]
        y = x * lax.rsqrt(jnp.mean(x * x, axis=-1, keepdims=True) + EPS) * g_ref[0]
        h_scr[...] = (y * (1.0 + m[:, D_MODEL:2 * D_MODEL]) + m[:, 0:D_MODEL]).astype(BF16)

    acc = jnp.dot(h_scr[...], w_scr[j], preferred_element_type=F32)
    if n_f32 == 0:
        zb_ref[...] = acc.astype(BF16)
    else:
        @pl.when(j < n_f32)
        def _():
            zf_ref[...] = acc

        @pl.when(j >= n_f32)
        def _():
            zb_ref[...] = acc.astype(BF16)


IN_TILE = 1024


def _in_proj(x, mod_rows, norm_g, w_in, layer, n_f32):
    n, d = x.shape
    n_cond = mod_rows.shape[0]
    tm, tn = 1024, IN_TILE
    nj = IN_COLS // tn
    per_cond = n // n_cond
    out_specs = [pl.BlockSpec((tm, tn), lambda i, j: (i, jnp.maximum(j - n_f32, 0)))]
    out_shape = [jax.ShapeDtypeStruct((n, IN_COLS - n_f32 * tn), BF16)]
    if n_f32:
        out_specs.insert(0, pl.BlockSpec((tm, tn), lambda i, j: (i, jnp.minimum(j, n_f32 - 1))))
        out_shape.insert(0, jax.ShapeDtypeStruct((n, n_f32 * tn), F32))
    return pl.pallas_call(
        functools.partial(_inproj_body, n_f32=n_f32),
        grid=(n // tm, nj),
        in_specs=[pl.BlockSpec((tm, d), lambda i, j: (i, 0)),
                  pl.BlockSpec((1, 1, 6 * d), lambda i, j: (i * tm // per_cond, 0, 0)),
                  pl.BlockSpec((1, 1, d), lambda i, j: (layer, 0, 0)),
                  pl.BlockSpec((1, d, tn), lambda i, j: (layer, 0, jnp.where(i == 0, j, nj - 1)))],
        out_specs=out_specs,
        out_shape=out_shape,
        scratch_shapes=[pltpu.VMEM((tm, d), BF16), pltpu.VMEM((nj, d, tn), BF16)],
        compiler_params=_params(("arbitrary", "arbitrary"), 48),
        name="in_proj",
    )(x, mod_rows, norm_g.reshape(norm_g.shape[0], 1, d), w_in)


def _log_sigmoid(x):
    return jnp.minimum(x, 0.0) - jnp.log1p(jnp.exp(-jnp.abs(x)))


def _retention(z_ref, c0, decay_ref, ng_ref, s0_ref, o_ref, st_ref, tab_scr, d_scr, u_scr, s_scr, n_tok):
    C = RET_CHUNK
    nc = n_tok // C
    d = HEAD_DIM
    @pl.when(pl.program_id(0) == 0)
    def _():
        pos = lax.broadcasted_iota(jnp.int32, (C, d), 0).astype(F32)
        ri = lax.broadcasted_iota(jnp.int32, (C, C), 0).astype(F32)
        ci = lax.broadcasted_iota(jnp.int32, (C, C), 1).astype(F32)
        diff = ri - ci
        for h in range(N_HEADS_RET):
            lgf = _log_sigmoid(jnp.full((C, C), decay_ref[0, h], F32))
            lgb = _log_sigmoid(jnp.full((C, C), decay_ref[1, h], F32))
            d_scr[h] = (jnp.where(diff >= 0, jnp.exp(lgf * jnp.maximum(diff, 0.0)), 0.0)
                        + jnp.where(diff <= 0, jnp.exp(lgb * jnp.maximum(-diff, 0.0)), 0.0))
            lf = _log_sigmoid(jnp.full((C, d), decay_ref[0, h], F32))
            lb = _log_sigmoid(jnp.full((C, d), decay_ref[1, h], F32))
            hs = slice(h * d, (h + 1) * d)
            tab_scr[0, :, hs] = jnp.exp(lf * (C - 1.0 - pos))
            tab_scr[1, :, hs] = jnp.exp(lb * pos)
            tab_scr[2, :, hs] = jnp.exp(lf * (pos + 1.0))
            tab_scr[3, :, hs] = jnp.exp(lb * (C - pos))
            tab_scr[4, :, hs] = jnp.exp(lf * C)
            tab_scr[5, :, hs] = jnp.exp(lb * C)

    def rows(c):
        return pl.ds(pl.multiple_of(c * C, C), C)

    def part(p):
        return slice(c0 + p * WIDTH_RET, c0 + (p + 1) * WIDTH_RET)

    W = WIDTH_RET
    head_shift = d.bit_length() - 1
    assert d == 1 << head_shift
    same_head = ((lax.broadcasted_iota(jnp.int32, (W, W), 0) >> head_shift)
                 == (lax.broadcasted_iota(jnp.int32, (W, W), 1) >> head_shift))
    lane_head = lax.broadcasted_iota(jnp.int32, (C, W), 1) >> head_shift
    head_mean = jnp.where(same_head, 1.0 / d, 0.0).astype(BF16)

    def upd(c, carry):
        k = z_ref[rows(c), part(1)].astype(F32) * SCALE
        v = z_ref[rows(c), part(2)]
        u_scr[0, c] = jnp.where(same_head, _mm((k * tab_scr[0]).T, v), 0.0)
        u_scr[1, c] = jnp.where(same_head, _mm((k * tab_scr[1]).T, v), 0.0)
        return carry

    lax.fori_loop(0, nc, upd, 0)

    for direction in range(2):
        if s0_ref is None:
            s_init = jnp.zeros((W, W), F32)
        else:
            u_scr[direction, nc] = jnp.zeros((W, W), F32)
            for h in range(N_HEADS_RET):
                u_scr[direction, nc, h * d:(h + 1) * d, h * d:(h + 1) * d] = s0_ref[0, 0, direction, h]
            s_init = u_scr[direction, nc]
        chunk_decay = tab_scr[4 + direction, 0:1, :]

        def scan(t, s):
            c = t if direction == 0 else nc - 1 - t
            s_scr[direction, c] = s.astype(BF16)
            return chunk_decay * s + u_scr[direction, c]

        s_fin = lax.fori_loop(0, nc, scan, s_init)
        if st_ref is not None:
            for h in range(N_HEADS_RET):
                st_ref[0, direction, h] = s_fin[h * d:(h + 1) * d, h * d:(h + 1) * d]

    def out(c, carry):
        q = z_ref[rows(c), part(0)].astype(BF16)
        k = (z_ref[rows(c), part(1)].astype(F32) * SCALE).astype(BF16)
        v = z_ref[rows(c), part(2)].astype(BF16)
        gate = z_ref[rows(c), part(3)].astype(F32)
        o = (jnp.dot(q, s_scr[0, c], preferred_element_type=F32) * tab_scr[2]
             + jnp.dot(q, s_scr[1, c], preferred_element_type=F32) * tab_scr[3])
        intra = [_mm_nt(jnp.where(lane_head == h, q, jnp.zeros_like(q)), k) * d_scr[h]
                 for h in range(N_HEADS_RET)]
        for h in range(N_HEADS_RET):
            o = o + jnp.where(lane_head == h, _mm(intra[h], v), 0.0)
        sq = o * o
        sq_hi = sq.astype(BF16)
        sq_lo = (sq - sq_hi.astype(F32)).astype(BF16)
        ms = (jnp.dot(sq_hi, head_mean, preferred_element_type=F32)
              + jnp.dot(sq_lo, head_mean, preferred_element_type=F32))
        o = o * lax.rsqrt(ms + EPS) * ng_ref[...]
        o_ref[rows(c), :] = (o * _silu(gate)).astype(o_ref.dtype)
        return carry

    lax.fori_loop(0, nc, out, 0)


def _ret_scratch(n_tok):
    nc = n_tok // RET_CHUNK
    return [pltpu.VMEM((6, RET_CHUNK, WIDTH_RET), F32),
            pltpu.VMEM((N_HEADS_RET, RET_CHUNK, RET_CHUNK), F32),
            pltpu.VMEM((2, nc + 1, WIDTH_RET, WIDTH_RET), F32),
            pltpu.VMEM((2, nc, WIDTH_RET, WIDTH_RET), BF16)]


def _ctx_attn_body(sink_ref, decay_ref, z_ref, zr_ref, ng_ref, *refs, seq, n_prev, layer, depth):
    (ona_ref, oswa_ref, oret_ref, kna_ref, vna_ref, kswa_ref, vswa_ref, st_ref,
     tab_scr, d_scr, u_scr, s_scr) = refs[n_prev:]
    cache_refs = (kna_ref, vna_ref, kswa_ref, vswa_ref, st_ref)
    if n_prev == 0:
        for ref in cache_refs:
            for other in range(depth):
                if other != layer:
                    ref[0, other] = jnp.zeros(ref.shape[2:], ref.dtype)
        kna_ref, vna_ref, kswa_ref, vswa_ref, st_ref = (ref.at[:, pl.ds(layer, 1)] for ref in cache_refs)
    d = HEAD_DIM
    low_half = lax.broadcasted_iota(jnp.int32, (seq, 2 * d), 1) < d

    def pair_scores(q2, k2, head_is_low):
        return _mm_nt(jnp.where(low_half == head_is_low, q2, jnp.zeros_like(q2)), k2)

    jobs = []
    for pair in range(N_HEADS_NA // 2):
        cols = lambda part: slice(OFF_NA + part * WIDTH_NA + pair * 2 * d,
                                  OFF_NA + part * WIDTH_NA + (pair + 1) * 2 * d)
        q2 = (z_ref[:, cols(0)] * SCALE).astype(BF16)
        k2 = z_ref[:, cols(1)]
        v2 = z_ref[:, cols(2)]
        for hh in range(2):
            kna_ref[0, 0, 2 * pair + hh] = k2[:, hh * d:(hh + 1) * d]
            vna_ref[0, 0, 2 * pair + hh] = v2[:, hh * d:(hh + 1) * d]
        k2 = k2.astype(BF16)
        v2 = v2.astype(BF16)
        jobs.append([(pair_scores(q2, k2, hh == 0), v2, hh == 0, None) for hh in range(2)]
                    + [ona_ref, pair])
    lo = OFF_SWA + WIDTH_SWA
    k2 = z_ref[:, lo:lo + WIDTH_KV_SWA]
    v2 = z_ref[:, lo + WIDTH_KV_SWA:lo + 2 * WIDTH_KV_SWA]
    for g in range(N_KV_SWA):
        kswa_ref[0, 0, g] = k2[:, g * d:(g + 1) * d]
        vswa_ref[0, 0, g] = v2[:, g * d:(g + 1) * d]
    k2 = k2.astype(BF16)
    v2 = v2.astype(BF16)
    group = N_HEADS_SWA // N_KV_SWA
    for pair in range(N_HEADS_SWA // 2):
        q2 = (z_ref[:, OFF_SWA + pair * 2 * d:OFF_SWA + (pair + 1) * 2 * d] * SCALE).astype(BF16)
        heads = []
        for hh in range(2):
            h = 2 * pair + hh
            g = h // group
            qh = q2[:, hh * d:(hh + 1) * d]
            q_at_g = jnp.concatenate([qh, qh], axis=-1) if g != hh else q2
            heads.append((pair_scores(q_at_g, k2, g == 0), v2, g == 0, sink_ref[h]))
        jobs.append(heads + [oswa_ref, pair])
    probs = []
    for job in jobs:
        for s, v2, v_is_low, sink in job[:2]:
            m = jnp.max(s, axis=-1, keepdims=True)
            if sink is not None:
                m = jnp.maximum(m, sink)
            e = jnp.exp(s - m)
            den = jnp.sum(e, axis=-1, keepdims=True)
            if sink is not None:
                den = den + jnp.exp(sink - m)
            probs.append((e, den))
    for j, job in enumerate(jobs):
        halves = []
        for hh, (s, v2, v_is_low, sink) in enumerate(job[:2]):
            e, den = probs[2 * j + hh]
            o2 = _mm(e, v2) / den
            halves.append(o2[:, :d] if v_is_low else o2[:, d:])
        o_ref, pair = job[2], job[3]
        o_ref[:, pair * 2 * d:(pair + 1) * 2 * d] = jnp.concatenate(halves, axis=-1).astype(o_ref.dtype)
    _retention(zr_ref, 0, decay_ref, ng_ref, None, oret_ref, st_ref.at[0],
               tab_scr, d_scr, u_scr, s_scr, seq)


def _ctx_mixers(z_att, z_rest, swa_sink, ret_decay, ret_norm_g, caches, layer, depth, batch, seq):
    n = batch * seq
    assert z_att.shape[1] == OFF_RET
    smem = pl.BlockSpec(memory_space=pltpu.SMEM)
    first = caches is None
    layers_in_block = depth if first else 1
    block_layer = 0 if first else layer
    kv = lambda heads: pl.BlockSpec((1, layers_in_block, heads, seq, HEAD_DIM),
                                    lambda b: (b, block_layer, 0, 0, 0))
    kv_shape = lambda heads: jax.ShapeDtypeStruct((batch, depth, heads, seq, HEAD_DIM), F32)
    in_specs = [smem, smem,
                pl.BlockSpec((seq, OFF_RET), lambda b: (b, 0)),
                pl.BlockSpec((seq, 4 * WIDTH_RET), lambda b: (b, 0)),
                pl.BlockSpec((1, WIDTH_RET), lambda b: (0, 0))]
    args = [swa_sink, ret_decay, z_att, z_rest, ret_norm_g.reshape(1, WIDTH_RET)]
    n_prev = 0 if caches is None else len(caches)
    aliases = {len(args) + i: 3 + i for i in range(n_prev)}
    if caches is not None:
        in_specs += [pl.BlockSpec(memory_space=pl.ANY)] * n_prev
        args += list(caches)
    return pl.pallas_call(
        functools.partial(_ctx_attn_body, seq=seq, n_prev=n_prev, layer=layer, depth=depth),
        grid=(batch,),
        in_specs=in_specs,
        out_specs=[pl.BlockSpec((seq, WIDTH_NA), lambda b: (b, 0)),
                   pl.BlockSpec((seq, WIDTH_SWA), lambda b: (b, 0)),
                   pl.BlockSpec((seq, WIDTH_RET), lambda b: (b, 0)),
                   kv(N_HEADS_NA), kv(N_HEADS_NA), kv(N_KV_SWA), kv(N_KV_SWA),
                   pl.BlockSpec((1, layers_in_block, 2, N_HEADS_RET, HEAD_DIM, HEAD_DIM),
                                lambda b: (b, block_layer, 0, 0, 0, 0))],
        out_shape=[jax.ShapeDtypeStruct((n, WIDTH_NA), BF16),
                   jax.ShapeDtypeStruct((n, WIDTH_SWA), BF16),
                   jax.ShapeDtypeStruct((n, WIDTH_RET), BF16),
                   kv_shape(N_HEADS_NA), kv_shape(N_HEADS_NA), kv_shape(N_KV_SWA), kv_shape(N_KV_SWA),
                   jax.ShapeDtypeStruct((batch, depth, 2, N_HEADS_RET, HEAD_DIM, HEAD_DIM), F32)],
        scratch_shapes=_ret_scratch(seq),
        input_output_aliases=aliases,
        compiler_params=_params(("arbitrary",), 32),
        name="ctx_mixers",
    )(*args)


NA_QROWS = 4
NA_WROWS = NA_QROWS + WIN_ROWS


def _na_bias_table(rpb, rows):
    n_groups = rows // NA_QROWS
    assert rows % NA_QROWS == 0 and n_groups >= 3
    n_heads, n_ro, n_co = rpb.shape
    half = WIN_ROWS // 2
    cc = np.arange(GRID_W)
    col_start = np.clip(cc - WIN_COLS // 2, 0, GRID_W - WIN_COLS)
    valid = (cc[None, :] >= col_start[:, None]) & (cc[None, :] < col_start[:, None] + WIN_COLS)
    span = 2 * GRID_W - 1
    lead = GRID_W - WIN_COLS
    ext = jnp.pad(rpb.astype(F32), ((0, 0), (0, 0), (lead, span + 1 - lead - n_co)))
    band = jnp.tile(ext, (1, 1, GRID_W))[..., :GRID_W * span].reshape(n_heads, n_ro, GRID_W, span)
    band = jnp.where(valid[None, None], band[..., GRID_W - 1:], MASKED)
    band = jnp.concatenate([band, jnp.full((n_heads, 1, GRID_W, GRID_W), MASKED, F32)], axis=1)
    slab = np.full((3, NA_QROWS, NA_WROWS), n_ro, np.int32)
    for cls, g in enumerate((0, 1, n_groups - 1)):
        ws = min(max(NA_QROWS * g - half, 0), rows - NA_WROWS)
        for i in range(NA_QROWS):
            r = NA_QROWS * g + i
            rs = min(max(r - half, 0), rows - WIN_ROWS)
            for j in range(NA_WROWS):
                if rs <= ws + j < rs + WIN_ROWS:
                    slab[cls, i, j] = ws + j - r + WIN_ROWS - 1

    def assemble(slab_ref, band_ref, o_ref):
        cls = pl.program_id(0)
        for i in range(NA_QROWS):
            for j in range(NA_WROWS):
                o_ref[0, 0, i * GRID_W:(i + 1) * GRID_W, j * GRID_W:(j + 1) * GRID_W] = (
                    band_ref[0, slab_ref[(cls * NA_QROWS + i) * NA_WROWS + j]])

    return pl.pallas_call(
        assemble,
        grid_spec=pltpu.PrefetchScalarGridSpec(
            num_scalar_prefetch=1,
            grid=(3, n_heads),
            in_specs=[pl.BlockSpec((1, n_ro + 1, GRID_W, GRID_W), lambda c, h, *_: (h, 0, 0, 0))],
            out_specs=pl.BlockSpec((1, 1, NA_QROWS * GRID_W, NA_WROWS * GRID_W), lambda c, h, *_: (c, h, 0, 0))),
        out_shape=jax.ShapeDtypeStruct((3, n_heads, NA_QROWS * GRID_W, NA_WROWS * GRID_W), F32),
        compiler_params=_params(("arbitrary", "arbitrary"), 32),
        name="na_bias_table",
    )(jnp.asarray(slab.reshape(-1)), band)


def _lat_na_body(z_ref, kc_ref, vc_ref, bias_ref, o_ref, kc_scr, vc_scr, *, rows):
    d = HEAD_DIM
    nq = NA_QROWS * GRID_W
    g = pl.program_id(1)

    @pl.when(g == 0)
    def _():
        kc_scr[...] = kc_ref[0, 0].astype(BF16)
        vc_scr[...] = vc_ref[0, 0].astype(BF16)

    ws = jnp.clip(NA_QROWS * g - WIN_ROWS // 2, 0, rows - NA_WROWS)
    qrows = pl.ds(pl.multiple_of(g * nq, nq), nq)
    wrows = pl.ds(pl.multiple_of(ws * GRID_W, GRID_W), NA_WROWS * GRID_W)
    low_half = lax.broadcasted_iota(jnp.int32, (nq, 2 * d), 1) < d
    for pair in range(N_HEADS_NA // 2):
        cols = slice(pair * 2 * d, (pair + 1) * 2 * d)
        q2 = (z_ref[qrows, cols] * SCALE).astype(BF16)
        kl2 = z_ref[wrows, WIDTH_NA + pair * 2 * d:WIDTH_NA + (pair + 1) * 2 * d].astype(BF16)
        vl2 = z_ref[wrows, 2 * WIDTH_NA + pair * 2 * d:2 * WIDTH_NA + (pair + 1) * 2 * d].astype(BF16)
        for hh in range(2):
            h = 2 * pair + hh
            qm = jnp.where(low_half == (hh == 0), q2, jnp.zeros_like(q2))
            s_loc = _mm_nt(qm, kl2) + bias_ref[0, h]
            s_ctx = _mm_nt(q2[:, hh * d:(hh + 1) * d], kc_scr[h])
            m = jnp.maximum(jnp.max(s_ctx, axis=-1, keepdims=True), jnp.max(s_loc, axis=-1, keepdims=True))
            e_ctx = jnp.exp(s_ctx - m)
            e_loc = jnp.exp(s_loc - m)
            den = jnp.sum(e_ctx, axis=-1, keepdims=True) + jnp.sum(e_loc, axis=-1, keepdims=True)
            o = (_mm(e_ctx, vc_scr[h]) + _mm(e_loc, vl2)[:, hh * d:(hh + 1) * d]) / den
            o_ref[:, h * d:(h + 1) * d] = o.astype(o_ref.dtype)


def _lat_na(z, cache_k, cache_v, bias_tab, layer, batch, seq):
    rows = seq // GRID_W
    n_groups = rows // NA_QROWS
    nq = NA_QROWS * GRID_W
    past = cache_k.shape[3]

    def group_class(g):
        return jnp.where(g == 0, 0, jnp.where(g == n_groups - 1, 2, 1))

    cache = pl.BlockSpec((1, 1, N_HEADS_NA, past, HEAD_DIM), lambda b, g: (b, layer, 0, 0, 0))
    return pl.pallas_call(
        functools.partial(_lat_na_body, rows=rows),
        grid=(batch, n_groups),
        in_specs=[pl.BlockSpec((seq, 3 * WIDTH_NA), lambda b, g: (b, 0)),
                  cache, cache,
                  pl.BlockSpec((1, N_HEADS_NA, nq, NA_WROWS * GRID_W),
                               lambda b, g: (group_class(g), 0, 0, 0))],
        out_specs=pl.BlockSpec((nq, WIDTH_NA), lambda b, g: (b * n_groups + g, 0)),
        out_shape=jax.ShapeDtypeStruct((batch * seq, WIDTH_NA), BF16),
        scratch_shapes=[pltpu.VMEM((N_HEADS_NA, past, HEAD_DIM), BF16),
                        pltpu.VMEM((N_HEADS_NA, past, HEAD_DIM), BF16)],
        compiler_params=_params(("arbitrary", "arbitrary"), 58),
        name="lat_na",
    )(z, cache_k, cache_v, bias_tab)


def _rope_tables(seq):
    t = jnp.arange(seq)
    n_freq = HEAD_DIM // 4
    inv = ROPE_BASE ** (-jnp.arange(n_freq, dtype=F32) / n_freq)
    ang = jnp.concatenate([(t // GRID_W).astype(F32)[:, None] * inv,
                           (t % GRID_W).astype(F32)[:, None] * inv], axis=-1)
    cos = jnp.repeat(jnp.cos(ang), 2, axis=-1)
    sign = jnp.tile(jnp.asarray([-1.0, 1.0], F32), HEAD_DIM // 2)
    sin = jnp.repeat(jnp.sin(ang), 2, axis=-1) * sign
    return jnp.tile(cos, (1, N_HEADS_SWA)), jnp.tile(sin, (1, N_HEADS_SWA))


def _rope(x, cos, sin_signed):
    width = x.shape[-1]
    lane = lax.broadcasted_iota(jnp.int32, x.shape, x.ndim - 1)
    partner = jnp.where((lane & 1) == 0, pltpu.roll(x, width - 1, x.ndim - 1), pltpu.roll(x, 1, x.ndim - 1))
    return x * cos + partner * sin_signed


def _lat_swa_body(sink_ref, z_ref, kc_ref, vc_ref, cos_ref, sin_ref, o_ref, kr_scr, kc_scr, vc_scr, *, seq):
    d = HEAD_DIM
    blk = SWA_BLOCK
    n = pl.program_id(1)
    assert N_KV_SWA == 2

    @pl.when(n == 0)
    def _():
        k = z_ref[:, WIDTH_SWA:WIDTH_SWA + WIDTH_KV_SWA].astype(F32)
        kr_scr[...] = _rope(k, cos_ref[:, :WIDTH_KV_SWA], sin_ref[:, :WIDTH_KV_SWA]).astype(BF16)
        kc_scr[...] = jnp.concatenate([kc_ref[0, 0, 0], kc_ref[0, 0, 1]], axis=-1).astype(BF16)
        vc_scr[...] = jnp.concatenate([vc_ref[0, 0, 0], vc_ref[0, 0, 1]], axis=-1).astype(BF16)

    qrows = pl.ds(pl.multiple_of(n * blk, blk), blk)
    kstart = jnp.clip((n - 1) * blk, 0, seq - 3 * blk)
    krows = pl.ds(pl.multiple_of(kstart, blk), 3 * blk)
    q_all = _rope(z_ref[qrows, 0:WIDTH_SWA].astype(F32), cos_ref[qrows, :], sin_ref[qrows, :]) * SCALE
    q_all = q_all.astype(BF16)
    qpos = n * blk + lax.broadcasted_iota(jnp.int32, (blk, 3 * blk), 0)
    kpos = kstart + lax.broadcasted_iota(jnp.int32, (blk, 3 * blk), 1)
    valid = jnp.abs(qpos - kpos) <= SWA_RADIUS
    kl2 = kr_scr[krows, :]
    vl2 = z_ref[krows, WIDTH_SWA + WIDTH_KV_SWA:WIDTH_SWA + 2 * WIDTH_KV_SWA]
    group = N_HEADS_SWA // N_KV_SWA
    zero = jnp.zeros((blk, d), BF16)
    scores = []
    for h in range(N_HEADS_SWA):
        g = h // group
        qh = q_all[:, h * d:(h + 1) * d]
        q_at_g = jnp.concatenate([qh, zero] if g == 0 else [zero, qh], axis=-1)
        scores.append((jnp.where(valid, _mm_nt(q_at_g, kl2), MASKED), _mm_nt(q_at_g, kc_scr[...])))
    probs = []
    for h, (s_loc, s_ctx) in enumerate(scores):
        m = jnp.maximum(jnp.maximum(jnp.max(s_loc, axis=-1, keepdims=True),
                                    jnp.max(s_ctx, axis=-1, keepdims=True)), sink_ref[h])
        e_loc = jnp.exp(s_loc - m)
        e_ctx = jnp.exp(s_ctx - m)
        den = (jnp.sum(e_loc, axis=-1, keepdims=True) + jnp.sum(e_ctx, axis=-1, keepdims=True)
               + jnp.exp(sink_ref[h] - m))
        probs.append((e_loc, e_ctx, den))
    outs = []
    for h, (e_loc, e_ctx, den) in enumerate(probs):
        g = h // group
        o2 = (_mm(e_ctx, vc_scr[...]) + _mm(e_loc, vl2)) / den
        outs.append(o2[:, g * d:(g + 1) * d])
    o_ref[...] = jnp.concatenate(outs, axis=-1).astype(o_ref.dtype)


def _lat_swa(z, cache_k, cache_v, swa_sink, cos, sin, layer, batch, seq):
    nb = seq // SWA_BLOCK
    past = cache_k.shape[3]
    width = WIDTH_SWA + 2 * WIDTH_KV_SWA
    assert OFF_SWA % width == 0 and seq >= 3 * SWA_BLOCK
    cache = pl.BlockSpec((1, 1, N_KV_SWA, past, HEAD_DIM), lambda b, n: (b, layer, 0, 0, 0))
    table = pl.BlockSpec((seq, WIDTH_SWA), lambda b, n: (0, 0))
    return pl.pallas_call(
        functools.partial(_lat_swa_body, seq=seq),
        grid=(batch, nb),
        in_specs=[pl.BlockSpec(memory_space=pltpu.SMEM),
                  pl.BlockSpec((seq, width), lambda b, n: (b, OFF_SWA // width)),
                  cache, cache, table, table],
        out_specs=pl.BlockSpec((SWA_BLOCK, WIDTH_SWA), lambda b, n: (b * nb + n, 0)),
        out_shape=jax.ShapeDtypeStruct((batch * seq, WIDTH_SWA), BF16),
        scratch_shapes=[pltpu.VMEM((seq, WIDTH_KV_SWA), BF16),
                        pltpu.VMEM((past, WIDTH_KV_SWA), BF16), pltpu.VMEM((past, WIDTH_KV_SWA), BF16)],
        compiler_params=_params(("arbitrary", "arbitrary"), 32),
        name="lat_swa",
    )(swa_sink, z, cache_k, cache_v, cos, sin)


def _lat_ret_body(decay_ref, z_ref, ng_ref, s0_ref, o_ref, tab_scr, d_scr, u_scr, s_scr, *, seq):
    _retention(z_ref, 0, decay_ref, ng_ref, s0_ref, o_ref, None, tab_scr, d_scr, u_scr, s_scr, seq)


def _lat_ret(z, ret_decay, ret_norm_g, state, layer, batch, seq):
    width = 4 * WIDTH_RET
    assert OFF_RET % width == 0
    return pl.pallas_call(
        functools.partial(_lat_ret_body, seq=seq),
        grid=(batch,),
        in_specs=[pl.BlockSpec(memory_space=pltpu.SMEM),
                  pl.BlockSpec((seq, width), lambda b: (b, OFF_RET // width)),
                  pl.BlockSpec((1, WIDTH_RET), lambda b: (0, 0)),
                  pl.BlockSpec((1, 1, 2, N_HEADS_RET, HEAD_DIM, HEAD_DIM),
                               lambda b: (b, layer, 0, 0, 0, 0))],
        out_specs=pl.BlockSpec((seq, WIDTH_RET), lambda b: (b, 0)),
        out_shape=jax.ShapeDtypeStruct((batch * seq, WIDTH_RET), BF16),
        scratch_shapes=_ret_scratch(seq),
        compiler_params=_params(("arbitrary",), 40),
        name="lat_ret",
    )(ret_decay, z, ret_norm_g.reshape(1, WIDTH_RET), state)


def _merge_body(ona_ref, oswa_ref, oret_ref, ga_ref, gs_ref, gr_ref, x_ref, mod_ref, g2_ref,
                wna_ref, wswa_ref, wret_ref, wout_ref, rw_ref, rb_ref,
                hx_ref, wna_scr, wswa_scr, wret_scr, wout_scr, rw_scr, *, per_cond):
    @pl.when(pl.program_id(0) == 0)
    def _():
        wna_scr[...] = wna_ref[0].astype(BF16)
        wswa_scr[...] = wswa_ref[0].astype(BF16)
        wret_scr[...] = wret_ref[0].astype(BF16)
        wout_scr[...] = wout_ref[0].astype(BF16)
        rw = rw_ref[0]
        rw_hi = rw.astype(BF16)
        rw_scr[0] = rw_hi
        rw_scr[1] = (rw - rw_hi.astype(F32)).astype(BF16)

    D = D_MODEL
    m = mod_ref[0]
    sub = x_ref.shape[0] // MERGE_SPLIT
    parts = [pl.ds(s * sub, sub) for s in range(MERGE_SPLIT)]
    gate = lambda ref, rows: _sigmoid(ref[rows, :].astype(F32))
    zs = [(gate(ga_ref, r) * jnp.dot(ona_ref[r, :], wna_scr[...], preferred_element_type=F32)
           + gate(gs_ref, r) * jnp.dot(oswa_ref[r, :], wswa_scr[...], preferred_element_type=F32)
           + gate(gr_ref, r) * jnp.dot(oret_ref[r, :], wret_scr[...], preferred_element_type=F32)
           ).astype(BF16) for r in parts]
    ys = [jnp.dot(z, wout_scr[...], preferred_element_type=F32) for z in zs]
    logits = []
    for r, y in zip(parts, ys):
        x1 = x_ref[r, :] + m[:, 2 * D:3 * D] * y
        hx_ref[r, HX_X1:HX_X1 + D] = x1
        h2 = x1 * lax.rsqrt(jnp.mean(x1 * x1, axis=-1, keepdims=True) + EPS) * g2_ref[0]
        h2 = h2 * (1.0 + m[:, 4 * D:5 * D]) + m[:, 3 * D:4 * D]
        hx_ref[r, 0:D] = h2
        h_hi = h2.astype(BF16)
        h_lo = (h2 - h_hi.astype(F32)).astype(BF16)
        logits.append((jnp.dot(h_hi, rw_scr[0], preferred_element_type=F32)
                       + jnp.dot(h_hi, rw_scr[1], preferred_element_type=F32)
                       + jnp.dot(h_lo, rw_scr[0], preferred_element_type=F32)) + rb_ref[0])
    for r, logit in zip(parts, logits):
        route = _route(logit)
        cond = (pl.program_id(0) * x_ref.shape[0] // per_cond).astype(F32)
        lane = lax.broadcasted_iota(jnp.int32, route.shape, 1)
        hx_ref[r, D:D + ROUTER_LANES] = jnp.where(lane == ROUTE_COND_LANE, cond, route)


MERGE_SPLIT = 2


def _route(logit):
    lane = lax.broadcasted_iota(jnp.int32, logit.shape, 1).astype(F32)
    neg = jnp.float32(-jnp.inf)
    big = jnp.float32(ROUTER_LANES)
    is_group = lane < N_GROUPS
    is_expert = (lane >= N_GROUPS) & (lane < N_GROUPS + N_EXPERTS)
    gl = jnp.where(is_group, logit, neg)
    gmax = jnp.max(gl, axis=-1, keepdims=True)
    gsel = jnp.min(jnp.where(gl == gmax, lane, big), axis=-1, keepdims=True)
    p_group = 1.0 / jnp.sum(jnp.where(is_group, jnp.exp(gl - gmax), 0.0), axis=-1, keepdims=True)
    eidx = lane - N_GROUPS
    in_group = is_expert & (jnp.floor(eidx / EXPERTS_PER_GROUP) == gsel)
    el = jnp.where(in_group, logit, neg)
    top1 = jnp.max(el, axis=-1, keepdims=True)
    sel1 = jnp.min(jnp.where(el == top1, lane, big), axis=-1, keepdims=True)
    el2 = jnp.where(lane == sel1, neg, el)
    top2 = jnp.max(el2, axis=-1, keepdims=True)
    sel2 = jnp.min(jnp.where(el2 == top2, lane, big), axis=-1, keepdims=True)
    e2 = jnp.exp(top2 - top1)
    w1 = p_group / (1.0 + e2)
    w2 = p_group * e2 / (1.0 + e2)
    comb = jnp.where(lane == sel1, w1, 0.0) + jnp.where(lane == sel2, w2, 0.0)
    return comb + jnp.where(lane == ROUTE_GROUP_LANE, gsel, 0.0)


def _merge(o_na, o_swa, o_ret, z, gate_tile, x, mod_rows, norm2_g, w_br_na, w_br_swa, w_br_ret, w_out,
           router_w, router_b, layer):
    n, d = x.shape
    n_cond = mod_rows.shape[0]
    per_cond = n // n_cond
    tm = 512
    gate = lambda k: pl.BlockSpec((tm, d), lambda i: (i, gate_tile + k))
    row = lambda w: pl.BlockSpec((tm, w), lambda i: (i, 0))
    weight = lambda k: pl.BlockSpec((1, k, d), lambda i: (layer, 0, 0))
    return pl.pallas_call(
        functools.partial(_merge_body, per_cond=per_cond),
        grid=(n // tm,),
        in_specs=[row(WIDTH_NA), row(WIDTH_SWA), row(WIDTH_RET),
                  gate(0), gate(1), gate(2),
                  row(d),
                  pl.BlockSpec((1, 1, 6 * d), lambda i: (i * tm // per_cond, 0, 0)),
                  pl.BlockSpec((1, 1, d), lambda i: (layer, 0, 0)),
                  weight(WIDTH_NA), weight(WIDTH_SWA), weight(WIDTH_RET), weight(d),
                  pl.BlockSpec((1, d, ROUTER_LANES), lambda i: (layer, 0, 0)),
                  pl.BlockSpec((1, 1, ROUTER_LANES), lambda i: (layer, 0, 0))],
        out_specs=row(HX_WIDTH),
        out_shape=jax.ShapeDtypeStruct((n, HX_WIDTH), F32),
        scratch_shapes=[pltpu.VMEM((WIDTH_NA, d), BF16), pltpu.VMEM((WIDTH_SWA, d), BF16),
                        pltpu.VMEM((WIDTH_RET, d), BF16), pltpu.VMEM((d, d), BF16),
                        pltpu.VMEM((2, d, ROUTER_LANES), BF16)],
        compiler_params=_params(("arbitrary",), 56),
        name="merge_router",
    )(o_na, o_swa, o_ret, z, z, z, x, mod_rows, norm2_g.reshape(norm2_g.shape[0], 1, d),
      w_br_na, w_br_swa, w_br_ret, w_out, router_w, router_b)


MOE_TILE = 512


def _route_plan(hx, n_tiles):
    n = hx.shape[0]
    g = hx[:, D_MODEL + ROUTE_GROUP_LANE].astype(jnp.int32)
    onehot = (g[:, None] == jnp.arange(N_GROUPS, dtype=jnp.int32)[None, :]).astype(jnp.int32)
    csum = jnp.cumsum(onehot, axis=0)
    counts = csum[-1]
    ends = jnp.cumsum(counts)
    dest = jnp.sum(onehot * ((ends - counts)[None, :] + csum - 1), axis=1)
    first = jnp.arange(n_tiles, dtype=jnp.int32) * MOE_TILE
    tile_lo = jnp.sum((first[:, None] >= ends[None, :]).astype(jnp.int32), axis=1)
    tile_hi = jnp.sum(((first + MOE_TILE - 1)[:, None] >= ends[None, :]).astype(jnp.int32), axis=1)
    return dest, tile_lo, tile_hi


def _moe_body(dest_ref, lo_ref, hi_ref, hx_hbm, wg_hbm, wu_hbm, wd_hbm, mod_ref, fg_ref, out_hbm,
              order_scr, hbuf, obuf, y_scr, wg_scr, wu_scr, wd_scr, in_sem, out_sem, w_sem,
              *, layer, n_tiles, n_tok, n_cond, final):
    step = pl.program_id(0)
    last_step = pl.num_programs(0) - 1
    rows_per_expert = MOE_TILE // EXPERTS_PER_GROUP

    def in_copy(tile, i, slot):
        return pltpu.make_async_copy(hx_hbm.at[pl.ds(order_scr[tile * MOE_TILE + i], 1)],
                                     hbuf.at[slot, pl.ds(i, 1)], in_sem.at[slot])

    def out_copy(tile, i, slot):
        return pltpu.make_async_copy(obuf.at[slot, pl.ds(i, 1)],
                                     out_hbm.at[pl.ds(order_scr[tile * MOE_TILE + i], 1)], out_sem.at[slot])

    def for_rows(fn):
        def body(i, carry):
            fn(i)
            return carry
        lax.fori_loop(0, MOE_TILE, body, 0, unroll=8)

    @pl.when(step == 0)
    def _():
        copies = [pltpu.make_async_copy(src.at[layer], dst, w_sem.at[k])
                  for k, (src, dst) in enumerate(((wg_hbm, wg_scr), (wu_hbm, wu_scr), (wd_hbm, wd_scr)))]
        for copy in copies:
            copy.start()

        def invert(i, carry):
            order_scr[dest_ref[i]] = i
            return carry
        lax.fori_loop(0, n_tok, invert, 0, unroll=8)
        for_rows(lambda i: in_copy(0, i, 0).start())
        for copy in copies:
            copy.wait()

    def run_tile(tile, slot, nxt, write_back):
        other = 1 - slot
        for_rows(lambda i: in_copy(tile, i, slot).wait())
        hx = hbuf[slot]
        h = hx[:, 0:D_MODEL].astype(BF16)
        route = hx[:, D_MODEL:D_MODEL + ROUTER_LANES]
        lane = lax.broadcasted_iota(jnp.int32, route.shape, 1)

        def group(k, copy_back):
            first = (lo_ref[tile] + k) * EXPERTS_PER_GROUP
            y = None
            for j in range(EXPERTS_PER_GROUP):
                e = first + j
                hg = jnp.dot(h, wg_scr[e], preferred_element_type=F32)
                hu = jnp.dot(h, wu_scr[e], preferred_element_type=F32)
                c = jnp.sum(jnp.where(lane == ROUTE_EXPERT_LANE + e, route, 0.0), axis=-1, keepdims=True)
                a = _silu(hg) * hu * c
                part = jnp.dot(a.astype(BF16), wd_scr[e], preferred_element_type=F32)
                y = part if y is None else y + part
                if k == 0:
                    for i in range(j * rows_per_expert, (j + 1) * rows_per_expert):
                        in_copy(nxt, i, other).start(priority=i % 2)
                        if copy_back:
                            out_copy(tile - 1, i, other).start(priority=i % 2)
            if k == 0:
                y_scr[...] = y
            else:
                y_scr[...] += y

        if write_back is True:
            group(0, True)
        else:
            pl.when(write_back)(functools.partial(group, 0, True))
            pl.when(jnp.logical_not(write_back))(functools.partial(group, 0, False))
        for k in range(1, N_GROUPS):
            pl.when(lo_ref[tile] + k <= hi_ref[tile])(functools.partial(group, k, False))

        cond = route[:, ROUTE_COND_LANE:ROUTE_COND_LANE + 1]
        gate2 = jnp.zeros((1, D_MODEL), F32)
        for c in range(n_cond):
            gate2 = jnp.where(cond == c, mod_ref[c][:, 5 * D_MODEL:6 * D_MODEL], gate2)
        x2 = hx[:, HX_X1:HX_X1 + D_MODEL] + gate2 * y_scr[...]
        if final:
            x2 = x2 * lax.rsqrt(jnp.mean(x2 * x2, axis=-1, keepdims=True) + EPS) * fg_ref[...]
        obuf[slot] = x2
        if write_back is True:
            for_rows(lambda i: out_copy(tile - 1, i, other).wait())
        else:
            @pl.when(write_back)
            def _():
                for_rows(lambda i: out_copy(tile - 1, i, other).wait())

    first_tile = 2 * step
    second_tile = first_tile + 1
    after = jnp.minimum(second_tile + 1, n_tiles - 1)
    run_tile(first_tile, 0, second_tile, step > 0)
    run_tile(second_tile, 1, after, True)

    @pl.when(step == last_step)
    def _():
        for_rows(lambda i: in_copy(after, i, 0).wait())
        for_rows(lambda i: out_copy(second_tile, i, 1).start())
        for_rows(lambda i: out_copy(second_tile, i, 1).wait())


def _moe(hx, mod_rows, w_gate, w_up, w_down, final_g, layer, final):
    n = hx.shape[0]
    d = D_MODEL
    n_tiles = n // MOE_TILE
    assert n_tiles % 2 == 0
    n_cond = mod_rows.shape[0]
    dest, tile_lo, tile_hi = _route_plan(hx, n_tiles)
    any_space = pl.BlockSpec(memory_space=pl.ANY)
    return pl.pallas_call(
        functools.partial(_moe_body, layer=layer, n_tiles=n_tiles, n_tok=n, n_cond=n_cond, final=final),
        grid_spec=pltpu.PrefetchScalarGridSpec(
            num_scalar_prefetch=3,
            grid=(n_tiles // 2,),
            in_specs=[any_space, any_space, any_space, any_space,
                      pl.BlockSpec((n_cond, 1, 6 * d), lambda t, *_: (0, 0, 0)),
                      pl.BlockSpec((1, d), lambda t, *_: (0, 0))],
            out_specs=any_space,
            scratch_shapes=[pltpu.SMEM((n,), jnp.int32),
                            pltpu.VMEM((2, MOE_TILE, HX_WIDTH), F32),
                            pltpu.VMEM((2, MOE_TILE, d), F32),
                            pltpu.VMEM((MOE_TILE, d), F32),
                            pltpu.VMEM((N_EXPERTS, d, D_EXPERT), BF16),
                            pltpu.VMEM((N_EXPERTS, d, D_EXPERT), BF16),
                            pltpu.VMEM((N_EXPERTS, D_EXPERT, d), BF16),
                            pltpu.SemaphoreType.DMA((2,)),
                            pltpu.SemaphoreType.DMA((2,)),
                            pltpu.SemaphoreType.DMA((3,))]),
        out_shape=jax.ShapeDtypeStruct((n, d), F32),
        compiler_params=_params(("arbitrary",), 56),
        name="moe",
    )(dest, tile_lo, tile_hi, hx, w_gate, w_up, w_down, mod_rows, final_g.reshape(1, d))


def kernel(x_prompt, x_sample, c, cache_na_k, cache_na_v, cache_swa_k, cache_swa_v, state_ret, c_ctx, ada_w, ada_b, norm1_g, norm2_g, w_in, na_rpb, swa_sink, ret_decay, ret_norm_g, w_br_na, w_br_swa, w_br_ret, w_out, router_group_w, router_group_b, router_expert_w, router_expert_b, w_gate, w_up, w_down, final_norm_g):
    batch, seq, d = x_prompt.shape
    dec_batch, dec_seq, _ = x_sample.shape
    depth = ada_w.shape[0]
    assert d == D_MODEL and dec_batch + 1 <= 8
    assert OFF_RET % IN_TILE == 0 and OFF_GATES % d == 0 and (OFF_GATES - OFF_RET) % d == 0

    cond = jnp.concatenate([c_ctx[None], c, jnp.zeros((8 - 1 - dec_batch, d), F32)], axis=0)
    mod = _modulation(cond, ada_w, ada_b)
    router_w = jnp.concatenate([router_group_w, router_expert_w], axis=-1)
    router_w = jnp.pad(router_w, ((0, 0), (0, 0), (0, ROUTER_LANES - router_w.shape[-1])))
    router_b = jnp.concatenate([router_group_b, router_expert_b], axis=-1)
    router_b = jnp.pad(router_b, ((0, 0), (0, ROUTER_LANES - router_b.shape[-1])))[:, None, :]
    cos, sin = _rope_tables(dec_seq)
    moe_weights = (w_gate.astype(BF16), w_up.astype(BF16), w_down.astype(BF16))

    xp = x_prompt.reshape(batch * seq, d)
    xs = x_sample.reshape(dec_batch * dec_seq, d)
    caches = None
    for l in range(depth):
        final = l == depth - 1
        mod_ctx = mod[l, 0:1][:, None, :]
        mod_lat = mod[l, 1:1 + dec_batch][:, None, :]
        merge_w = (norm2_g, w_br_na, w_br_swa, w_br_ret, w_out, router_w, router_b, l)
        moe_w = (*moe_weights, final_norm_g, l, final)

        z_att, z_rest = _in_proj(xp, mod_ctx, norm1_g, w_in, l, OFF_RET // IN_TILE)
        o_na, o_swa, o_ret, *caches = _ctx_mixers(
            z_att, z_rest, swa_sink[l], ret_decay[l], ret_norm_g[l], caches, l, depth, batch, seq)
        hx = _merge(o_na, o_swa, o_ret, z_rest, (OFF_GATES - OFF_RET) // d, xp, mod_ctx, *merge_w)
        xp = _moe(hx, mod_ctx, *moe_w)

        z, = _in_proj(xs, mod_lat, norm1_g, w_in, l, 0)
        o_na = _lat_na(z, cache_na_k, cache_na_v, _na_bias_table(na_rpb[l], dec_seq // GRID_W),
                       l, dec_batch, dec_seq)
        o_swa = _lat_swa(z, cache_swa_k, cache_swa_v, swa_sink[l], cos, sin, l, dec_batch, dec_seq)
        o_ret = _lat_ret(z, ret_decay[l], ret_norm_g[l], state_ret, l, dec_batch, dec_seq)
        hx = _merge(o_na, o_swa, o_ret, z, OFF_GATES // d, xs, mod_lat, *merge_w)
        xs = _moe(hx, mod_lat, *moe_w)

    return (xp.reshape(batch, seq, d), xs.reshape(dec_batch, dec_seq, d), *caches)
```

```python
import functools

import numpy as np
import jax
import jax.numpy as jnp
from jax import lax
from jax.experimental import pallas as pl
from jax.experimental.pallas import tpu as pltpu

F32 = jnp.float32
BF16 = jnp.bfloat16

D_MODEL = 1024
HEAD_DIM = 64
GRID_W = 64
N_HEADS_NA = 8
WIN_ROWS = 8
WIN_COLS = 16
N_HEADS_SWA = 4
N_KV_SWA = 2
SWA_RADIUS = 128
SWA_BLOCK = 128
N_HEADS_RET = 4
RET_CHUNK = 128
WIDTH_NA = N_HEADS_NA * HEAD_DIM
WIDTH_SWA = N_HEADS_SWA * HEAD_DIM
WIDTH_KV_SWA = N_KV_SWA * HEAD_DIM
WIDTH_RET = N_HEADS_RET * HEAD_DIM
IN_COLS = 3 * WIDTH_NA + WIDTH_SWA + 2 * WIDTH_KV_SWA + 4 * WIDTH_RET + 3 * D_MODEL
OFF_NA = 0
OFF_SWA = 3 * WIDTH_NA
OFF_RET = OFF_SWA + WIDTH_SWA + 2 * WIDTH_KV_SWA
OFF_GATES = OFF_RET + 4 * WIDTH_RET
N_GROUPS = 4
EXPERTS_PER_GROUP = 4
N_EXPERTS = N_GROUPS * EXPERTS_PER_GROUP
D_EXPERT = D_MODEL // 4
ROPE_BASE = 10000.0
EPS = 1e-6
SCALE = HEAD_DIM ** -0.5
MASKED = -1e30
ROUTER_LANES = 128
ROUTE_GROUP_LANE = 0
ROUTE_EXPERT_LANE = N_GROUPS
ROUTE_COND_LANE = N_GROUPS + N_EXPERTS
HX_X1 = D_MODEL + ROUTER_LANES
HX_WIDTH = HX_X1 + D_MODEL

MIB = 1024 * 1024


def _mm(a, b):
    return jnp.dot(a.astype(BF16), b.astype(BF16), preferred_element_type=F32)


def _mm_nt(a, b):
    return lax.dot_general(a.astype(BF16), b.astype(BF16), (((1,), (1,)), ((), ())),
                           preferred_element_type=F32)


def _sigmoid(x):
    return 1.0 / (1.0 + jnp.exp(-x))


def _silu(x):
    return x * _sigmoid(x)


def _params(semantics, vmem_mib):
    return pltpu.CompilerParams(dimension_semantics=semantics, vmem_limit_bytes=vmem_mib * MIB)


def _mod_body(cond_ref, w_ref, b_ref, o_ref):
    o_ref[0] = _mm(_silu(cond_ref[...]), w_ref[0]) + b_ref[0]


def _modulation(cond, ada_w, ada_b):
    depth, d, n = ada_w.shape
    tn = 1024
    return pl.pallas_call(
        _mod_body,
        grid=(depth, n // tn),
        in_specs=[pl.BlockSpec((8, d), lambda l, j: (0, 0)),
                  pl.BlockSpec((1, d, tn), lambda l, j: (l, 0, j)),
                  pl.BlockSpec((1, 1, tn), lambda l, j: (l, 0, j))],
        out_specs=pl.BlockSpec((1, 8, tn), lambda l, j: (l, 0, j)),
        out_shape=jax.ShapeDtypeStruct((depth, 8, n), F32),
        compiler_params=_params(("arbitrary", "arbitrary"), 32),
        name="modulation",
    )(cond, ada_w, ada_b.reshape(depth, 1, n))


def _inproj_body(x_ref, mod_ref, g_ref, w_ref, *refs, n_f32):
    zf_ref = refs[0] if n_f32 else None
    zb_ref, h_scr, w_scr = refs[-3:]
    i = pl.program_id(0)
    j = pl.program_id(1)

    @pl.when(i == 0)
    def _():
        w_scr[j] = w_ref[0].astype(BF16)

    @pl.when(j == 0)
    def _():
        x = x_ref[...]
        m = mod_ref[0]
        y = x * lax.rsqrt(jnp.mean(x * x, axis=-1, keepdims=True) + EPS) * g_ref[0]
        h_scr[...] = (y * (1.0 + m[:, D_MODEL:2 * D_MODEL]) + m[:, 0:D_MODEL]).astype(BF16)

    acc = jnp.dot(h_scr[...], w_scr[j], preferred_element_type=F32)
    if n_f32 == 0:
        zb_ref[...] = acc.astype(BF16)
    else:
        @pl.when(j < n_f32)
        def _():
            zf_ref[...] = acc

        @pl.when(j >= n_f32)
        def _():
            zb_ref[...] = acc.astype(BF16)


IN_TILE = 1024


def _in_proj(x, mod_rows, norm_g, w_in, layer, n_f32):
    n, d = x.shape
    n_cond = mod_rows.shape[0]
    tm, tn = 1024, IN_TILE
    nj = IN_COLS // tn
    per_cond = n // n_cond
    out_specs = [pl.BlockSpec((tm, tn), lambda i, j: (i, jnp.maximum(j - n_f32, 0)))]
    out_shape = [jax.ShapeDtypeStruct((n, IN_COLS - n_f32 * tn), BF16)]
    if n_f32:
        out_specs.insert(0, pl.BlockSpec((tm, tn), lambda i, j: (i, jnp.minimum(j, n_f32 - 1))))
        out_shape.insert(0, jax.ShapeDtypeStruct((n, n_f32 * tn), F32))
    return pl.pallas_call(
        functools.partial(_inproj_body, n_f32=n_f32),
        grid=(n // tm, nj),
        in_specs=[pl.BlockSpec((tm, d), lambda i, j: (i, 0)),
                  pl.BlockSpec((1, 1, 6 * d), lambda i, j: (i * tm // per_cond, 0, 0)),
                  pl.BlockSpec((1, 1, d), lambda i, j: (layer, 0, 0)),
                  pl.BlockSpec((1, d, tn), lambda i, j: (layer, 0, jnp.where(i == 0, j, nj - 1)))],
        out_specs=out_specs,
        out_shape=out_shape,
        scratch_shapes=[pltpu.VMEM((tm, d), BF16), pltpu.VMEM((nj, d, tn), BF16)],
        compiler_params=_params(("arbitrary", "arbitrary"), 48),
        name="in_proj",
    )(x, mod_rows, norm_g.reshape(norm_g.shape[0], 1, d), w_in)


def _log_sigmoid(x):
    return jnp.minimum(x, 0.0) - jnp.log1p(jnp.exp(-jnp.abs(x)))


def _retention(z_ref, c0, decay_ref, ng_ref, s0_ref, o_ref, st_ref, tab_scr, d_scr, u_scr, s_scr, n_tok):
    C = RET_CHUNK
    nc = n_tok // C
    d = HEAD_DIM
    @pl.when(pl.program_id(0) == 0)
    def _():
        pos = lax.broadcasted_iota(jnp.int32, (C, d), 0).astype(F32)
        ri = lax.broadcasted_iota(jnp.int32, (C, C), 0).astype(F32)
        ci = lax.broadcasted_iota(jnp.int32, (C, C), 1).astype(F32)
        diff = ri - ci
        for h in range(N_HEADS_RET):
            lgf = _log_sigmoid(jnp.full((C, C), decay_ref[0, h], F32))
            lgb = _log_sigmoid(jnp.full((C, C), decay_ref[1, h], F32))
            d_scr[h] = (jnp.where(diff >= 0, jnp.exp(lgf * jnp.maximum(diff, 0.0)), 0.0)
                        + jnp.where(diff <= 0, jnp.exp(lgb * jnp.maximum(-diff, 0.0)), 0.0))
            lf = _log_sigmoid(jnp.full((C, d), decay_ref[0, h], F32))
            lb = _log_sigmoid(jnp.full((C, d), decay_ref[1, h], F32))
            hs = slice(h * d, (h + 1) * d)
            tab_scr[0, :, hs] = jnp.exp(lf * (C - 1.0 - pos))
            tab_scr[1, :, hs] = jnp.exp(lb * pos)
            tab_scr[2, :, hs] = jnp.exp(lf * (pos + 1.0))
            tab_scr[3, :, hs] = jnp.exp(lb * (C - pos))
            tab_scr[4, :, hs] = jnp.exp(lf * C)
            tab_scr[5, :, hs] = jnp.exp(lb * C)

    def rows(c):
        return pl.ds(pl.multiple_of(c * C, C), C)

    def part(p):
        return slice(c0 + p * WIDTH_RET, c0 + (p + 1) * WIDTH_RET)

    W = WIDTH_RET
    head_shift = d.bit_length() - 1
    assert d == 1 << head_shift
    same_head = ((lax.broadcasted_iota(jnp.int32, (W, W), 0) >> head_shift)
                 == (lax.broadcasted_iota(jnp.int32, (W, W), 1) >> head_shift))
    lane_head = lax.broadcasted_iota(jnp.int32, (C, W), 1) >> head_shift
    head_mean = jnp.where(same_head, 1.0 / d, 0.0).astype(BF16)

    def upd(c, carry):
        k = z_ref[rows(c), part(1)].astype(F32) * SCALE
        v = z_ref[rows(c), part(2)]
        u_scr[0, c] = jnp.where(same_head, _mm((k * tab_scr[0]).T, v), 0.0)
        u_scr[1, c] = jnp.where(same_head, _mm((k * tab_scr[1]).T, v), 0.0)
        return carry

    lax.fori_loop(0, nc, upd, 0)

    for direction in range(2):
        if s0_ref is None:
            s_init = jnp.zeros((W, W), F32)
        else:
            u_scr[direction, nc] = jnp.zeros((W, W), F32)
            for h in range(N_HEADS_RET):
                u_scr[direction, nc, h * d:(h + 1) * d, h * d:(h + 1) * d] = s0_ref[0, 0, direction, h]
            s_init = u_scr[direction, nc]
        chunk_decay = tab_scr[4 + direction, 0:1, :]

        def scan(t, s):
            c = t if direction == 0 else nc - 1 - t
            s_scr[direction, c] = s.astype(BF16)
            return chunk_decay * s + u_scr[direction, c]

        s_fin = lax.fori_loop(0, nc, scan, s_init)
        if st_ref is not None:
            for h in range(N_HEADS_RET):
                st_ref[0, direction, h] = s_fin[h * d:(h + 1) * d, h * d:(h + 1) * d]

    def out(c, carry):
        q = z_ref[rows(c), part(0)].astype(BF16)
        k = (z_ref[rows(c), part(1)].astype(F32) * SCALE).astype(BF16)
        v = z_ref[rows(c), part(2)].astype(BF16)
        gate = z_ref[rows(c), part(3)].astype(F32)
        o = (jnp.dot(q, s_scr[0, c], preferred_element_type=F32) * tab_scr[2]
             + jnp.dot(q, s_scr[1, c], preferred_element_type=F32) * tab_scr[3])
        intra = [_mm_nt(jnp.where(lane_head == h, q, jnp.zeros_like(q)), k) * d_scr[h]
                 for h in range(N_HEADS_RET)]
        for h in range(N_HEADS_RET):
            o = o + jnp.where(lane_head == h, _mm(intra[h], v), 0.0)
        sq = o * o
        sq_hi = sq.astype(BF16)
        sq_lo = (sq - sq_hi.astype(F32)).astype(BF16)
        ms = (jnp.dot(sq_hi, head_mean, preferred_element_type=F32)
              + jnp.dot(sq_lo, head_mean, preferred_element_type=F32))
        o = o * lax.rsqrt(ms + EPS) * ng_ref[...]
        o_ref[rows(c), :] = (o * _silu(gate)).astype(o_ref.dtype)
        return carry

    lax.fori_loop(0, nc, out, 0)


def _ret_scratch(n_tok):
    nc = n_tok // RET_CHUNK
    return [pltpu.VMEM((6, RET_CHUNK, WIDTH_RET), F32),
            pltpu.VMEM((N_HEADS_RET, RET_CHUNK, RET_CHUNK), F32),
            pltpu.VMEM((2, nc + 1, WIDTH_RET, WIDTH_RET), F32),
            pltpu.VMEM((2, nc, WIDTH_RET, WIDTH_RET), BF16)]


def _ctx_attn_body(sink_ref, decay_ref, z_ref, zr_ref, ng_ref, *refs, seq, n_prev, layer, depth):
    (ona_ref, oswa_ref, oret_ref, kna_ref, vna_ref, kswa_ref, vswa_ref, st_ref,
     tab_scr, d_scr, u_scr, s_scr) = refs[n_prev:]
    cache_refs = (kna_ref, vna_ref, kswa_ref, vswa_ref, st_ref)
    if n_prev == 0:
        for ref in cache_refs:
            for other in range(depth):
                if other != layer:
                    ref[0, other] = jnp.zeros(ref.shape[2:], ref.dtype)
        kna_ref, vna_ref, kswa_ref, vswa_ref, st_ref = (ref.at[:, pl.ds(layer, 1)] for ref in cache_refs)
    d = HEAD_DIM
    low_half = lax.broadcasted_iota(jnp.int32, (seq, 2 * d), 1) < d

    def pair_scores(q2, k2, head_is_low):
        return _mm_nt(jnp.where(low_half == head_is_low, q2, jnp.zeros_like(q2)), k2)

    jobs = []
    for pair in range(N_HEADS_NA // 2):
        cols = lambda part: slice(OFF_NA + part * WIDTH_NA + pair * 2 * d,
                                  OFF_NA + part * WIDTH_NA + (pair + 1) * 2 * d)
        q2 = (z_ref[:, cols(0)] * SCALE).astype(BF16)
        k2 = z_ref[:, cols(1)]
        v2 = z_ref[:, cols(2)]
        k2_t = k2.T
        v2_t = v2.T
        for hh in range(2):
            kna_ref[0, 0, 2 * pair + hh] = k2_t[hh * d:(hh + 1) * d, :]
            vna_ref[0, 0, 2 * pair + hh] = v2_t[hh * d:(hh + 1) * d, :]
        k2 = k2.astype(BF16)
        v2 = v2.astype(BF16)
        jobs.append([(pair_scores(q2, k2, hh == 0), v2, hh == 0, None) for hh in range(2)]
                    + [ona_ref, pair])
    lo = OFF_SWA + WIDTH_SWA
    k2 = z_ref[:, lo:lo + WIDTH_KV_SWA]
    v2 = z_ref[:, lo + WIDTH_KV_SWA:lo + 2 * WIDTH_KV_SWA]
    k2_t = k2.T
    v2_t = v2.T
    for g in range(N_KV_SWA):
        kswa_ref[0, 0, g] = k2_t[g * d:(g + 1) * d, :]
        vswa_ref[0, 0, g] = v2_t[g * d:(g + 1) * d, :]
    k2 = k2.astype(BF16)
    v2 = v2.astype(BF16)
    group = N_HEADS_SWA // N_KV_SWA
    for pair in range(N_HEADS_SWA // 2):
        q2 = (z_ref[:, OFF_SWA + pair * 2 * d:OFF_SWA + (pair + 1) * 2 * d] * SCALE).astype(BF16)
        heads = []
        for hh in range(2):
            h = 2 * pair + hh
            g = h // group
            qh = q2[:, hh * d:(hh + 1) * d]
            q_at_g = jnp.concatenate([qh, qh], axis=-1) if g != hh else q2
            heads.append((pair_scores(q_at_g, k2, g == 0), v2, g == 0, sink_ref[h]))
        jobs.append(heads + [oswa_ref, pair])
    probs = []
    for job in jobs:
        for s, v2, v_is_low, sink in job[:2]:
            m = jnp.max(s, axis=-1, keepdims=True)
            if sink is not None:
                m = jnp.maximum(m, sink)
            e = jnp.exp(s - m)
            den = jnp.sum(e, axis=-1, keepdims=True)
            if sink is not None:
                den = den + jnp.exp(sink - m)
            probs.append((e, den))
    for j, job in enumerate(jobs):
        halves = []
        for hh, (s, v2, v_is_low, sink) in enumerate(job[:2]):
            e, den = probs[2 * j + hh]
            o2 = _mm(e, v2) / den
            halves.append(o2[:, :d] if v_is_low else o2[:, d:])
        o_ref, pair = job[2], job[3]
        o_ref[:, pair * 2 * d:(pair + 1) * 2 * d] = jnp.concatenate(halves, axis=-1).astype(o_ref.dtype)
    _retention(zr_ref, 0, decay_ref, ng_ref, None, oret_ref, st_ref.at[0],
               tab_scr, d_scr, u_scr, s_scr, seq)


def _ctx_mixers(z_att, z_rest, swa_sink, ret_decay, ret_norm_g, caches, layer, depth, batch, seq):
    n = batch * seq
    assert z_att.shape[1] == OFF_RET
    smem = pl.BlockSpec(memory_space=pltpu.SMEM)
    first = caches is None
    layers_in_block = depth if first else 1
    block_layer = 0 if first else layer
    kv = lambda heads: pl.BlockSpec((1, layers_in_block, heads, HEAD_DIM, seq),
                                    lambda b: (b, block_layer, 0, 0, 0))
    kv_shape = lambda heads: jax.ShapeDtypeStruct((batch, depth, heads, HEAD_DIM, seq), F32)
    in_specs = [smem, smem,
                pl.BlockSpec((seq, OFF_RET), lambda b: (b, 0)),
                pl.BlockSpec((seq, 4 * WIDTH_RET), lambda b: (b, 0)),
                pl.BlockSpec((1, WIDTH_RET), lambda b: (0, 0))]
    args = [swa_sink, ret_decay, z_att, z_rest, ret_norm_g.reshape(1, WIDTH_RET)]
    n_prev = 0 if caches is None else len(caches)
    aliases = {len(args) + i: 3 + i for i in range(n_prev)}
    if caches is not None:
        in_specs += [pl.BlockSpec(memory_space=pl.ANY)] * n_prev
        args += list(caches)
    return pl.pallas_call(
        functools.partial(_ctx_attn_body, seq=seq, n_prev=n_prev, layer=layer, depth=depth),
        grid=(batch,),
        in_specs=in_specs,
        out_specs=[pl.BlockSpec((seq, WIDTH_NA), lambda b: (b, 0)),
                   pl.BlockSpec((seq, WIDTH_SWA), lambda b: (b, 0)),
                   pl.BlockSpec((seq, WIDTH_RET), lambda b: (b, 0)),
                   kv(N_HEADS_NA), kv(N_HEADS_NA), kv(N_KV_SWA), kv(N_KV_SWA),
                   pl.BlockSpec((1, layers_in_block, 2, N_HEADS_RET, HEAD_DIM, HEAD_DIM),
                                lambda b: (b, block_layer, 0, 0, 0, 0))],
        out_shape=[jax.ShapeDtypeStruct((n, WIDTH_NA), BF16),
                   jax.ShapeDtypeStruct((n, WIDTH_SWA), BF16),
                   jax.ShapeDtypeStruct((n, WIDTH_RET), BF16),
                   kv_shape(N_HEADS_NA), kv_shape(N_HEADS_NA), kv_shape(N_KV_SWA), kv_shape(N_KV_SWA),
                   jax.ShapeDtypeStruct((batch, depth, 2, N_HEADS_RET, HEAD_DIM, HEAD_DIM), F32)],
        scratch_shapes=_ret_scratch(seq),
        input_output_aliases=aliases,
        compiler_params=_params(("arbitrary",), 32),
        name="ctx_mixers",
    )(*args)


NA_QROWS = 4
NA_WROWS = NA_QROWS + WIN_ROWS


def _na_bias_table(rpb, rows):
    n_groups = rows // NA_QROWS
    assert rows % NA_QROWS == 0 and n_groups >= 3
    n_heads, n_ro, n_co = rpb.shape
    half = WIN_ROWS // 2
    cc = np.arange(GRID_W)
    col_start = np.clip(cc - WIN_COLS // 2, 0, GRID_W - WIN_COLS)
    valid = (cc[None, :] >= col_start[:, None]) & (cc[None, :] < col_start[:, None] + WIN_COLS)
    span = 2 * GRID_W - 1
    lead = GRID_W - WIN_COLS
    ext = jnp.pad(rpb.astype(F32), ((0, 0), (0, 0), (lead, span + 1 - lead - n_co)))
    band = jnp.tile(ext, (1, 1, GRID_W))[..., :GRID_W * span].reshape(n_heads, n_ro, GRID_W, span)
    band = jnp.where(valid[None, None], band[..., GRID_W - 1:], MASKED)
    band = jnp.concatenate([band, jnp.full((n_heads, 1, GRID_W, GRID_W), MASKED, F32)], axis=1)
    slab = np.full((3, NA_QROWS, NA_WROWS), n_ro, np.int32)
    for cls, g in enumerate((0, 1, n_groups - 1)):
        ws = min(max(NA_QROWS * g - half, 0), rows - NA_WROWS)
        for i in range(NA_QROWS):
            r = NA_QROWS * g + i
            rs = min(max(r - half, 0), rows - WIN_ROWS)
            for j in range(NA_WROWS):
                if rs <= ws + j < rs + WIN_ROWS:
                    slab[cls, i, j] = ws + j - r + WIN_ROWS - 1

    def assemble(slab_ref, band_ref, o_ref):
        cls = pl.program_id(0)
        for i in range(NA_QROWS):
            for j in range(NA_WROWS):
                o_ref[0, 0, i * GRID_W:(i + 1) * GRID_W, j * GRID_W:(j + 1) * GRID_W] = (
                    band_ref[0, slab_ref[(cls * NA_QROWS + i) * NA_WROWS + j]])

    return pl.pallas_call(
        assemble,
        grid_spec=pltpu.PrefetchScalarGridSpec(
            num_scalar_prefetch=1,
            grid=(3, n_heads),
            in_specs=[pl.BlockSpec((1, n_ro + 1, GRID_W, GRID_W), lambda c, h, *_: (h, 0, 0, 0))],
            out_specs=pl.BlockSpec((1, 1, NA_QROWS * GRID_W, NA_WROWS * GRID_W), lambda c, h, *_: (c, h, 0, 0))),
        out_shape=jax.ShapeDtypeStruct((3, n_heads, NA_QROWS * GRID_W, NA_WROWS * GRID_W), F32),
        compiler_params=_params(("arbitrary", "arbitrary"), 32),
        name="na_bias_table",
    )(jnp.asarray(slab.reshape(-1)), band)


def _lat_na_body(z_ref, kc_ref, vc_ref, bias_ref, o_ref, kc_scr, vc_scr, *, rows):
    d = HEAD_DIM
    nq = NA_QROWS * GRID_W
    g = pl.program_id(1)

    @pl.when(g == 0)
    def _():
        kc_scr[...] = kc_ref[0, 0].astype(BF16)
        vc_scr[...] = vc_ref[0, 0].astype(BF16)

    ws = jnp.clip(NA_QROWS * g - WIN_ROWS // 2, 0, rows - NA_WROWS)
    qrows = pl.ds(pl.multiple_of(g * nq, nq), nq)
    wrows = pl.ds(pl.multiple_of(ws * GRID_W, GRID_W), NA_WROWS * GRID_W)
    low_half = lax.broadcasted_iota(jnp.int32, (nq, 2 * d), 1) < d
    for pair in range(N_HEADS_NA // 2):
        cols = slice(pair * 2 * d, (pair + 1) * 2 * d)
        q2 = (z_ref[qrows, cols] * SCALE).astype(BF16)
        kl2 = z_ref[wrows, WIDTH_NA + pair * 2 * d:WIDTH_NA + (pair + 1) * 2 * d].astype(BF16)
        vl2 = z_ref[wrows, 2 * WIDTH_NA + pair * 2 * d:2 * WIDTH_NA + (pair + 1) * 2 * d].astype(BF16)
        for hh in range(2):
            h = 2 * pair + hh
            qm = jnp.where(low_half == (hh == 0), q2, jnp.zeros_like(q2))
            s_loc = _mm_nt(qm, kl2) + bias_ref[0, h]
            s_ctx = _mm(q2[:, hh * d:(hh + 1) * d], kc_scr[h])
            m = jnp.maximum(jnp.max(s_ctx, axis=-1, keepdims=True), jnp.max(s_loc, axis=-1, keepdims=True))
            e_ctx = jnp.exp(s_ctx - m)
            e_loc = jnp.exp(s_loc - m)
            den = jnp.sum(e_ctx, axis=-1, keepdims=True) + jnp.sum(e_loc, axis=-1, keepdims=True)
            o = (_mm_nt(e_ctx, vc_scr[h]) + _mm(e_loc, vl2)[:, hh * d:(hh + 1) * d]) / den
            o_ref[:, h * d:(h + 1) * d] = o.astype(o_ref.dtype)


def _lat_na(z, cache_kt, cache_vt, bias_tab, layer, batch, seq):
    rows = seq // GRID_W
    n_groups = rows // NA_QROWS
    nq = NA_QROWS * GRID_W
    past = cache_kt.shape[4]

    def group_class(g):
        return jnp.where(g == 0, 0, jnp.where(g == n_groups - 1, 2, 1))

    cache = pl.BlockSpec((1, 1, N_HEADS_NA, HEAD_DIM, past), lambda b, g: (b, layer, 0, 0, 0))
    return pl.pallas_call(
        functools.partial(_lat_na_body, rows=rows),
        grid=(batch, n_groups),
        in_specs=[pl.BlockSpec((seq, 3 * WIDTH_NA), lambda b, g: (b, 0)),
                  cache, cache,
                  pl.BlockSpec((1, N_HEADS_NA, nq, NA_WROWS * GRID_W),
                               lambda b, g: (group_class(g), 0, 0, 0))],
        out_specs=pl.BlockSpec((nq, WIDTH_NA), lambda b, g: (b * n_groups + g, 0)),
        out_shape=jax.ShapeDtypeStruct((batch * seq, WIDTH_NA), BF16),
        scratch_shapes=[pltpu.VMEM((N_HEADS_NA, HEAD_DIM, past), BF16),
                        pltpu.VMEM((N_HEADS_NA, HEAD_DIM, past), BF16)],
        compiler_params=_params(("arbitrary", "arbitrary"), 58),
        name="lat_na",
    )(z, cache_kt, cache_vt, bias_tab)


def _rope_tables(seq):
    t = jnp.arange(seq)
    n_freq = HEAD_DIM // 4
    inv = ROPE_BASE ** (-jnp.arange(n_freq, dtype=F32) / n_freq)
    ang = jnp.concatenate([(t // GRID_W).astype(F32)[:, None] * inv,
                           (t % GRID_W).astype(F32)[:, None] * inv], axis=-1)
    cos = jnp.repeat(jnp.cos(ang), 2, axis=-1)
    sign = jnp.tile(jnp.asarray([-1.0, 1.0], F32), HEAD_DIM // 2)
    sin = jnp.repeat(jnp.sin(ang), 2, axis=-1) * sign
    return jnp.tile(cos, (1, N_HEADS_SWA)), jnp.tile(sin, (1, N_HEADS_SWA))


def _rope(x, cos, sin_signed):
    width = x.shape[-1]
    lane = lax.broadcasted_iota(jnp.int32, x.shape, x.ndim - 1)
    partner = jnp.where((lane & 1) == 0, pltpu.roll(x, width - 1, x.ndim - 1), pltpu.roll(x, 1, x.ndim - 1))
    return x * cos + partner * sin_signed


def _lat_swa_body(sink_ref, z_ref, kc_ref, vc_ref, cos_ref, sin_ref, o_ref, kr_scr, kc_scr, vc_scr, *, seq):
    d = HEAD_DIM
    blk = SWA_BLOCK
    n = pl.program_id(1)
    assert N_KV_SWA == 2

    @pl.when(n == 0)
    def _():
        k = z_ref[:, WIDTH_SWA:WIDTH_SWA + WIDTH_KV_SWA].astype(F32)
        kr_scr[...] = _rope(k, cos_ref[:, :WIDTH_KV_SWA], sin_ref[:, :WIDTH_KV_SWA]).astype(BF16)
        kc_scr[...] = jnp.concatenate([kc_ref[0, 0, 0], kc_ref[0, 0, 1]], axis=0).astype(BF16)
        vc_scr[...] = jnp.concatenate([vc_ref[0, 0, 0], vc_ref[0, 0, 1]], axis=0).astype(BF16)

    qrows = pl.ds(pl.multiple_of(n * blk, blk), blk)
    kstart = jnp.clip((n - 1) * blk, 0, seq - 3 * blk)
    krows = pl.ds(pl.multiple_of(kstart, blk), 3 * blk)
    q_all = _rope(z_ref[qrows, 0:WIDTH_SWA].astype(F32), cos_ref[qrows, :], sin_ref[qrows, :]) * SCALE
    q_all = q_all.astype(BF16)
    qpos = n * blk + lax.broadcasted_iota(jnp.int32, (blk, 3 * blk), 0)
    kpos = kstart + lax.broadcasted_iota(jnp.int32, (blk, 3 * blk), 1)
    valid = jnp.abs(qpos - kpos) <= SWA_RADIUS
    kl2 = kr_scr[krows, :]
    vl2 = z_ref[krows, WIDTH_SWA + WIDTH_KV_SWA:WIDTH_SWA + 2 * WIDTH_KV_SWA]
    group = N_HEADS_SWA // N_KV_SWA
    zero = jnp.zeros((blk, d), BF16)
    scores = []
    for h in range(N_HEADS_SWA):
        g = h // group
        qh = q_all[:, h * d:(h + 1) * d]
        q_at_g = jnp.concatenate([qh, zero] if g == 0 else [zero, qh], axis=-1)
        scores.append((jnp.where(valid, _mm_nt(q_at_g, kl2), MASKED), _mm(q_at_g, kc_scr[...])))
    probs = []
    for h, (s_loc, s_ctx) in enumerate(scores):
        m = jnp.maximum(jnp.maximum(jnp.max(s_loc, axis=-1, keepdims=True),
                                    jnp.max(s_ctx, axis=-1, keepdims=True)), sink_ref[h])
        e_loc = jnp.exp(s_loc - m)
        e_ctx = jnp.exp(s_ctx - m)
        den = (jnp.sum(e_loc, axis=-1, keepdims=True) + jnp.sum(e_ctx, axis=-1, keepdims=True)
               + jnp.exp(sink_ref[h] - m))
        probs.append((e_loc, e_ctx, den))
    outs = []
    for h, (e_loc, e_ctx, den) in enumerate(probs):
        g = h // group
        o2 = (_mm_nt(e_ctx, vc_scr[...]) + _mm(e_loc, vl2)) / den
        outs.append(o2[:, g * d:(g + 1) * d])
    o_ref[...] = jnp.concatenate(outs, axis=-1).astype(o_ref.dtype)


def _lat_swa(z, cache_kt, cache_vt, swa_sink, cos, sin, layer, batch, seq):
    nb = seq // SWA_BLOCK
    past = cache_kt.shape[4]
    width = WIDTH_SWA + 2 * WIDTH_KV_SWA
    assert OFF_SWA % width == 0 and seq >= 3 * SWA_BLOCK
    cache = pl.BlockSpec((1, 1, N_KV_SWA, HEAD_DIM, past), lambda b, n: (b, layer, 0, 0, 0))
    table = pl.BlockSpec((seq, WIDTH_SWA), lambda b, n: (0, 0))
    return pl.pallas_call(
        functools.partial(_lat_swa_body, seq=seq),
        grid=(batch, nb),
        in_specs=[pl.BlockSpec(memory_space=pltpu.SMEM),
                  pl.BlockSpec((seq, width), lambda b, n: (b, OFF_SWA // width)),
                  cache, cache, table, table],
        out_specs=pl.BlockSpec((SWA_BLOCK, WIDTH_SWA), lambda b, n: (b * nb + n, 0)),
        out_shape=jax.ShapeDtypeStruct((batch * seq, WIDTH_SWA), BF16),
        scratch_shapes=[pltpu.VMEM((seq, WIDTH_KV_SWA), BF16),
                        pltpu.VMEM((WIDTH_KV_SWA, past), BF16), pltpu.VMEM((WIDTH_KV_SWA, past), BF16)],
        compiler_params=_params(("arbitrary", "arbitrary"), 32),
        name="lat_swa",
    )(swa_sink, z, cache_kt, cache_vt, cos, sin)


def _lat_ret_body(decay_ref, z_ref, ng_ref, s0_ref, o_ref, tab_scr, d_scr, u_scr, s_scr, *, seq):
    _retention(z_ref, 0, decay_ref, ng_ref, s0_ref, o_ref, None, tab_scr, d_scr, u_scr, s_scr, seq)


def _lat_ret(z, ret_decay, ret_norm_g, state, layer, batch, seq):
    width = 4 * WIDTH_RET
    assert OFF_RET % width == 0
    return pl.pallas_call(
        functools.partial(_lat_ret_body, seq=seq),
        grid=(batch,),
        in_specs=[pl.BlockSpec(memory_space=pltpu.SMEM),
                  pl.BlockSpec((seq, width), lambda b: (b, OFF_RET // width)),
                  pl.BlockSpec((1, WIDTH_RET), lambda b: (0, 0)),
                  pl.BlockSpec((1, 1, 2, N_HEADS_RET, HEAD_DIM, HEAD_DIM),
                               lambda b: (b, layer, 0, 0, 0, 0))],
        out_specs=pl.BlockSpec((seq, WIDTH_RET), lambda b: (b, 0)),
        out_shape=jax.ShapeDtypeStruct((batch * seq, WIDTH_RET), BF16),
        scratch_shapes=_ret_scratch(seq),
        compiler_params=_params(("arbitrary",), 40),
        name="lat_ret",
    )(ret_decay, z, ret_norm_g.reshape(1, WIDTH_RET), state)


def _merge_body(ona_ref, oswa_ref, oret_ref, ga_ref, gs_ref, gr_ref, x_ref, mod_ref, g2_ref,
                wna_ref, wswa_ref, wret_ref, wout_ref, rw_ref, rb_ref,
                hx_ref, wna_scr, wswa_scr, wret_scr, wout_scr, rw_scr, *, per_cond):
    @pl.when(pl.program_id(0) == 0)
    def _():
        wna_scr[...] = wna_ref[0].astype(BF16)
        wswa_scr[...] = wswa_ref[0].astype(BF16)
        wret_scr[...] = wret_ref[0].astype(BF16)
        wout_scr[...] = wout_ref[0].astype(BF16)
        rw = rw_ref[0]
        rw_hi = rw.astype(BF16)
        rw_scr[0] = rw_hi
        rw_scr[1] = (rw - rw_hi.astype(F32)).astype(BF16)

    D = D_MODEL
    m = mod_ref[0]
    sub = x_ref.shape[0] // MERGE_SPLIT
    parts = [pl.ds(s * sub, sub) for s in range(MERGE_SPLIT)]
    gate = lambda ref, rows: _sigmoid(ref[rows, :].astype(F32))
    zs = [(gate(ga_ref, r) * jnp.dot(ona_ref[r, :], wna_scr[...], preferred_element_type=F32)
           + gate(gs_ref, r) * jnp.dot(oswa_ref[r, :], wswa_scr[...], preferred_element_type=F32)
           + gate(gr_ref, r) * jnp.dot(oret_ref[r, :], wret_scr[...], preferred_element_type=F32)
           ).astype(BF16) for r in parts]
    ys = [jnp.dot(z, wout_scr[...], preferred_element_type=F32) for z in zs]
    logits = []
    for r, y in zip(parts, ys):
        x1 = x_ref[r, :] + m[:, 2 * D:3 * D] * y
        hx_ref[r, HX_X1:HX_X1 + D] = x1
        h2 = x1 * lax.rsqrt(jnp.mean(x1 * x1, axis=-1, keepdims=True) + EPS) * g2_ref[0]
        h2 = h2 * (1.0 + m[:, 4 * D:5 * D]) + m[:, 3 * D:4 * D]
        hx_ref[r, 0:D] = h2
        h_hi = h2.astype(BF16)
        h_lo = (h2 - h_hi.astype(F32)).astype(BF16)
        logits.append((jnp.dot(h_hi, rw_scr[0], preferred_element_type=F32)
                       + jnp.dot(h_hi, rw_scr[1], preferred_element_type=F32)
                       + jnp.dot(h_lo, rw_scr[0], preferred_element_type=F32)) + rb_ref[0])
    for r, logit in zip(parts, logits):
        route = _route(logit)
        cond = (pl.program_id(0) * x_ref.shape[0] // per_cond).astype(F32)
        lane = lax.broadcasted_iota(jnp.int32, route.shape, 1)
        hx_ref[r, D:D + ROUTER_LANES] = jnp.where(lane == ROUTE_COND_LANE, cond, route)


MERGE_SPLIT = 2


def _route(logit):
    lane = lax.broadcasted_iota(jnp.int32, logit.shape, 1).astype(F32)
    neg = jnp.float32(-jnp.inf)
    big = jnp.float32(ROUTER_LANES)
    is_group = lane < N_GROUPS
    is_expert = (lane >= N_GROUPS) & (lane < N_GROUPS + N_EXPERTS)
    gl = jnp.where(is_group, logit, neg)
    gmax = jnp.max(gl, axis=-1, keepdims=True)
    gsel = jnp.min(jnp.where(gl == gmax, lane, big), axis=-1, keepdims=True)
    p_group = 1.0 / jnp.sum(jnp.where(is_group, jnp.exp(gl - gmax), 0.0), axis=-1, keepdims=True)
    eidx = lane - N_GROUPS
    in_group = is_expert & (jnp.floor(eidx / EXPERTS_PER_GROUP) == gsel)
    el = jnp.where(in_group, logit, neg)
    top1 = jnp.max(el, axis=-1, keepdims=True)
    sel1 = jnp.min(jnp.where(el == top1, lane, big), axis=-1, keepdims=True)
    el2 = jnp.where(lane == sel1, neg, el)
    top2 = jnp.max(el2, axis=-1, keepdims=True)
    sel2 = jnp.min(jnp.where(el2 == top2, lane, big), axis=-1, keepdims=True)
    e2 = jnp.exp(top2 - top1)
    w1 = p_group / (1.0 + e2)
    w2 = p_group * e2 / (1.0 + e2)
    comb = jnp.where(lane == sel1, w1, 0.0) + jnp.where(lane == sel2, w2, 0.0)
    return comb + jnp.where(lane == ROUTE_GROUP_LANE, gsel, 0.0)


def _merge(o_na, o_swa, o_ret, z, gate_tile, x, mod_rows, norm2_g, w_br_na, w_br_swa, w_br_ret, w_out,
           router_w, router_b, layer):
    n, d = x.shape
    n_cond = mod_rows.shape[0]
    per_cond = n // n_cond
    tm = 512
    gate = lambda k: pl.BlockSpec((tm, d), lambda i: (i, gate_tile + k))
    row = lambda w: pl.BlockSpec((tm, w), lambda i: (i, 0))
    weight = lambda k: pl.BlockSpec((1, k, d), lambda i: (layer, 0, 0))
    return pl.pallas_call(
        functools.partial(_merge_body, per_cond=per_cond),
        grid=(n // tm,),
        in_specs=[row(WIDTH_NA), row(WIDTH_SWA), row(WIDTH_RET),
                  gate(0), gate(1), gate(2),
                  row(d),
                  pl.BlockSpec((1, 1, 6 * d), lambda i: (i * tm // per_cond, 0, 0)),
                  pl.BlockSpec((1, 1, d), lambda i: (layer, 0, 0)),
                  weight(WIDTH_NA), weight(WIDTH_SWA), weight(WIDTH_RET), weight(d),
                  pl.BlockSpec((1, d, ROUTER_LANES), lambda i: (layer, 0, 0)),
                  pl.BlockSpec((1, 1, ROUTER_LANES), lambda i: (layer, 0, 0))],
        out_specs=row(HX_WIDTH),
        out_shape=jax.ShapeDtypeStruct((n, HX_WIDTH), F32),
        scratch_shapes=[pltpu.VMEM((WIDTH_NA, d), BF16), pltpu.VMEM((WIDTH_SWA, d), BF16),
                        pltpu.VMEM((WIDTH_RET, d), BF16), pltpu.VMEM((d, d), BF16),
                        pltpu.VMEM((2, d, ROUTER_LANES), BF16)],
        compiler_params=_params(("arbitrary",), 56),
        name="merge_router",
    )(o_na, o_swa, o_ret, z, z, z, x, mod_rows, norm2_g.reshape(norm2_g.shape[0], 1, d),
      w_br_na, w_br_swa, w_br_ret, w_out, router_w, router_b)


MOE_TILE = 512


def _route_plan(hx, n_tiles):
    n = hx.shape[0]
    g = hx[:, D_MODEL + ROUTE_GROUP_LANE].astype(jnp.int32)
    onehot = (g[:, None] == jnp.arange(N_GROUPS, dtype=jnp.int32)[None, :]).astype(jnp.int32)
    csum = jnp.cumsum(onehot, axis=0)
    counts = csum[-1]
    ends = jnp.cumsum(counts)
    dest = jnp.sum(onehot * ((ends - counts)[None, :] + csum - 1), axis=1)
    first = jnp.arange(n_tiles, dtype=jnp.int32) * MOE_TILE
    tile_lo = jnp.sum((first[:, None] >= ends[None, :]).astype(jnp.int32), axis=1)
    tile_hi = jnp.sum(((first + MOE_TILE - 1)[:, None] >= ends[None, :]).astype(jnp.int32), axis=1)
    return dest, tile_lo, tile_hi


def _moe_body(dest_ref, lo_ref, hi_ref, hx_hbm, wg_hbm, wu_hbm, wd_hbm, mod_ref, fg_ref, out_hbm,
              order_scr, hbuf, obuf, y_scr, wg_scr, wu_scr, wd_scr, in_sem, out_sem, w_sem,
              *, layer, n_tiles, n_tok, n_cond, final):
    step = pl.program_id(0)
    last_step = pl.num_programs(0) - 1
    rows_per_expert = MOE_TILE // EXPERTS_PER_GROUP

    def in_copy(tile, i, slot):
        return pltpu.make_async_copy(hx_hbm.at[pl.ds(order_scr[tile * MOE_TILE + i], 1)],
                                     hbuf.at[slot, pl.ds(i, 1)], in_sem.at[slot])

    def out_copy(tile, i, slot):
        return pltpu.make_async_copy(obuf.at[slot, pl.ds(i, 1)],
                                     out_hbm.at[pl.ds(order_scr[tile * MOE_TILE + i], 1)], out_sem.at[slot])

    def for_rows(fn):
        def body(i, carry):
            fn(i)
            return carry
        lax.fori_loop(0, MOE_TILE, body, 0, unroll=8)

    @pl.when(step == 0)
    def _():
        copies = [pltpu.make_async_copy(src.at[layer], dst, w_sem.at[k])
                  for k, (src, dst) in enumerate(((wg_hbm, wg_scr), (wu_hbm, wu_scr), (wd_hbm, wd_scr)))]
        for copy in copies:
            copy.start()

        def invert(i, carry):
            order_scr[dest_ref[i]] = i
            return carry
        lax.fori_loop(0, n_tok, invert, 0, unroll=8)
        for_rows(lambda i: in_copy(0, i, 0).start())
        for copy in copies:
            copy.wait()

    def run_tile(tile, slot, nxt, write_back):
        other = 1 - slot
        for_rows(lambda i: in_copy(tile, i, slot).wait())
        hx = hbuf[slot]
        h = hx[:, 0:D_MODEL].astype(BF16)
        route = hx[:, D_MODEL:D_MODEL + ROUTER_LANES]
        lane = lax.broadcasted_iota(jnp.int32, route.shape, 1)

        def group(k, copy_back):
            first = (lo_ref[tile] + k) * EXPERTS_PER_GROUP
            y = None
            for j in range(EXPERTS_PER_GROUP):
                e = first + j
                hg = jnp.dot(h, wg_scr[e], preferred_element_type=F32)
                hu = jnp.dot(h, wu_scr[e], preferred_element_type=F32)
                c = jnp.sum(jnp.where(lane == ROUTE_EXPERT_LANE + e, route, 0.0), axis=-1, keepdims=True)
                a = _silu(hg) * hu * c
                part = jnp.dot(a.astype(BF16), wd_scr[e], preferred_element_type=F32)
                y = part if y is None else y + part
                if k == 0:
                    for i in range(j * rows_per_expert, (j + 1) * rows_per_expert):
                        in_copy(nxt, i, other).start(priority=i % 2)
                        if copy_back:
                            out_copy(tile - 1, i, other).start(priority=i % 2)
            if k == 0:
                y_scr[...] = y
            else:
                y_scr[...] += y

        if write_back is True:
            group(0, True)
        else:
            pl.when(write_back)(functools.partial(group, 0, True))
            pl.when(jnp.logical_not(write_back))(functools.partial(group, 0, False))
        for k in range(1, N_GROUPS):
            pl.when(lo_ref[tile] + k <= hi_ref[tile])(functools.partial(group, k, False))

        cond = route[:, ROUTE_COND_LANE:ROUTE_COND_LANE + 1]
        gate2 = jnp.zeros((1, D_MODEL), F32)
        for c in range(n_cond):
            gate2 = jnp.where(cond == c, mod_ref[c][:, 5 * D_MODEL:6 * D_MODEL], gate2)
        x2 = hx[:, HX_X1:HX_X1 + D_MODEL] + gate2 * y_scr[...]
        if final:
            x2 = x2 * lax.rsqrt(jnp.mean(x2 * x2, axis=-1, keepdims=True) + EPS) * fg_ref[...]
        obuf[slot] = x2
        if write_back is True:
            for_rows(lambda i: out_copy(tile - 1, i, other).wait())
        else:
            @pl.when(write_back)
            def _():
                for_rows(lambda i: out_copy(tile - 1, i, other).wait())

    first_tile = 2 * step
    second_tile = first_tile + 1
    after = jnp.minimum(second_tile + 1, n_tiles - 1)
    run_tile(first_tile, 0, second_tile, step > 0)
    run_tile(second_tile, 1, after, True)

    @pl.when(step == last_step)
    def _():
        for_rows(lambda i: in_copy(after, i, 0).wait())
        for_rows(lambda i: out_copy(second_tile, i, 1).start())
        for_rows(lambda i: out_copy(second_tile, i, 1).wait())


def _moe(hx, mod_rows, w_gate, w_up, w_down, final_g, layer, final):
    n = hx.shape[0]
    d = D_MODEL
    n_tiles = n // MOE_TILE
    assert n_tiles % 2 == 0
    n_cond = mod_rows.shape[0]
    dest, tile_lo, tile_hi = _route_plan(hx, n_tiles)
    any_space = pl.BlockSpec(memory_space=pl.ANY)
    return pl.pallas_call(
        functools.partial(_moe_body, layer=layer, n_tiles=n_tiles, n_tok=n, n_cond=n_cond, final=final),
        grid_spec=pltpu.PrefetchScalarGridSpec(
            num_scalar_prefetch=3,
            grid=(n_tiles // 2,),
            in_specs=[any_space, any_space, any_space, any_space,
                      pl.BlockSpec((n_cond, 1, 6 * d), lambda t, *_: (0, 0, 0)),
                      pl.BlockSpec((1, d), lambda t, *_: (0, 0))],
            out_specs=any_space,
            scratch_shapes=[pltpu.SMEM((n,), jnp.int32),
                            pltpu.VMEM((2, MOE_TILE, HX_WIDTH), F32),
                            pltpu.VMEM((2, MOE_TILE, d), F32),
                            pltpu.VMEM((MOE_TILE, d), F32),
                            pltpu.VMEM((N_EXPERTS, d, D_EXPERT), BF16),
                            pltpu.VMEM((N_EXPERTS, d, D_EXPERT), BF16),
                            pltpu.VMEM((N_EXPERTS, D_EXPERT, d), BF16),
                            pltpu.SemaphoreType.DMA((2,)),
                            pltpu.SemaphoreType.DMA((2,)),
                            pltpu.SemaphoreType.DMA((3,))]),
        out_shape=jax.ShapeDtypeStruct((n, d), F32),
        compiler_params=_params(("arbitrary",), 56),
        name="moe",
    )(dest, tile_lo, tile_hi, hx, w_gate, w_up, w_down, mod_rows, final_g.reshape(1, d))


def kernel(x_prompt, x_sample, c, cache_na_k, cache_na_v, cache_swa_k, cache_swa_v, state_ret, c_ctx, ada_w, ada_b, norm1_g, norm2_g, w_in, na_rpb, swa_sink, ret_decay, ret_norm_g, w_br_na, w_br_swa, w_br_ret, w_out, router_group_w, router_group_b, router_expert_w, router_expert_b, w_gate, w_up, w_down, final_norm_g):
    batch, seq, d = x_prompt.shape
    dec_batch, dec_seq, _ = x_sample.shape
    depth = ada_w.shape[0]
    assert d == D_MODEL and dec_batch + 1 <= 8
    assert OFF_RET % IN_TILE == 0 and OFF_GATES % d == 0 and (OFF_GATES - OFF_RET) % d == 0

    cond = jnp.concatenate([c_ctx[None], c, jnp.zeros((8 - 1 - dec_batch, d), F32)], axis=0)
    mod = _modulation(cond, ada_w, ada_b)
    router_w = jnp.concatenate([router_group_w, router_expert_w], axis=-1)
    router_w = jnp.pad(router_w, ((0, 0), (0, 0), (0, ROUTER_LANES - router_w.shape[-1])))
    router_b = jnp.concatenate([router_group_b, router_expert_b], axis=-1)
    router_b = jnp.pad(router_b, ((0, 0), (0, ROUTER_LANES - router_b.shape[-1])))[:, None, :]
    cos, sin = _rope_tables(dec_seq)
    na_kt, na_vt, swa_kt, swa_vt = (jnp.swapaxes(t, -1, -2)
                                    for t in (cache_na_k, cache_na_v, cache_swa_k, cache_swa_v))
    moe_weights = (w_gate.astype(BF16), w_up.astype(BF16), w_down.astype(BF16))

    xp = x_prompt.reshape(batch * seq, d)
    xs = x_sample.reshape(dec_batch * dec_seq, d)
    caches = None
    for l in range(depth):
        final = l == depth - 1
        mod_ctx = mod[l, 0:1][:, None, :]
        mod_lat = mod[l, 1:1 + dec_batch][:, None, :]
        merge_w = (norm2_g, w_br_na, w_br_swa, w_br_ret, w_out, router_w, router_b, l)
        moe_w = (*moe_weights, final_norm_g, l, final)

        z_att, z_rest = _in_proj(xp, mod_ctx, norm1_g, w_in, l, OFF_RET // IN_TILE)
        o_na, o_swa, o_ret, *caches = _ctx_mixers(
            z_att, z_rest, swa_sink[l], ret_decay[l], ret_norm_g[l], caches, l, depth, batch, seq)
        hx = _merge(o_na, o_swa, o_ret, z_rest, (OFF_GATES - OFF_RET) // d, xp, mod_ctx, *merge_w)
        xp = _moe(hx, mod_ctx, *moe_w)

        z, = _in_proj(xs, mod_lat, norm1_g, w_in, l, 0)
        o_na = _lat_na(z, na_kt, na_vt, _na_bias_table(na_rpb[l], dec_seq // GRID_W),
                       l, dec_batch, dec_seq)
        o_swa = _lat_swa(z, swa_kt, swa_vt, swa_sink[l], cos, sin, l, dec_batch, dec_seq)
        o_ret = _lat_ret(z, ret_decay[l], ret_norm_g[l], state_ret, l, dec_batch, dec_seq)
        hx = _merge(o_na, o_swa, o_ret, z, OFF_GATES // d, xs, mod_lat, *merge_w)
        xs = _moe(hx, mod_lat, *moe_w)

    new_kv = [jnp.swapaxes(t, -1, -2) for t in caches[:4]]
    return (xp.reshape(batch, seq, d), xs.reshape(dec_batch, dec_seq, d), *new_kv, caches[4])
```

```python
import functools

import numpy as np
import jax
import jax.numpy as jnp
from jax import lax
from jax.experimental import pallas as pl
from jax.experimental.pallas import tpu as pltpu

F32 = jnp.float32
BF16 = jnp.bfloat16

D_MODEL = 1024
HEAD_DIM = 64
GRID_W = 64
N_HEADS_NA = 8
WIN_ROWS = 8
WIN_COLS = 16
N_HEADS_SWA = 4
N_KV_SWA = 2
SWA_RADIUS = 128
SWA_BLOCK = 128
N_HEADS_RET = 4
RET_CHUNK = 128
WIDTH_NA = N_HEADS_NA * HEAD_DIM
WIDTH_SWA = N_HEADS_SWA * HEAD_DIM
WIDTH_KV_SWA = N_KV_SWA * HEAD_DIM
WIDTH_RET = N_HEADS_RET * HEAD_DIM
IN_COLS = 3 * WIDTH_NA + WIDTH_SWA + 2 * WIDTH_KV_SWA + 4 * WIDTH_RET + 3 * D_MODEL
OFF_NA = 0
OFF_SWA = 3 * WIDTH_NA
OFF_RET = OFF_SWA + WIDTH_SWA + 2 * WIDTH_KV_SWA
OFF_GATES = OFF_RET + 4 * WIDTH_RET
N_GROUPS = 4
EXPERTS_PER_GROUP = 4
N_EXPERTS = N_GROUPS * EXPERTS_PER_GROUP
D_EXPERT = D_MODEL // 4
ROPE_BASE = 10000.0
EPS = 1e-6
SCALE = HEAD_DIM ** -0.5
MASKED = -1e30
ROUTER_LANES = 128
ROUTE_GROUP_LANE = 0
ROUTE_EXPERT_LANE = N_GROUPS
ROUTE_COND_LANE = N_GROUPS + N_EXPERTS
HX_X1 = D_MODEL + ROUTER_LANES
HX_WIDTH = HX_X1 + D_MODEL

MIB = 1024 * 1024


def _mm(a, b):
    return jnp.dot(a.astype(BF16), b.astype(BF16), preferred_element_type=F32)


def _mm_nt(a, b):
    return lax.dot_general(a.astype(BF16), b.astype(BF16), (((1,), (1,)), ((), ())),
                           preferred_element_type=F32)


def _sigmoid(x):
    return 1.0 / (1.0 + jnp.exp(-x))


def _silu(x):
    return x * _sigmoid(x)


def _params(semantics, vmem_mib):
    return pltpu.CompilerParams(dimension_semantics=semantics, vmem_limit_bytes=vmem_mib * MIB)


def _mod_body(cond_ref, w_ref, b_ref, o_ref):
    o_ref[0] = _mm(_silu(cond_ref[...]), w_ref[0]) + b_ref[0]


def _modulation(cond, ada_w, ada_b):
    depth, d, n = ada_w.shape
    tn = 2048
    return pl.pallas_call(
        _mod_body,
        grid=(depth, n // tn),
        in_specs=[pl.BlockSpec((8, d), lambda l, j: (0, 0)),
                  pl.BlockSpec((1, d, tn), lambda l, j: (l, 0, j)),
                  pl.BlockSpec((1, 1, tn), lambda l, j: (l, 0, j))],
        out_specs=pl.BlockSpec((1, 8, tn), lambda l, j: (l, 0, j)),
        out_shape=jax.ShapeDtypeStruct((depth, 8, n), F32),
        compiler_params=_params(("arbitrary", "arbitrary"), 32),
        name="modulation",
    )(cond, ada_w, ada_b.reshape(depth, 1, n))


def _inproj_body(x_ref, mod_ref, g_ref, w_ref, *refs, n_f32):
    zf_ref = refs[0] if n_f32 else None
    zb_ref, h_scr, w_scr = refs[-3:]
    i = pl.program_id(0)
    j = pl.program_id(1)

    @pl.when(i == 0)
    def _():
        w_scr[j] = w_ref[0].astype(BF16)

    @pl.when(j == 0)
    def _():
        x = x_ref[...]
        m = mod_ref[0]
        y = x * lax.rsqrt(jnp.mean(x * x, axis=-1, keepdims=True) + EPS) * g_ref[0]
        h_scr[...] = (y * (1.0 + m[:, D_MODEL:2 * D_MODEL]) + m[:, 0:D_MODEL]).astype(BF16)

    acc = jnp.dot(h_scr[...], w_scr[j], preferred_element_type=F32)
    if n_f32 == 0:
        zb_ref[...] = acc.astype(BF16)
    else:
        @pl.when(j < n_f32)
        def _():
            zf_ref[...] = acc

        @pl.when(j >= n_f32)
        def _():
            zb_ref[...] = acc.astype(BF16)


IN_TILE = 1024


def _in_proj(x, mod_rows, norm_g, w_in, layer, n_f32):
    n, d = x.shape
    n_cond = mod_rows.shape[0]
    tm, tn = 1024, IN_TILE
    nj = IN_COLS // tn
    per_cond = n // n_cond
    out_specs = [pl.BlockSpec((tm, tn), lambda i, j: (i, jnp.maximum(j - n_f32, 0)))]
    out_shape = [jax.ShapeDtypeStruct((n, IN_COLS - n_f32 * tn), BF16)]
    if n_f32:
        out_specs.insert(0, pl.BlockSpec((tm, tn), lambda i, j: (i, jnp.minimum(j, n_f32 - 1))))
        out_shape.insert(0, jax.ShapeDtypeStruct((n, n_f32 * tn), F32))
    return pl.pallas_call(
        functools.partial(_inproj_body, n_f32=n_f32),
        grid=(n // tm, nj),
        in_specs=[pl.BlockSpec((tm, d), lambda i, j: (i, 0)),
                  pl.BlockSpec((1, 1, 6 * d), lambda i, j: (i * tm // per_cond, 0, 0)),
                  pl.BlockSpec((1, 1, d), lambda i, j: (layer, 0, 0)),
                  pl.BlockSpec((1, d, tn), lambda i, j: (layer, 0, jnp.where(i == 0, j, nj - 1)))],
        out_specs=out_specs,
        out_shape=out_shape,
        scratch_shapes=[pltpu.VMEM((tm, d), BF16), pltpu.VMEM((nj, d, tn), BF16)],
        compiler_params=_params(("arbitrary", "arbitrary"), 48),
        name="in_proj",
    )(x, mod_rows, norm_g.reshape(norm_g.shape[0], 1, d), w_in)


def _log_sigmoid(x):
    return jnp.minimum(x, 0.0) - jnp.log1p(jnp.exp(-jnp.abs(x)))


def _retention(z_ref, c0, decay_ref, ng_ref, s0_ref, o_ref, st_ref, tab_scr, d_scr, u_scr, s_scr, n_tok):
    C = RET_CHUNK
    nc = n_tok // C
    d = HEAD_DIM
    @pl.when(pl.program_id(0) == 0)
    def _():
        pos = lax.broadcasted_iota(jnp.int32, (C, d), 0).astype(F32)
        ri = lax.broadcasted_iota(jnp.int32, (C, C), 0).astype(F32)
        ci = lax.broadcasted_iota(jnp.int32, (C, C), 1).astype(F32)
        diff = ri - ci
        for h in range(N_HEADS_RET):
            lgf = _log_sigmoid(jnp.full((C, C), decay_ref[0, h], F32))
            lgb = _log_sigmoid(jnp.full((C, C), decay_ref[1, h], F32))
            d_scr[h] = (jnp.where(diff >= 0, jnp.exp(lgf * jnp.maximum(diff, 0.0)), 0.0)
                        + jnp.where(diff <= 0, jnp.exp(lgb * jnp.maximum(-diff, 0.0)), 0.0))
            lf = _log_sigmoid(jnp.full((C, d), decay_ref[0, h], F32))
            lb = _log_sigmoid(jnp.full((C, d), decay_ref[1, h], F32))
            hs = slice(h * d, (h + 1) * d)
            tab_scr[0, :, hs] = jnp.exp(lf * (C - 1.0 - pos))
            tab_scr[1, :, hs] = jnp.exp(lb * pos)
            tab_scr[2, :, hs] = jnp.exp(lf * (pos + 1.0))
            tab_scr[3, :, hs] = jnp.exp(lb * (C - pos))
            tab_scr[4, :, hs] = jnp.exp(lf * C)
            tab_scr[5, :, hs] = jnp.exp(lb * C)

    def rows(c):
        return pl.ds(pl.multiple_of(c * C, C), C)

    def part(p):
        return slice(c0 + p * WIDTH_RET, c0 + (p + 1) * WIDTH_RET)

    W = WIDTH_RET
    head_shift = d.bit_length() - 1
    assert d == 1 << head_shift
    same_head = ((lax.broadcasted_iota(jnp.int32, (W, W), 0) >> head_shift)
                 == (lax.broadcasted_iota(jnp.int32, (W, W), 1) >> head_shift))
    lane_head = lax.broadcasted_iota(jnp.int32, (C, W), 1) >> head_shift
    head_mean = jnp.where(same_head, 1.0 / d, 0.0).astype(BF16)

    def upd(c, carry):
        k = z_ref[rows(c), part(1)].astype(F32) * SCALE
        v = z_ref[rows(c), part(2)]
        u_scr[0, c] = jnp.where(same_head, _mm((k * tab_scr[0]).T, v), 0.0)
        u_scr[1, c] = jnp.where(same_head, _mm((k * tab_scr[1]).T, v), 0.0)
        return carry

    lax.fori_loop(0, nc, upd, 0)

    for direction in range(2):
        if s0_ref is None:
            s_init = jnp.zeros((W, W), F32)
        else:
            u_scr[direction, nc] = jnp.zeros((W, W), F32)
            for h in range(N_HEADS_RET):
                u_scr[direction, nc, h * d:(h + 1) * d, h * d:(h + 1) * d] = s0_ref[0, 0, direction, h]
            s_init = u_scr[direction, nc]
        chunk_decay = tab_scr[4 + direction, 0:1, :]

        def scan(t, s):
            c = t if direction == 0 else nc - 1 - t
            s_scr[direction, c] = s.astype(BF16)
            return chunk_decay * s + u_scr[direction, c]

        s_fin = lax.fori_loop(0, nc, scan, s_init)
        if st_ref is not None:
            for h in range(N_HEADS_RET):
                st_ref[0, direction, h] = s_fin[h * d:(h + 1) * d, h * d:(h + 1) * d]

    def out(c, carry):
        q = z_ref[rows(c), part(0)].astype(BF16)
        k = (z_ref[rows(c), part(1)].astype(F32) * SCALE).astype(BF16)
        v = z_ref[rows(c), part(2)].astype(BF16)
        gate = z_ref[rows(c), part(3)].astype(F32)
        o = (jnp.dot(q, s_scr[0, c], preferred_element_type=F32) * tab_scr[2]
             + jnp.dot(q, s_scr[1, c], preferred_element_type=F32) * tab_scr[3])
        intra = [_mm_nt(jnp.where(lane_head == h, q, jnp.zeros_like(q)), k) * d_scr[h]
                 for h in range(N_HEADS_RET)]
        for h in range(N_HEADS_RET):
            o = o + jnp.where(lane_head == h, _mm(intra[h], v), 0.0)
        sq = o * o
        sq_hi = sq.astype(BF16)
        sq_lo = (sq - sq_hi.astype(F32)).astype(BF16)
        ms = (jnp.dot(sq_hi, head_mean, preferred_element_type=F32)
              + jnp.dot(sq_lo, head_mean, preferred_element_type=F32))
        o = o * lax.rsqrt(ms + EPS) * ng_ref[...]
        o_ref[rows(c), :] = (o * _silu(gate)).astype(o_ref.dtype)
        return carry

    lax.fori_loop(0, nc, out, 0)


def _ret_scratch(n_tok):
    nc = n_tok // RET_CHUNK
    return [pltpu.VMEM((6, RET_CHUNK, WIDTH_RET), F32),
            pltpu.VMEM((N_HEADS_RET, RET_CHUNK, RET_CHUNK), F32),
            pltpu.VMEM((2, nc + 1, WIDTH_RET, WIDTH_RET), F32),
            pltpu.VMEM((2, nc, WIDTH_RET, WIDTH_RET), BF16)]


def _ctx_attn_body(sink_ref, decay_ref, z_ref, zr_ref, ng_ref, *refs, seq, n_prev, layer, depth):
    (ona_ref, oswa_ref, oret_ref, kna_ref, vna_ref, kswa_ref, vswa_ref, st_ref,
     tab_scr, d_scr, u_scr, s_scr) = refs[n_prev:]
    cache_refs = (kna_ref, vna_ref, kswa_ref, vswa_ref, st_ref)
    if n_prev == 0:
        for ref in cache_refs:
            for other in range(depth):
                if other != layer:
                    ref[0, other] = jnp.zeros(ref.shape[2:], ref.dtype)
        kna_ref, vna_ref, kswa_ref, vswa_ref, st_ref = (ref.at[:, pl.ds(layer, 1)] for ref in cache_refs)
    d = HEAD_DIM
    low_half = lax.broadcasted_iota(jnp.int32, (seq, 2 * d), 1) < d

    def pair_scores(q2, k2, head_is_low):
        return _mm_nt(jnp.where(low_half == head_is_low, q2, jnp.zeros_like(q2)), k2)

    jobs = []
    for pair in range(N_HEADS_NA // 2):
        cols = lambda part: slice(OFF_NA + part * WIDTH_NA + pair * 2 * d,
                                  OFF_NA + part * WIDTH_NA + (pair + 1) * 2 * d)
        q2 = (z_ref[:, cols(0)] * SCALE).astype(BF16)
        k2 = z_ref[:, cols(1)]
        v2 = z_ref[:, cols(2)]
        k2_t = k2.T
        v2_t = v2.T
        for hh in range(2):
            kna_ref[0, 0, 2 * pair + hh] = k2_t[hh * d:(hh + 1) * d, :]
            vna_ref[0, 0, 2 * pair + hh] = v2_t[hh * d:(hh + 1) * d, :]
        k2 = k2.astype(BF16)
        v2 = v2.astype(BF16)
        jobs.append([(pair_scores(q2, k2, hh == 0), v2, hh == 0, None) for hh in range(2)]
                    + [ona_ref, pair])
    lo = OFF_SWA + WIDTH_SWA
    k2 = z_ref[:, lo:lo + WIDTH_KV_SWA]
    v2 = z_ref[:, lo + WIDTH_KV_SWA:lo + 2 * WIDTH_KV_SWA]
    k2_t = k2.T
    v2_t = v2.T
    for g in range(N_KV_SWA):
        kswa_ref[0, 0, g] = k2_t[g * d:(g + 1) * d, :]
        vswa_ref[0, 0, g] = v2_t[g * d:(g + 1) * d, :]
    k2 = k2.astype(BF16)
    v2 = v2.astype(BF16)
    group = N_HEADS_SWA // N_KV_SWA
    for pair in range(N_HEADS_SWA // 2):
        q2 = (z_ref[:, OFF_SWA + pair * 2 * d:OFF_SWA + (pair + 1) * 2 * d] * SCALE).astype(BF16)
        heads = []
        for hh in range(2):
            h = 2 * pair + hh
            g = h // group
            qh = q2[:, hh * d:(hh + 1) * d]
            q_at_g = jnp.concatenate([qh, qh], axis=-1) if g != hh else q2
            heads.append((pair_scores(q_at_g, k2, g == 0), v2, g == 0, sink_ref[h]))
        jobs.append(heads + [oswa_ref, pair])
    probs = []
    for job in jobs:
        for s, v2, v_is_low, sink in job[:2]:
            m = jnp.max(s, axis=-1, keepdims=True)
            if sink is not None:
                m = jnp.maximum(m, sink)
            e = jnp.exp(s - m)
            den = jnp.sum(e, axis=-1, keepdims=True)
            if sink is not None:
                den = den + jnp.exp(sink - m)
            probs.append((e, den))
    for j, job in enumerate(jobs):
        halves = []
        for hh, (s, v2, v_is_low, sink) in enumerate(job[:2]):
            e, den = probs[2 * j + hh]
            o2 = _mm(e, v2) / den
            halves.append(o2[:, :d] if v_is_low else o2[:, d:])
        o_ref, pair = job[2], job[3]
        o_ref[:, pair * 2 * d:(pair + 1) * 2 * d] = jnp.concatenate(halves, axis=-1).astype(o_ref.dtype)
    _retention(zr_ref, 0, decay_ref, ng_ref, None, oret_ref, st_ref.at[0],
               tab_scr, d_scr, u_scr, s_scr, seq)


def _ctx_mixers(z_att, z_rest, swa_sink, ret_decay, ret_norm_g, caches, layer, depth, batch, seq):
    n = batch * seq
    assert z_att.shape[1] == OFF_RET
    smem = pl.BlockSpec(memory_space=pltpu.SMEM)
    first = caches is None
    layers_in_block = depth if first else 1
    block_layer = 0 if first else layer
    kv = lambda heads: pl.BlockSpec((1, layers_in_block, heads, HEAD_DIM, seq),
                                    lambda b: (b, block_layer, 0, 0, 0))
    kv_shape = lambda heads: jax.ShapeDtypeStruct((batch, depth, heads, HEAD_DIM, seq), F32)
    in_specs = [smem, smem,
                pl.BlockSpec((seq, OFF_RET), lambda b: (b, 0)),
                pl.BlockSpec((seq, 4 * WIDTH_RET), lambda b: (b, 0)),
                pl.BlockSpec((1, WIDTH_RET), lambda b: (0, 0))]
    args = [swa_sink, ret_decay, z_att, z_rest, ret_norm_g.reshape(1, WIDTH_RET)]
    n_prev = 0 if caches is None else len(caches)
    aliases = {len(args) + i: 3 + i for i in range(n_prev)}
    if caches is not None:
        in_specs += [pl.BlockSpec(memory_space=pl.ANY)] * n_prev
        args += list(caches)
    return pl.pallas_call(
        functools.partial(_ctx_attn_body, seq=seq, n_prev=n_prev, layer=layer, depth=depth),
        grid=(batch,),
        in_specs=in_specs,
        out_specs=[pl.BlockSpec((seq, WIDTH_NA), lambda b: (b, 0)),
                   pl.BlockSpec((seq, WIDTH_SWA), lambda b: (b, 0)),
                   pl.BlockSpec((seq, WIDTH_RET), lambda b: (b, 0)),
                   kv(N_HEADS_NA), kv(N_HEADS_NA), kv(N_KV_SWA), kv(N_KV_SWA),
                   pl.BlockSpec((1, layers_in_block, 2, N_HEADS_RET, HEAD_DIM, HEAD_DIM),
                                lambda b: (b, block_layer, 0, 0, 0, 0))],
        out_shape=[jax.ShapeDtypeStruct((n, WIDTH_NA), BF16),
                   jax.ShapeDtypeStruct((n, WIDTH_SWA), BF16),
                   jax.ShapeDtypeStruct((n, WIDTH_RET), BF16),
                   kv_shape(N_HEADS_NA), kv_shape(N_HEADS_NA), kv_shape(N_KV_SWA), kv_shape(N_KV_SWA),
                   jax.ShapeDtypeStruct((batch, depth, 2, N_HEADS_RET, HEAD_DIM, HEAD_DIM), F32)],
        scratch_shapes=_ret_scratch(seq),
        input_output_aliases=aliases,
        compiler_params=_params(("arbitrary",), 32),
        name="ctx_mixers",
    )(*args)


NA_QROWS = 4
NA_WROWS = NA_QROWS + WIN_ROWS


def _na_bias_table(rpb, rows):
    n_groups = rows // NA_QROWS
    assert rows % NA_QROWS == 0 and n_groups >= 3
    n_heads, n_ro, n_co = rpb.shape
    half = WIN_ROWS // 2
    cc = np.arange(GRID_W)
    col_start = np.clip(cc - WIN_COLS // 2, 0, GRID_W - WIN_COLS)
    valid = (cc[None, :] >= col_start[:, None]) & (cc[None, :] < col_start[:, None] + WIN_COLS)
    span = 2 * GRID_W - 1
    lead = GRID_W - WIN_COLS
    ext = jnp.pad(rpb.astype(F32), ((0, 0), (0, 0), (lead, span + 1 - lead - n_co)))
    band = jnp.tile(ext, (1, 1, GRID_W))[..., :GRID_W * span].reshape(n_heads, n_ro, GRID_W, span)
    band = jnp.where(valid[None, None], band[..., GRID_W - 1:], MASKED)
    band = jnp.concatenate([band, jnp.full((n_heads, 1, GRID_W, GRID_W), MASKED, F32)], axis=1)
    slab = np.full((3, NA_QROWS, NA_WROWS), n_ro, np.int32)
    for cls, g in enumerate((0, 1, n_groups - 1)):
        ws = min(max(NA_QROWS * g - half, 0), rows - NA_WROWS)
        for i in range(NA_QROWS):
            r = NA_QROWS * g + i
            rs = min(max(r - half, 0), rows - WIN_ROWS)
            for j in range(NA_WROWS):
                if rs <= ws + j < rs + WIN_ROWS:
                    slab[cls, i, j] = ws + j - r + WIN_ROWS - 1

    def assemble(slab_ref, band_ref, o_ref):
        cls = pl.program_id(0)
        for i in range(NA_QROWS):
            for j in range(NA_WROWS):
                o_ref[0, 0, i * GRID_W:(i + 1) * GRID_W, j * GRID_W:(j + 1) * GRID_W] = (
                    band_ref[0, slab_ref[(cls * NA_QROWS + i) * NA_WROWS + j]])

    return pl.pallas_call(
        assemble,
        grid_spec=pltpu.PrefetchScalarGridSpec(
            num_scalar_prefetch=1,
            grid=(3, n_heads),
            in_specs=[pl.BlockSpec((1, n_ro + 1, GRID_W, GRID_W), lambda c, h, *_: (h, 0, 0, 0))],
            out_specs=pl.BlockSpec((1, 1, NA_QROWS * GRID_W, NA_WROWS * GRID_W), lambda c, h, *_: (c, h, 0, 0))),
        out_shape=jax.ShapeDtypeStruct((3, n_heads, NA_QROWS * GRID_W, NA_WROWS * GRID_W), F32),
        compiler_params=_params(("arbitrary", "arbitrary"), 32),
        name="na_bias_table",
    )(jnp.asarray(slab.reshape(-1)), band)


def _lat_na_body(z_ref, kc_ref, vc_ref, bias_ref, o_ref, kc_scr, vc_scr, *, rows):
    d = HEAD_DIM
    nq = NA_QROWS * GRID_W
    g = pl.program_id(1)

    @pl.when(g == 0)
    def _():
        kc_scr[...] = kc_ref[0, 0].astype(BF16)
        vc_scr[...] = vc_ref[0, 0].astype(BF16)

    ws = jnp.clip(NA_QROWS * g - WIN_ROWS // 2, 0, rows - NA_WROWS)
    qrows = pl.ds(pl.multiple_of(g * nq, nq), nq)
    wrows = pl.ds(pl.multiple_of(ws * GRID_W, GRID_W), NA_WROWS * GRID_W)
    low_half = lax.broadcasted_iota(jnp.int32, (nq, 2 * d), 1) < d
    for pair in range(N_HEADS_NA // 2):
        cols = slice(pair * 2 * d, (pair + 1) * 2 * d)
        q2 = (z_ref[qrows, cols] * SCALE).astype(BF16)
        kl2 = z_ref[wrows, WIDTH_NA + pair * 2 * d:WIDTH_NA + (pair + 1) * 2 * d].astype(BF16)
        vl2 = z_ref[wrows, 2 * WIDTH_NA + pair * 2 * d:2 * WIDTH_NA + (pair + 1) * 2 * d].astype(BF16)
        for hh in range(2):
            h = 2 * pair + hh
            qm = jnp.where(low_half == (hh == 0), q2, jnp.zeros_like(q2))
            s_loc = _mm_nt(qm, kl2) + bias_ref[0, h]
            s_ctx = _mm(q2[:, hh * d:(hh + 1) * d], kc_scr[h])
            m = jnp.maximum(jnp.max(s_ctx, axis=-1, keepdims=True), jnp.max(s_loc, axis=-1, keepdims=True))
            e_ctx = jnp.exp(s_ctx - m)
            e_loc = jnp.exp(s_loc - m)
            den = jnp.sum(e_ctx, axis=-1, keepdims=True) + jnp.sum(e_loc, axis=-1, keepdims=True)
            o = (_mm_nt(e_ctx, vc_scr[h]) + _mm(e_loc, vl2)[:, hh * d:(hh + 1) * d]) / den
            o_ref[:, h * d:(h + 1) * d] = o.astype(o_ref.dtype)


def _lat_na(z, cache_kt, cache_vt, bias_tab, layer, batch, seq):
    rows = seq // GRID_W
    n_groups = rows // NA_QROWS
    nq = NA_QROWS * GRID_W
    past = cache_kt.shape[4]

    def group_class(g):
        return jnp.where(g == 0, 0, jnp.where(g == n_groups - 1, 2, 1))

    cache = pl.BlockSpec((1, 1, N_HEADS_NA, HEAD_DIM, past), lambda b, g: (b, layer, 0, 0, 0))
    return pl.pallas_call(
        functools.partial(_lat_na_body, rows=rows),
        grid=(batch, n_groups),
        in_specs=[pl.BlockSpec((seq, 3 * WIDTH_NA), lambda b, g: (b, 0)),
                  cache, cache,
                  pl.BlockSpec((1, N_HEADS_NA, nq, NA_WROWS * GRID_W),
                               lambda b, g: (group_class(g), 0, 0, 0))],
        out_specs=pl.BlockSpec((nq, WIDTH_NA), lambda b, g: (b * n_groups + g, 0)),
        out_shape=jax.ShapeDtypeStruct((batch * seq, WIDTH_NA), BF16),
        scratch_shapes=[pltpu.VMEM((N_HEADS_NA, HEAD_DIM, past), BF16),
                        pltpu.VMEM((N_HEADS_NA, HEAD_DIM, past), BF16)],
        compiler_params=_params(("arbitrary", "arbitrary"), 58),
        name="lat_na",
    )(z, cache_kt, cache_vt, bias_tab)


def _rope_tables(seq):
    t = jnp.arange(seq)
    n_freq = HEAD_DIM // 4
    inv = ROPE_BASE ** (-jnp.arange(n_freq, dtype=F32) / n_freq)
    ang = jnp.concatenate([(t // GRID_W).astype(F32)[:, None] * inv,
                           (t % GRID_W).astype(F32)[:, None] * inv], axis=-1)
    cos = jnp.repeat(jnp.cos(ang), 2, axis=-1)
    sign = jnp.tile(jnp.asarray([-1.0, 1.0], F32), HEAD_DIM // 2)
    sin = jnp.repeat(jnp.sin(ang), 2, axis=-1) * sign
    return jnp.tile(cos, (1, N_HEADS_SWA)), jnp.tile(sin, (1, N_HEADS_SWA))


def _rope(x, cos, sin_signed):
    width = x.shape[-1]
    lane = lax.broadcasted_iota(jnp.int32, x.shape, x.ndim - 1)
    partner = jnp.where((lane & 1) == 0, pltpu.roll(x, width - 1, x.ndim - 1), pltpu.roll(x, 1, x.ndim - 1))
    return x * cos + partner * sin_signed


def _lat_swa_body(sink_ref, z_ref, kc_ref, vc_ref, cos_ref, sin_ref, o_ref, kr_scr, kc_scr, vc_scr, *, seq):
    d = HEAD_DIM
    blk = SWA_BLOCK
    n = pl.program_id(1)
    assert N_KV_SWA == 2

    @pl.when(n == 0)
    def _():
        k = z_ref[:, WIDTH_SWA:WIDTH_SWA + WIDTH_KV_SWA].astype(F32)
        kr_scr[...] = _rope(k, cos_ref[:, :WIDTH_KV_SWA], sin_ref[:, :WIDTH_KV_SWA]).astype(BF16)
        kc_scr[...] = jnp.concatenate([kc_ref[0, 0, 0], kc_ref[0, 0, 1]], axis=0).astype(BF16)
        vc_scr[...] = jnp.concatenate([vc_ref[0, 0, 0], vc_ref[0, 0, 1]], axis=0).astype(BF16)

    qrows = pl.ds(pl.multiple_of(n * blk, blk), blk)
    kstart = jnp.clip((n - 1) * blk, 0, seq - 3 * blk)
    krows = pl.ds(pl.multiple_of(kstart, blk), 3 * blk)
    q_all = _rope(z_ref[qrows, 0:WIDTH_SWA].astype(F32), cos_ref[qrows, :], sin_ref[qrows, :]) * SCALE
    q_all = q_all.astype(BF16)
    qpos = n * blk + lax.broadcasted_iota(jnp.int32, (blk, 3 * blk), 0)
    kpos = kstart + lax.broadcasted_iota(jnp.int32, (blk, 3 * blk), 1)
    valid = jnp.abs(qpos - kpos) <= SWA_RADIUS
    kl2 = kr_scr[krows, :]
    vl2 = z_ref[krows, WIDTH_SWA + WIDTH_KV_SWA:WIDTH_SWA + 2 * WIDTH_KV_SWA]
    group = N_HEADS_SWA // N_KV_SWA
    zero = jnp.zeros((blk, d), BF16)
    scores = []
    for h in range(N_HEADS_SWA):
        g = h // group
        qh = q_all[:, h * d:(h + 1) * d]
        q_at_g = jnp.concatenate([qh, zero] if g == 0 else [zero, qh], axis=-1)
        scores.append((jnp.where(valid, _mm_nt(q_at_g, kl2), MASKED), _mm(q_at_g, kc_scr[...])))
    probs = []
    for h, (s_loc, s_ctx) in enumerate(scores):
        m = jnp.maximum(jnp.maximum(jnp.max(s_loc, axis=-1, keepdims=True),
                                    jnp.max(s_ctx, axis=-1, keepdims=True)), sink_ref[h])
        e_loc = jnp.exp(s_loc - m)
        e_ctx = jnp.exp(s_ctx - m)
        den = (jnp.sum(e_loc, axis=-1, keepdims=True) + jnp.sum(e_ctx, axis=-1, keepdims=True)
               + jnp.exp(sink_ref[h] - m))
        probs.append((e_loc, e_ctx, den))
    outs = []
    for h, (e_loc, e_ctx, den) in enumerate(probs):
        g = h // group
        o2 = (_mm_nt(e_ctx, vc_scr[...]) + _mm(e_loc, vl2)) / den
        outs.append(o2[:, g * d:(g + 1) * d])
    o_ref[...] = jnp.concatenate(outs, axis=-1).astype(o_ref.dtype)


def _lat_swa(z, cache_kt, cache_vt, swa_sink, cos, sin, layer, batch, seq):
    nb = seq // SWA_BLOCK
    past = cache_kt.shape[4]
    width = WIDTH_SWA + 2 * WIDTH_KV_SWA
    assert OFF_SWA % width == 0 and seq >= 3 * SWA_BLOCK
    cache = pl.BlockSpec((1, 1, N_KV_SWA, HEAD_DIM, past), lambda b, n: (b, layer, 0, 0, 0))
    table = pl.BlockSpec((seq, WIDTH_SWA), lambda b, n: (0, 0))
    return pl.pallas_call(
        functools.partial(_lat_swa_body, seq=seq),
        grid=(batch, nb),
        in_specs=[pl.BlockSpec(memory_space=pltpu.SMEM),
                  pl.BlockSpec((seq, width), lambda b, n: (b, OFF_SWA // width)),
                  cache, cache, table, table],
        out_specs=pl.BlockSpec((SWA_BLOCK, WIDTH_SWA), lambda b, n: (b * nb + n, 0)),
        out_shape=jax.ShapeDtypeStruct((batch * seq, WIDTH_SWA), BF16),
        scratch_shapes=[pltpu.VMEM((seq, WIDTH_KV_SWA), BF16),
                        pltpu.VMEM((WIDTH_KV_SWA, past), BF16), pltpu.VMEM((WIDTH_KV_SWA, past), BF16)],
        compiler_params=_params(("arbitrary", "arbitrary"), 32),
        name="lat_swa",
    )(swa_sink, z, cache_kt, cache_vt, cos, sin)


def _lat_ret_body(decay_ref, z_ref, ng_ref, s0_ref, o_ref, tab_scr, d_scr, u_scr, s_scr, *, seq):
    _retention(z_ref, 0, decay_ref, ng_ref, s0_ref, o_ref, None, tab_scr, d_scr, u_scr, s_scr, seq)


def _lat_ret(z, ret_decay, ret_norm_g, state, layer, batch, seq):
    width = 4 * WIDTH_RET
    assert OFF_RET % width == 0
    return pl.pallas_call(
        functools.partial(_lat_ret_body, seq=seq),
        grid=(batch,),
        in_specs=[pl.BlockSpec(memory_space=pltpu.SMEM),
                  pl.BlockSpec((seq, width), lambda b: (b, OFF_RET // width)),
                  pl.BlockSpec((1, WIDTH_RET), lambda b: (0, 0)),
                  pl.BlockSpec((1, 1, 2, N_HEADS_RET, HEAD_DIM, HEAD_DIM),
                               lambda b: (b, layer, 0, 0, 0, 0))],
        out_specs=pl.BlockSpec((seq, WIDTH_RET), lambda b: (b, 0)),
        out_shape=jax.ShapeDtypeStruct((batch * seq, WIDTH_RET), BF16),
        scratch_shapes=_ret_scratch(seq),
        compiler_params=_params(("arbitrary",), 40),
        name="lat_ret",
    )(ret_decay, z, ret_norm_g.reshape(1, WIDTH_RET), state)


def _merge_body(ona_ref, oswa_ref, oret_ref, ga_ref, gs_ref, gr_ref, x_ref, mod_ref, g2_ref,
                wna_ref, wswa_ref, wret_ref, wout_ref, rw_ref, rb_ref,
                hx_ref, wna_scr, wswa_scr, wret_scr, wout_scr, rw_scr, *, per_cond):
    @pl.when(pl.program_id(0) == 0)
    def _():
        wna_scr[...] = wna_ref[0].astype(BF16)
        wswa_scr[...] = wswa_ref[0].astype(BF16)
        wret_scr[...] = wret_ref[0].astype(BF16)
        wout_scr[...] = wout_ref[0].astype(BF16)
        rw = rw_ref[0]
        rw_hi = rw.astype(BF16)
        rw_scr[0] = rw_hi
        rw_scr[1] = (rw - rw_hi.astype(F32)).astype(BF16)

    D = D_MODEL
    m = mod_ref[0]
    sub = x_ref.shape[0] // MERGE_SPLIT
    parts = [pl.ds(s * sub, sub) for s in range(MERGE_SPLIT)]
    gate = lambda ref, rows: _sigmoid(ref[rows, :].astype(F32))
    zs = [(gate(ga_ref, r) * jnp.dot(ona_ref[r, :], wna_scr[...], preferred_element_type=F32)
           + gate(gs_ref, r) * jnp.dot(oswa_ref[r, :], wswa_scr[...], preferred_element_type=F32)
           + gate(gr_ref, r) * jnp.dot(oret_ref[r, :], wret_scr[...], preferred_element_type=F32)
           ).astype(BF16) for r in parts]
    ys = [jnp.dot(z, wout_scr[...], preferred_element_type=F32) for z in zs]
    logits = []
    for r, y in zip(parts, ys):
        x1 = x_ref[r, :] + m[:, 2 * D:3 * D] * y
        hx_ref[r, HX_X1:HX_X1 + D] = x1
        h2 = x1 * lax.rsqrt(jnp.mean(x1 * x1, axis=-1, keepdims=True) + EPS) * g2_ref[0]
        h2 = h2 * (1.0 + m[:, 4 * D:5 * D]) + m[:, 3 * D:4 * D]
        hx_ref[r, 0:D] = h2
        h_hi = h2.astype(BF16)
        h_lo = (h2 - h_hi.astype(F32)).astype(BF16)
        logits.append((jnp.dot(h_hi, rw_scr[0], preferred_element_type=F32)
                       + jnp.dot(h_hi, rw_scr[1], preferred_element_type=F32)
                       + jnp.dot(h_lo, rw_scr[0], preferred_element_type=F32)) + rb_ref[0])
    for r, logit in zip(parts, logits):
        route = _route(logit)
        cond = (pl.program_id(0) * x_ref.shape[0] // per_cond).astype(F32)
        lane = lax.broadcasted_iota(jnp.int32, route.shape, 1)
        hx_ref[r, D:D + ROUTER_LANES] = jnp.where(lane == ROUTE_COND_LANE, cond, route)


MERGE_SPLIT = 2


def _route(logit):
    lane = lax.broadcasted_iota(jnp.int32, logit.shape, 1).astype(F32)
    neg = jnp.float32(-jnp.inf)
    big = jnp.float32(ROUTER_LANES)
    is_group = lane < N_GROUPS
    is_expert = (lane >= N_GROUPS) & (lane < N_GROUPS + N_EXPERTS)
    gl = jnp.where(is_group, logit, neg)
    gmax = jnp.max(gl, axis=-1, keepdims=True)
    gsel = jnp.min(jnp.where(gl == gmax, lane, big), axis=-1, keepdims=True)
    p_group = 1.0 / jnp.sum(jnp.where(is_group, jnp.exp(gl - gmax), 0.0), axis=-1, keepdims=True)
    eidx = lane - N_GROUPS
    in_group = is_expert & (jnp.floor(eidx / EXPERTS_PER_GROUP) == gsel)
    el = jnp.where(in_group, logit, neg)
    top1 = jnp.max(el, axis=-1, keepdims=True)
    sel1 = jnp.min(jnp.where(el == top1, lane, big), axis=-1, keepdims=True)
    el2 = jnp.where(lane == sel1, neg, el)
    top2 = jnp.max(el2, axis=-1, keepdims=True)
    sel2 = jnp.min(jnp.where(el2 == top2, lane, big), axis=-1, keepdims=True)
    e2 = jnp.exp(top2 - top1)
    w1 = p_group / (1.0 + e2)
    w2 = p_group * e2 / (1.0 + e2)
    comb = jnp.where(lane == sel1, w1, 0.0) + jnp.where(lane == sel2, w2, 0.0)
    return comb + jnp.where(lane == ROUTE_GROUP_LANE, gsel, 0.0)


def _merge(o_na, o_swa, o_ret, z, gate_tile, x, mod_rows, norm2_g, w_br_na, w_br_swa, w_br_ret, w_out,
           router_w, router_b, layer):
    n, d = x.shape
    n_cond = mod_rows.shape[0]
    per_cond = n // n_cond
    tm = 512
    gate = lambda k: pl.BlockSpec((tm, d), lambda i: (i, gate_tile + k))
    row = lambda w: pl.BlockSpec((tm, w), lambda i: (i, 0))
    weight = lambda k: pl.BlockSpec((1, k, d), lambda i: (layer, 0, 0))
    return pl.pallas_call(
        functools.partial(_merge_body, per_cond=per_cond),
        grid=(n // tm,),
        in_specs=[row(WIDTH_NA), row(WIDTH_SWA), row(WIDTH_RET),
                  gate(0), gate(1), gate(2),
                  row(d),
                  pl.BlockSpec((1, 1, 6 * d), lambda i: (i * tm // per_cond, 0, 0)),
                  pl.BlockSpec((1, 1, d), lambda i: (layer, 0, 0)),
                  weight(WIDTH_NA), weight(WIDTH_SWA), weight(WIDTH_RET), weight(d),
                  pl.BlockSpec((1, d, ROUTER_LANES), lambda i: (layer, 0, 0)),
                  pl.BlockSpec((1, 1, ROUTER_LANES), lambda i: (layer, 0, 0))],
        out_specs=row(HX_WIDTH),
        out_shape=jax.ShapeDtypeStruct((n, HX_WIDTH), F32),
        scratch_shapes=[pltpu.VMEM((WIDTH_NA, d), BF16), pltpu.VMEM((WIDTH_SWA, d), BF16),
                        pltpu.VMEM((WIDTH_RET, d), BF16), pltpu.VMEM((d, d), BF16),
                        pltpu.VMEM((2, d, ROUTER_LANES), BF16)],
        compiler_params=_params(("arbitrary",), 56),
        name="merge_router",
    )(o_na, o_swa, o_ret, z, z, z, x, mod_rows, norm2_g.reshape(norm2_g.shape[0], 1, d),
      w_br_na, w_br_swa, w_br_ret, w_out, router_w, router_b)


MOE_TILE = 512


def _route_plan(hx, n_tiles):
    n = hx.shape[0]
    g = hx[:, D_MODEL + ROUTE_GROUP_LANE].astype(jnp.int32)
    onehot = (g[:, None] == jnp.arange(N_GROUPS, dtype=jnp.int32)[None, :]).astype(jnp.int32)
    csum = jnp.cumsum(onehot, axis=0)
    counts = csum[-1]
    ends = jnp.cumsum(counts)
    dest = jnp.sum(onehot * ((ends - counts)[None, :] + csum - 1), axis=1)
    first = jnp.arange(n_tiles, dtype=jnp.int32) * MOE_TILE
    tile_lo = jnp.sum((first[:, None] >= ends[None, :]).astype(jnp.int32), axis=1)
    tile_hi = jnp.sum(((first + MOE_TILE - 1)[:, None] >= ends[None, :]).astype(jnp.int32), axis=1)
    return dest, tile_lo, tile_hi


def _moe_body(dest_ref, lo_ref, hi_ref, hx_hbm, wg_hbm, wu_hbm, wd_hbm, mod_ref, fg_ref, out_hbm,
              order_scr, hbuf, obuf, y_scr, wg_scr, wu_scr, wd_scr, in_sem, out_sem, w_sem,
              *, layer, n_tiles, n_tok, n_cond, final):
    step = pl.program_id(0)
    last_step = pl.num_programs(0) - 1
    rows_per_expert = MOE_TILE // EXPERTS_PER_GROUP

    def in_copy(tile, i, slot):
        return pltpu.make_async_copy(hx_hbm.at[pl.ds(order_scr[tile * MOE_TILE + i], 1)],
                                     hbuf.at[slot, pl.ds(i, 1)], in_sem.at[slot])

    def out_copy(tile, i, slot):
        return pltpu.make_async_copy(obuf.at[slot, pl.ds(i, 1)],
                                     out_hbm.at[pl.ds(order_scr[tile * MOE_TILE + i], 1)], out_sem.at[slot])

    def for_rows(fn):
        def body(i, carry):
            fn(i)
            return carry
        lax.fori_loop(0, MOE_TILE, body, 0, unroll=8)

    @pl.when(step == 0)
    def _():
        copies = [pltpu.make_async_copy(src.at[layer], dst, w_sem.at[k])
                  for k, (src, dst) in enumerate(((wg_hbm, wg_scr), (wu_hbm, wu_scr), (wd_hbm, wd_scr)))]
        for copy in copies:
            copy.start()

        def invert(i, carry):
            order_scr[dest_ref[i]] = i
            return carry
        lax.fori_loop(0, n_tok, invert, 0, unroll=16)
        for_rows(lambda i: in_copy(0, i, 0).start())
        for copy in copies:
            copy.wait()

    def run_tile(tile, slot, nxt, write_back):
        other = 1 - slot
        for_rows(lambda i: in_copy(tile, i, slot).wait())
        hx = hbuf[slot]
        h = hx[:, 0:D_MODEL].astype(BF16)
        route = hx[:, D_MODEL:D_MODEL + ROUTER_LANES]
        lane = lax.broadcasted_iota(jnp.int32, route.shape, 1)

        def group(k, copy_back):
            first = (lo_ref[tile] + k) * EXPERTS_PER_GROUP
            y = None
            for j in range(EXPERTS_PER_GROUP):
                e = first + j
                hg = jnp.dot(h, wg_scr[e], preferred_element_type=F32)
                hu = jnp.dot(h, wu_scr[e], preferred_element_type=F32)
                c = jnp.sum(jnp.where(lane == ROUTE_EXPERT_LANE + e, route, 0.0), axis=-1, keepdims=True)
                a = _silu(hg) * hu * c
                part = jnp.dot(a.astype(BF16), wd_scr[e], preferred_element_type=F32)
                y = part if y is None else y + part
                if k == 0:
                    for i in range(j * rows_per_expert, (j + 1) * rows_per_expert):
                        in_copy(nxt, i, other).start(priority=i % 2)
                        if copy_back:
                            out_copy(tile - 1, i, other).start(priority=i % 2)
            if k == 0:
                y_scr[...] = y
            else:
                y_scr[...] += y

        if write_back is True:
            group(0, True)
        else:
            pl.when(write_back)(functools.partial(group, 0, True))
            pl.when(jnp.logical_not(write_back))(functools.partial(group, 0, False))
        for k in range(1, N_GROUPS):
            pl.when(lo_ref[tile] + k <= hi_ref[tile])(functools.partial(group, k, False))

        cond = route[:, ROUTE_COND_LANE:ROUTE_COND_LANE + 1]
        gate2 = jnp.zeros((1, D_MODEL), F32)
        for c in range(n_cond):
            gate2 = jnp.where(cond == c, mod_ref[c][:, 5 * D_MODEL:6 * D_MODEL], gate2)
        x2 = hx[:, HX_X1:HX_X1 + D_MODEL] + gate2 * y_scr[...]
        if final:
            x2 = x2 * lax.rsqrt(jnp.mean(x2 * x2, axis=-1, keepdims=True) + EPS) * fg_ref[...]
        obuf[slot] = x2
        if write_back is True:
            for_rows(lambda i: out_copy(tile - 1, i, other).wait())
        else:
            @pl.when(write_back)
            def _():
                for_rows(lambda i: out_copy(tile - 1, i, other).wait())

    first_tile = 2 * step
    second_tile = first_tile + 1
    after = jnp.minimum(second_tile + 1, n_tiles - 1)
    run_tile(first_tile, 0, second_tile, step > 0)
    run_tile(second_tile, 1, after, True)

    @pl.when(step == last_step)
    def _():
        for_rows(lambda i: in_copy(after, i, 0).wait())
        for_rows(lambda i: out_copy(second_tile, i, 1).start())
        for_rows(lambda i: out_copy(second_tile, i, 1).wait())


def _moe(hx, mod_rows, w_gate, w_up, w_down, final_g, layer, final):
    n = hx.shape[0]
    d = D_MODEL
    n_tiles = n // MOE_TILE
    assert n_tiles % 2 == 0
    n_cond = mod_rows.shape[0]
    dest, tile_lo, tile_hi = _route_plan(hx, n_tiles)
    any_space = pl.BlockSpec(memory_space=pl.ANY)
    return pl.pallas_call(
        functools.partial(_moe_body, layer=layer, n_tiles=n_tiles, n_tok=n, n_cond=n_cond, final=final),
        grid_spec=pltpu.PrefetchScalarGridSpec(
            num_scalar_prefetch=3,
            grid=(n_tiles // 2,),
            in_specs=[any_space, any_space, any_space, any_space,
                      pl.BlockSpec((n_cond, 1, 6 * d), lambda t, *_: (0, 0, 0)),
                      pl.BlockSpec((1, d), lambda t, *_: (0, 0))],
            out_specs=any_space,
            scratch_shapes=[pltpu.SMEM((n,), jnp.int32),
                            pltpu.VMEM((2, MOE_TILE, HX_WIDTH), F32),
                            pltpu.VMEM((2, MOE_TILE, d), F32),
                            pltpu.VMEM((MOE_TILE, d), F32),
                            pltpu.VMEM((N_EXPERTS, d, D_EXPERT), BF16),
                            pltpu.VMEM((N_EXPERTS, d, D_EXPERT), BF16),
                            pltpu.VMEM((N_EXPERTS, D_EXPERT, d), BF16),
                            pltpu.SemaphoreType.DMA((2,)),
                            pltpu.SemaphoreType.DMA((2,)),
                            pltpu.SemaphoreType.DMA((3,))]),
        out_shape=jax.ShapeDtypeStruct((n, d), F32),
        compiler_params=_params(("arbitrary",), 56),
        name="moe",
    )(dest, tile_lo, tile_hi, hx, w_gate, w_up, w_down, mod_rows, final_g.reshape(1, d))


def kernel(x_prompt, x_sample, c, cache_na_k, cache_na_v, cache_swa_k, cache_swa_v, state_ret, c_ctx, ada_w, ada_b, norm1_g, norm2_g, w_in, na_rpb, swa_sink, ret_decay, ret_norm_g, w_br_na, w_br_swa, w_br_ret, w_out, router_group_w, router_group_b, router_expert_w, router_expert_b, w_gate, w_up, w_down, final_norm_g):
    batch, seq, d = x_prompt.shape
    dec_batch, dec_seq, _ = x_sample.shape
    depth = ada_w.shape[0]
    assert d == D_MODEL and dec_batch + 1 <= 8
    assert OFF_RET % IN_TILE == 0 and OFF_GATES % d == 0 and (OFF_GATES - OFF_RET) % d == 0

    cond = jnp.concatenate([c_ctx[None], c, jnp.zeros((8 - 1 - dec_batch, d), F32)], axis=0)
    mod = _modulation(cond, ada_w, ada_b)
    router_w = jnp.concatenate([router_group_w, router_expert_w], axis=-1)
    router_w = jnp.pad(router_w, ((0, 0), (0, 0), (0, ROUTER_LANES - router_w.shape[-1])))
    router_b = jnp.concatenate([router_group_b, router_expert_b], axis=-1)
    router_b = jnp.pad(router_b, ((0, 0), (0, ROUTER_LANES - router_b.shape[-1])))[:, None, :]
    cos, sin = _rope_tables(dec_seq)
    na_kt, na_vt, swa_kt, swa_vt = (jnp.swapaxes(t, -1, -2)
                                    for t in (cache_na_k, cache_na_v, cache_swa_k, cache_swa_v))
    moe_weights = (w_gate.astype(BF16), w_up.astype(BF16), w_down.astype(BF16))

    xp = x_prompt.reshape(batch * seq, d)
    xs = x_sample.reshape(dec_batch * dec_seq, d)
    caches = None
    for l in range(depth):
        final = l == depth - 1
        mod_ctx = mod[l, 0:1][:, None, :]
        mod_lat = mod[l, 1:1 + dec_batch][:, None, :]
        merge_w = (norm2_g, w_br_na, w_br_swa, w_br_ret, w_out, router_w, router_b, l)
        moe_w = (*moe_weights, final_norm_g, l, final)

        z_att, z_rest = _in_proj(xp, mod_ctx, norm1_g, w_in, l, OFF_RET // IN_TILE)
        o_na, o_swa, o_ret, *caches = _ctx_mixers(
            z_att, z_rest, swa_sink[l], ret_decay[l], ret_norm_g[l], caches, l, depth, batch, seq)
        hx = _merge(o_na, o_swa, o_ret, z_rest, (OFF_GATES - OFF_RET) // d, xp, mod_ctx, *merge_w)
        xp = _moe(hx, mod_ctx, *moe_w)

        z, = _in_proj(xs, mod_lat, norm1_g, w_in, l, 0)
        o_na = _lat_na(z, na_kt, na_vt, _na_bias_table(na_rpb[l], dec_seq // GRID_W),
                       l, dec_batch, dec_seq)
        o_swa = _lat_swa(z, swa_kt, swa_vt, swa_sink[l], cos, sin, l, dec_batch, dec_seq)
        o_ret = _lat_ret(z, ret_decay[l], ret_norm_g[l], state_ret, l, dec_batch, dec_seq)
        hx = _merge(o_na, o_swa, o_ret, z, OFF_GATES // d, xs, mod_lat, *merge_w)
        xs = _moe(hx, mod_lat, *moe_w)

    new_kv = [jnp.swapaxes(t, -1, -2) for t in caches[:4]]
    return (xp.reshape(batch, seq, d), xs.reshape(dec_batch, dec_seq, d), *new_kv, caches[4])
```

```python
import functools

import numpy as np
import jax
import jax.numpy as jnp
from jax import lax
from jax.experimental import pallas as pl
from jax.experimental.pallas import tpu as pltpu

F32 = jnp.float32
BF16 = jnp.bfloat16

D_MODEL = 1024
HEAD_DIM = 64
GRID_W = 64
N_HEADS_NA = 8
WIN_ROWS = 8
WIN_COLS = 16
N_HEADS_SWA = 4
N_KV_SWA = 2
SWA_RADIUS = 128
SWA_BLOCK = 128
N_HEADS_RET = 4
RET_CHUNK = 128
WIDTH_NA = N_HEADS_NA * HEAD_DIM
WIDTH_SWA = N_HEADS_SWA * HEAD_DIM
WIDTH_KV_SWA = N_KV_SWA * HEAD_DIM
WIDTH_RET = N_HEADS_RET * HEAD_DIM
IN_COLS = 3 * WIDTH_NA + WIDTH_SWA + 2 * WIDTH_KV_SWA + 4 * WIDTH_RET + 3 * D_MODEL
OFF_NA = 0
OFF_SWA = 3 * WIDTH_NA
OFF_RET = OFF_SWA + WIDTH_SWA + 2 * WIDTH_KV_SWA
OFF_GATES = OFF_RET + 4 * WIDTH_RET
N_GROUPS = 4
EXPERTS_PER_GROUP = 4
N_EXPERTS = N_GROUPS * EXPERTS_PER_GROUP
D_EXPERT = D_MODEL // 4
ROPE_BASE = 10000.0
EPS = 1e-6
SCALE = HEAD_DIM ** -0.5
MASKED = -1e30
ROUTER_LANES = 128
ROUTE_GROUP_LANE = 0
ROUTE_EXPERT_LANE = N_GROUPS
ROUTE_COND_LANE = N_GROUPS + N_EXPERTS
HX_X1 = D_MODEL + ROUTER_LANES
HX_WIDTH = HX_X1 + D_MODEL

MIB = 1024 * 1024


def _mm(a, b):
    return jnp.dot(a.astype(BF16), b.astype(BF16), preferred_element_type=F32)


def _mm_nt(a, b):
    return lax.dot_general(a.astype(BF16), b.astype(BF16), (((1,), (1,)), ((), ())),
                           preferred_element_type=F32)


def _sigmoid(x):
    return 1.0 / (1.0 + jnp.exp(-x))


def _silu(x):
    return x * _sigmoid(x)


def _params(semantics, vmem_mib):
    return pltpu.CompilerParams(dimension_semantics=semantics, vmem_limit_bytes=vmem_mib * MIB)


def _mod_body(cond_ref, w_ref, b_ref, o_ref):
    o_ref[0] = _mm(_silu(cond_ref[...]), w_ref[0]) + b_ref[0]


def _modulation(cond, ada_w, ada_b):
    depth, d, n = ada_w.shape
    tn = 2048
    return pl.pallas_call(
        _mod_body,
        grid=(depth, n // tn),
        in_specs=[pl.BlockSpec((8, d), lambda l, j: (0, 0)),
                  pl.BlockSpec((1, d, tn), lambda l, j: (l, 0, j)),
                  pl.BlockSpec((1, 1, tn), lambda l, j: (l, 0, j))],
        out_specs=pl.BlockSpec((1, 8, tn), lambda l, j: (l, 0, j)),
        out_shape=jax.ShapeDtypeStruct((depth, 8, n), F32),
        compiler_params=_params(("arbitrary", "arbitrary"), 32),
        name="modulation",
    )(cond, ada_w, ada_b.reshape(depth, 1, n))


def _inproj_body(x_ref, mod_ref, g_ref, w_ref, *refs, n_f32):
    zf_ref = refs[0] if n_f32 else None
    zb_ref, h_scr, w_scr = refs[-3:]
    i = pl.program_id(0)
    j = pl.program_id(1)

    @pl.when(i == 0)
    def _():
        w_scr[j] = w_ref[0].astype(BF16)

    @pl.when(j == 0)
    def _():
        x = x_ref[...]
        m = mod_ref[0]
        y = x * lax.rsqrt(jnp.mean(x * x, axis=-1, keepdims=True) + EPS) * g_ref[0]
        h_scr[...] = (y * (1.0 + m[:, D_MODEL:2 * D_MODEL]) + m[:, 0:D_MODEL]).astype(BF16)

    acc = jnp.dot(h_scr[...], w_scr[j], preferred_element_type=F32)
    if n_f32 == 0:
        zb_ref[...] = acc.astype(BF16)
    else:
        @pl.when(j < n_f32)
        def _():
            zf_ref[...] = acc

        @pl.when(j >= n_f32)
        def _():
            zb_ref[...] = acc.astype(BF16)


IN_TILE = 1024


def _in_proj(x, mod_rows, norm_g, w_in, layer, n_f32):
    n, d = x.shape
    n_cond = mod_rows.shape[0]
    tm, tn = 1024, IN_TILE
    nj = IN_COLS // tn
    per_cond = n // n_cond
    out_specs = [pl.BlockSpec((tm, tn), lambda i, j: (i, jnp.maximum(j - n_f32, 0)))]
    out_shape = [jax.ShapeDtypeStruct((n, IN_COLS - n_f32 * tn), BF16)]
    if n_f32:
        out_specs.insert(0, pl.BlockSpec((tm, tn), lambda i, j: (i, jnp.minimum(j, n_f32 - 1))))
        out_shape.insert(0, jax.ShapeDtypeStruct((n, n_f32 * tn), F32))
    return pl.pallas_call(
        functools.partial(_inproj_body, n_f32=n_f32),
        grid=(n // tm, nj),
        in_specs=[pl.BlockSpec((tm, d), lambda i, j: (i, 0)),
                  pl.BlockSpec((1, 1, 6 * d), lambda i, j: (i * tm // per_cond, 0, 0)),
                  pl.BlockSpec((1, 1, d), lambda i, j: (layer, 0, 0)),
                  pl.BlockSpec((1, d, tn), lambda i, j: (layer, 0, jnp.where(i == 0, j, nj - 1)))],
        out_specs=out_specs,
        out_shape=out_shape,
        scratch_shapes=[pltpu.VMEM((tm, d), BF16), pltpu.VMEM((nj, d, tn), BF16)],
        compiler_params=_params(("arbitrary", "arbitrary"), 48),
        name="in_proj",
    )(x, mod_rows, norm_g.reshape(norm_g.shape[0], 1, d), w_in)


def _log_sigmoid(x):
    return jnp.minimum(x, 0.0) - jnp.log1p(jnp.exp(-jnp.abs(x)))


def _retention(z_ref, c0, decay_ref, ng_ref, s0_ref, o_ref, st_ref, tab_scr, d_scr, u_scr, s_scr, n_tok):
    C = RET_CHUNK
    nc = n_tok // C
    d = HEAD_DIM
    @pl.when(pl.program_id(0) == 0)
    def _():
        pos = lax.broadcasted_iota(jnp.int32, (C, d), 0).astype(F32)
        ri = lax.broadcasted_iota(jnp.int32, (C, C), 0).astype(F32)
        ci = lax.broadcasted_iota(jnp.int32, (C, C), 1).astype(F32)
        diff = ri - ci
        for h in range(N_HEADS_RET):
            lgf = _log_sigmoid(jnp.full((C, C), decay_ref[0, h], F32))
            lgb = _log_sigmoid(jnp.full((C, C), decay_ref[1, h], F32))
            d_scr[h] = (jnp.where(diff >= 0, jnp.exp(lgf * jnp.maximum(diff, 0.0)), 0.0)
                        + jnp.where(diff <= 0, jnp.exp(lgb * jnp.maximum(-diff, 0.0)), 0.0))
            lf = _log_sigmoid(jnp.full((C, d), decay_ref[0, h], F32))
            lb = _log_sigmoid(jnp.full((C, d), decay_ref[1, h], F32))
            hs = slice(h * d, (h + 1) * d)
            tab_scr[0, :, hs] = jnp.exp(lf * (C - 1.0 - pos))
            tab_scr[1, :, hs] = jnp.exp(lb * pos)
            tab_scr[2, :, hs] = jnp.exp(lf * (pos + 1.0))
            tab_scr[3, :, hs] = jnp.exp(lb * (C - pos))
            tab_scr[4, :, hs] = jnp.exp(lf * C)
            tab_scr[5, :, hs] = jnp.exp(lb * C)

    def rows(c):
        return pl.ds(pl.multiple_of(c * C, C), C)

    def part(p):
        return slice(c0 + p * WIDTH_RET, c0 + (p + 1) * WIDTH_RET)

    W = WIDTH_RET
    head_shift = d.bit_length() - 1
    assert d == 1 << head_shift
    same_head = ((lax.broadcasted_iota(jnp.int32, (W, W), 0) >> head_shift)
                 == (lax.broadcasted_iota(jnp.int32, (W, W), 1) >> head_shift))
    lane_head = lax.broadcasted_iota(jnp.int32, (C, W), 1) >> head_shift
    head_mean = jnp.where(same_head, 1.0 / d, 0.0).astype(BF16)

    def upd(c, carry):
        k = z_ref[rows(c), part(1)].astype(F32) * SCALE
        v = z_ref[rows(c), part(2)]
        u_scr[0, c] = jnp.where(same_head, _mm((k * tab_scr[0]).T, v), 0.0)
        u_scr[1, c] = jnp.where(same_head, _mm((k * tab_scr[1]).T, v), 0.0)
        return carry

    lax.fori_loop(0, nc, upd, 0)

    for direction in range(2):
        if s0_ref is None:
            s_init = jnp.zeros((W, W), F32)
        else:
            u_scr[direction, nc] = jnp.zeros((W, W), F32)
            for h in range(N_HEADS_RET):
                u_scr[direction, nc, h * d:(h + 1) * d, h * d:(h + 1) * d] = s0_ref[0, 0, direction, h]
            s_init = u_scr[direction, nc]
        chunk_decay = tab_scr[4 + direction, 0:1, :]

        def scan(t, s):
            c = t if direction == 0 else nc - 1 - t
            s_scr[direction, c] = s.astype(BF16)
            return chunk_decay * s + u_scr[direction, c]

        s_fin = lax.fori_loop(0, nc, scan, s_init)
        if st_ref is not None:
            for h in range(N_HEADS_RET):
                st_ref[0, direction, h] = s_fin[h * d:(h + 1) * d, h * d:(h + 1) * d]

    def out(c, carry):
        q = z_ref[rows(c), part(0)].astype(BF16)
        k = (z_ref[rows(c), part(1)].astype(F32) * SCALE).astype(BF16)
        v = z_ref[rows(c), part(2)].astype(BF16)
        gate = z_ref[rows(c), part(3)].astype(F32)
        o = (jnp.dot(q, s_scr[0, c], preferred_element_type=F32) * tab_scr[2]
             + jnp.dot(q, s_scr[1, c], preferred_element_type=F32) * tab_scr[3])
        intra = [_mm_nt(jnp.where(lane_head == h, q, jnp.zeros_like(q)), k) * d_scr[h]
                 for h in range(N_HEADS_RET)]
        for h in range(N_HEADS_RET):
            o = o + jnp.where(lane_head == h, _mm(intra[h], v), 0.0)
        sq = o * o
        sq_hi = sq.astype(BF16)
        sq_lo = (sq - sq_hi.astype(F32)).astype(BF16)
        ms = (jnp.dot(sq_hi, head_mean, preferred_element_type=F32)
              + jnp.dot(sq_lo, head_mean, preferred_element_type=F32))
        o = o * lax.rsqrt(ms + EPS) * ng_ref[...]
        o_ref[rows(c), :] = (o * _silu(gate)).astype(o_ref.dtype)
        return carry

    lax.fori_loop(0, nc, out, 0)


def _ret_scratch(n_tok):
    nc = n_tok // RET_CHUNK
    return [pltpu.VMEM((6, RET_CHUNK, WIDTH_RET), F32),
            pltpu.VMEM((N_HEADS_RET, RET_CHUNK, RET_CHUNK), F32),
            pltpu.VMEM((2, nc + 1, WIDTH_RET, WIDTH_RET), F32),
            pltpu.VMEM((2, nc, WIDTH_RET, WIDTH_RET), BF16)]


def _ctx_attn_body(sink_ref, decay_ref, z_ref, zr_ref, ng_ref, *refs, seq, n_prev, layer, depth):
    (ona_ref, oswa_ref, oret_ref, kna_ref, vna_ref, kswa_ref, vswa_ref, st_ref,
     tab_scr, d_scr, u_scr, s_scr) = refs[n_prev:]
    cache_refs = (kna_ref, vna_ref, kswa_ref, vswa_ref, st_ref)
    if n_prev == 0:
        for ref in cache_refs:
            for other in range(depth):
                if other != layer:
                    ref[0, other] = jnp.zeros(ref.shape[2:], ref.dtype)
        kna_ref, vna_ref, kswa_ref, vswa_ref, st_ref = (ref.at[:, pl.ds(layer, 1)] for ref in cache_refs)
    d = HEAD_DIM
    low_half = lax.broadcasted_iota(jnp.int32, (seq, 2 * d), 1) < d

    def pair_scores(q2, k2, head_is_low):
        return _mm_nt(jnp.where(low_half == head_is_low, q2, jnp.zeros_like(q2)), k2)

    jobs = []
    for pair in range(N_HEADS_NA // 2):
        cols = lambda part: slice(OFF_NA + part * WIDTH_NA + pair * 2 * d,
                                  OFF_NA + part * WIDTH_NA + (pair + 1) * 2 * d)
        q2 = (z_ref[:, cols(0)] * SCALE).astype(BF16)
        k2 = z_ref[:, cols(1)]
        v2 = z_ref[:, cols(2)]
        k2_t = k2.T
        v2_t = v2.T
        for hh in range(2):
            kna_ref[0, 0, 2 * pair + hh] = k2_t[hh * d:(hh + 1) * d, :]
            vna_ref[0, 0, 2 * pair + hh] = v2_t[hh * d:(hh + 1) * d, :]
        k2 = k2.astype(BF16)
        v2 = v2.astype(BF16)
        jobs.append([(pair_scores(q2, k2, hh == 0), v2, hh == 0, None) for hh in range(2)]
                    + [ona_ref, pair])
    lo = OFF_SWA + WIDTH_SWA
    k2 = z_ref[:, lo:lo + WIDTH_KV_SWA]
    v2 = z_ref[:, lo + WIDTH_KV_SWA:lo + 2 * WIDTH_KV_SWA]
    k2_t = k2.T
    v2_t = v2.T
    for g in range(N_KV_SWA):
        kswa_ref[0, 0, g] = k2_t[g * d:(g + 1) * d, :]
        vswa_ref[0, 0, g] = v2_t[g * d:(g + 1) * d, :]
    k2 = k2.astype(BF16)
    v2 = v2.astype(BF16)
    group = N_HEADS_SWA // N_KV_SWA
    for pair in range(N_HEADS_SWA // 2):
        q2 = (z_ref[:, OFF_SWA + pair * 2 * d:OFF_SWA + (pair + 1) * 2 * d] * SCALE).astype(BF16)
        heads = []
        for hh in range(2):
            h = 2 * pair + hh
            g = h // group
            qh = q2[:, hh * d:(hh + 1) * d]
            q_at_g = jnp.concatenate([qh, qh], axis=-1) if g != hh else q2
            heads.append((pair_scores(q_at_g, k2, g == 0), v2, g == 0, sink_ref[h]))
        jobs.append(heads + [oswa_ref, pair])
    probs = []
    for job in jobs:
        for s, v2, v_is_low, sink in job[:2]:
            m = jnp.max(s, axis=-1, keepdims=True)
            if sink is not None:
                m = jnp.maximum(m, sink)
            e = jnp.exp(s - m)
            den = jnp.sum(e, axis=-1, keepdims=True)
            if sink is not None:
                den = den + jnp.exp(sink - m)
            probs.append((e, den))
    for j, job in enumerate(jobs):
        halves = []
        for hh, (s, v2, v_is_low, sink) in enumerate(job[:2]):
            e, den = probs[2 * j + hh]
            o2 = _mm(e, v2) / den
            halves.append(o2[:, :d] if v_is_low else o2[:, d:])
        o_ref, pair = job[2], job[3]
        o_ref[:, pair * 2 * d:(pair + 1) * 2 * d] = jnp.concatenate(halves, axis=-1).astype(o_ref.dtype)
    _retention(zr_ref, 0, decay_ref, ng_ref, None, oret_ref, st_ref.at[0],
               tab_scr, d_scr, u_scr, s_scr, seq)


def _ctx_mixers(z_att, z_rest, swa_sink, ret_decay, ret_norm_g, caches, layer, depth, batch, seq):
    n = batch * seq
    assert z_att.shape[1] == OFF_RET
    smem = pl.BlockSpec(memory_space=pltpu.SMEM)
    first = caches is None
    layers_in_block = depth if first else 1
    block_layer = 0 if first else layer
    kv = lambda heads: pl.BlockSpec((1, layers_in_block, heads, HEAD_DIM, seq),
                                    lambda b: (b, block_layer, 0, 0, 0))
    kv_shape = lambda heads: jax.ShapeDtypeStruct((batch, depth, heads, HEAD_DIM, seq), F32)
    in_specs = [smem, smem,
                pl.BlockSpec((seq, OFF_RET), lambda b: (b, 0)),
                pl.BlockSpec((seq, 4 * WIDTH_RET), lambda b: (b, 0)),
                pl.BlockSpec((1, WIDTH_RET), lambda b: (0, 0))]
    args = [swa_sink, ret_decay, z_att, z_rest, ret_norm_g.reshape(1, WIDTH_RET)]
    n_prev = 0 if caches is None else len(caches)
    aliases = {len(args) + i: 3 + i for i in range(n_prev)}
    if caches is not None:
        in_specs += [pl.BlockSpec(memory_space=pl.ANY)] * n_prev
        args += list(caches)
    return pl.pallas_call(
        functools.partial(_ctx_attn_body, seq=seq, n_prev=n_prev, layer=layer, depth=depth),
        grid=(batch,),
        in_specs=in_specs,
        out_specs=[pl.BlockSpec((seq, WIDTH_NA), lambda b: (b, 0)),
                   pl.BlockSpec((seq, WIDTH_SWA), lambda b: (b, 0)),
                   pl.BlockSpec((seq, WIDTH_RET), lambda b: (b, 0)),
                   kv(N_HEADS_NA), kv(N_HEADS_NA), kv(N_KV_SWA), kv(N_KV_SWA),
                   pl.BlockSpec((1, layers_in_block, 2, N_HEADS_RET, HEAD_DIM, HEAD_DIM),
                                lambda b: (b, block_layer, 0, 0, 0, 0))],
        out_shape=[jax.ShapeDtypeStruct((n, WIDTH_NA), BF16),
                   jax.ShapeDtypeStruct((n, WIDTH_SWA), BF16),
                   jax.ShapeDtypeStruct((n, WIDTH_RET), BF16),
                   kv_shape(N_HEADS_NA), kv_shape(N_HEADS_NA), kv_shape(N_KV_SWA), kv_shape(N_KV_SWA),
                   jax.ShapeDtypeStruct((batch, depth, 2, N_HEADS_RET, HEAD_DIM, HEAD_DIM), F32)],
        scratch_shapes=_ret_scratch(seq),
        input_output_aliases=aliases,
        compiler_params=_params(("arbitrary",), 32),
        name="ctx_mixers",
    )(*args)


NA_QROWS = 4
NA_WROWS = NA_QROWS + WIN_ROWS


def _na_bias_table(rpb, rows):
    n_groups = rows // NA_QROWS
    assert rows % NA_QROWS == 0 and n_groups >= 3
    n_heads, n_ro, n_co = rpb.shape
    half = WIN_ROWS // 2
    cc = np.arange(GRID_W)
    col_start = np.clip(cc - WIN_COLS // 2, 0, GRID_W - WIN_COLS)
    valid = (cc[None, :] >= col_start[:, None]) & (cc[None, :] < col_start[:, None] + WIN_COLS)
    span = 2 * GRID_W - 1
    lead = GRID_W - WIN_COLS
    ext = jnp.pad(rpb.astype(F32), ((0, 0), (0, 0), (lead, span + 1 - lead - n_co)))
    band = jnp.tile(ext, (1, 1, GRID_W))[..., :GRID_W * span].reshape(n_heads, n_ro, GRID_W, span)
    band = jnp.where(valid[None, None], band[..., GRID_W - 1:], MASKED)
    band = jnp.concatenate([band, jnp.full((n_heads, 1, GRID_W, GRID_W), MASKED, F32)], axis=1)
    slab = np.full((3, NA_QROWS, NA_WROWS), n_ro, np.int32)
    for cls, g in enumerate((0, 1, n_groups - 1)):
        ws = min(max(NA_QROWS * g - half, 0), rows - NA_WROWS)
        for i in range(NA_QROWS):
            r = NA_QROWS * g + i
            rs = min(max(r - half, 0), rows - WIN_ROWS)
            for j in range(NA_WROWS):
                if rs <= ws + j < rs + WIN_ROWS:
                    slab[cls, i, j] = ws + j - r + WIN_ROWS - 1

    def assemble(slab_ref, band_ref, o_ref):
        cls = pl.program_id(0)
        for i in range(NA_QROWS):
            for j in range(NA_WROWS):
                o_ref[0, 0, i * GRID_W:(i + 1) * GRID_W, j * GRID_W:(j + 1) * GRID_W] = (
                    band_ref[0, slab_ref[(cls * NA_QROWS + i) * NA_WROWS + j]])

    return pl.pallas_call(
        assemble,
        grid_spec=pltpu.PrefetchScalarGridSpec(
            num_scalar_prefetch=1,
            grid=(3, n_heads),
            in_specs=[pl.BlockSpec((1, n_ro + 1, GRID_W, GRID_W), lambda c, h, *_: (h, 0, 0, 0))],
            out_specs=pl.BlockSpec((1, 1, NA_QROWS * GRID_W, NA_WROWS * GRID_W), lambda c, h, *_: (c, h, 0, 0))),
        out_shape=jax.ShapeDtypeStruct((3, n_heads, NA_QROWS * GRID_W, NA_WROWS * GRID_W), F32),
        compiler_params=_params(("arbitrary", "arbitrary"), 32),
        name="na_bias_table",
    )(jnp.asarray(slab.reshape(-1)), band)


def _lat_na_body(z_ref, kc_ref, vc_ref, bias_ref, o_ref, kc_scr, vc_scr, *, rows):
    d = HEAD_DIM
    nq = NA_QROWS * GRID_W
    g = pl.program_id(1)

    @pl.when(g == 0)
    def _():
        kc_scr[...] = kc_ref[0, 0].astype(BF16)
        vc_scr[...] = vc_ref[0, 0].astype(BF16)

    ws = jnp.clip(NA_QROWS * g - WIN_ROWS // 2, 0, rows - NA_WROWS)
    qrows = pl.ds(pl.multiple_of(g * nq, nq), nq)
    wrows = pl.ds(pl.multiple_of(ws * GRID_W, GRID_W), NA_WROWS * GRID_W)
    low_half = lax.broadcasted_iota(jnp.int32, (nq, 2 * d), 1) < d
    for pair in range(N_HEADS_NA // 2):
        cols = slice(pair * 2 * d, (pair + 1) * 2 * d)
        q2 = (z_ref[qrows, cols] * SCALE).astype(BF16)
        kl2 = z_ref[wrows, WIDTH_NA + pair * 2 * d:WIDTH_NA + (pair + 1) * 2 * d].astype(BF16)
        vl2 = z_ref[wrows, 2 * WIDTH_NA + pair * 2 * d:2 * WIDTH_NA + (pair + 1) * 2 * d].astype(BF16)
        for hh in range(2):
            h = 2 * pair + hh
            qm = jnp.where(low_half == (hh == 0), q2, jnp.zeros_like(q2))
            s_loc = _mm_nt(qm, kl2) + bias_ref[0, h]
            s_ctx = _mm(q2[:, hh * d:(hh + 1) * d], kc_scr[h])
            m = jnp.maximum(jnp.max(s_ctx, axis=-1, keepdims=True), jnp.max(s_loc, axis=-1, keepdims=True))
            e_ctx = jnp.exp(s_ctx - m)
            e_loc = jnp.exp(s_loc - m)
            den = jnp.sum(e_ctx, axis=-1, keepdims=True) + jnp.sum(e_loc, axis=-1, keepdims=True)
            o = (_mm_nt(e_ctx, vc_scr[h]) + _mm(e_loc, vl2)[:, hh * d:(hh + 1) * d]) / den
            o_ref[:, h * d:(h + 1) * d] = o.astype(o_ref.dtype)


def _lat_na(z, cache_kt, cache_vt, bias_tab, layer, batch, seq):
    rows = seq // GRID_W
    n_groups = rows // NA_QROWS
    nq = NA_QROWS * GRID_W
    past = cache_kt.shape[4]

    def group_class(g):
        return jnp.where(g == 0, 0, jnp.where(g == n_groups - 1, 2, 1))

    cache = pl.BlockSpec((1, 1, N_HEADS_NA, HEAD_DIM, past), lambda b, g: (b, layer, 0, 0, 0))
    return pl.pallas_call(
        functools.partial(_lat_na_body, rows=rows),
        grid=(batch, n_groups),
        in_specs=[pl.BlockSpec((seq, 3 * WIDTH_NA), lambda b, g: (b, 0)),
                  cache, cache,
                  pl.BlockSpec((1, N_HEADS_NA, nq, NA_WROWS * GRID_W),
                               lambda b, g: (group_class(g), 0, 0, 0))],
        out_specs=pl.BlockSpec((nq, WIDTH_NA), lambda b, g: (b * n_groups + g, 0)),
        out_shape=jax.ShapeDtypeStruct((batch * seq, WIDTH_NA), BF16),
        scratch_shapes=[pltpu.VMEM((N_HEADS_NA, HEAD_DIM, past), BF16),
                        pltpu.VMEM((N_HEADS_NA, HEAD_DIM, past), BF16)],
        compiler_params=_params(("arbitrary", "arbitrary"), 58),
        name="lat_na",
    )(z, cache_kt, cache_vt, bias_tab)


def _rope_tables(seq):
    t = jnp.arange(seq)
    n_freq = HEAD_DIM // 4
    inv = ROPE_BASE ** (-jnp.arange(n_freq, dtype=F32) / n_freq)
    ang = jnp.concatenate([(t // GRID_W).astype(F32)[:, None] * inv,
                           (t % GRID_W).astype(F32)[:, None] * inv], axis=-1)
    cos = jnp.repeat(jnp.cos(ang), 2, axis=-1)
    sign = jnp.tile(jnp.asarray([-1.0, 1.0], F32), HEAD_DIM // 2)
    sin = jnp.repeat(jnp.sin(ang), 2, axis=-1) * sign
    return jnp.tile(cos, (1, N_HEADS_SWA)), jnp.tile(sin, (1, N_HEADS_SWA))


def _rope(x, cos, sin_signed):
    width = x.shape[-1]
    lane = lax.broadcasted_iota(jnp.int32, x.shape, x.ndim - 1)
    partner = jnp.where((lane & 1) == 0, pltpu.roll(x, width - 1, x.ndim - 1), pltpu.roll(x, 1, x.ndim - 1))
    return x * cos + partner * sin_signed


def _lat_swa_body(sink_ref, z_ref, kc_ref, vc_ref, cos_ref, sin_ref, o_ref, kr_scr, kc_scr, vc_scr, *, seq):
    d = HEAD_DIM
    blk = SWA_BLOCK
    n = pl.program_id(1)
    assert N_KV_SWA == 2

    @pl.when(n == 0)
    def _():
        k = z_ref[:, WIDTH_SWA:WIDTH_SWA + WIDTH_KV_SWA].astype(F32)
        kr_scr[...] = _rope(k, cos_ref[:, :WIDTH_KV_SWA], sin_ref[:, :WIDTH_KV_SWA]).astype(BF16)
        kc_scr[...] = jnp.concatenate([kc_ref[0, 0, 0], kc_ref[0, 0, 1]], axis=0).astype(BF16)
        vc_scr[...] = jnp.concatenate([vc_ref[0, 0, 0], vc_ref[0, 0, 1]], axis=0).astype(BF16)

    qrows = pl.ds(pl.multiple_of(n * blk, blk), blk)
    kstart = jnp.clip((n - 1) * blk, 0, seq - 3 * blk)
    krows = pl.ds(pl.multiple_of(kstart, blk), 3 * blk)
    q_all = _rope(z_ref[qrows, 0:WIDTH_SWA].astype(F32), cos_ref[qrows, :], sin_ref[qrows, :]) * SCALE
    q_all = q_all.astype(BF16)
    qpos = n * blk + lax.broadcasted_iota(jnp.int32, (blk, 3 * blk), 0)
    kpos = kstart + lax.broadcasted_iota(jnp.int32, (blk, 3 * blk), 1)
    valid = jnp.abs(qpos - kpos) <= SWA_RADIUS
    kl2 = kr_scr[krows, :]
    vl2 = z_ref[krows, WIDTH_SWA + WIDTH_KV_SWA:WIDTH_SWA + 2 * WIDTH_KV_SWA]
    group = N_HEADS_SWA // N_KV_SWA
    zero = jnp.zeros((blk, d), BF16)
    scores = []
    for h in range(N_HEADS_SWA):
        g = h // group
        qh = q_all[:, h * d:(h + 1) * d]
        q_at_g = jnp.concatenate([qh, zero] if g == 0 else [zero, qh], axis=-1)
        scores.append((jnp.where(valid, _mm_nt(q_at_g, kl2), MASKED), _mm(q_at_g, kc_scr[...])))
    probs = []
    for h, (s_loc, s_ctx) in enumerate(scores):
        m = jnp.maximum(jnp.maximum(jnp.max(s_loc, axis=-1, keepdims=True),
                                    jnp.max(s_ctx, axis=-1, keepdims=True)), sink_ref[h])
        e_loc = jnp.exp(s_loc - m)
        e_ctx = jnp.exp(s_ctx - m)
        den = (jnp.sum(e_loc, axis=-1, keepdims=True) + jnp.sum(e_ctx, axis=-1, keepdims=True)
               + jnp.exp(sink_ref[h] - m))
        probs.append((e_loc, e_ctx, den))
    outs = []
    for h, (e_loc, e_ctx, den) in enumerate(probs):
        g = h // group
        o2 = (_mm_nt(e_ctx, vc_scr[...]) + _mm(e_loc, vl2)) / den
        outs.append(o2[:, g * d:(g + 1) * d])
    o_ref[...] = jnp.concatenate(outs, axis=-1).astype(o_ref.dtype)


def _lat_swa(z, cache_kt, cache_vt, swa_sink, cos, sin, layer, batch, seq):
    nb = seq // SWA_BLOCK
    past = cache_kt.shape[4]
    width = WIDTH_SWA + 2 * WIDTH_KV_SWA
    assert OFF_SWA % width == 0 and seq >= 3 * SWA_BLOCK
    cache = pl.BlockSpec((1, 1, N_KV_SWA, HEAD_DIM, past), lambda b, n: (b, layer, 0, 0, 0))
    table = pl.BlockSpec((seq, WIDTH_SWA), lambda b, n: (0, 0))
    return pl.pallas_call(
        functools.partial(_lat_swa_body, seq=seq),
        grid=(batch, nb),
        in_specs=[pl.BlockSpec(memory_space=pltpu.SMEM),
                  pl.BlockSpec((seq, width), lambda b, n: (b, OFF_SWA // width)),
                  cache, cache, table, table],
        out_specs=pl.BlockSpec((SWA_BLOCK, WIDTH_SWA), lambda b, n: (b * nb + n, 0)),
        out_shape=jax.ShapeDtypeStruct((batch * seq, WIDTH_SWA), BF16),
        scratch_shapes=[pltpu.VMEM((seq, WIDTH_KV_SWA), BF16),
                        pltpu.VMEM((WIDTH_KV_SWA, past), BF16), pltpu.VMEM((WIDTH_KV_SWA, past), BF16)],
        compiler_params=_params(("arbitrary", "arbitrary"), 32),
        name="lat_swa",
    )(swa_sink, z, cache_kt, cache_vt, cos, sin)


def _lat_ret_body(decay_ref, z_ref, ng_ref, s0_ref, o_ref, tab_scr, d_scr, u_scr, s_scr, *, seq):
    _retention(z_ref, 0, decay_ref, ng_ref, s0_ref, o_ref, None, tab_scr, d_scr, u_scr, s_scr, seq)


def _lat_ret(z, ret_decay, ret_norm_g, state, layer, batch, seq):
    width = 4 * WIDTH_RET
    assert OFF_RET % width == 0
    return pl.pallas_call(
        functools.partial(_lat_ret_body, seq=seq),
        grid=(batch,),
        in_specs=[pl.BlockSpec(memory_space=pltpu.SMEM),
                  pl.BlockSpec((seq, width), lambda b: (b, OFF_RET // width)),
                  pl.BlockSpec((1, WIDTH_RET), lambda b: (0, 0)),
                  pl.BlockSpec((1, 1, 2, N_HEADS_RET, HEAD_DIM, HEAD_DIM),
                               lambda b: (b, layer, 0, 0, 0, 0))],
        out_specs=pl.BlockSpec((seq, WIDTH_RET), lambda b: (b, 0)),
        out_shape=jax.ShapeDtypeStruct((batch * seq, WIDTH_RET), BF16),
        scratch_shapes=_ret_scratch(seq),
        compiler_params=_params(("arbitrary",), 40),
        name="lat_ret",
    )(ret_decay, z, ret_norm_g.reshape(1, WIDTH_RET), state)


def _merge_body(ona_ref, oswa_ref, oret_ref, ga_ref, gs_ref, gr_ref, x_ref, mod_ref, g2_ref,
                wna_ref, wswa_ref, wret_ref, wout_ref, rw_ref, rb_ref,
                hx_ref, wna_scr, wswa_scr, wret_scr, wout_scr, rw_scr, *, per_cond):
    @pl.when(pl.program_id(0) == 0)
    def _():
        wna_scr[...] = wna_ref[0].astype(BF16)
        wswa_scr[...] = wswa_ref[0].astype(BF16)
        wret_scr[...] = wret_ref[0].astype(BF16)
        wout_scr[...] = wout_ref[0].astype(BF16)
        rw = rw_ref[0]
        rw_hi = rw.astype(BF16)
        rw_scr[0] = rw_hi
        rw_scr[1] = (rw - rw_hi.astype(F32)).astype(BF16)

    D = D_MODEL
    m = mod_ref[0]
    sub = x_ref.shape[0] // MERGE_SPLIT
    parts = [pl.ds(s * sub, sub) for s in range(MERGE_SPLIT)]
    gate = lambda ref, rows: _sigmoid(ref[rows, :].astype(F32))
    zs = [(gate(ga_ref, r) * jnp.dot(ona_ref[r, :], wna_scr[...], preferred_element_type=F32)
           + gate(gs_ref, r) * jnp.dot(oswa_ref[r, :], wswa_scr[...], preferred_element_type=F32)
           + gate(gr_ref, r) * jnp.dot(oret_ref[r, :], wret_scr[...], preferred_element_type=F32)
           ).astype(BF16) for r in parts]
    ys = [jnp.dot(z, wout_scr[...], preferred_element_type=F32) for z in zs]
    logits = []
    for r, y in zip(parts, ys):
        x1 = x_ref[r, :] + m[:, 2 * D:3 * D] * y
        hx_ref[r, HX_X1:HX_X1 + D] = x1
        h2 = x1 * lax.rsqrt(jnp.mean(x1 * x1, axis=-1, keepdims=True) + EPS) * g2_ref[0]
        h2 = h2 * (1.0 + m[:, 4 * D:5 * D]) + m[:, 3 * D:4 * D]
        hx_ref[r, 0:D] = h2
        h_hi = h2.astype(BF16)
        h_lo = (h2 - h_hi.astype(F32)).astype(BF16)
        logits.append((jnp.dot(h_hi, rw_scr[0], preferred_element_type=F32)
                       + jnp.dot(h_hi, rw_scr[1], preferred_element_type=F32)
                       + jnp.dot(h_lo, rw_scr[0], preferred_element_type=F32)) + rb_ref[0])
    for r, logit in zip(parts, logits):
        route = _route(logit)
        cond = (pl.program_id(0) * x_ref.shape[0] // per_cond).astype(F32)
        lane = lax.broadcasted_iota(jnp.int32, route.shape, 1)
        hx_ref[r, D:D + ROUTER_LANES] = jnp.where(lane == ROUTE_COND_LANE, cond, route)


MERGE_SPLIT = 2


def _route(logit):
    lane = lax.broadcasted_iota(jnp.int32, logit.shape, 1).astype(F32)
    neg = jnp.float32(-jnp.inf)
    big = jnp.float32(ROUTER_LANES)
    is_group = lane < N_GROUPS
    is_expert = (lane >= N_GROUPS) & (lane < N_GROUPS + N_EXPERTS)
    gl = jnp.where(is_group, logit, neg)
    gmax = jnp.max(gl, axis=-1, keepdims=True)
    gsel = jnp.min(jnp.where(gl == gmax, lane, big), axis=-1, keepdims=True)
    p_group = 1.0 / jnp.sum(jnp.where(is_group, jnp.exp(gl - gmax), 0.0), axis=-1, keepdims=True)
    eidx = lane - N_GROUPS
    in_group = is_expert & (jnp.floor(eidx / EXPERTS_PER_GROUP) == gsel)
    el = jnp.where(in_group, logit, neg)
    top1 = jnp.max(el, axis=-1, keepdims=True)
    sel1 = jnp.min(jnp.where(el == top1, lane, big), axis=-1, keepdims=True)
    el2 = jnp.where(lane == sel1, neg, el)
    top2 = jnp.max(el2, axis=-1, keepdims=True)
    sel2 = jnp.min(jnp.where(el2 == top2, lane, big), axis=-1, keepdims=True)
    e2 = jnp.exp(top2 - top1)
    w1 = p_group / (1.0 + e2)
    w2 = p_group * e2 / (1.0 + e2)
    comb = jnp.where(lane == sel1, w1, 0.0) + jnp.where(lane == sel2, w2, 0.0)
    return comb + jnp.where(lane == ROUTE_GROUP_LANE, gsel, 0.0)


def _merge(o_na, o_swa, o_ret, z, gate_tile, x, mod_rows, norm2_g, w_br_na, w_br_swa, w_br_ret, w_out,
           router_w, router_b, layer):
    n, d = x.shape
    n_cond = mod_rows.shape[0]
    per_cond = n // n_cond
    tm = 1024
    gate = lambda k: pl.BlockSpec((tm, d), lambda i: (i, gate_tile + k))
    row = lambda w: pl.BlockSpec((tm, w), lambda i: (i, 0))
    weight = lambda k: pl.BlockSpec((1, k, d), lambda i: (layer, 0, 0), pipeline_mode=pl.Buffered(1))
    return pl.pallas_call(
        functools.partial(_merge_body, per_cond=per_cond),
        grid=(n // tm,),
        in_specs=[row(WIDTH_NA), row(WIDTH_SWA), row(WIDTH_RET),
                  gate(0), gate(1), gate(2),
                  row(d),
                  pl.BlockSpec((1, 1, 6 * d), lambda i: (i * tm // per_cond, 0, 0)),
                  pl.BlockSpec((1, 1, d), lambda i: (layer, 0, 0)),
                  weight(WIDTH_NA), weight(WIDTH_SWA), weight(WIDTH_RET), weight(d),
                  pl.BlockSpec((1, d, ROUTER_LANES), lambda i: (layer, 0, 0)),
                  pl.BlockSpec((1, 1, ROUTER_LANES), lambda i: (layer, 0, 0))],
        out_specs=pl.BlockSpec((tm, HX_WIDTH), lambda i: (i, 0)),
        out_shape=jax.ShapeDtypeStruct((n, HX_WIDTH), F32),
        scratch_shapes=[pltpu.VMEM((WIDTH_NA, d), BF16), pltpu.VMEM((WIDTH_SWA, d), BF16),
                        pltpu.VMEM((WIDTH_RET, d), BF16), pltpu.VMEM((d, d), BF16),
                        pltpu.VMEM((2, d, ROUTER_LANES), BF16)],
        compiler_params=_params(("arbitrary",), 62),
        name="merge_router",
    )(o_na, o_swa, o_ret, z, z, z, x, mod_rows, norm2_g.reshape(norm2_g.shape[0], 1, d),
      w_br_na, w_br_swa, w_br_ret, w_out, router_w, router_b)


MOE_TILE = 512


def _route_plan(hx, n_tiles):
    n = hx.shape[0]
    g = hx[:, D_MODEL + ROUTE_GROUP_LANE].astype(jnp.int32)
    onehot = (g[:, None] == jnp.arange(N_GROUPS, dtype=jnp.int32)[None, :]).astype(jnp.int32)
    csum = jnp.cumsum(onehot, axis=0)
    counts = csum[-1]
    ends = jnp.cumsum(counts)
    dest = jnp.sum(onehot * ((ends - counts)[None, :] + csum - 1), axis=1)
    first = jnp.arange(n_tiles, dtype=jnp.int32) * MOE_TILE
    tile_lo = jnp.sum((first[:, None] >= ends[None, :]).astype(jnp.int32), axis=1)
    tile_hi = jnp.sum(((first + MOE_TILE - 1)[:, None] >= ends[None, :]).astype(jnp.int32), axis=1)
    return dest, tile_lo, tile_hi


def _moe_body(dest_ref, lo_ref, hi_ref, hx_hbm, wg_hbm, wu_hbm, wd_hbm, mod_ref, fg_ref, out_hbm,
              order_scr, hbuf, obuf, y_scr, wg_scr, wu_scr, wd_scr, in_sem, out_sem, w_sem,
              *, layer, n_tiles, n_tok, n_cond, final):
    step = pl.program_id(0)
    last_step = pl.num_programs(0) - 1
    rows_per_expert = MOE_TILE // EXPERTS_PER_GROUP

    def in_copy(tile, i, slot):
        return pltpu.make_async_copy(hx_hbm.at[pl.ds(order_scr[tile * MOE_TILE + i], 1)],
                                     hbuf.at[slot, pl.ds(i, 1)], in_sem.at[slot])

    def out_copy(tile, i, slot):
        return pltpu.make_async_copy(obuf.at[slot, pl.ds(i, 1)],
                                     out_hbm.at[pl.ds(order_scr[tile * MOE_TILE + i], 1)], out_sem.at[slot])

    def for_rows(fn):
        def body(i, carry):
            fn(i)
            return carry
        lax.fori_loop(0, MOE_TILE, body, 0, unroll=8)

    @pl.when(step == 0)
    def _():
        copies = [pltpu.make_async_copy(src.at[layer], dst, w_sem.at[k])
                  for k, (src, dst) in enumerate(((wg_hbm, wg_scr), (wu_hbm, wu_scr), (wd_hbm, wd_scr)))]
        for copy in copies:
            copy.start()

        def invert(i, carry):
            order_scr[dest_ref[i]] = i
            return carry
        lax.fori_loop(0, n_tok, invert, 0, unroll=16)
        for_rows(lambda i: in_copy(0, i, 0).start())
        for copy in copies:
            copy.wait()

    def run_tile(tile, slot, nxt, write_back):
        other = 1 - slot
        for_rows(lambda i: in_copy(tile, i, slot).wait())
        hx = hbuf[slot]
        h = hx[:, 0:D_MODEL].astype(BF16)
        route = hx[:, D_MODEL:D_MODEL + ROUTER_LANES]
        lane = lax.broadcasted_iota(jnp.int32, route.shape, 1)

        def group(k, copy_back):
            first = (lo_ref[tile] + k) * EXPERTS_PER_GROUP
            y = None
            for j in range(EXPERTS_PER_GROUP):
                e = first + j
                hg = jnp.dot(h, wg_scr[e], preferred_element_type=F32)
                hu = jnp.dot(h, wu_scr[e], preferred_element_type=F32)
                c = jnp.sum(jnp.where(lane == ROUTE_EXPERT_LANE + e, route, 0.0), axis=-1, keepdims=True)
                a = _silu(hg) * hu * c
                part = jnp.dot(a.astype(BF16), wd_scr[e], preferred_element_type=F32)
                y = part if y is None else y + part
                if k == 0:
                    for i in range(j * rows_per_expert, (j + 1) * rows_per_expert):
                        in_copy(nxt, i, other).start(priority=i % 2)
                        if copy_back:
                            out_copy(tile - 1, i, other).start(priority=i % 2)
            if k == 0:
                y_scr[...] = y
            else:
                y_scr[...] += y

        if write_back is True:
            group(0, True)
        else:
            pl.when(write_back)(functools.partial(group, 0, True))
            pl.when(jnp.logical_not(write_back))(functools.partial(group, 0, False))
        for k in range(1, N_GROUPS):
            pl.when(lo_ref[tile] + k <= hi_ref[tile])(functools.partial(group, k, False))

        cond = route[:, ROUTE_COND_LANE:ROUTE_COND_LANE + 1]
        gate2 = jnp.zeros((1, D_MODEL), F32)
        for c in range(n_cond):
            gate2 = jnp.where(cond == c, mod_ref[c][:, 5 * D_MODEL:6 * D_MODEL], gate2)
        x2 = hx[:, HX_X1:HX_X1 + D_MODEL] + gate2 * y_scr[...]
        if final:
            x2 = x2 * lax.rsqrt(jnp.mean(x2 * x2, axis=-1, keepdims=True) + EPS) * fg_ref[...]
        obuf[slot] = x2
        if write_back is True:
            for_rows(lambda i: out_copy(tile - 1, i, other).wait())
        else:
            @pl.when(write_back)
            def _():
                for_rows(lambda i: out_copy(tile - 1, i, other).wait())

    first_tile = 2 * step
    second_tile = first_tile + 1
    after = jnp.minimum(second_tile + 1, n_tiles - 1)
    run_tile(first_tile, 0, second_tile, step > 0)
    run_tile(second_tile, 1, after, True)

    @pl.when(step == last_step)
    def _():
        for_rows(lambda i: in_copy(after, i, 0).wait())
        for_rows(lambda i: out_copy(second_tile, i, 1).start())
        for_rows(lambda i: out_copy(second_tile, i, 1).wait())


def _moe(hx, mod_rows, w_gate, w_up, w_down, final_g, layer, final):
    n = hx.shape[0]
    d = D_MODEL
    n_tiles = n // MOE_TILE
    assert n_tiles % 2 == 0
    n_cond = mod_rows.shape[0]
    dest, tile_lo, tile_hi = _route_plan(hx, n_tiles)
    any_space = pl.BlockSpec(memory_space=pl.ANY)
    return pl.pallas_call(
        functools.partial(_moe_body, layer=layer, n_tiles=n_tiles, n_tok=n, n_cond=n_cond, final=final),
        grid_spec=pltpu.PrefetchScalarGridSpec(
            num_scalar_prefetch=3,
            grid=(n_tiles // 2,),
            in_specs=[any_space, any_space, any_space, any_space,
                      pl.BlockSpec((n_cond, 1, 6 * d), lambda t, *_: (0, 0, 0)),
                      pl.BlockSpec((1, d), lambda t, *_: (0, 0))],
            out_specs=any_space,
            scratch_shapes=[pltpu.SMEM((n,), jnp.int32),
                            pltpu.VMEM((2, MOE_TILE, HX_WIDTH), F32),
                            pltpu.VMEM((2, MOE_TILE, d), F32),
                            pltpu.VMEM((MOE_TILE, d), F32),
                            pltpu.VMEM((N_EXPERTS, d, D_EXPERT), BF16),
                            pltpu.VMEM((N_EXPERTS, d, D_EXPERT), BF16),
                            pltpu.VMEM((N_EXPERTS, D_EXPERT, d), BF16),
                            pltpu.SemaphoreType.DMA((2,)),
                            pltpu.SemaphoreType.DMA((2,)),
                            pltpu.SemaphoreType.DMA((3,))]),
        out_shape=jax.ShapeDtypeStruct((n, d), F32),
        compiler_params=_params(("arbitrary",), 56),
        name="moe",
    )(dest, tile_lo, tile_hi, hx, w_gate, w_up, w_down, mod_rows, final_g.reshape(1, d))


def kernel(x_prompt, x_sample, c, cache_na_k, cache_na_v, cache_swa_k, cache_swa_v, state_ret, c_ctx, ada_w, ada_b, norm1_g, norm2_g, w_in, na_rpb, swa_sink, ret_decay, ret_norm_g, w_br_na, w_br_swa, w_br_ret, w_out, router_group_w, router_group_b, router_expert_w, router_expert_b, w_gate, w_up, w_down, final_norm_g):
    batch, seq, d = x_prompt.shape
    dec_batch, dec_seq, _ = x_sample.shape
    depth = ada_w.shape[0]
    assert d == D_MODEL and dec_batch + 1 <= 8
    assert OFF_RET % IN_TILE == 0 and OFF_GATES % d == 0 and (OFF_GATES - OFF_RET) % d == 0

    cond = jnp.concatenate([c_ctx[None], c, jnp.zeros((8 - 1 - dec_batch, d), F32)], axis=0)
    mod = _modulation(cond, ada_w, ada_b)
    router_w = jnp.concatenate([router_group_w, router_expert_w], axis=-1)
    router_w = jnp.pad(router_w, ((0, 0), (0, 0), (0, ROUTER_LANES - router_w.shape[-1])))
    router_b = jnp.concatenate([router_group_b, router_expert_b], axis=-1)
    router_b = jnp.pad(router_b, ((0, 0), (0, ROUTER_LANES - router_b.shape[-1])))[:, None, :]
    cos, sin = _rope_tables(dec_seq)
    na_kt, na_vt, swa_kt, swa_vt = (jnp.swapaxes(t, -1, -2)
                                    for t in (cache_na_k, cache_na_v, cache_swa_k, cache_swa_v))
    moe_weights = (w_gate.astype(BF16), w_up.astype(BF16), w_down.astype(BF16))

    xp = x_prompt.reshape(batch * seq, d)
    xs = x_sample.reshape(dec_batch * dec_seq, d)
    caches = None
    for l in range(depth):
        final = l == depth - 1
        mod_ctx = mod[l, 0:1][:, None, :]
        mod_lat = mod[l, 1:1 + dec_batch][:, None, :]
        merge_w = (norm2_g, w_br_na, w_br_swa, w_br_ret, w_out, router_w, router_b, l)
        moe_w = (*moe_weights, final_norm_g, l, final)

        z_att, z_rest = _in_proj(xp, mod_ctx, norm1_g, w_in, l, OFF_RET // IN_TILE)
        o_na, o_swa, o_ret, *caches = _ctx_mixers(
            z_att, z_rest, swa_sink[l], ret_decay[l], ret_norm_g[l], caches, l, depth, batch, seq)
        hx = _merge(o_na, o_swa, o_ret, z_rest, (OFF_GATES - OFF_RET) // d, xp, mod_ctx, *merge_w)
        xp = _moe(hx, mod_ctx, *moe_w)

        z, = _in_proj(xs, mod_lat, norm1_g, w_in, l, 0)
        o_na = _lat_na(z, na_kt, na_vt, _na_bias_table(na_rpb[l], dec_seq // GRID_W),
                       l, dec_batch, dec_seq)
        o_swa = _lat_swa(z, swa_kt, swa_vt, swa_sink[l], cos, sin, l, dec_batch, dec_seq)
        o_ret = _lat_ret(z, ret_decay[l], ret_norm_g[l], state_ret, l, dec_batch, dec_seq)
        hx = _merge(o_na, o_swa, o_ret, z, OFF_GATES // d, xs, mod_lat, *merge_w)
        xs = _moe(hx, mod_lat, *moe_w)

    new_kv = [jnp.swapaxes(t, -1, -2) for t in caches[:4]]
    return (xp.reshape(batch, seq, d), xs.reshape(dec_batch, dec_seq, d), *new_kv, caches[4])
```
